```python
import jax, jax.numpy as jnp
from jax import lax
import numpy as np

D_MODEL = 1024
BATCH = 4
SEQ = 4096
DEPTH = 2
DEC_BATCH = 16
DEC_SEQ = 64
PAST_LEN = 4096

CHUNK = 64
N_AB = (DEPTH + 1) // 2
N_CD = DEPTH // 2
D_A = D_MODEL // 2
D_B = D_MODEL // 2
CONV_A = 31
CONV_B = 3
D_C = D_MODEL // 2
POOL_WINDOWS = (2, 4, 8, 16)
N_POOL = len(POOL_WINDOWS)
POOL_BUF = max(POOL_WINDOWS) - 1
D_D = D_MODEL // 2
N_HEAD_D = 4
GMLP_CHUNK = 128
N_MEM = 256
MEM_HEADS = 4
MEM_HEAD_DIM = D_MODEL // MEM_HEADS
N_EXPERTS = 16
N_EXPERT_GROUPS = 4
TOP_K = 2
D_EXPERT = D_MODEL // 2
ALPHA = (2 * DEPTH) ** 0.25
BETA = (8 * DEPTH) ** -0.25
LN_EPS = 1e-5

kernel_name = 'hybrid_streaming_conv_pool_gmlp_moe_step'


def layer_norm(x, g, b):
    xf = x.astype(jnp.float32)
    mu = jnp.mean(xf, -1, keepdims=True)
    var = jnp.mean(jnp.square(xf - mu), -1, keepdims=True)
    return ((xf - mu) * lax.rsqrt(var + LN_EPS) * g + b).astype(x.dtype)


def causal_dwconv(x, buf, w):
    xp = jnp.concatenate([buf.astype(x.dtype), x], axis=1)
    y = lax.conv_general_dilated(xp, w[:, None, :].astype(x.dtype), window_strides=(1,), padding='VALID',
                                 dimension_numbers=('NWC', 'WIO', 'NWC'), feature_group_count=x.shape[-1])
    return y, xp[:, -(w.shape[0] - 1):]


def causal_multiscale_pool(x, buf, pos0):
    bsz, L, _ = x.shape
    xcat = jnp.concatenate([buf.astype(x.dtype), x], axis=1)
    xp = xcat.astype(jnp.float32).reshape(bsz, POOL_BUF + L, N_POOL, D_C // N_POOL)
    csum = jnp.concatenate([jnp.zeros_like(xp[:, :1]), jnp.cumsum(xp, axis=1)], axis=1)
    end = csum[:, POOL_BUF + 1:]
    pos = pos0 + jnp.arange(L)
    outs = []
    for g, w in enumerate(POOL_WINDOWS):
        start = csum[:, POOL_BUF + 1 - w:POOL_BUF + 1 - w + L, g]
        cnt = jnp.minimum(pos + 1, w).astype(jnp.float32)[None, :, None]
        outs.append((end[:, :, g] - start) / cnt)
    pooled = jnp.stack(outs, axis=2) - xp[:, POOL_BUF:]
    return pooled.astype(x.dtype), xcat[:, -POOL_BUF:]


def mixer_ab(x, buf_a, buf_b, w_in, conv_a_w, conv_a_b, norm_a_g, norm_a_b, conv_b_w, w_out):
    h = x @ w_in
    a_val, a_gate, b_gate, c_gate, b_in = jnp.split(h, [D_A, 2 * D_A, 2 * D_A + D_B, 2 * D_A + 2 * D_B], axis=-1)
    a = a_val * jax.nn.sigmoid(a_gate)
    a_conv, new_buf_a = causal_dwconv(a, buf_a, conv_a_w)
    a = jax.nn.silu(layer_norm(a_conv + conv_a_b, norm_a_g, norm_a_b))
    b_conv, new_buf_b = causal_dwconv(c_gate * b_in, buf_b, conv_b_w)
    b = b_gate * b_conv
    y = jnp.concatenate([a, b], axis=-1) @ w_out
    return y, new_buf_a, new_buf_b


def mixer_cd(x, buf_pool, pos0, w_in, pool_w, pool_scale, v_norm_g, v_norm_b, w_s, b_s, w_out):
    bsz, L, _ = x.shape
    h = x @ w_in
    c_in, gd = jnp.split(h, [D_C], axis=-1)
    pooled, new_buf = causal_multiscale_pool(c_in, buf_pool, pos0)
    c = jnp.einsum('blgc,gcd->blgd', pooled.reshape(bsz, L, N_POOL, D_C // N_POOL), pool_w).reshape(bsz, L, D_C) * pool_scale
    z = jax.nn.gelu(gd)
    u, v = jnp.split(z, 2, axis=-1)
    v = layer_norm(v, v_norm_g, v_norm_b)
    n = min(L, GMLP_CHUNK)
    idx = jnp.arange(n)
    mask = (idx[None, :] // CHUNK) <= (idx[:, None] // CHUNK)
    ws = jnp.where(mask[None], w_s[:, :n, :n], 0.0)
    vc = v.reshape(bsz, L // n, n, N_HEAD_D, D_D // N_HEAD_D)
    mixed = jnp.einsum('gij,bnjgc->bnigc', ws, vc) + b_s[:, :n].T[None, None, :, :, None]
    d = u * mixed.reshape(bsz, L, D_D)
    y = jnp.concatenate([c, d], axis=-1) @ w_out
    return y, new_buf, v


def mem_cross_attention(x, k, v, wq, wo):
    bsz, L, _ = x.shape
    q = (x @ wq).reshape(bsz, L, MEM_HEADS, MEM_HEAD_DIM)
    s = jnp.einsum('blhd,bmhd->bhlm', q, k).astype(jnp.float32) * (MEM_HEAD_DIM ** -0.5)
    p = jax.nn.softmax(s, axis=-1).astype(x.dtype)
    o = jnp.einsum('bhlm,bmhd->blhd', p, v).reshape(bsz, L, D_MODEL)
    return o @ wo


def group_limited_top2_gates(x, router_w, router_b):
    logits = (x @ router_w).astype(jnp.float32) + router_b.astype(jnp.float32)
    scores = jax.nn.softmax(logits, axis=-1)
    per = N_EXPERTS // N_EXPERT_GROUPS
    group_score = scores.reshape(scores.shape[:-1] + (N_EXPERT_GROUPS, per)).max(-1)
    g_sel = jnp.argmax(group_score, axis=-1)
    in_group = (jnp.arange(N_EXPERTS) // per) == g_sel[..., None]
    masked = jnp.where(in_group, scores, -1.0)
    top_v, top_i = lax.top_k(masked, TOP_K)
    top_v = top_v / jnp.sum(top_v, axis=-1, keepdims=True)
    return jnp.sum(jax.nn.one_hot(top_i, N_EXPERTS, dtype=jnp.float32) * top_v[..., None], axis=-2)


def expert_mixture(x, gates, w_gate, w_up, w_down):
    hg = jnp.einsum('bld,edf->blef', x, w_gate)
    hu = jnp.einsum('bld,edf->blef', x, w_up)
    h = jax.nn.silu(hg) * hu * gates[..., None].astype(x.dtype)
    return jnp.einsum('blef,efd->bld', h, w_down)


def trunk(x, mem_k, mem_v, bufs_a, bufs_b, bufs_pool, pos0, ab, cd, shared):
    ln_g, ln_b, ca_wq, ca_wo, router_w, router_b, w_gate, w_up, w_down = shared
    new_a, new_b, new_pool, v_rows = [], [], [], []
    for layer in range(DEPTH):
        j = layer // 2
        if layer % 2 == 0:
            h, ba, bb = mixer_ab(x, bufs_a[j], bufs_b[j], *[w[j] for w in ab])
            new_a.append(ba)
            new_b.append(bb)
        else:
            h, bp, v = mixer_cd(x, bufs_pool[j], pos0, *[w[j] for w in cd])
            new_pool.append(bp)
            v_rows.append(v)
        x = layer_norm(ALPHA * x + h, ln_g[layer, 0], ln_b[layer, 0])
        x = layer_norm(ALPHA * x + mem_cross_attention(x, mem_k[layer], mem_v[layer], ca_wq[layer], ca_wo[layer]),
                       ln_g[layer, 1], ln_b[layer, 1])
        gates = group_limited_top2_gates(x, router_w, router_b)
        x = layer_norm(ALPHA * x + expert_mixture(x, gates, w_gate[layer], w_up[layer], w_down[layer]),
                       ln_g[layer, 2], ln_b[layer, 2])
    return x, jnp.stack(new_a), jnp.stack(new_b), jnp.stack(new_pool), jnp.stack(v_rows)


def setup_inputs(seed: int = 0) -> dict:
    key = jax.random.key(seed)
    ks = iter(jax.random.split(key, 40))

    def nrm(shape, scale=1.0):
        return jax.random.normal(next(ks), shape, jnp.float32) * scale

    d_ab = 2 * D_A + 3 * D_B
    d_cd = D_C + 2 * D_D
    mem_shape = (DEPTH, DEC_BATCH, N_MEM, MEM_HEADS, MEM_HEAD_DIM)
    return {
        'x_prompt': nrm((BATCH, SEQ, D_MODEL)),
        'x_sample': nrm((DEC_BATCH, DEC_SEQ, D_MODEL)),
        'mem_prompt': nrm((BATCH, N_MEM, D_MODEL)),
        'cache_mem_k': nrm(mem_shape),
        'cache_mem_v': nrm(mem_shape, BETA),
        'state_conv_a': nrm((N_AB, DEC_BATCH, CONV_A - 1, D_A), 0.5),
        'state_conv_b': nrm((N_AB, DEC_BATCH, CONV_B - 1, D_B)),
        'state_pool': nrm((N_CD, DEC_BATCH, POOL_BUF, D_C)),
        'ln_g': 1.0 + nrm((DEPTH, 3, D_MODEL), 0.02),
        'ln_b': nrm((DEPTH, 3, D_MODEL), 0.02),
        'ab_w_in': nrm((N_AB, D_MODEL, d_ab), D_MODEL ** -0.5),
        'ab_conv_a_w': nrm((N_AB, CONV_A, D_A), CONV_A ** -0.5),
        'ab_conv_a_b': nrm((N_AB, D_A), 0.02),
        'ab_norm_a_g': 1.0 + nrm((N_AB, D_A), 0.02),
        'ab_norm_a_b': nrm((N_AB, D_A), 0.02),
        'ab_conv_b_w': nrm((N_AB, CONV_B, D_B), CONV_B ** -0.5),
        'ab_w_out': nrm((N_AB, D_A + D_B, D_MODEL), BETA * (D_A + D_B) ** -0.5),
        'cd_w_in': nrm((N_CD, D_MODEL, d_cd), D_MODEL ** -0.5),
        'cd_pool_w': nrm((N_CD, N_POOL, D_C // N_POOL, D_C // N_POOL), (D_C // N_POOL) ** -0.5),
        'cd_pool_scale': 1.0 + nrm((N_CD, D_C), 0.02),
        'cd_v_norm_g': 1.0 + nrm((N_CD, D_D), 0.02),
        'cd_v_norm_b': nrm((N_CD, D_D), 0.02),
        'cd_w_s': nrm((N_CD, N_HEAD_D, GMLP_CHUNK, GMLP_CHUNK), GMLP_CHUNK ** -0.5),
        'cd_b_s': 1.0 + nrm((N_CD, N_HEAD_D, GMLP_CHUNK), 0.02),
        'cd_w_out': nrm((N_CD, D_C + D_D, D_MODEL), BETA * (D_C + D_D) ** -0.5),
        'ca_wq': nrm((DEPTH, D_MODEL, D_MODEL), D_MODEL ** -0.5),
        'ca_wk': nrm((DEPTH, D_MODEL, D_MODEL), D_MODEL ** -0.5),
        'ca_wv': nrm((DEPTH, D_MODEL, D_MODEL), BETA * D_MODEL ** -0.5),
        'ca_wo': nrm((DEPTH, D_MODEL, D_MODEL), BETA * D_MODEL ** -0.5),
        'router_w': nrm((D_MODEL, N_EXPERTS), D_MODEL ** -0.5),
        'router_b': nrm((N_EXPERTS,), 0.01),
        'moe_w_gate': nrm((DEPTH, N_EXPERTS, D_MODEL, D_EXPERT), D_MODEL ** -0.5),
        'moe_w_up': nrm((DEPTH, N_EXPERTS, D_MODEL, D_EXPERT), D_MODEL ** -0.5),
        'moe_w_down': nrm((DEPTH, N_EXPERTS, D_EXPERT, D_MODEL), BETA * D_EXPERT ** -0.5),
    }


def reference(x_prompt, x_sample, mem_prompt, cache_mem_k, cache_mem_v, state_conv_a, state_conv_b, state_pool,
              ln_g, ln_b, ab_w_in, ab_conv_a_w, ab_conv_a_b, ab_norm_a_g, ab_norm_a_b, ab_conv_b_w, ab_w_out,
              cd_w_in, cd_pool_w, cd_pool_scale, cd_v_norm_g, cd_v_norm_b, cd_w_s, cd_b_s, cd_w_out,
              ca_wq, ca_wk, ca_wv, ca_wo, router_w, router_b, moe_w_gate, moe_w_up, moe_w_down):
    ab = (ab_w_in, ab_conv_a_w, ab_conv_a_b, ab_norm_a_g, ab_norm_a_b, ab_conv_b_w, ab_w_out)
    cd = (cd_w_in, cd_pool_w, cd_pool_scale, cd_v_norm_g, cd_v_norm_b, cd_w_s, cd_b_s, cd_w_out)
    shared = (ln_g, ln_b, ca_wq, ca_wo, router_w, router_b, moe_w_gate, moe_w_up, moe_w_down)
    bsz = x_prompt.shape[0]
    dt = x_prompt.dtype
    mem_k_prompt = jnp.einsum('bmd,ldk->lbmk', mem_prompt, ca_wk).reshape(DEPTH, bsz, N_MEM, MEM_HEADS, MEM_HEAD_DIM)
    mem_v_prompt = jnp.einsum('bmd,ldk->lbmk', mem_prompt, ca_wv).reshape(DEPTH, bsz, N_MEM, MEM_HEADS, MEM_HEAD_DIM)
    zeros_a = jnp.zeros((N_AB, bsz, CONV_A - 1, D_A), dt)
    zeros_b = jnp.zeros((N_AB, bsz, CONV_B - 1, D_B), dt)
    zeros_pool = jnp.zeros((N_CD, bsz, POOL_BUF, D_C), dt)
    y_prompt, conv_a_prompt, conv_b_prompt, pool_prompt, _ = trunk(
        x_prompt, mem_k_prompt, mem_v_prompt, zeros_a, zeros_b, zeros_pool, 0, ab, cd, shared)
    y_sample, conv_a_sample, conv_b_sample, pool_sample, gmlp_v_sample = trunk(
        x_sample, cache_mem_k, cache_mem_v, state_conv_a, state_conv_b, state_pool, PAST_LEN, ab, cd, shared)
    return (y_prompt, y_sample, mem_k_prompt, mem_v_prompt, conv_a_prompt, conv_b_prompt, pool_prompt,
            conv_a_sample, conv_b_sample, pool_sample, gmlp_v_sample)
```

```python
import functools

import jax
import jax.numpy as jnp
import numpy as np
from jax import lax
from jax.experimental import pallas as pl
from jax.experimental.pallas import tpu as pltpu

F32 = jnp.float32
BF16 = jnp.bfloat16

D_MODEL = 1024
D_HALF = D_MODEL // 2
DEPTH = 2
PAST_LEN = 4096
CHUNK = 64
CHUNK_SHIFT = CHUNK.bit_length() - 1
CONV_A = 31
CONV_B = 3
POOL_WINDOWS = (2, 4, 8, 16)
POOL_BUF = max(POOL_WINDOWS) - 1
N_HEAD_D = 4
GMLP_CHUNK = 128
N_MEM = 256
MEM_HEADS = 4
MEM_HEAD_DIM = D_MODEL // MEM_HEADS
N_EXPERTS = 16
N_EXPERT_GROUPS = 4
GROUP_SIZE = N_EXPERTS // N_EXPERT_GROUPS
GROUP_SHIFT = GROUP_SIZE.bit_length() - 1
PAIRS_PER_GROUP = GROUP_SIZE * (GROUP_SIZE - 1) // 2
N_CLASSES = N_EXPERT_GROUPS * PAIRS_PER_GROUP
D_EXPERT = D_MODEL // 2
ALPHA = (2 * DEPTH) ** 0.25
LN_EPS = 1e-5

LANES = 128
SUBLANES = 8
ROW_W = D_MODEL + LANES
HIST_A = 32
HIST_B = 8
HIST_P = 16
CONV_ROWS = 64
MOE_TILE = 256
CLASS_ROWS = 32
DMA_CHUNK = 512
VMEM_LIMIT = 52 * 1024 * 1024

_PAIR_AB = [(a, b) for a in range(GROUP_SIZE) for b in range(a + 1, GROUP_SIZE)]
_CLASS_A = np.array([g * GROUP_SIZE + a for g in range(N_EXPERT_GROUPS) for a, _ in _PAIR_AB], np.int32)
_CLASS_B = np.array([g * GROUP_SIZE + b for g in range(N_EXPERT_GROUPS) for _, b in _PAIR_AB], np.int32)


def _layer_norm(x, g, b):
    mu = jnp.mean(x, axis=-1, keepdims=True)
    xc = x - mu
    var = jnp.mean(xc * xc, axis=-1, keepdims=True)
    return xc * lax.rsqrt(var + LN_EPS) * g + b


def _bdot(a, w):
    return jnp.dot(a.astype(BF16), w, preferred_element_type=F32)


def _split(a):
    hi = a.astype(BF16)
    return hi, (a - hi.astype(F32)).astype(BF16)


def _dot3(a, w_hi, w_lo):
    a_hi, a_lo = _split(a)
    return (jnp.dot(a_hi, w_hi, preferred_element_type=F32) + jnp.dot(a_lo, w_hi, preferred_element_type=F32)
            + jnp.dot(a_hi, w_lo, preferred_element_type=F32))


def _params(sem):
    return pltpu.CompilerParams(dimension_semantics=sem, vmem_limit_bytes=VMEM_LIMIT)


def _full(shape):
    return pl.BlockSpec(shape, lambda *_: (0,) * len(shape), pipeline_mode=pl.Buffered(1))


def _proj_kernel(x_ref, w_ref, o_ref):
    o_ref[0] = _bdot(x_ref[...], w_ref[0])


def _mem_projection(mem, w):
    n, rows = w.shape[0], mem.shape[0]
    return pl.pallas_call(
        _proj_kernel,
        grid=(n,),
        in_specs=[pl.BlockSpec((rows, D_MODEL), lambda j: (0, 0)),
                  pl.BlockSpec((1, D_MODEL, D_MODEL), lambda j: (j, 0, 0))],
        out_specs=pl.BlockSpec((1, rows, D_MODEL), lambda j: (j, 0, 0)),
        out_shape=jax.ShapeDtypeStruct((n, rows, D_MODEL), F32),
        compiler_params=_params(("arbitrary",)),
        name="mem_projection",
    )(mem, w)


def _load_history(ext_ref, buf_ref, first, hist, keep, seg):
    @pl.when(first)
    def _():
        ext_ref[:, hist - keep:hist, :] = buf_ref[...]

    @pl.when(jnp.logical_not(first))
    def _():
        ext_ref[:, hist - keep:hist, :] = ext_ref[:, seg + hist - keep:seg + hist, :]


def _depthwise_conv(ext_ref, w_ref, out_ref, *, n_seq, seg, taps, hist):
    rc = min(CONV_ROWS, seg)
    off0 = hist - (taps - 1)
    for s in range(n_seq):
        for r0 in range(0, seg, rc):
            for lb in range(0, D_HALF, LANES):
                acc = None
                for k in range(taps):
                    term = w_ref[k:k + 1, lb:lb + LANES] * ext_ref[s, off0 + k + r0:off0 + k + r0 + rc, lb:lb + LANES]
                    acc = term if acc is None else acc + term
                out_ref[s * seg + r0:s * seg + r0 + rc, lb:lb + LANES] = acc


def _mixer_ab_kernel(x_ref, bufa_ref, bufb_ref, w_in_ref, w_in_lo_ref, caw_ref, cab_ref, nag_ref, nab_ref, cbw_ref,
                     w_out_ref, w_out_lo_ref, lng_ref, lnb_ref,
                     y_ref, nbufa_ref, nbufb_ref,
                     a_ext, cb_ext, conv_a, conv_b, *, n_seq, seg, precise):
    rows = n_seq * seg
    first = pl.program_id(1) == 0
    x = x_ref[...].reshape(rows, D_MODEL)
    if precise:
        h = _dot3(x, w_in_ref[...], w_in_lo_ref[...])
    else:
        h = _bdot(x, w_in_ref[...])
    a = h[:, 0:D_HALF] * jax.nn.sigmoid(h[:, D_HALF:2 * D_HALF])
    cb = h[:, 3 * D_HALF:4 * D_HALF] * h[:, 4 * D_HALF:5 * D_HALF]

    _load_history(a_ext, bufa_ref, first, HIST_A, CONV_A - 1, seg)
    _load_history(cb_ext, bufb_ref, first, HIST_B, CONV_B - 1, seg)
    a_ext[:, HIST_A:HIST_A + seg, :] = a.reshape(n_seq, seg, D_HALF)
    cb_ext[:, HIST_B:HIST_B + seg, :] = cb.reshape(n_seq, seg, D_HALF)
    nbufa_ref[...] = a_ext[:, seg + HIST_A - (CONV_A - 1):seg + HIST_A, :]
    nbufb_ref[...] = cb_ext[:, seg + HIST_B - (CONV_B - 1):seg + HIST_B, :]

    _depthwise_conv(a_ext, caw_ref, conv_a, n_seq=n_seq, seg=seg, taps=CONV_A, hist=HIST_A)
    _depthwise_conv(cb_ext, cbw_ref, conv_b, n_seq=n_seq, seg=seg, taps=CONV_B, hist=HIST_B)

    a2 = _layer_norm(conv_a[...] + cab_ref[...], nag_ref[...], nab_ref[...])
    a2 = a2 * jax.nn.sigmoid(a2)
    b2 = h[:, 2 * D_HALF:3 * D_HALF] * conv_b[...]
    if precise:
        y = (_dot3(a2, w_out_ref[0:D_HALF, :], w_out_lo_ref[0:D_HALF, :])
             + _dot3(b2, w_out_ref[D_HALF:D_MODEL, :], w_out_lo_ref[D_HALF:D_MODEL, :]))
    else:
        y = _bdot(a2, w_out_ref[0:D_HALF, :]) + _bdot(b2, w_out_ref[D_HALF:D_MODEL, :])
    y_ref[...] = _layer_norm(ALPHA * x + y, lng_ref[...], lnb_ref[...]).reshape(n_seq, seg, D_MODEL)


def _mixer_ab(x, buf_a, buf_b, w_in, w_in_lo, caw, cab, nag, nab, cbw, w_out, w_out_lo, lng, lnb, *,
              n_seq, seg, precise):
    batch, length, _ = x.shape
    d_in = w_in.shape[1]
    grid = (batch // n_seq, length // seg)
    rows = n_seq * seg
    row2 = lambda v: v.reshape(1, -1)
    lo_in = _full((D_MODEL, d_in)) if precise else _full((SUBLANES, LANES))
    lo_out = _full((D_MODEL, D_MODEL)) if precise else _full((SUBLANES, LANES))
    return pl.pallas_call(
        functools.partial(_mixer_ab_kernel, n_seq=n_seq, seg=seg, precise=precise),
        grid=grid,
        in_specs=[pl.BlockSpec((n_seq, seg, D_MODEL), lambda b, l: (b, l, 0)),
                  pl.BlockSpec((n_seq, CONV_A - 1, D_HALF), lambda b, l: (b, 0, 0)),
                  pl.BlockSpec((n_seq, CONV_B - 1, D_HALF), lambda b, l: (b, 0, 0)),
                  _full((D_MODEL, d_in)), lo_in, _full((CONV_A, D_HALF)), _full((1, D_HALF)), _full((1, D_HALF)),
                  _full((1, D_HALF)), _full((CONV_B, D_HALF)), _full((D_MODEL, D_MODEL)), lo_out,
                  _full((1, D_MODEL)), _full((1, D_MODEL))],
        out_specs=[pl.BlockSpec((n_seq, seg, D_MODEL), lambda b, l: (b, l, 0)),
                   pl.BlockSpec((n_seq, CONV_A - 1, D_HALF), lambda b, l: (b, 0, 0)),
                   pl.BlockSpec((n_seq, CONV_B - 1, D_HALF), lambda b, l: (b, 0, 0))],
        out_shape=[jax.ShapeDtypeStruct((batch, length, D_MODEL), F32),
                   jax.ShapeDtypeStruct((batch, CONV_A - 1, D_HALF), F32),
                   jax.ShapeDtypeStruct((batch, CONV_B - 1, D_HALF), F32)],
        scratch_shapes=[pltpu.VMEM((n_seq, HIST_A + seg, D_HALF), F32),
                        pltpu.VMEM((n_seq, HIST_B + seg, D_HALF), F32),
                        pltpu.VMEM((rows, D_HALF), F32),
                        pltpu.VMEM((rows, D_HALF), F32)],
        compiler_params=_params(("arbitrary", "arbitrary")),
        name="mixer_ab",
    )(x, buf_a, buf_b, w_in, w_in_lo, caw, row2(cab), row2(nag), row2(nab), cbw, w_out, w_out_lo,
      row2(lng), row2(lnb))


def _mixer_cd_kernel(x_ref, bufp_ref, w_in_ref, pw_ref, ps_ref, vg_ref, vb_ref, ws_ref, bs_ref, w_out_ref,
                     lng_ref, lnb_ref,
                     y_ref, nbufp_ref, v_ref,
                     c_ext, pooled, mixed, *, n_seq, seg, pos0, n_mix):
    rows = n_seq * seg
    li = pl.program_id(1)
    x = x_ref[...].reshape(rows, D_MODEL)
    h = _bdot(x, w_in_ref[...])
    c_in = h[:, 0:D_HALF]

    _load_history(c_ext, bufp_ref, li == 0, HIST_P, POOL_BUF, seg)
    c_ext[:, HIST_P:HIST_P + seg, :] = c_in.reshape(n_seq, seg, D_HALF)
    nbufp_ref[...] = c_ext[:, seg + HIST_P - POOL_BUF:seg + HIST_P, :]

    rc = min(CONV_ROWS, seg)
    for g, win in enumerate(POOL_WINDOWS):
        lanes = slice(g * LANES, (g + 1) * LANES)
        for s in range(n_seq):
            for r0 in range(0, seg, rc):
                cur = c_ext[s, HIST_P + r0:HIST_P + r0 + rc, lanes]
                acc = cur
                for j in range(1, win):
                    acc = acc + c_ext[s, HIST_P + r0 - j:HIST_P + r0 - j + rc, lanes]
                pos = pos0 + li * seg + r0 + lax.broadcasted_iota(jnp.int32, (rc, LANES), 0)
                cnt = jnp.minimum(pos + 1, win).astype(F32)
                pooled[s * seg + r0:s * seg + r0 + rc, lanes] = acc / cnt - cur
    c = _bdot(pooled[...], pw_ref[...]) * ps_ref[...]

    z = jax.nn.gelu(h[:, D_HALF:3 * D_HALF], approximate=True)
    u = z[:, 0:D_HALF]
    v = _layer_norm(z[:, D_HALF:2 * D_HALF], vg_ref[...], vb_ref[...])
    v_ref[...] = v.reshape(n_seq, seg, D_HALF)
    vb16 = v.astype(BF16)

    ri = lax.broadcasted_iota(jnp.int32, (n_mix, n_mix), 0) >> CHUNK_SHIFT
    ci = lax.broadcasted_iota(jnp.int32, (n_mix, n_mix), 1) >> CHUNK_SHIFT
    for g in range(N_HEAD_D):
        lanes = slice(g * LANES, (g + 1) * LANES)
        ws = jnp.where(ci <= ri, ws_ref[g], 0.0).astype(BF16)
        bias = bs_ref[:, g:g + 1]
        for r0 in range(0, rows, n_mix):
            mixed[r0:r0 + n_mix, lanes] = jnp.dot(ws, vb16[r0:r0 + n_mix, lanes], preferred_element_type=F32) + bias
    d = u * mixed[...]
    y = _bdot(c, w_out_ref[0:D_HALF, :]) + _bdot(d, w_out_ref[D_HALF:D_MODEL, :])
    y_ref[...] = _layer_norm(ALPHA * x + y, lng_ref[...], lnb_ref[...]).reshape(n_seq, seg, D_MODEL)


def _mixer_cd(x, buf_p, w_in, pw, ps, vg, vb, ws, bs, w_out, lng, lnb, *, n_seq, seg, pos0):
    batch, length, _ = x.shape
    d_in = w_in.shape[1]
    n_mix = min(length, GMLP_CHUNK)
    assert seg % n_mix == 0
    rows = n_seq * seg
    row2 = lambda v: v.reshape(1, -1)
    ws_n = ws[:, :n_mix, :n_mix]
    bs_t = bs[:, :n_mix].T
    return pl.pallas_call(
        functools.partial(_mixer_cd_kernel, n_seq=n_seq, seg=seg, pos0=pos0, n_mix=n_mix),
        grid=(batch // n_seq, length // seg),
        in_specs=[pl.BlockSpec((n_seq, seg, D_MODEL), lambda b, l: (b, l, 0)),
                  pl.BlockSpec((n_seq, POOL_BUF, D_HALF), lambda b, l: (b, 0, 0)),
                  _full((D_MODEL, d_in)), _full((D_HALF, D_HALF)), _full((1, D_HALF)), _full((1, D_HALF)),
                  _full((1, D_HALF)), _full((N_HEAD_D, n_mix, n_mix)), _full((n_mix, N_HEAD_D)),
                  _full((D_MODEL, D_MODEL)), _full((1, D_MODEL)), _full((1, D_MODEL))],
        out_specs=[pl.BlockSpec((n_seq, seg, D_MODEL), lambda b, l: (b, l, 0)),
                   pl.BlockSpec((n_seq, POOL_BUF, D_HALF), lambda b, l: (b, 0, 0)),
                   pl.BlockSpec((n_seq, seg, D_HALF), lambda b, l: (b, l, 0))],
        out_shape=[jax.ShapeDtypeStruct((batch, length, D_MODEL), F32),
                   jax.ShapeDtypeStruct((batch, POOL_BUF, D_HALF), F32),
                   jax.ShapeDtypeStruct((batch, length, D_HALF), F32)],
        scratch_shapes=[pltpu.VMEM((n_seq, HIST_P + seg, D_HALF), F32),
                        pltpu.VMEM((rows, D_HALF), F32),
                        pltpu.VMEM((rows, D_HALF), F32)],
        compiler_params=_params(("arbitrary", "arbitrary")),
        name="mixer_cd",
    )(x, buf_p, w_in, pw, row2(ps), row2(vg), row2(vb), ws_n, bs_t, w_out, row2(lng), row2(lnb))


def _attn_router_kernel(x_ref, k_ref, v_ref, wq_ref, wo_ref, lng_ref, lnb_ref, rw_ref, rb_ref,
                        rows_ref, meta_ref, counts_ref,
                        o_scr, carry, *, n_seq, seg):
    rows = n_seq * seg
    x = x_ref[...].reshape(rows, D_MODEL)
    q = (_bdot(x, wq_ref[...]) * (MEM_HEAD_DIM ** -0.5)).astype(BF16)
    for s in range(n_seq):
        for hd in range(MEM_HEADS):
            cols = slice(hd * MEM_HEAD_DIM, (hd + 1) * MEM_HEAD_DIM)
            qh = q[s * seg:(s + 1) * seg, cols]
            kh = k_ref[s, :, cols].astype(BF16)
            sc = lax.dot_general(qh, kh, (((1,), (1,)), ((), ())), preferred_element_type=F32)
            p = jnp.exp(sc - jnp.max(sc, axis=-1, keepdims=True))
            den = jnp.sum(p, axis=-1, keepdims=True)
            o_scr[s * seg:(s + 1) * seg, cols] = _bdot(p, v_ref[s, :, cols].astype(BF16)) / den
    x2 = _layer_norm(ALPHA * x + _bdot(o_scr[...], wo_ref[...]), lng_ref[...], lnb_ref[...])
    rows_ref[:, 0:D_MODEL] = x2

    nt = (((1,), (1,)), ((), ()))
    x2_hi, x2_lo = _split(x2)
    both = lax.dot_general(rw_ref[...], x2_hi, nt, preferred_element_type=F32)
    cross = lax.dot_general(rw_ref[0:N_EXPERTS, :], x2_lo, nt, preferred_element_type=F32)
    logits = both[0:N_EXPERTS] + both[N_EXPERTS:2 * N_EXPERTS] + cross + rb_ref[...]
    e = jnp.exp(logits - jnp.max(logits, axis=0, keepdims=True))
    scores = e / jnp.sum(e, axis=0, keepdims=True)
    eid = lax.broadcasted_iota(jnp.int32, (N_EXPERTS, rows), 0)
    egrp = eid >> GROUP_SHIFT
    best = jnp.max(jnp.where(egrp == 0, scores, -1.0), axis=0, keepdims=True)
    g_sel = jnp.zeros((1, rows), jnp.int32)
    for g in range(1, N_EXPERT_GROUPS):
        gs = jnp.max(jnp.where(egrp == g, scores, -1.0), axis=0, keepdims=True)
        upd = gs > best
        g_sel = jnp.where(upd, g, g_sel)
        best = jnp.where(upd, gs, best)
    masked = jnp.where(egrp == g_sel, scores, -1.0)
    m1 = jnp.max(masked, axis=0, keepdims=True)
    i1 = jnp.min(jnp.where(masked == m1, eid, N_EXPERTS), axis=0, keepdims=True)
    masked2 = jnp.where(eid == i1, -2.0, masked)
    m2 = jnp.max(masked2, axis=0, keepdims=True)
    i2 = jnp.min(jnp.where(masked2 == m2, eid, N_EXPERTS), axis=0, keepdims=True)
    tot = m1 + m2
    g1 = m1 / tot
    g2 = m2 / tot
    first_low = i1 < i2
    ea = jnp.where(first_low, i1, i2) & (GROUP_SIZE - 1)
    eb = jnp.where(first_low, i2, i1) & (GROUP_SIZE - 1)
    gate_a = jnp.where(first_low, g1, g2)
    gate_b = jnp.where(first_low, g2, g1)
    pair = eb - 1 + jnp.where(ea == 1, 2, 0) + jnp.where(ea == 2, 3, 0)
    cls = g_sel * PAIRS_PER_GROUP + pair

    lane_row = lax.broadcasted_iota(jnp.int32, (LANES, rows), 0)
    gates_t = jnp.where(lane_row == 0, gate_a, jnp.where(lane_row == 1, gate_b, 0.0))
    rows_ref[:, D_MODEL:ROW_W] = gates_t.T

    @pl.when((pl.program_id(0) == 0) & (pl.program_id(1) == 0))
    def _():
        carry[...] = jnp.zeros_like(carry)

    onehot = (lax.broadcasted_iota(jnp.int32, (CLASS_ROWS, rows), 0) == cls).astype(F32)
    tri = (lax.broadcasted_iota(jnp.int32, (rows, rows), 0) <= lax.broadcasted_iota(jnp.int32, (rows, rows), 1))
    cum = jnp.dot(onehot.astype(BF16), tri.astype(F32).astype(BF16), preferred_element_type=F32)
    before = carry[:, 0:1]
    rank = jnp.sum(onehot * (before + cum), axis=0, keepdims=True) - 1.0
    sub = lax.broadcasted_iota(jnp.int32, (SUBLANES, rows), 0)
    meta_ref[...] = jnp.where(sub == 0, cls, jnp.where(sub == 1, rank.astype(jnp.int32), 0))
    carry[...] = carry[...] + cum[:, rows - 1:rows]
    counts_ref[...] = carry[...].astype(jnp.int32)


def _attn_router(x, mem_k, mem_v, wq, wo, lng, lnb, rw_t, rb, *, n_seq, seg):
    batch, length, _ = x.shape
    rows = n_seq * seg
    total = batch * length
    n_l = length // seg
    row2 = lambda v: v.reshape(1, -1)
    return pl.pallas_call(
        functools.partial(_attn_router_kernel, n_seq=n_seq, seg=seg),
        grid=(batch // n_seq, n_l),
        in_specs=[pl.BlockSpec((n_seq, seg, D_MODEL), lambda b, l: (b, l, 0)),
                  pl.BlockSpec((n_seq, N_MEM, D_MODEL), lambda b, l: (b, 0, 0)),
                  pl.BlockSpec((n_seq, N_MEM, D_MODEL), lambda b, l: (b, 0, 0)),
                  _full((D_MODEL, D_MODEL)), _full((D_MODEL, D_MODEL)), _full((1, D_MODEL)), _full((1, D_MODEL)),
                  _full((2 * N_EXPERTS, D_MODEL)), _full((N_EXPERTS, 1))],
        out_specs=[pl.BlockSpec((rows, ROW_W), lambda b, l: (b * n_l + l, 0)),
                   pl.BlockSpec((SUBLANES, rows), lambda b, l: (0, b * n_l + l)),
                   pl.BlockSpec((CLASS_ROWS, LANES), lambda b, l: (0, 0))],
        out_shape=[jax.ShapeDtypeStruct((total, ROW_W), F32),
                   jax.ShapeDtypeStruct((SUBLANES, total), jnp.int32),
                   jax.ShapeDtypeStruct((CLASS_ROWS, LANES), jnp.int32)],
        scratch_shapes=[pltpu.VMEM((rows, D_MODEL), F32),
                        pltpu.VMEM((CLASS_ROWS, LANES), F32)],
        compiler_params=_params(("arbitrary", "arbitrary")),
        name="attn_router",
    )(x, mem_k, mem_v, wq, wo, row2(lng), row2(lnb), rw_t, rb.reshape(N_EXPERTS, 1))


def _row_copy_loop(n_rows, make_copy, wait_copy):
    n_chunks = n_rows // DMA_CHUNK

    def issue_chunk(c):
        def body(i, carry_):
            make_copy(c * DMA_CHUNK + i).start()
            return carry_
        lax.fori_loop(0, DMA_CHUNK, body, 0, unroll=8)

    issue_chunk(0)

    def chunk_body(c, carry_):
        issue_chunk(c)
        wait_copy().wait()
        return carry_
    lax.fori_loop(1, n_chunks, chunk_body, 0)
    wait_copy().wait()


def _dispatch_kernel(cls_ref, rank_ref, offs_ref, rows_hbm, init_hbm, xs_hbm, sem, *, n_rows):
    del init_hbm

    def make_copy(t):
        p = offs_ref[cls_ref[t]] + rank_ref[t]
        return pltpu.make_async_copy(rows_hbm.at[pl.ds(t, 1)], xs_hbm.at[pl.ds(p, 1)], sem)

    def wait_copy():
        return pltpu.make_async_copy(rows_hbm.at[pl.ds(0, DMA_CHUNK)], xs_hbm.at[pl.ds(0, DMA_CHUNK)], sem)

    _row_copy_loop(n_rows, make_copy, wait_copy)


def _undispatch_kernel(cls_ref, rank_ref, offs_ref, ys_hbm, out_hbm, sem, *, n_rows):
    def make_copy(t):
        p = offs_ref[cls_ref[t]] + rank_ref[t]
        return pltpu.make_async_copy(ys_hbm.at[pl.ds(p, 1)], out_hbm.at[pl.ds(t, 1)], sem)

    def wait_copy():
        return pltpu.make_async_copy(ys_hbm.at[pl.ds(0, DMA_CHUNK)], out_hbm.at[pl.ds(0, DMA_CHUNK)], sem)

    _row_copy_loop(n_rows, make_copy, wait_copy)


def _dispatch(cls, rank, offs, rows, n_sorted):
    n_rows = rows.shape[0]
    init = jnp.zeros((n_sorted, ROW_W), F32)
    any_spec = pl.BlockSpec(memory_space=pl.ANY)
    return pl.pallas_call(
        functools.partial(_dispatch_kernel, n_rows=n_rows),
        grid_spec=pltpu.PrefetchScalarGridSpec(
            num_scalar_prefetch=3, grid=(1,), in_specs=[any_spec, any_spec], out_specs=any_spec,
            scratch_shapes=[pltpu.SemaphoreType.DMA(())]),
        out_shape=jax.ShapeDtypeStruct((n_sorted, ROW_W), F32),
        input_output_aliases={4: 0},
        compiler_params=_params(("arbitrary",)),
        name="dispatch",
    )(cls, rank, offs, rows, init)


def _undispatch(cls, rank, offs, ys):
    n_rows = cls.shape[0]
    any_spec = pl.BlockSpec(memory_space=pl.ANY)
    return pl.pallas_call(
        functools.partial(_undispatch_kernel, n_rows=n_rows),
        grid_spec=pltpu.PrefetchScalarGridSpec(
            num_scalar_prefetch=3, grid=(1,), in_specs=[any_spec], out_specs=any_spec,
            scratch_shapes=[pltpu.SemaphoreType.DMA(())]),
        out_shape=jax.ShapeDtypeStruct((n_rows, D_MODEL), F32),
        compiler_params=_params(("arbitrary",)),
        name="undispatch",
    )(cls, rank, offs, ys)


def _experts_kernel(ea_ref, eb_ref, valid_ref, xs_ref, wga_ref, wua_ref, wda_ref, wgb_ref, wub_ref, wdb_ref,
                    lng_ref, lnb_ref, ys_ref):
    del ea_ref, eb_ref
    valid = valid_ref[pl.program_id(0)] == 1

    @pl.when(jnp.logical_not(valid))
    def _():
        ys_ref[...] = jnp.zeros_like(ys_ref)

    @pl.when(valid)
    def _():
        x = xs_ref[:, 0:D_MODEL]
        xb = x.astype(BF16)
        y = None
        for lane, (wg, wu, wd) in enumerate(((wga_ref, wua_ref, wda_ref), (wgb_ref, wub_ref, wdb_ref))):
            gate = xs_ref[:, D_MODEL + lane:D_MODEL + lane + 1]
            hg = jnp.dot(xb, wg[0], preferred_element_type=F32)
            hu = jnp.dot(xb, wu[0], preferred_element_type=F32)
            hid = hg * jax.nn.sigmoid(hg) * hu * gate
            part = _bdot(hid, wd[0])
            y = part if y is None else y + part
        ys_ref[...] = _layer_norm(ALPHA * x + y, lng_ref[...], lnb_ref[...])


def _experts(tile_a, tile_b, tile_valid, xs, w_gate, w_up, w_down, lng, lnb):
    n_sorted = xs.shape[0]
    n_tiles = n_sorted // MOE_TILE
    up_a = pl.BlockSpec((1, D_MODEL, D_EXPERT), lambda i, ea, eb, va: (ea[i], 0, 0))
    up_b = pl.BlockSpec((1, D_MODEL, D_EXPERT), lambda i, ea, eb, va: (eb[i], 0, 0))
    dn_a = pl.BlockSpec((1, D_EXPERT, D_MODEL), lambda i, ea, eb, va: (ea[i], 0, 0))
    dn_b = pl.BlockSpec((1, D_EXPERT, D_MODEL), lambda i, ea, eb, va: (eb[i], 0, 0))
    vec = pl.BlockSpec((1, D_MODEL), lambda i, ea, eb, va: (0, 0))
    return pl.pallas_call(
        _experts_kernel,
        grid_spec=pltpu.PrefetchScalarGridSpec(
            num_scalar_prefetch=3, grid=(n_tiles,),
            in_specs=[pl.BlockSpec((MOE_TILE, ROW_W), lambda i, ea, eb, va: (i, 0)),
                      up_a, up_a, dn_a, up_b, up_b, dn_b, vec, vec],
            out_specs=pl.BlockSpec((MOE_TILE, D_MODEL), lambda i, ea, eb, va: (i, 0))),
        out_shape=jax.ShapeDtypeStruct((n_sorted, D_MODEL), F32),
        compiler_params=_params(("arbitrary",)),
        name="experts",
    )(tile_a, tile_b, tile_valid, xs, w_gate, w_up, w_down, w_gate, w_up, w_down,
      lng.reshape(1, -1), lnb.reshape(1, -1))


def _moe(rows, meta, counts, w_gate, w_up, w_down, lng, lnb):
    n_rows = rows.shape[0]
    n_tiles = n_rows // MOE_TILE + N_CLASSES
    cnt = counts[:N_CLASSES, 0]
    tiles_per = (cnt + MOE_TILE - 1) // MOE_TILE
    tile_end = jnp.cumsum(tiles_per)
    offs = jnp.zeros((CLASS_ROWS,), jnp.int32).at[:N_CLASSES].set((tile_end - tiles_per) * MOE_TILE)
    tile_id = jnp.arange(n_tiles, dtype=jnp.int32)
    used = tile_end[N_CLASSES - 1]
    tile_cls = jnp.searchsorted(tile_end, jnp.minimum(tile_id, used - 1), side="right").astype(jnp.int32)
    tile_a = jnp.asarray(_CLASS_A)[tile_cls]
    tile_b = jnp.asarray(_CLASS_B)[tile_cls]
    tile_valid = (tile_id < used).astype(jnp.int32)
    cls, rank = meta[0], meta[1]
    xs = _dispatch(cls, rank, offs, rows, n_tiles * MOE_TILE)
    ys = _experts(tile_a, tile_b, tile_valid, xs, w_gate, w_up, w_down, lng, lnb)
    return _undispatch(cls, rank, offs, ys)


def _trunk(x, mem_k, mem_v, buf_a, buf_b, buf_p, pos0, w, *, mix_seq, mix_seg, att_seq, att_seg, precise):
    batch, length, _ = x.shape
    x, new_a, new_b = _mixer_ab(x, buf_a, buf_b, w["ab_w_in"], w["ab_w_in_lo"], w["ab_conv_a_w"], w["ab_conv_a_b"],
                                w["ab_norm_a_g"], w["ab_norm_a_b"], w["ab_conv_b_w"], w["ab_w_out"], w["ab_w_out_lo"],
                                w["ln_g"][0, 0], w["ln_b"][0, 0], n_seq=mix_seq, seg=mix_seg, precise=precise)
    new_p = v_rows = None
    for layer in range(DEPTH):
        if layer == 1:
            x, new_p, v_rows = _mixer_cd(x, buf_p, w["cd_w_in"], w["cd_pool_w"], w["cd_pool_scale"], w["cd_v_norm_g"],
                                         w["cd_v_norm_b"], w["cd_w_s"], w["cd_b_s"], w["cd_w_out"],
                                         w["ln_g"][1, 0], w["ln_b"][1, 0], n_seq=mix_seq, seg=mix_seg, pos0=pos0)
        rows, meta, counts = _attn_router(x, mem_k[layer], mem_v[layer], w["ca_wq"][layer], w["ca_wo"][layer],
                                          w["ln_g"][layer, 1], w["ln_b"][layer, 1], w["router_w_t"], w["router_b"],
                                          n_seq=att_seq, seg=att_seg)
        x = _moe(rows, meta, counts, w["moe_w_gate"][layer], w["moe_w_up"][layer], w["moe_w_down"][layer],
                 w["ln_g"][layer, 2], w["ln_b"][layer, 2]).reshape(batch, length, D_MODEL)
    return x, new_a, new_b, new_p, v_rows


def kernel(x_prompt, x_sample, mem_prompt, cache_mem_k, cache_mem_v, state_conv_a, state_conv_b, state_pool,
           ln_g, ln_b, ab_w_in, ab_conv_a_w, ab_conv_a_b, ab_norm_a_g, ab_norm_a_b, ab_conv_b_w, ab_w_out,
           cd_w_in, cd_pool_w, cd_pool_scale, cd_v_norm_g, cd_v_norm_b, cd_w_s, cd_b_s, cd_w_out,
           ca_wq, ca_wk, ca_wv, ca_wo, router_w, router_b, moe_w_gate, moe_w_up, moe_w_down):
    assert ln_g.shape[0] == DEPTH and ab_w_in.shape[0] == 1 and cd_w_in.shape[0] == 1
    bsz, seq, _ = x_prompt.shape
    dec_b, dec_seq, _ = x_sample.shape
    n_pool = len(POOL_WINDOWS)
    pool_c = D_HALF // n_pool
    pool_bd = jnp.zeros((D_HALF, D_HALF), F32)
    for g in range(n_pool):
        pool_bd = pool_bd.at[g * pool_c:(g + 1) * pool_c, g * pool_c:(g + 1) * pool_c].set(cd_pool_w[0, g])
    def hi_lo(v):
        hi = v.astype(BF16)
        return hi, (v - hi.astype(F32)).astype(BF16)

    ab_w_in_hi, ab_w_in_lo = hi_lo(ab_w_in[0])
    ab_w_out_hi, ab_w_out_lo = hi_lo(ab_w_out[0])
    rw_hi, rw_lo = hi_lo(router_w.T)
    w = {
        "ln_g": ln_g, "ln_b": ln_b,
        "ab_w_in": ab_w_in_hi, "ab_w_in_lo": ab_w_in_lo, "ab_conv_a_w": ab_conv_a_w[0], "ab_conv_a_b": ab_conv_a_b[0],
        "ab_norm_a_g": ab_norm_a_g[0], "ab_norm_a_b": ab_norm_a_b[0], "ab_conv_b_w": ab_conv_b_w[0],
        "ab_w_out": ab_w_out_hi, "ab_w_out_lo": ab_w_out_lo,
        "cd_w_in": cd_w_in[0].astype(BF16), "cd_pool_w": pool_bd.astype(BF16), "cd_pool_scale": cd_pool_scale[0],
        "cd_v_norm_g": cd_v_norm_g[0], "cd_v_norm_b": cd_v_norm_b[0], "cd_w_s": cd_w_s[0], "cd_b_s": cd_b_s[0],
        "cd_w_out": cd_w_out[0].astype(BF16),
        "ca_wq": ca_wq.astype(BF16), "ca_wo": ca_wo.astype(BF16),
        "router_w_t": jnp.concatenate([rw_hi, rw_lo], axis=0), "router_b": router_b,
        "moe_w_gate": moe_w_gate.astype(BF16), "moe_w_up": moe_w_up.astype(BF16),
        "moe_w_down": moe_w_down.astype(BF16),
    }

    kv = _mem_projection(mem_prompt.reshape(bsz * N_MEM, D_MODEL),
                         jnp.concatenate([ca_wk, ca_wv], axis=0).astype(BF16))
    kv = kv.reshape(2, DEPTH, bsz, N_MEM, D_MODEL)
    mem_k_prompt, mem_v_prompt = kv[0], kv[1]
    y_prompt, conv_a_p, conv_b_p, pool_p, _ = _trunk(
        x_prompt, mem_k_prompt, mem_v_prompt,
        jnp.zeros((bsz, CONV_A - 1, D_HALF), F32), jnp.zeros((bsz, CONV_B - 1, D_HALF), F32),
        jnp.zeros((bsz, POOL_BUF, D_HALF), F32), 0, w,
        mix_seq=1, mix_seg=512, att_seq=1, att_seg=512, precise=True)

    y_sample, conv_a_s, conv_b_s, pool_s, v_s = _trunk(
        x_sample, cache_mem_k.reshape(DEPTH, dec_b, N_MEM, D_MODEL), cache_mem_v.reshape(DEPTH, dec_b, N_MEM, D_MODEL),
        state_conv_a[0], state_conv_b[0], state_pool[0], PAST_LEN, w,
        mix_seq=8, mix_seg=dec_seq, att_seq=4, att_seg=dec_seq, precise=False)

    kv_shape = (DEPTH, bsz, N_MEM, MEM_HEADS, MEM_HEAD_DIM)
    return (y_prompt, y_sample, mem_k_prompt.reshape(kv_shape), mem_v_prompt.reshape(kv_shape),
            conv_a_p[None], conv_b_p[None], pool_p[None], conv_a_s[None], conv_b_s[None], pool_s[None], v_s[None])
```

```python
import functools

import jax
import jax.numpy as jnp
import numpy as np
from jax import lax
from jax.experimental import pallas as pl
from jax.experimental.pallas import tpu as pltpu

F32 = jnp.float32
BF16 = jnp.bfloat16

D_MODEL = 1024
D_HALF = D_MODEL // 2
DEPTH = 2
PAST_LEN = 4096
CHUNK = 64
CHUNK_SHIFT = CHUNK.bit_length() - 1
CONV_A = 31
CONV_B = 3
POOL_WINDOWS = (2, 4, 8, 16)
POOL_BUF = max(POOL_WINDOWS) - 1
N_HEAD_D = 4
GMLP_CHUNK = 128
N_MEM = 256
MEM_HEADS = 4
MEM_HEAD_DIM = D_MODEL // MEM_HEADS
N_EXPERTS = 16
N_EXPERT_GROUPS = 4
GROUP_SIZE = N_EXPERTS // N_EXPERT_GROUPS
GROUP_SHIFT = GROUP_SIZE.bit_length() - 1
PAIRS_PER_GROUP = GROUP_SIZE * (GROUP_SIZE - 1) // 2
N_CLASSES = N_EXPERT_GROUPS * PAIRS_PER_GROUP
D_EXPERT = D_MODEL // 2
ALPHA = (2 * DEPTH) ** 0.25
LN_EPS = 1e-5

LANES = 128
SUBLANES = 8
ROW_W = D_MODEL + LANES
HIST_A = 32
HIST_B = 8
HIST_P = 16
CONV_ROWS = 64
MOE_TILE = 256
CLASS_ROWS = 32
DISPATCH_TILE = 512
SPLIT_ROWS = 256
VMEM_LIMIT = 52 * 1024 * 1024

_PAIR_AB = [(a, b) for a in range(GROUP_SIZE) for b in range(a + 1, GROUP_SIZE)]
_CLASS_A = np.array([g * GROUP_SIZE + a for g in range(N_EXPERT_GROUPS) for a, _ in _PAIR_AB], np.int32)
_CLASS_B = np.array([g * GROUP_SIZE + b for g in range(N_EXPERT_GROUPS) for _, b in _PAIR_AB], np.int32)


def _layer_norm(x, g, b):
    mu = jnp.mean(x, axis=-1, keepdims=True)
    xc = x - mu
    var = jnp.mean(xc * xc, axis=-1, keepdims=True)
    return xc * lax.rsqrt(var + LN_EPS) * g + b


def _bdot(a, w):
    return jnp.dot(a.astype(BF16), w, preferred_element_type=F32)


def _split(a):
    hi = a.astype(BF16)
    return hi, (a - hi.astype(F32)).astype(BF16)


def _dot3(a, w_hi, w_lo):
    a_hi, a_lo = _split(a)
    return (jnp.dot(a_hi, w_hi, preferred_element_type=F32) + jnp.dot(a_lo, w_hi, preferred_element_type=F32)
            + jnp.dot(a_hi, w_lo, preferred_element_type=F32))


def _params(sem):
    return pltpu.CompilerParams(dimension_semantics=sem, vmem_limit_bytes=VMEM_LIMIT)


def _full(shape):
    return pl.BlockSpec(shape, lambda *_: (0,) * len(shape), pipeline_mode=pl.Buffered(1))


def _split_weight_kernel(w_ref, hi_ref, lo_ref):
    hi, lo = _split(w_ref[...])
    hi_ref[...] = hi
    lo_ref[...] = lo


def _split_weight(w):
    rows, cols = w.shape
    blk = min(rows, SPLIT_ROWS)
    spec = pl.BlockSpec((blk, cols), lambda i: (i, 0))
    return pl.pallas_call(
        _split_weight_kernel,
        grid=(rows // blk,),
        in_specs=[spec],
        out_specs=[spec, spec],
        out_shape=[jax.ShapeDtypeStruct(w.shape, BF16)] * 2,
        compiler_params=_params(("arbitrary",)),
        name="split_weight",
    )(w)


def _proj_kernel(x_ref, w_ref, o_ref):
    o_ref[0] = _bdot(x_ref[...], w_ref[0])


def _mem_projection(mem, w):
    n, rows = w.shape[0], mem.shape[0]
    return pl.pallas_call(
        _proj_kernel,
        grid=(n,),
        in_specs=[pl.BlockSpec((rows, D_MODEL), lambda j: (0, 0)),
                  pl.BlockSpec((1, D_MODEL, D_MODEL), lambda j: (j, 0, 0))],
        out_specs=pl.BlockSpec((1, rows, D_MODEL), lambda j: (j, 0, 0)),
        out_shape=jax.ShapeDtypeStruct((n, rows, D_MODEL), F32),
        compiler_params=_params(("arbitrary",)),
        name="mem_projection",
    )(mem, w)


def _load_history(ext_ref, buf_ref, first, hist, keep, seg):
    @pl.when(first)
    def _():
        ext_ref[:, hist - keep:hist, :] = buf_ref[...]

    @pl.when(jnp.logical_not(first))
    def _():
        ext_ref[:, hist - keep:hist, :] = ext_ref[:, seg + hist - keep:seg + hist, :]


def _depthwise_conv(ext_ref, w_ref, out_ref, *, n_seq, seg, taps, hist):
    rc = min(CONV_ROWS, seg)
    off0 = hist - (taps - 1)
    for s in range(n_seq):
        for r0 in range(0, seg, rc):
            for lb in range(0, D_HALF, LANES):
                acc = None
                for k in range(taps):
                    term = w_ref[k:k + 1, lb:lb + LANES] * ext_ref[s, off0 + k + r0:off0 + k + r0 + rc, lb:lb + LANES]
                    acc = term if acc is None else acc + term
                out_ref[s * seg + r0:s * seg + r0 + rc, lb:lb + LANES] = acc


def _mixer_ab_kernel(x_ref, bufa_ref, bufb_ref, w_in_ref, w_in_lo_ref, caw_ref, cab_ref, nag_ref, nab_ref, cbw_ref,
                     w_out_ref, w_out_lo_ref, lng_ref, lnb_ref,
                     y_ref, nbufa_ref, nbufb_ref,
                     a_ext, cb_ext, conv_a, conv_b, *, n_seq, seg, precise):
    rows = n_seq * seg
    first = pl.program_id(1) == 0
    x = x_ref[...].reshape(rows, D_MODEL)
    if precise:
        h = _dot3(x, w_in_ref[...], w_in_lo_ref[...])
    else:
        h = _bdot(x, w_in_ref[...])
    a = h[:, 0:D_HALF] * jax.nn.sigmoid(h[:, D_HALF:2 * D_HALF])
    cb = h[:, 3 * D_HALF:4 * D_HALF] * h[:, 4 * D_HALF:5 * D_HALF]

    _load_history(a_ext, bufa_ref, first, HIST_A, CONV_A - 1, seg)
    _load_history(cb_ext, bufb_ref, first, HIST_B, CONV_B - 1, seg)
    a_ext[:, HIST_A:HIST_A + seg, :] = a.reshape(n_seq, seg, D_HALF)
    cb_ext[:, HIST_B:HIST_B + seg, :] = cb.reshape(n_seq, seg, D_HALF)
    nbufa_ref[...] = a_ext[:, seg + HIST_A - (CONV_A - 1):seg + HIST_A, :]
    nbufb_ref[...] = cb_ext[:, seg + HIST_B - (CONV_B - 1):seg + HIST_B, :]

    _depthwise_conv(a_ext, caw_ref, conv_a, n_seq=n_seq, seg=seg, taps=CONV_A, hist=HIST_A)
    _depthwise_conv(cb_ext, cbw_ref, conv_b, n_seq=n_seq, seg=seg, taps=CONV_B, hist=HIST_B)

    a2 = _layer_norm(conv_a[...] + cab_ref[...], nag_ref[...], nab_ref[...])
    a2 = a2 * jax.nn.sigmoid(a2)
    b2 = h[:, 2 * D_HALF:3 * D_HALF] * conv_b[...]
    if precise:
        y = (_dot3(a2, w_out_ref[0:D_HALF, :], w_out_lo_ref[0:D_HALF, :])
             + _dot3(b2, w_out_ref[D_HALF:D_MODEL, :], w_out_lo_ref[D_HALF:D_MODEL, :]))
    else:
        y = _bdot(a2, w_out_ref[0:D_HALF, :]) + _bdot(b2, w_out_ref[D_HALF:D_MODEL, :])
    y_ref[...] = _layer_norm(ALPHA * x + y, lng_ref[...], lnb_ref[...]).reshape(n_seq, seg, D_MODEL)


def _mixer_ab(x, buf_a, buf_b, w_in, w_in_lo, caw, cab, nag, nab, cbw, w_out, w_out_lo, lng, lnb, *,
              n_seq, seg, precise):
    batch, length, _ = x.shape
    d_in = w_in.shape[1]
    grid = (batch // n_seq, length // seg)
    rows = n_seq * seg
    row2 = lambda v: v.reshape(1, -1)
    lo_in = _full((D_MODEL, d_in)) if precise else _full((SUBLANES, LANES))
    lo_out = _full((D_MODEL, D_MODEL)) if precise else _full((SUBLANES, LANES))
    return pl.pallas_call(
        functools.partial(_mixer_ab_kernel, n_seq=n_seq, seg=seg, precise=precise),
        grid=grid,
        in_specs=[pl.BlockSpec((n_seq, seg, D_MODEL), lambda b, l: (b, l, 0)),
                  pl.BlockSpec((n_seq, CONV_A - 1, D_HALF), lambda b, l: (b, 0, 0)),
                  pl.BlockSpec((n_seq, CONV_B - 1, D_HALF), lambda b, l: (b, 0, 0)),
                  _full((D_MODEL, d_in)), lo_in, _full((CONV_A, D_HALF)), _full((1, D_HALF)), _full((1, D_HALF)),
                  _full((1, D_HALF)), _full((CONV_B, D_HALF)), _full((D_MODEL, D_MODEL)), lo_out,
                  _full((1, D_MODEL)), _full((1, D_MODEL))],
        out_specs=[pl.BlockSpec((n_seq, seg, D_MODEL), lambda b, l: (b, l, 0)),
                   pl.BlockSpec((n_seq, CONV_A - 1, D_HALF), lambda b, l: (b, 0, 0)),
                   pl.BlockSpec((n_seq, CONV_B - 1, D_HALF), lambda b, l: (b, 0, 0))],
        out_shape=[jax.ShapeDtypeStruct((batch, length, D_MODEL), F32),
                   jax.ShapeDtypeStruct((batch, CONV_A - 1, D_HALF), F32),
                   jax.ShapeDtypeStruct((batch, CONV_B - 1, D_HALF), F32)],
        scratch_shapes=[pltpu.VMEM((n_seq, HIST_A + seg, D_HALF), F32),
                        pltpu.VMEM((n_seq, HIST_B + seg, D_HALF), F32),
                        pltpu.VMEM((rows, D_HALF), F32),
                        pltpu.VMEM((rows, D_HALF), F32)],
        compiler_params=_params(("arbitrary", "arbitrary")),
        name="mixer_ab",
    )(x, buf_a, buf_b, w_in, w_in_lo, caw, row2(cab), row2(nag), row2(nab), cbw, w_out, w_out_lo,
      row2(lng), row2(lnb))


def _mixer_cd_kernel(x_ref, bufp_ref, w_in_ref, pw_ref, ps_ref, vg_ref, vb_ref, ws_ref, bs_ref, w_out_ref,
                     lng_ref, lnb_ref,
                     y_ref, nbufp_ref, v_ref,
                     c_ext, pooled, mixed, *, n_seq, seg, pos0, n_mix):
    rows = n_seq * seg
    li = pl.program_id(1)
    x = x_ref[...].reshape(rows, D_MODEL)
    h = _bdot(x, w_in_ref[...])
    c_in = h[:, 0:D_HALF]

    _load_history(c_ext, bufp_ref, li == 0, HIST_P, POOL_BUF, seg)
    c_ext[:, HIST_P:HIST_P + seg, :] = c_in.reshape(n_seq, seg, D_HALF)
    nbufp_ref[...] = c_ext[:, seg + HIST_P - POOL_BUF:seg + HIST_P, :]

    rc = min(CONV_ROWS, seg)
    for g, win in enumerate(POOL_WINDOWS):
        lanes = slice(g * LANES, (g + 1) * LANES)
        for s in range(n_seq):
            for r0 in range(0, seg, rc):
                cur = c_ext[s, HIST_P + r0:HIST_P + r0 + rc, lanes]
                acc = cur
                for j in range(1, win):
                    acc = acc + c_ext[s, HIST_P + r0 - j:HIST_P + r0 - j + rc, lanes]
                pos = pos0 + li * seg + r0 + lax.broadcasted_iota(jnp.int32, (rc, LANES), 0)
                cnt = jnp.minimum(pos + 1, win).astype(F32)
                pooled[s * seg + r0:s * seg + r0 + rc, lanes] = acc / cnt - cur
    c = _bdot(pooled[...], pw_ref[...]) * ps_ref[...]

    z = jax.nn.gelu(h[:, D_HALF:3 * D_HALF], approximate=True)
    u = z[:, 0:D_HALF]
    v = _layer_norm(z[:, D_HALF:2 * D_HALF], vg_ref[...], vb_ref[...])
    v_ref[...] = v.reshape(n_seq, seg, D_HALF)
    vb16 = v.astype(BF16)

    ri = lax.broadcasted_iota(jnp.int32, (n_mix, n_mix), 0) >> CHUNK_SHIFT
    ci = lax.broadcasted_iota(jnp.int32, (n_mix, n_mix), 1) >> CHUNK_SHIFT
    for g in range(N_HEAD_D):
        lanes = slice(g * LANES, (g + 1) * LANES)
        ws = jnp.where(ci <= ri, ws_ref[g], 0.0).astype(BF16)
        bias = bs_ref[:, g:g + 1]
        for r0 in range(0, rows, n_mix):
            mixed[r0:r0 + n_mix, lanes] = jnp.dot(ws, vb16[r0:r0 + n_mix, lanes], preferred_element_type=F32) + bias
    d = u * mixed[...]
    y = _bdot(c, w_out_ref[0:D_HALF, :]) + _bdot(d, w_out_ref[D_HALF:D_MODEL, :])
    y_ref[...] = _layer_norm(ALPHA * x + y, lng_ref[...], lnb_ref[...]).reshape(n_seq, seg, D_MODEL)


def _mixer_cd(x, buf_p, w_in, pw, ps, vg, vb, ws, bs, w_out, lng, lnb, *, n_seq, seg, pos0):
    batch, length, _ = x.shape
    d_in = w_in.shape[1]
    n_mix = min(length, GMLP_CHUNK)
    assert seg % n_mix == 0
    rows = n_seq * seg
    row2 = lambda v: v.reshape(1, -1)
    ws_n = ws[:, :n_mix, :n_mix]
    bs_t = bs[:, :n_mix].T
    return pl.pallas_call(
        functools.partial(_mixer_cd_kernel, n_seq=n_seq, seg=seg, pos0=pos0, n_mix=n_mix),
        grid=(batch // n_seq, length // seg),
        in_specs=[pl.BlockSpec((n_seq, seg, D_MODEL), lambda b, l: (b, l, 0)),
                  pl.BlockSpec((n_seq, POOL_BUF, D_HALF), lambda b, l: (b, 0, 0)),
                  _full((D_MODEL, d_in)), _full((D_HALF, D_HALF)), _full((1, D_HALF)), _full((1, D_HALF)),
                  _full((1, D_HALF)), _full((N_HEAD_D, n_mix, n_mix)), _full((n_mix, N_HEAD_D)),
                  _full((D_MODEL, D_MODEL)), _full((1, D_MODEL)), _full((1, D_MODEL))],
        out_specs=[pl.BlockSpec((n_seq, seg, D_MODEL), lambda b, l: (b, l, 0)),
                   pl.BlockSpec((n_seq, POOL_BUF, D_HALF), lambda b, l: (b, 0, 0)),
                   pl.BlockSpec((n_seq, seg, D_HALF), lambda b, l: (b, l, 0))],
        out_shape=[jax.ShapeDtypeStruct((batch, length, D_MODEL), F32),
                   jax.ShapeDtypeStruct((batch, POOL_BUF, D_HALF), F32),
                   jax.ShapeDtypeStruct((batch, length, D_HALF), F32)],
        scratch_shapes=[pltpu.VMEM((n_seq, HIST_P + seg, D_HALF), F32),
                        pltpu.VMEM((rows, D_HALF), F32),
                        pltpu.VMEM((rows, D_HALF), F32)],
        compiler_params=_params(("arbitrary", "arbitrary")),
        name="mixer_cd",
    )(x, buf_p, w_in, pw, row2(ps), row2(vg), row2(vb), ws_n, bs_t, w_out, row2(lng), row2(lnb))


def _attn_router_kernel(x_ref, k_ref, v_ref, wq_ref, wo_ref, lng_ref, lnb_ref, rw_ref, rb_ref,
                        rows_ref, meta_ref, counts_ref,
                        o_scr, carry, *, n_seq, seg):
    rows = n_seq * seg
    x = x_ref[...].reshape(rows, D_MODEL)
    q = (_bdot(x, wq_ref[...]) * (MEM_HEAD_DIM ** -0.5)).astype(BF16)
    for s in range(n_seq):
        for hd in range(MEM_HEADS):
            cols = slice(hd * MEM_HEAD_DIM, (hd + 1) * MEM_HEAD_DIM)
            qh = q[s * seg:(s + 1) * seg, cols]
            kh = k_ref[s, :, cols].astype(BF16)
            sc = lax.dot_general(qh, kh, (((1,), (1,)), ((), ())), preferred_element_type=F32)
            p = jnp.exp(sc - jnp.max(sc, axis=-1, keepdims=True))
            den = jnp.sum(p, axis=-1, keepdims=True)
            o_scr[s * seg:(s + 1) * seg, cols] = _bdot(p, v_ref[s, :, cols].astype(BF16)) / den
    x2 = _layer_norm(ALPHA * x + _bdot(o_scr[...], wo_ref[...]), lng_ref[...], lnb_ref[...])
    rows_ref[:, 0:D_MODEL] = x2

    nt = (((1,), (1,)), ((), ()))
    x2_hi, x2_lo = _split(x2)
    both = lax.dot_general(rw_ref[...], x2_hi, nt, preferred_element_type=F32)
    cross = lax.dot_general(rw_ref[0:N_EXPERTS, :], x2_lo, nt, preferred_element_type=F32)
    logits = both[0:N_EXPERTS] + both[N_EXPERTS:2 * N_EXPERTS] + cross + rb_ref[...]
    e = jnp.exp(logits - jnp.max(logits, axis=0, keepdims=True))
    scores = e / jnp.sum(e, axis=0, keepdims=True)
    eid = lax.broadcasted_iota(jnp.int32, (N_EXPERTS, rows), 0)
    egrp = eid >> GROUP_SHIFT
    best = jnp.max(jnp.where(egrp == 0, scores, -1.0), axis=0, keepdims=True)
    g_sel = jnp.zeros((1, rows), jnp.int32)
    for g in range(1, N_EXPERT_GROUPS):
        gs = jnp.max(jnp.where(egrp == g, scores, -1.0), axis=0, keepdims=True)
        upd = gs > best
        g_sel = jnp.where(upd, g, g_sel)
        best = jnp.where(upd, gs, best)
    masked = jnp.where(egrp == g_sel, scores, -1.0)
    m1 = jnp.max(masked, axis=0, keepdims=True)
    i1 = jnp.min(jnp.where(masked == m1, eid, N_EXPERTS), axis=0, keepdims=True)
    masked2 = jnp.where(eid == i1, -2.0, masked)
    m2 = jnp.max(masked2, axis=0, keepdims=True)
    i2 = jnp.min(jnp.where(masked2 == m2, eid, N_EXPERTS), axis=0, keepdims=True)
    tot = m1 + m2
    g1 = m1 / tot
    g2 = m2 / tot
    first_low = i1 < i2
    ea = jnp.where(first_low, i1, i2) & (GROUP_SIZE - 1)
    eb = jnp.where(first_low, i2, i1) & (GROUP_SIZE - 1)
    gate_a = jnp.where(first_low, g1, g2)
    gate_b = jnp.where(first_low, g2, g1)
    pair = eb - 1 + jnp.where(ea == 1, 2, 0) + jnp.where(ea == 2, 3, 0)
    cls = g_sel * PAIRS_PER_GROUP + pair

    lane_row = lax.broadcasted_iota(jnp.int32, (LANES, rows), 0)
    gates_t = jnp.where(lane_row == 0, gate_a, jnp.where(lane_row == 1, gate_b, 0.0))
    rows_ref[:, D_MODEL:ROW_W] = gates_t.T

    @pl.when((pl.program_id(0) == 0) & (pl.program_id(1) == 0))
    def _():
        carry[...] = jnp.zeros_like(carry)

    onehot = (lax.broadcasted_iota(jnp.int32, (CLASS_ROWS, rows), 0) == cls).astype(F32)
    tri = (lax.broadcasted_iota(jnp.int32, (rows, rows), 0) <= lax.broadcasted_iota(jnp.int32, (rows, rows), 1))
    cum = jnp.dot(onehot.astype(BF16), tri.astype(F32).astype(BF16), preferred_element_type=F32)
    before = carry[:, 0:1]
    rank = jnp.sum(onehot * (before + cum), axis=0, keepdims=True) - 1.0
    sub = lax.broadcasted_iota(jnp.int32, (SUBLANES, rows), 0)
    meta_ref[...] = jnp.where(sub == 0, cls, jnp.where(sub == 1, rank.astype(jnp.int32), 0))
    carry[...] = carry[...] + cum[:, rows - 1:rows]
    counts_ref[...] = carry[...].astype(jnp.int32)


def _attn_router(x, mem_k, mem_v, wq, wo, lng, lnb, rw_t, rb, *, n_seq, seg):
    batch, length, _ = x.shape
    rows = n_seq * seg
    total = batch * length
    n_l = length // seg
    row2 = lambda v: v.reshape(1, -1)
    return pl.pallas_call(
        functools.partial(_attn_router_kernel, n_seq=n_seq, seg=seg),
        grid=(batch // n_seq, n_l),
        in_specs=[pl.BlockSpec((n_seq, seg, D_MODEL), lambda b, l: (b, l, 0)),
                  pl.BlockSpec((n_seq, N_MEM, D_MODEL), lambda b, l: (b, 0, 0)),
                  pl.BlockSpec((n_seq, N_MEM, D_MODEL), lambda b, l: (b, 0, 0)),
                  _full((D_MODEL, D_MODEL)), _full((D_MODEL, D_MODEL)), _full((1, D_MODEL)), _full((1, D_MODEL)),
                  _full((2 * N_EXPERTS, D_MODEL)), _full((N_EXPERTS, 1))],
        out_specs=[pl.BlockSpec((rows, ROW_W), lambda b, l: (b * n_l + l, 0)),
                   pl.BlockSpec((SUBLANES, rows), lambda b, l: (0, b * n_l + l)),
                   pl.BlockSpec((CLASS_ROWS, LANES), lambda b, l: (0, 0))],
        out_shape=[jax.ShapeDtypeStruct((total, ROW_W), F32),
                   jax.ShapeDtypeStruct((SUBLANES, total), jnp.int32),
                   jax.ShapeDtypeStruct((CLASS_ROWS, LANES), jnp.int32)],
        scratch_shapes=[pltpu.VMEM((rows, D_MODEL), F32),
                        pltpu.VMEM((CLASS_ROWS, LANES), F32)],
        compiler_params=_params(("arbitrary", "arbitrary")),
        name="attn_router",
    )(x, mem_k, mem_v, wq, wo, row2(lng), row2(lnb), rw_t, rb.reshape(N_EXPERTS, 1))


def _dispatch_kernel(cls_ref, rank_ref, offs_ref, cnt_ref, nxt_ref, rows_ref, init_hbm, xs_hbm, tok_ref, sem, *, tile):
    del init_hbm
    i = pl.program_id(0)

    @pl.when(i == 0)
    def _():
        def fill(p, carry_):
            tok_ref[p] = -1
            return carry_
        for c in range(N_CLASSES):
            lax.fori_loop(offs_ref[c] + cnt_ref[c], nxt_ref[c], fill, 0)

    def body(r, carry_):
        t = i * tile + r
        p = offs_ref[cls_ref[t]] + rank_ref[t]
        tok_ref[p] = t
        pltpu.make_async_copy(rows_ref.at[pl.ds(r, 1)], xs_hbm.at[pl.ds(p, 1)], sem).start()
        return carry_
    lax.fori_loop(0, tile, body, 0, unroll=8)
    pltpu.make_async_copy(rows_ref, xs_hbm.at[pl.ds(0, tile)], sem).wait()


def _dispatch(cls, rank, offs, cnt, nxt, rows, n_sorted):
    n_rows = rows.shape[0]
    init = jnp.zeros((n_sorted, ROW_W), F32)
    any_spec = pl.BlockSpec(memory_space=pl.ANY)
    return pl.pallas_call(
        functools.partial(_dispatch_kernel, tile=DISPATCH_TILE),
        grid_spec=pltpu.PrefetchScalarGridSpec(
            num_scalar_prefetch=5, grid=(n_rows // DISPATCH_TILE,),
            in_specs=[pl.BlockSpec((DISPATCH_TILE, ROW_W), lambda i, *_: (i, 0)), any_spec],
            out_specs=[any_spec, pl.BlockSpec(memory_space=pltpu.SMEM)],
            scratch_shapes=[pltpu.SemaphoreType.DMA(())]),
        out_shape=[jax.ShapeDtypeStruct((n_sorted, ROW_W), F32), jax.ShapeDtypeStruct((n_sorted,), jnp.int32)],
        input_output_aliases={6: 0},
        compiler_params=_params(("arbitrary",)),
        name="dispatch",
    )(cls, rank, offs, cnt, nxt, rows, init)


def _experts_kernel(ea_ref, eb_ref, nvalid_ref, tok_ref, xs_ref, wga_ref, wua_ref, wda_ref, wgb_ref, wub_ref, wdb_ref,
                    lng_ref, lnb_ref, out_hbm, ybuf, sems):
    del ea_ref, eb_ref
    i = pl.program_id(0)
    slot = i % 2
    n = nvalid_ref[i]

    def wait_rows(s, count):
        aligned = pl.multiple_of((count // SUBLANES) * SUBLANES, SUBLANES)

        @pl.when(aligned > 0)
        def _():
            pltpu.make_async_copy(ybuf.at[s, pl.ds(0, aligned)], out_hbm.at[pl.ds(0, aligned)], sems.at[s]).wait()

        def one(r, carry_):
            pltpu.make_async_copy(ybuf.at[s, pl.ds(0, 1)], out_hbm.at[pl.ds(0, 1)], sems.at[s]).wait()
            return carry_
        lax.fori_loop(aligned, count, one, 0)

    @pl.when(n > 0)
    def _():
        x = xs_ref[:, 0:D_MODEL]
        xb = x.astype(BF16)
        y = None
        for lane, (wg, wu, wd) in enumerate(((wga_ref, wua_ref, wda_ref), (wgb_ref, wub_ref, wdb_ref))):
            gate = xs_ref[:, D_MODEL + lane:D_MODEL + lane + 1]
            hg = jnp.dot(xb, wg[0], preferred_element_type=F32)
            hu = jnp.dot(xb, wu[0], preferred_element_type=F32)
            hid = hg * jax.nn.sigmoid(hg) * hu * gate
            part = _bdot(hid, wd[0])
            y = part if y is None else y + part
        ybuf[slot] = _layer_norm(ALPHA * x + y, lng_ref[...], lnb_ref[...])

        def start_row(r):
            t = tok_ref[i * MOE_TILE + r]
            pltpu.make_async_copy(ybuf.at[slot, pl.ds(r, 1)], out_hbm.at[pl.ds(t, 1)], sems.at[slot]).start()

        def group(j, carry_):
            for u in range(SUBLANES):
                start_row(j * SUBLANES + u)
            return carry_

        def single(r, carry_):
            start_row(r)
            return carry_
        n_groups = n // SUBLANES
        lax.fori_loop(0, n_groups, group, 0)
        lax.fori_loop(n_groups * SUBLANES, n, single, 0)

    @pl.when(i > 0)
    def _():
        wait_rows(1 - slot, nvalid_ref[jnp.maximum(i - 1, 0)])

    @pl.when(i == pl.num_programs(0) - 1)
    def _():
        wait_rows(slot, n)


def _experts(tile_a, tile_b, tile_nvalid, tok, xs, w_gate, w_up, w_down, lng, lnb, n_rows):
    n_tiles = xs.shape[0] // MOE_TILE
    up_a = pl.BlockSpec((1, D_MODEL, D_EXPERT), lambda i, ea, eb, *_: (ea[i], 0, 0))
    up_b = pl.BlockSpec((1, D_MODEL, D_EXPERT), lambda i, ea, eb, *_: (eb[i], 0, 0))
    dn_a = pl.BlockSpec((1, D_EXPERT, D_MODEL), lambda i, ea, eb, *_: (ea[i], 0, 0))
    dn_b = pl.BlockSpec((1, D_EXPERT, D_MODEL), lambda i, ea, eb, *_: (eb[i], 0, 0))
    vec = pl.BlockSpec((1, D_MODEL), lambda i, *_: (0, 0))
    return pl.pallas_call(
        _experts_kernel,
        grid_spec=pltpu.PrefetchScalarGridSpec(
            num_scalar_prefetch=4, grid=(n_tiles,),
            in_specs=[pl.BlockSpec((MOE_TILE, ROW_W), lambda i, *_: (i, 0)),
                      up_a, up_a, dn_a, up_b, up_b, dn_b, vec, vec],
            out_specs=pl.BlockSpec(memory_space=pl.ANY),
            scratch_shapes=[pltpu.VMEM((2, MOE_TILE, D_MODEL), F32), pltpu.SemaphoreType.DMA((2,))]),
        out_shape=jax.ShapeDtypeStruct((n_rows, D_MODEL), F32),
        compiler_params=_params(("arbitrary",)),
        name="experts",
    )(tile_a, tile_b, tile_nvalid, tok, xs, w_gate, w_up, w_down, w_gate, w_up, w_down,
      lng.reshape(1, -1), lnb.reshape(1, -1))


def _lookup(table, idx):
    pick = idx[:, None] == jnp.arange(table.shape[0], dtype=jnp.int32)[None, :]
    return jnp.sum(jnp.where(pick, table[None, :], 0), axis=1)


def _moe(rows, meta, counts, w_gate, w_up, w_down, lng, lnb):
    n_rows = rows.shape[0]
    n_tiles = n_rows // MOE_TILE + N_CLASSES
    n_sorted = n_tiles * MOE_TILE
    cnt = counts[:N_CLASSES, 0]
    tiles_per = (cnt + MOE_TILE - 1) // MOE_TILE
    cls_id = jnp.arange(N_CLASSES, dtype=jnp.int32)
    tile_end = jnp.sum(jnp.where(cls_id[None, :] <= cls_id[:, None], tiles_per[None, :], 0), axis=1)
    tile_start = tile_end - tiles_per
    offs = tile_start * MOE_TILE
    nxt = jnp.concatenate([offs[1:], jnp.full((1,), n_sorted, jnp.int32)])
    tile_id = jnp.arange(n_tiles, dtype=jnp.int32)
    last_used = tile_end[N_CLASSES - 1] - 1
    tile_cls = jnp.sum((jnp.minimum(tile_id, last_used)[:, None] >= tile_end[None, :]).astype(jnp.int32), axis=1)
    tile_a = _lookup(jnp.asarray(_CLASS_A), tile_cls)
    tile_b = _lookup(jnp.asarray(_CLASS_B), tile_cls)
    tile_nvalid = jnp.clip(_lookup(cnt, tile_cls) - (tile_id - _lookup(tile_start, tile_cls)) * MOE_TILE, 0, MOE_TILE)
    xs, tok = _dispatch(meta[0], meta[1], offs, cnt, nxt, rows, n_sorted)
    return _experts(tile_a, tile_b, tile_nvalid, tok, xs, w_gate, w_up, w_down, lng, lnb, n_rows)


def _trunk(x, mem_k, mem_v, buf_a, buf_b, buf_p, pos0, w, *, mix_seq, mix_seg, att_seq, att_seg, precise):
    batch, length, _ = x.shape
    x, new_a, new_b = _mixer_ab(x, buf_a, buf_b, w["ab_w_in"], w["ab_w_in_lo"], w["ab_conv_a_w"], w["ab_conv_a_b"],
                                w["ab_norm_a_g"], w["ab_norm_a_b"], w["ab_conv_b_w"], w["ab_w_out"], w["ab_w_out_lo"],
                                w["ln_g"][0, 0], w["ln_b"][0, 0], n_seq=mix_seq, seg=mix_seg, precise=precise)
    new_p = v_rows = None
    for layer in range(DEPTH):
        if layer == 1:
            x, new_p, v_rows = _mixer_cd(x, buf_p, w["cd_w_in"], w["cd_pool_w"], w["cd_pool_scale"], w["cd_v_norm_g"],
                                         w["cd_v_norm_b"], w["cd_w_s"], w["cd_b_s"], w["cd_w_out"],
                                         w["ln_g"][1, 0], w["ln_b"][1, 0], n_seq=mix_seq, seg=mix_seg, pos0=pos0)
        rows, meta, counts = _attn_router(x, mem_k[layer], mem_v[layer], w["ca_wq"][layer], w["ca_wo"][layer],
                                          w["ln_g"][layer, 1], w["ln_b"][layer, 1], w["router_w_t"], w["router_b"],
                                          n_seq=att_seq, seg=att_seg)
        x = _moe(rows, meta, counts, w["moe_w_gate"][layer], w["moe_w_up"][layer], w["moe_w_down"][layer],
                 w["ln_g"][layer, 2], w["ln_b"][layer, 2]).reshape(batch, length, D_MODEL)
    return x, new_a, new_b, new_p, v_rows


def kernel(x_prompt, x_sample, mem_prompt, cache_mem_k, cache_mem_v, state_conv_a, state_conv_b, state_pool,
           ln_g, ln_b, ab_w_in, ab_conv_a_w, ab_conv_a_b, ab_norm_a_g, ab_norm_a_b, ab_conv_b_w, ab_w_out,
           cd_w_in, cd_pool_w, cd_pool_scale, cd_v_norm_g, cd_v_norm_b, cd_w_s, cd_b_s, cd_w_out,
           ca_wq, ca_wk, ca_wv, ca_wo, router_w, router_b, moe_w_gate, moe_w_up, moe_w_down):
    assert ln_g.shape[0] == DEPTH and ab_w_in.shape[0] == 1 and cd_w_in.shape[0] == 1
    bsz, seq, _ = x_prompt.shape
    dec_b, dec_seq, _ = x_sample.shape
    n_pool = len(POOL_WINDOWS)
    pool_c = D_HALF // n_pool
    pool_bd = jnp.zeros((D_HALF, D_HALF), F32)
    for g in range(n_pool):
        pool_bd = pool_bd.at[g * pool_c:(g + 1) * pool_c, g * pool_c:(g + 1) * pool_c].set(cd_pool_w[0, g])
    ab_w_in_hi, ab_w_in_lo = _split_weight(ab_w_in[0])
    ab_w_out_hi, ab_w_out_lo = _split_weight(ab_w_out[0])
    rw_hi, rw_lo = _split_weight(router_w.T)
    w = {
        "ln_g": ln_g, "ln_b": ln_b,
        "ab_w_in": ab_w_in_hi, "ab_w_in_lo": ab_w_in_lo, "ab_conv_a_w": ab_conv_a_w[0], "ab_conv_a_b": ab_conv_a_b[0],
        "ab_norm_a_g": ab_norm_a_g[0], "ab_norm_a_b": ab_norm_a_b[0], "ab_conv_b_w": ab_conv_b_w[0],
        "ab_w_out": ab_w_out_hi, "ab_w_out_lo": ab_w_out_lo,
        "cd_w_in": cd_w_in[0].astype(BF16), "cd_pool_w": pool_bd.astype(BF16), "cd_pool_scale": cd_pool_scale[0],
        "cd_v_norm_g": cd_v_norm_g[0], "cd_v_norm_b": cd_v_norm_b[0], "cd_w_s": cd_w_s[0], "cd_b_s": cd_b_s[0],
        "cd_w_out": cd_w_out[0].astype(BF16),
        "ca_wq": ca_wq.astype(BF16), "ca_wo": ca_wo.astype(BF16),
        "router_w_t": jnp.concatenate([rw_hi, rw_lo], axis=0), "router_b": router_b,
        "moe_w_gate": moe_w_gate.astype(BF16), "moe_w_up": moe_w_up.astype(BF16),
        "moe_w_down": moe_w_down.astype(BF16),
    }

    kv = _mem_projection(mem_prompt.reshape(bsz * N_MEM, D_MODEL),
                         jnp.concatenate([ca_wk, ca_wv], axis=0).astype(BF16))
    kv = kv.reshape(2, DEPTH, bsz, N_MEM, D_MODEL)
    mem_k_prompt, mem_v_prompt = kv[0], kv[1]
    y_prompt, conv_a_p, conv_b_p, pool_p, _ = _trunk(
        x_prompt, mem_k_prompt, mem_v_prompt,
        jnp.zeros((bsz, CONV_A - 1, D_HALF), F32), jnp.zeros((bsz, CONV_B - 1, D_HALF), F32),
        jnp.zeros((bsz, POOL_BUF, D_HALF), F32), 0, w,
        mix_seq=1, mix_seg=512, att_seq=1, att_seg=512, precise=True)

    y_sample, conv_a_s, conv_b_s, pool_s, v_s = _trunk(
        x_sample, cache_mem_k.reshape(DEPTH, dec_b, N_MEM, D_MODEL), cache_mem_v.reshape(DEPTH, dec_b, N_MEM, D_MODEL),
        state_conv_a[0], state_conv_b[0], state_pool[0], PAST_LEN, w,
        mix_seq=8, mix_seg=dec_seq, att_seq=4, att_seg=dec_seq, precise=False)

    kv_shape = (DEPTH, bsz, N_MEM, MEM_HEADS, MEM_HEAD_DIM)
    return (y_prompt, y_sample, mem_k_prompt.reshape(kv_shape), mem_v_prompt.reshape(kv_shape),
            conv_a_p[None], conv_b_p[None], pool_p[None], conv_a_s[None], conv_b_s[None], pool_s[None], v_s[None])
```

```python
import functools

import jax
import jax.numpy as jnp
import numpy as np
from jax import lax
from jax.experimental import pallas as pl
from jax.experimental.pallas import tpu as pltpu

F32 = jnp.float32
BF16 = jnp.bfloat16

D_MODEL = 1024
D_HALF = D_MODEL // 2
DEPTH = 2
PAST_LEN = 4096
CHUNK = 64
CHUNK_SHIFT = CHUNK.bit_length() - 1
CONV_A = 31
CONV_B = 3
POOL_WINDOWS = (2, 4, 8, 16)
POOL_BUF = max(POOL_WINDOWS) - 1
N_HEAD_D = 4
GMLP_CHUNK = 128
N_MEM = 256
MEM_HEADS = 4
MEM_HEAD_DIM = D_MODEL // MEM_HEADS
N_EXPERTS = 16
N_EXPERT_GROUPS = 4
GROUP_SIZE = N_EXPERTS // N_EXPERT_GROUPS
GROUP_SHIFT = GROUP_SIZE.bit_length() - 1
PAIRS_PER_GROUP = GROUP_SIZE * (GROUP_SIZE - 1) // 2
N_CLASSES = N_EXPERT_GROUPS * PAIRS_PER_GROUP
D_EXPERT = D_MODEL // 2
ALPHA = (2 * DEPTH) ** 0.25
LN_EPS = 1e-5

LANES = 128
SUBLANES = 8
ROW_W = D_MODEL + LANES
HIST_A = 32
HIST_B = 8
HIST_P = 16
CONV_ROWS = 64
MOE_TILE = 256
CLASS_ROWS = 32
DISPATCH_TILE = 1024
SPLIT_ROWS = 256
STATE_TILES = 1
VMEM_LIMIT = 52 * 1024 * 1024

_PAIR_AB = [(a, b) for a in range(GROUP_SIZE) for b in range(a + 1, GROUP_SIZE)]
_CLASS_A = np.array([g * GROUP_SIZE + a for g in range(N_EXPERT_GROUPS) for a, _ in _PAIR_AB], np.int32)
_CLASS_B = np.array([g * GROUP_SIZE + b for g in range(N_EXPERT_GROUPS) for _, b in _PAIR_AB], np.int32)


def _layer_norm(x, g, b):
    mu = jnp.mean(x, axis=-1, keepdims=True)
    xc = x - mu
    var = jnp.mean(xc * xc, axis=-1, keepdims=True)
    return xc * lax.rsqrt(var + LN_EPS) * g + b


def _bdot(a, w):
    return jnp.dot(a.astype(BF16), w, preferred_element_type=F32)


def _split(a):
    hi = a.astype(BF16)
    return hi, (a - hi.astype(F32)).astype(BF16)


def _dot3(a, w_hi, w_lo):
    a_hi, a_lo = _split(a)
    return (jnp.dot(a_hi, w_hi, preferred_element_type=F32) + jnp.dot(a_lo, w_hi, preferred_element_type=F32)
            + jnp.dot(a_hi, w_lo, preferred_element_type=F32))


def _params(sem):
    return pltpu.CompilerParams(dimension_semantics=sem, vmem_limit_bytes=VMEM_LIMIT)


def _full(shape):
    return pl.BlockSpec(shape, lambda *_: (0,) * len(shape), pipeline_mode=pl.Buffered(1))


def _split_weight_kernel(w_ref, hi_ref, lo_ref):
    hi, lo = _split(w_ref[...])
    hi_ref[...] = hi
    lo_ref[...] = lo


def _split_weight(w):
    rows, cols = w.shape
    blk = min(rows, SPLIT_ROWS)
    spec = pl.BlockSpec((blk, cols), lambda i: (i, 0))
    return pl.pallas_call(
        _split_weight_kernel,
        grid=(rows // blk,),
        in_specs=[spec],
        out_specs=[spec, spec],
        out_shape=[jax.ShapeDtypeStruct(w.shape, BF16)] * 2,
        compiler_params=_params(("arbitrary",)),
        name="split_weight",
    )(w)


def _proj_kernel(x_ref, w_ref, o_ref):
    o_ref[0] = _bdot(x_ref[...], w_ref[0])


def _mem_projection(mem, w):
    n, rows = w.shape[0], mem.shape[0]
    return pl.pallas_call(
        _proj_kernel,
        grid=(n,),
        in_specs=[pl.BlockSpec((rows, D_MODEL), lambda j: (0, 0)),
                  pl.BlockSpec((1, D_MODEL, D_MODEL), lambda j: (j, 0, 0))],
        out_specs=pl.BlockSpec((1, rows, D_MODEL), lambda j: (j, 0, 0)),
        out_shape=jax.ShapeDtypeStruct((n, rows, D_MODEL), F32),
        compiler_params=_params(("arbitrary",)),
        name="mem_projection",
    )(mem, w)


def _load_history(ext_ref, buf_ref, first, hist, keep, seg):
    @pl.when(first)
    def _():
        ext_ref[:, hist - keep:hist, :] = buf_ref[...]

    @pl.when(jnp.logical_not(first))
    def _():
        ext_ref[:, hist - keep:hist, :] = ext_ref[:, seg + hist - keep:seg + hist, :]


def _depthwise_conv(ext_ref, w_ref, out_ref, *, n_seq, seg, taps, hist):
    rc = min(CONV_ROWS, seg)
    off0 = hist - (taps - 1)
    for s in range(n_seq):
        for r0 in range(0, seg, rc):
            for lb in range(0, D_HALF, LANES):
                acc = None
                for k in range(taps):
                    term = w_ref[k:k + 1, lb:lb + LANES] * ext_ref[s, off0 + k + r0:off0 + k + r0 + rc, lb:lb + LANES]
                    acc = term if acc is None else acc + term
                out_ref[s * seg + r0:s * seg + r0 + rc, lb:lb + LANES] = acc


def _mixer_ab_kernel(x_ref, bufa_ref, bufb_ref, w_in_ref, w_in_lo_ref, caw_ref, cab_ref, nag_ref, nab_ref, cbw_ref,
                     w_out_ref, w_out_lo_ref, lng_ref, lnb_ref,
                     y_ref, nbufa_ref, nbufb_ref,
                     h_scr, y_scr, a_ext, cb_ext, conv_a, conv_b, *, n_seq, seg, precise_tiles):
    rows = n_seq * seg
    first = pl.program_id(1) == 0
    x = x_ref[...].reshape(rows, D_MODEL)

    def project(compute3, compute1, out_ref):
        if precise_tiles == 0:
            out_ref[...] = compute1()
            return
        is_precise = pl.program_id(1) >= pl.num_programs(1) - precise_tiles

        @pl.when(is_precise)
        def _():
            out_ref[...] = compute3()

        @pl.when(jnp.logical_not(is_precise))
        def _():
            out_ref[...] = compute1()

    project(lambda: _dot3(x, w_in_ref[...], w_in_lo_ref[...]), lambda: _bdot(x, w_in_ref[...]), h_scr)
    a = h_scr[:, 0:D_HALF] * jax.nn.sigmoid(h_scr[:, D_HALF:2 * D_HALF])
    cb = h_scr[:, 3 * D_HALF:4 * D_HALF] * h_scr[:, 4 * D_HALF:5 * D_HALF]

    _load_history(a_ext, bufa_ref, first, HIST_A, CONV_A - 1, seg)
    _load_history(cb_ext, bufb_ref, first, HIST_B, CONV_B - 1, seg)
    a_ext[:, HIST_A:HIST_A + seg, :] = a.reshape(n_seq, seg, D_HALF)
    cb_ext[:, HIST_B:HIST_B + seg, :] = cb.reshape(n_seq, seg, D_HALF)
    nbufa_ref[...] = a_ext[:, seg + HIST_A - (CONV_A - 1):seg + HIST_A, :]
    nbufb_ref[...] = cb_ext[:, seg + HIST_B - (CONV_B - 1):seg + HIST_B, :]

    _depthwise_conv(a_ext, caw_ref, conv_a, n_seq=n_seq, seg=seg, taps=CONV_A, hist=HIST_A)
    _depthwise_conv(cb_ext, cbw_ref, conv_b, n_seq=n_seq, seg=seg, taps=CONV_B, hist=HIST_B)

    a2 = _layer_norm(conv_a[...] + cab_ref[...], nag_ref[...], nab_ref[...])
    a2 = a2 * jax.nn.sigmoid(a2)
    b2 = h_scr[:, 2 * D_HALF:3 * D_HALF] * conv_b[...]
    project(lambda: (_dot3(a2, w_out_ref[0:D_HALF, :], w_out_lo_ref[0:D_HALF, :])
                     + _dot3(b2, w_out_ref[D_HALF:D_MODEL, :], w_out_lo_ref[D_HALF:D_MODEL, :])),
            lambda: _bdot(a2, w_out_ref[0:D_HALF, :]) + _bdot(b2, w_out_ref[D_HALF:D_MODEL, :]), y_scr)
    y_ref[...] = _layer_norm(ALPHA * x + y_scr[...], lng_ref[...], lnb_ref[...]).reshape(n_seq, seg, D_MODEL)


def _mixer_ab(x, buf_a, buf_b, w_in, w_in_lo, caw, cab, nag, nab, cbw, w_out, w_out_lo, lng, lnb, *,
              n_seq, seg, precise_tiles):
    batch, length, _ = x.shape
    d_in = w_in.shape[1]
    grid = (batch // n_seq, length // seg)
    rows = n_seq * seg
    row2 = lambda v: v.reshape(1, -1)
    lo_in = _full((D_MODEL, d_in)) if precise_tiles else _full((SUBLANES, LANES))
    lo_out = _full((D_MODEL, D_MODEL)) if precise_tiles else _full((SUBLANES, LANES))
    return pl.pallas_call(
        functools.partial(_mixer_ab_kernel, n_seq=n_seq, seg=seg, precise_tiles=precise_tiles),
        grid=grid,
        in_specs=[pl.BlockSpec((n_seq, seg, D_MODEL), lambda b, l: (b, l, 0)),
                  pl.BlockSpec((n_seq, CONV_A - 1, D_HALF), lambda b, l: (b, 0, 0)),
                  pl.BlockSpec((n_seq, CONV_B - 1, D_HALF), lambda b, l: (b, 0, 0)),
                  _full((D_MODEL, d_in)), lo_in, _full((CONV_A, D_HALF)), _full((1, D_HALF)), _full((1, D_HALF)),
                  _full((1, D_HALF)), _full((CONV_B, D_HALF)), _full((D_MODEL, D_MODEL)), lo_out,
                  _full((1, D_MODEL)), _full((1, D_MODEL))],
        out_specs=[pl.BlockSpec((n_seq, seg, D_MODEL), lambda b, l: (b, l, 0)),
                   pl.BlockSpec((n_seq, CONV_A - 1, D_HALF), lambda b, l: (b, 0, 0)),
                   pl.BlockSpec((n_seq, CONV_B - 1, D_HALF), lambda b, l: (b, 0, 0))],
        out_shape=[jax.ShapeDtypeStruct((batch, length, D_MODEL), F32),
                   jax.ShapeDtypeStruct((batch, CONV_A - 1, D_HALF), F32),
                   jax.ShapeDtypeStruct((batch, CONV_B - 1, D_HALF), F32)],
        scratch_shapes=[pltpu.VMEM((rows, d_in), F32),
                        pltpu.VMEM((rows, D_MODEL), F32),
                        pltpu.VMEM((n_seq, HIST_A + seg, D_HALF), F32),
                        pltpu.VMEM((n_seq, HIST_B + seg, D_HALF), F32),
                        pltpu.VMEM((rows, D_HALF), F32),
                        pltpu.VMEM((rows, D_HALF), F32)],
        compiler_params=_params(("arbitrary", "arbitrary")),
        name="mixer_ab",
    )(x, buf_a, buf_b, w_in, w_in_lo, caw, row2(cab), row2(nag), row2(nab), cbw, w_out, w_out_lo,
      row2(lng), row2(lnb))


def _mixer_cd_kernel(x_ref, bufp_ref, w_in_ref, pw_ref, ps_ref, vg_ref, vb_ref, ws_ref, bs_ref, w_out_ref,
                     lng_ref, lnb_ref,
                     y_ref, nbufp_ref, v_ref,
                     c_ext, pooled, mixed, *, n_seq, seg, pos0, n_mix):
    rows = n_seq * seg
    li = pl.program_id(1)
    x = x_ref[...].reshape(rows, D_MODEL)
    h = _bdot(x, w_in_ref[...])
    c_in = h[:, 0:D_HALF]

    _load_history(c_ext, bufp_ref, li == 0, HIST_P, POOL_BUF, seg)
    c_ext[:, HIST_P:HIST_P + seg, :] = c_in.reshape(n_seq, seg, D_HALF)
    nbufp_ref[...] = c_ext[:, seg + HIST_P - POOL_BUF:seg + HIST_P, :]

    rc = min(CONV_ROWS, seg)
    for g, win in enumerate(POOL_WINDOWS):
        lanes = slice(g * LANES, (g + 1) * LANES)
        for s in range(n_seq):
            for r0 in range(0, seg, rc):
                cur = c_ext[s, HIST_P + r0:HIST_P + r0 + rc, lanes]
                acc = cur
                for j in range(1, win):
                    acc = acc + c_ext[s, HIST_P + r0 - j:HIST_P + r0 - j + rc, lanes]
                pos = pos0 + li * seg + r0 + lax.broadcasted_iota(jnp.int32, (rc, LANES), 0)
                cnt = jnp.minimum(pos + 1, win).astype(F32)
                pooled[s * seg + r0:s * seg + r0 + rc, lanes] = acc / cnt - cur
    c = _bdot(pooled[...], pw_ref[...]) * ps_ref[...]

    z = jax.nn.gelu(h[:, D_HALF:3 * D_HALF], approximate=True)
    u = z[:, 0:D_HALF]
    v = _layer_norm(z[:, D_HALF:2 * D_HALF], vg_ref[...], vb_ref[...])
    v_ref[...] = v.reshape(n_seq, seg, D_HALF)
    vb16 = v.astype(BF16)

    ri = lax.broadcasted_iota(jnp.int32, (n_mix, n_mix), 0) >> CHUNK_SHIFT
    ci = lax.broadcasted_iota(jnp.int32, (n_mix, n_mix), 1) >> CHUNK_SHIFT
    for g in range(N_HEAD_D):
        lanes = slice(g * LANES, (g + 1) * LANES)
        ws = jnp.where(ci <= ri, ws_ref[g], 0.0).astype(BF16)
        bias = bs_ref[:, g:g + 1]
        for r0 in range(0, rows, n_mix):
            mixed[r0:r0 + n_mix, lanes] = jnp.dot(ws, vb16[r0:r0 + n_mix, lanes], preferred_element_type=F32) + bias
    d = u * mixed[...]
    y = _bdot(c, w_out_ref[0:D_HALF, :]) + _bdot(d, w_out_ref[D_HALF:D_MODEL, :])
    y_ref[...] = _layer_norm(ALPHA * x + y, lng_ref[...], lnb_ref[...]).reshape(n_seq, seg, D_MODEL)


def _mixer_cd(x, buf_p, w_in, pw, ps, vg, vb, ws, bs, w_out, lng, lnb, *, n_seq, seg, pos0):
    batch, length, _ = x.shape
    d_in = w_in.shape[1]
    n_mix = min(length, GMLP_CHUNK)
    assert seg % n_mix == 0
    rows = n_seq * seg
    row2 = lambda v: v.reshape(1, -1)
    ws_n = ws[:, :n_mix, :n_mix]
    bs_t = bs[:, :n_mix].T
    return pl.pallas_call(
        functools.partial(_mixer_cd_kernel, n_seq=n_seq, seg=seg, pos0=pos0, n_mix=n_mix),
        grid=(batch // n_seq, length // seg),
        in_specs=[pl.BlockSpec((n_seq, seg, D_MODEL), lambda b, l: (b, l, 0)),
                  pl.BlockSpec((n_seq, POOL_BUF, D_HALF), lambda b, l: (b, 0, 0)),
                  _full((D_MODEL, d_in)), _full((D_HALF, D_HALF)), _full((1, D_HALF)), _full((1, D_HALF)),
                  _full((1, D_HALF)), _full((N_HEAD_D, n_mix, n_mix)), _full((n_mix, N_HEAD_D)),
                  _full((D_MODEL, D_MODEL)), _full((1, D_MODEL)), _full((1, D_MODEL))],
        out_specs=[pl.BlockSpec((n_seq, seg, D_MODEL), lambda b, l: (b, l, 0)),
                   pl.BlockSpec((n_seq, POOL_BUF, D_HALF), lambda b, l: (b, 0, 0)),
                   pl.BlockSpec((n_seq, seg, D_HALF), lambda b, l: (b, l, 0))],
        out_shape=[jax.ShapeDtypeStruct((batch, length, D_MODEL), F32),
                   jax.ShapeDtypeStruct((batch, POOL_BUF, D_HALF), F32),
                   jax.ShapeDtypeStruct((batch, length, D_HALF), F32)],
        scratch_shapes=[pltpu.VMEM((n_seq, HIST_P + seg, D_HALF), F32),
                        pltpu.VMEM((rows, D_HALF), F32),
                        pltpu.VMEM((rows, D_HALF), F32)],
        compiler_params=_params(("arbitrary", "arbitrary")),
        name="mixer_cd",
    )(x, buf_p, w_in, pw, row2(ps), row2(vg), row2(vb), ws_n, bs_t, w_out, row2(lng), row2(lnb))


def _attn_router_kernel(x_ref, k_ref, v_ref, wq_ref, wo_ref, lng_ref, lnb_ref, rw_ref, rb_ref,
                        rows_ref, meta_ref, counts_ref,
                        o_scr, carry, *, n_seq, seg):
    rows = n_seq * seg
    x = x_ref[...].reshape(rows, D_MODEL)
    q = (_bdot(x, wq_ref[...]) * (MEM_HEAD_DIM ** -0.5)).astype(BF16)
    for s in range(n_seq):
        for hd in range(MEM_HEADS):
            cols = slice(hd * MEM_HEAD_DIM, (hd + 1) * MEM_HEAD_DIM)
            qh = q[s * seg:(s + 1) * seg, cols]
            kh = k_ref[s, :, cols].astype(BF16)
            sc = lax.dot_general(qh, kh, (((1,), (1,)), ((), ())), preferred_element_type=F32)
            p = jnp.exp(sc - jnp.max(sc, axis=-1, keepdims=True))
            den = jnp.sum(p, axis=-1, keepdims=True)
            o_scr[s * seg:(s + 1) * seg, cols] = _bdot(p, v_ref[s, :, cols].astype(BF16)) / den
    x2 = _layer_norm(ALPHA * x + _bdot(o_scr[...], wo_ref[...]), lng_ref[...], lnb_ref[...])
    rows_ref[:, 0:D_MODEL] = x2

    nt = (((1,), (1,)), ((), ()))
    x2_hi, x2_lo = _split(x2)
    both = lax.dot_general(rw_ref[...], x2_hi, nt, preferred_element_type=F32)
    cross = lax.dot_general(rw_ref[0:N_EXPERTS, :], x2_lo, nt, preferred_element_type=F32)
    logits = both[0:N_EXPERTS] + both[N_EXPERTS:2 * N_EXPERTS] + cross + rb_ref[...]
    e = jnp.exp(logits - jnp.max(logits, axis=0, keepdims=True))
    scores = e / jnp.sum(e, axis=0, keepdims=True)
    eid = lax.broadcasted_iota(jnp.int32, (N_EXPERTS, rows), 0)
    egrp = eid >> GROUP_SHIFT
    best = jnp.max(jnp.where(egrp == 0, scores, -1.0), axis=0, keepdims=True)
    g_sel = jnp.zeros((1, rows), jnp.int32)
    for g in range(1, N_EXPERT_GROUPS):
        gs = jnp.max(jnp.where(egrp == g, scores, -1.0), axis=0, keepdims=True)
        upd = gs > best
        g_sel = jnp.where(upd, g, g_sel)
        best = jnp.where(upd, gs, best)
    masked = jnp.where(egrp == g_sel, scores, -1.0)
    m1 = jnp.max(masked, axis=0, keepdims=True)
    i1 = jnp.min(jnp.where(masked == m1, eid, N_EXPERTS), axis=0, keepdims=True)
    masked2 = jnp.where(eid == i1, -2.0, masked)
    m2 = jnp.max(masked2, axis=0, keepdims=True)
    i2 = jnp.min(jnp.where(masked2 == m2, eid, N_EXPERTS), axis=0, keepdims=True)
    tot = m1 + m2
    g1 = m1 / tot
    g2 = m2 / tot
    first_low = i1 < i2
    ea = jnp.where(first_low, i1, i2) & (GROUP_SIZE - 1)
    eb = jnp.where(first_low, i2, i1) & (GROUP_SIZE - 1)
    gate_a = jnp.where(first_low, g1, g2)
    gate_b = jnp.where(first_low, g2, g1)
    pair = eb - 1 + jnp.where(ea == 1, 2, 0) + jnp.where(ea == 2, 3, 0)
    cls = g_sel * PAIRS_PER_GROUP + pair

    lane_row = lax.broadcasted_iota(jnp.int32, (LANES, rows), 0)
    gates_t = jnp.where(lane_row == 0, gate_a, jnp.where(lane_row == 1, gate_b, 0.0))
    rows_ref[:, D_MODEL:ROW_W] = gates_t.T

    @pl.when((pl.program_id(0) == 0) & (pl.program_id(1) == 0))
    def _():
        carry[...] = jnp.zeros_like(carry)

    onehot = (lax.broadcasted_iota(jnp.int32, (CLASS_ROWS, rows), 0) == cls).astype(F32)
    tri = (lax.broadcasted_iota(jnp.int32, (rows, rows), 0) <= lax.broadcasted_iota(jnp.int32, (rows, rows), 1))
    cum = jnp.dot(onehot.astype(BF16), tri.astype(F32).astype(BF16), preferred_element_type=F32)
    before = carry[:, 0:1]
    rank = jnp.sum(onehot * (before + cum), axis=0, keepdims=True) - 1.0
    sub = lax.broadcasted_iota(jnp.int32, (SUBLANES, rows), 0)
    meta_ref[...] = jnp.where(sub == 0, cls, jnp.where(sub == 1, rank.astype(jnp.int32), 0))
    carry[...] = carry[...] + cum[:, rows - 1:rows]
    counts_ref[...] = carry[...].astype(jnp.int32)


def _attn_router(x, mem_k, mem_v, wq, wo, lng, lnb, rw_t, rb, *, n_seq, seg):
    batch, length, _ = x.shape
    rows = n_seq * seg
    total = batch * length
    n_l = length // seg
    row2 = lambda v: v.reshape(1, -1)
    return pl.pallas_call(
        functools.partial(_attn_router_kernel, n_seq=n_seq, seg=seg),
        grid=(batch // n_seq, n_l),
        in_specs=[pl.BlockSpec((n_seq, seg, D_MODEL), lambda b, l: (b, l, 0)),
                  pl.BlockSpec((n_seq, N_MEM, D_MODEL), lambda b, l: (b, 0, 0)),
                  pl.BlockSpec((n_seq, N_MEM, D_MODEL), lambda b, l: (b, 0, 0)),
                  _full((D_MODEL, D_MODEL)), _full((D_MODEL, D_MODEL)), _full((1, D_MODEL)), _full((1, D_MODEL)),
                  _full((2 * N_EXPERTS, D_MODEL)), _full((N_EXPERTS, 1))],
        out_specs=[pl.BlockSpec((rows, ROW_W), lambda b, l: (b * n_l + l, 0)),
                   pl.BlockSpec((SUBLANES, rows), lambda b, l: (0, b * n_l + l)),
                   pl.BlockSpec((CLASS_ROWS, LANES), lambda b, l: (0, 0))],
        out_shape=[jax.ShapeDtypeStruct((total, ROW_W), F32),
                   jax.ShapeDtypeStruct((SUBLANES, total), jnp.int32),
                   jax.ShapeDtypeStruct((CLASS_ROWS, LANES), jnp.int32)],
        scratch_shapes=[pltpu.VMEM((rows, D_MODEL), F32),
                        pltpu.VMEM((CLASS_ROWS, LANES), F32)],
        compiler_params=_params(("arbitrary", "arbitrary")),
        name="attn_router",
    )(x, mem_k, mem_v, wq, wo, row2(lng), row2(lnb), rw_t, rb.reshape(N_EXPERTS, 1))


def _dispatch_kernel(cls_ref, rank_ref, offs_ref, cnt_ref, nxt_ref, rows_ref, init_hbm, xs_hbm, tok_ref, sem, *, tile):
    del init_hbm
    i = pl.program_id(0)

    @pl.when(i == 0)
    def _():
        def fill(p, carry_):
            tok_ref[p] = -1
            return carry_
        for c in range(N_CLASSES):
            lax.fori_loop(offs_ref[c] + cnt_ref[c], nxt_ref[c], fill, 0)

    def body(r, carry_):
        t = i * tile + r
        p = offs_ref[cls_ref[t]] + rank_ref[t]
        tok_ref[p] = t
        pltpu.make_async_copy(rows_ref.at[pl.ds(r, 1)], xs_hbm.at[pl.ds(p, 1)], sem).start()
        return carry_
    lax.fori_loop(0, tile, body, 0, unroll=8)
    pltpu.make_async_copy(rows_ref, xs_hbm.at[pl.ds(0, tile)], sem).wait()


def _dispatch(cls, rank, offs, cnt, nxt, rows, n_sorted):
    n_rows = rows.shape[0]
    init = jnp.zeros((n_sorted, ROW_W), F32)
    any_spec = pl.BlockSpec(memory_space=pl.ANY)
    return pl.pallas_call(
        functools.partial(_dispatch_kernel, tile=DISPATCH_TILE),
        grid_spec=pltpu.PrefetchScalarGridSpec(
            num_scalar_prefetch=5, grid=(n_rows // DISPATCH_TILE,),
            in_specs=[pl.BlockSpec((DISPATCH_TILE, ROW_W), lambda i, *_: (i, 0)), any_spec],
            out_specs=[any_spec, pl.BlockSpec(memory_space=pltpu.SMEM)],
            scratch_shapes=[pltpu.SemaphoreType.DMA(())]),
        out_shape=[jax.ShapeDtypeStruct((n_sorted, ROW_W), F32), jax.ShapeDtypeStruct((n_sorted,), jnp.int32)],
        input_output_aliases={6: 0},
        compiler_params=_params(("arbitrary",)),
        name="dispatch",
    )(cls, rank, offs, cnt, nxt, rows, init)


def _experts_kernel(ea_ref, eb_ref, nvalid_ref, tok_ref, xs_ref, wga_ref, wua_ref, wda_ref, wgb_ref, wub_ref, wdb_ref,
                    lng_ref, lnb_ref, out_hbm, ybuf, sems):
    del ea_ref, eb_ref
    i = pl.program_id(0)
    slot = i % 2
    n = nvalid_ref[i]

    def wait_rows(s, count):
        aligned = pl.multiple_of((count // SUBLANES) * SUBLANES, SUBLANES)

        @pl.when(aligned > 0)
        def _():
            pltpu.make_async_copy(ybuf.at[s, pl.ds(0, aligned)], out_hbm.at[pl.ds(0, aligned)], sems.at[s]).wait()

        def one(r, carry_):
            pltpu.make_async_copy(ybuf.at[s, pl.ds(0, 1)], out_hbm.at[pl.ds(0, 1)], sems.at[s]).wait()
            return carry_
        lax.fori_loop(aligned, count, one, 0)

    @pl.when(n > 0)
    def _():
        x = xs_ref[:, 0:D_MODEL]
        xb = x.astype(BF16)
        y = None
        for lane, (wg, wu, wd) in enumerate(((wga_ref, wua_ref, wda_ref), (wgb_ref, wub_ref, wdb_ref))):
            gate = xs_ref[:, D_MODEL + lane:D_MODEL + lane + 1]
            hg = jnp.dot(xb, wg[0], preferred_element_type=F32)
            hu = jnp.dot(xb, wu[0], preferred_element_type=F32)
            hid = hg * jax.nn.sigmoid(hg) * hu * gate
            part = _bdot(hid, wd[0])
            y = part if y is None else y + part
        ybuf[slot] = _layer_norm(ALPHA * x + y, lng_ref[...], lnb_ref[...])

        def start_row(r):
            t = tok_ref[i * MOE_TILE + r]
            pltpu.make_async_copy(ybuf.at[slot, pl.ds(r, 1)], out_hbm.at[pl.ds(t, 1)], sems.at[slot]).start()

        def group(j, carry_):
            for u in range(SUBLANES):
                start_row(j * SUBLANES + u)
            return carry_

        def single(r, carry_):
            start_row(r)
            return carry_
        n_groups = n // SUBLANES
        lax.fori_loop(0, n_groups, group, 0)
        lax.fori_loop(n_groups * SUBLANES, n, single, 0)

    @pl.when(i > 0)
    def _():
        wait_rows(1 - slot, nvalid_ref[jnp.maximum(i - 1, 0)])

    @pl.when(i == pl.num_programs(0) - 1)
    def _():
        wait_rows(slot, n)


def _experts(tile_a, tile_b, tile_nvalid, tok, xs, w_gate, w_up, w_down, lng, lnb, n_rows):
    n_tiles = xs.shape[0] // MOE_TILE
    up_a = pl.BlockSpec((1, D_MODEL, D_EXPERT), lambda i, ea, eb, *_: (ea[i], 0, 0))
    up_b = pl.BlockSpec((1, D_MODEL, D_EXPERT), lambda i, ea, eb, *_: (eb[i], 0, 0))
    dn_a = pl.BlockSpec((1, D_EXPERT, D_MODEL), lambda i, ea, eb, *_: (ea[i], 0, 0))
    dn_b = pl.BlockSpec((1, D_EXPERT, D_MODEL), lambda i, ea, eb, *_: (eb[i], 0, 0))
    vec = pl.BlockSpec((1, D_MODEL), lambda i, *_: (0, 0))
    return pl.pallas_call(
        _experts_kernel,
        grid_spec=pltpu.PrefetchScalarGridSpec(
            num_scalar_prefetch=4, grid=(n_tiles,),
            in_specs=[pl.BlockSpec((MOE_TILE, ROW_W), lambda i, *_: (i, 0)),
                      up_a, up_a, dn_a, up_b, up_b, dn_b, vec, vec],
            out_specs=pl.BlockSpec(memory_space=pl.ANY),
            scratch_shapes=[pltpu.VMEM((2, MOE_TILE, D_MODEL), F32), pltpu.SemaphoreType.DMA((2,))]),
        out_shape=jax.ShapeDtypeStruct((n_rows, D_MODEL), F32),
        compiler_params=_params(("arbitrary",)),
        name="experts",
    )(tile_a, tile_b, tile_nvalid, tok, xs, w_gate, w_up, w_down, w_gate, w_up, w_down,
      lng.reshape(1, -1), lnb.reshape(1, -1))


def _lookup(table, idx):
    pick = idx[:, None] == jnp.arange(table.shape[0], dtype=jnp.int32)[None, :]
    return jnp.sum(jnp.where(pick, table[None, :], 0), axis=1)


def _moe(rows, meta, counts, w_gate, w_up, w_down, lng, lnb):
    n_rows = rows.shape[0]
    n_tiles = n_rows // MOE_TILE + N_CLASSES
    n_sorted = n_tiles * MOE_TILE
    cnt = counts[:N_CLASSES, 0]
    tiles_per = (cnt + MOE_TILE - 1) // MOE_TILE
    cls_id = jnp.arange(N_CLASSES, dtype=jnp.int32)
    tile_end = jnp.sum(jnp.where(cls_id[None, :] <= cls_id[:, None], tiles_per[None, :], 0), axis=1)
    tile_start = tile_end - tiles_per
    offs = tile_start * MOE_TILE
    nxt = jnp.concatenate([offs[1:], jnp.full((1,), n_sorted, jnp.int32)])
    tile_id = jnp.arange(n_tiles, dtype=jnp.int32)
    last_used = tile_end[N_CLASSES - 1] - 1
    tile_cls = jnp.sum((jnp.minimum(tile_id, last_used)[:, None] >= tile_end[None, :]).astype(jnp.int32), axis=1)
    tile_a = _lookup(jnp.asarray(_CLASS_A), tile_cls)
    tile_b = _lookup(jnp.asarray(_CLASS_B), tile_cls)
    tile_nvalid = jnp.clip(_lookup(cnt, tile_cls) - (tile_id - _lookup(tile_start, tile_cls)) * MOE_TILE, 0, MOE_TILE)
    xs, tok = _dispatch(meta[0], meta[1], offs, cnt, nxt, rows, n_sorted)
    return _experts(tile_a, tile_b, tile_nvalid, tok, xs, w_gate, w_up, w_down, lng, lnb, n_rows)


def _trunk(x, mem_k, mem_v, buf_a, buf_b, buf_p, pos0, w, *, mix_seq, mix_seg, att_seq, att_seg, precise_tiles):
    batch, length, _ = x.shape
    x, new_a, new_b = _mixer_ab(x, buf_a, buf_b, w["ab_w_in"], w["ab_w_in_lo"], w["ab_conv_a_w"], w["ab_conv_a_b"],
                                w["ab_norm_a_g"], w["ab_norm_a_b"], w["ab_conv_b_w"], w["ab_w_out"], w["ab_w_out_lo"],
                                w["ln_g"][0, 0], w["ln_b"][0, 0], n_seq=mix_seq, seg=mix_seg, precise_tiles=precise_tiles)
    new_p = v_rows = None
    for layer in range(DEPTH):
        if layer == 1:
            x, new_p, v_rows = _mixer_cd(x, buf_p, w["cd_w_in"], w["cd_pool_w"], w["cd_pool_scale"], w["cd_v_norm_g"],
                                         w["cd_v_norm_b"], w["cd_w_s"], w["cd_b_s"], w["cd_w_out"],
                                         w["ln_g"][1, 0], w["ln_b"][1, 0], n_seq=mix_seq, seg=mix_seg, pos0=pos0)
        rows, meta, counts = _attn_router(x, mem_k[layer], mem_v[layer], w["ca_wq"][layer], w["ca_wo"][layer],
                                          w["ln_g"][layer, 1], w["ln_b"][layer, 1], w["router_w_t"], w["router_b"],
                                          n_seq=att_seq, seg=att_seg)
        x = _moe(rows, meta, counts, w["moe_w_gate"][layer], w["moe_w_up"][layer], w["moe_w_down"][layer],
                 w["ln_g"][layer, 2], w["ln_b"][layer, 2]).reshape(batch, length, D_MODEL)
    return x, new_a, new_b, new_p, v_rows


def kernel(x_prompt, x_sample, mem_prompt, cache_mem_k, cache_mem_v, state_conv_a, state_conv_b, state_pool,
           ln_g, ln_b, ab_w_in, ab_conv_a_w, ab_conv_a_b, ab_norm_a_g, ab_norm_a_b, ab_conv_b_w, ab_w_out,
           cd_w_in, cd_pool_w, cd_pool_scale, cd_v_norm_g, cd_v_norm_b, cd_w_s, cd_b_s, cd_w_out,
           ca_wq, ca_wk, ca_wv, ca_wo, router_w, router_b, moe_w_gate, moe_w_up, moe_w_down):
    assert ln_g.shape[0] == DEPTH and ab_w_in.shape[0] == 1 and cd_w_in.shape[0] == 1
    bsz, seq, _ = x_prompt.shape
    dec_b, dec_seq, _ = x_sample.shape
    n_pool = len(POOL_WINDOWS)
    pool_c = D_HALF // n_pool
    pool_bd = jnp.zeros((D_HALF, D_HALF), F32)
    for g in range(n_pool):
        pool_bd = pool_bd.at[g * pool_c:(g + 1) * pool_c, g * pool_c:(g + 1) * pool_c].set(cd_pool_w[0, g])
    ab_w_in_hi, ab_w_in_lo = _split_weight(ab_w_in[0])
    ab_w_out_hi, ab_w_out_lo = _split_weight(ab_w_out[0])
    rw_hi, rw_lo = _split_weight(router_w.T)
    w = {
        "ln_g": ln_g, "ln_b": ln_b,
        "ab_w_in": ab_w_in_hi, "ab_w_in_lo": ab_w_in_lo, "ab_conv_a_w": ab_conv_a_w[0], "ab_conv_a_b": ab_conv_a_b[0],
        "ab_norm_a_g": ab_norm_a_g[0], "ab_norm_a_b": ab_norm_a_b[0], "ab_conv_b_w": ab_conv_b_w[0],
        "ab_w_out": ab_w_out_hi, "ab_w_out_lo": ab_w_out_lo,
        "cd_w_in": cd_w_in[0].astype(BF16), "cd_pool_w": pool_bd.astype(BF16), "cd_pool_scale": cd_pool_scale[0],
        "cd_v_norm_g": cd_v_norm_g[0], "cd_v_norm_b": cd_v_norm_b[0], "cd_w_s": cd_w_s[0], "cd_b_s": cd_b_s[0],
        "cd_w_out": cd_w_out[0].astype(BF16),
        "ca_wq": ca_wq.astype(BF16), "ca_wo": ca_wo.astype(BF16),
        "router_w_t": jnp.concatenate([rw_hi, rw_lo], axis=0), "router_b": router_b,
        "moe_w_gate": moe_w_gate.astype(BF16), "moe_w_up": moe_w_up.astype(BF16),
        "moe_w_down": moe_w_down.astype(BF16),
    }

    kv = _mem_projection(mem_prompt.reshape(bsz * N_MEM, D_MODEL),
                         jnp.concatenate([ca_wk, ca_wv], axis=0).astype(BF16))
    kv = kv.reshape(2, DEPTH, bsz, N_MEM, D_MODEL)
    mem_k_prompt, mem_v_prompt = kv[0], kv[1]
    y_prompt, conv_a_p, conv_b_p, pool_p, _ = _trunk(
        x_prompt, mem_k_prompt, mem_v_prompt,
        jnp.zeros((bsz, CONV_A - 1, D_HALF), F32), jnp.zeros((bsz, CONV_B - 1, D_HALF), F32),
        jnp.zeros((bsz, POOL_BUF, D_HALF), F32), 0, w,
        mix_seq=1, mix_seg=512, att_seq=1, att_seg=512, precise_tiles=STATE_TILES)

    y_sample, conv_a_s, conv_b_s, pool_s, v_s = _trunk(
        x_sample, cache_mem_k.reshape(DEPTH, dec_b, N_MEM, D_MODEL), cache_mem_v.reshape(DEPTH, dec_b, N_MEM, D_MODEL),
        state_conv_a[0], state_conv_b[0], state_pool[0], PAST_LEN, w,
        mix_seq=8, mix_seg=dec_seq, att_seq=4, att_seg=dec_seq, precise_tiles=0)

    kv_shape = (DEPTH, bsz, N_MEM, MEM_HEADS, MEM_HEAD_DIM)
    return (y_prompt, y_sample, mem_k_prompt.reshape(kv_shape), mem_v_prompt.reshape(kv_shape),
            conv_a_p[None], conv_b_p[None], pool_p[None], conv_a_s[None], conv_b_s[None], pool_s[None], v_s[None])
```

```python
import functools

import jax
import jax.numpy as jnp
import numpy as np
from jax import lax
from jax.experimental import pallas as pl
from jax.experimental.pallas import tpu as pltpu

F32 = jnp.float32
BF16 = jnp.bfloat16

D_MODEL = 1024
D_HALF = D_MODEL // 2
DEPTH = 2
PAST_LEN = 4096
CHUNK = 64
CHUNK_SHIFT = CHUNK.bit_length() - 1
CONV_A = 31
CONV_B = 3
POOL_WINDOWS = (2, 4, 8, 16)
POOL_BUF = max(POOL_WINDOWS) - 1
N_HEAD_D = 4
GMLP_CHUNK = 128
N_MEM = 256
MEM_HEADS = 4
MEM_HEAD_DIM = D_MODEL // MEM_HEADS
N_EXPERTS = 16
N_EXPERT_GROUPS = 4
GROUP_SIZE = N_EXPERTS // N_EXPERT_GROUPS
GROUP_SHIFT = GROUP_SIZE.bit_length() - 1
PAIRS_PER_GROUP = GROUP_SIZE * (GROUP_SIZE - 1) // 2
N_CLASSES = N_EXPERT_GROUPS * PAIRS_PER_GROUP
D_EXPERT = D_MODEL // 2
ALPHA = (2 * DEPTH) ** 0.25
LN_EPS = 1e-5

LANES = 128
SUBLANES = 8
ROW_W = D_MODEL + LANES
HIST_A = 32
HIST_B = 8
HIST_P = 16
CONV_ROWS = 64
ROW_TILE = 512
MOE_TILE = 256
CLASS_ROWS = 32
DISPATCH_TILE = 1024
SPLIT_ROWS = 256
STATE_TILES = 1
VMEM_LIMIT = 56 * 1024 * 1024

_PAIR_AB = [(a, b) for a in range(GROUP_SIZE) for b in range(a + 1, GROUP_SIZE)]
_CLASS_A = np.array([g * GROUP_SIZE + a for g in range(N_EXPERT_GROUPS) for a, _ in _PAIR_AB], np.int32)
_CLASS_B = np.array([g * GROUP_SIZE + b for g in range(N_EXPERT_GROUPS) for _, b in _PAIR_AB], np.int32)


def _layer_norm(x, g, b):
    mu = jnp.mean(x, axis=-1, keepdims=True)
    xc = x - mu
    var = jnp.mean(xc * xc, axis=-1, keepdims=True)
    return xc * lax.rsqrt(var + LN_EPS) * g + b


def _bdot(a, w):
    return jnp.dot(a.astype(BF16), w, preferred_element_type=F32)


def _split(a):
    hi = a.astype(BF16)
    return hi, (a - hi.astype(F32)).astype(BF16)


def _dot3(a, w_hi, w_lo):
    a_hi, a_lo = _split(a)
    return (jnp.dot(a_hi, w_hi, preferred_element_type=F32) + jnp.dot(a_lo, w_hi, preferred_element_type=F32)
            + jnp.dot(a_hi, w_lo, preferred_element_type=F32))


def _params(sem):
    return pltpu.CompilerParams(dimension_semantics=sem, vmem_limit_bytes=VMEM_LIMIT)


def _full(shape):
    return pl.BlockSpec(shape, lambda *_: (0,) * len(shape), pipeline_mode=pl.Buffered(1))


def _row2(v):
    return v.reshape(1, -1)


def _split_weight_kernel(w_ref, hi_ref, lo_ref):
    hi, lo = _split(w_ref[...])
    hi_ref[...] = hi
    lo_ref[...] = lo


def _split_weight(w):
    rows, cols = w.shape
    blk = min(rows, SPLIT_ROWS)
    spec = pl.BlockSpec((blk, cols), lambda i: (i, 0))
    return pl.pallas_call(
        _split_weight_kernel,
        grid=(rows // blk,),
        in_specs=[spec],
        out_specs=[spec, spec],
        out_shape=[jax.ShapeDtypeStruct(w.shape, BF16)] * 2,
        compiler_params=_params(("arbitrary",)),
        name="split_weight",
    )(w)


def _proj_kernel(x_ref, w_ref, o_ref, o16_ref):
    res = _bdot(x_ref[...], w_ref[0])
    o_ref[0] = res
    o16_ref[0] = res.astype(BF16)


def _mem_projection(mem, w):
    n, rows = w.shape[0], mem.shape[0]
    out = pl.BlockSpec((1, rows, D_MODEL), lambda j: (j, 0, 0))
    return pl.pallas_call(
        _proj_kernel,
        grid=(n,),
        in_specs=[pl.BlockSpec((rows, D_MODEL), lambda j: (0, 0)),
                  pl.BlockSpec((1, D_MODEL, D_MODEL), lambda j: (j, 0, 0))],
        out_specs=[out, out],
        out_shape=[jax.ShapeDtypeStruct((n, rows, D_MODEL), F32), jax.ShapeDtypeStruct((n, rows, D_MODEL), BF16)],
        compiler_params=_params(("arbitrary",)),
        name="mem_projection",
    )(mem, w)


def _load_history(ext_ref, buf_ref, first, hist, keep, seg):
    @pl.when(first)
    def _():
        ext_ref[:, hist - keep:hist, :] = buf_ref[...]

    @pl.when(jnp.logical_not(first))
    def _():
        ext_ref[:, hist - keep:hist, :] = ext_ref[:, seg + hist - keep:seg + hist, :]


def _depthwise_conv(ext_ref, w_ref, out_ref, *, n_seq, seg, taps, hist, shifted_ref=None):
    rc = min(CONV_ROWS, seg)
    off0 = hist - (taps - 1)
    length = hist + seg
    if shifted_ref is not None:
        for r in range(1, SUBLANES):
            shifted_ref[r - 1, :, 0:length - SUBLANES, :] = ext_ref[:, r:r + length - SUBLANES, :]
    for s in range(n_seq):
        for r0 in range(0, seg, rc):
            for lb in range(0, D_HALF, LANES):
                acc = None
                for k in range(taps):
                    lo = off0 + k + r0
                    shift = (off0 + k) % SUBLANES
                    if shifted_ref is None or shift == 0:
                        win = ext_ref[s, lo:lo + rc, lb:lb + LANES]
                    else:
                        win = shifted_ref[shift - 1, s, lo - shift:lo - shift + rc, lb:lb + LANES]
                    term = w_ref[k:k + 1, lb:lb + LANES] * win
                    acc = term if acc is None else acc + term
                out_ref[s * seg + r0:s * seg + r0 + rc, lb:lb + LANES] = acc


def _mixer_ab_kernel(x_ref, bufa_ref, bufb_ref, w_in_ref, w_in_lo_ref, caw_ref, cab_ref, nag_ref, nab_ref, cbw_ref,
                     w_out_ref, w_out_lo_ref, lng_ref, lnb_ref,
                     y_ref, nbufa_ref, nbufb_ref,
                     h_scr, y_scr, a_ext, a_shift, cb_ext, conv_a, conv_b, *, n_seq, seg, precise_tiles):
    first = pl.program_id(1) == 0
    x = x_ref[...]

    def project(compute3, compute1, out_ref):
        if precise_tiles == 0:
            out_ref[...] = compute1()
            return
        is_precise = pl.program_id(1) >= pl.num_programs(1) - precise_tiles

        @pl.when(is_precise)
        def _():
            out_ref[...] = compute3()

        @pl.when(jnp.logical_not(is_precise))
        def _():
            out_ref[...] = compute1()

    project(lambda: _dot3(x, w_in_ref[...], w_in_lo_ref[...]), lambda: _bdot(x, w_in_ref[...]), h_scr)
    a = h_scr[:, 0:D_HALF] * jax.nn.sigmoid(h_scr[:, D_HALF:2 * D_HALF])
    cb = h_scr[:, 3 * D_HALF:4 * D_HALF] * h_scr[:, 4 * D_HALF:5 * D_HALF]

    _load_history(a_ext, bufa_ref, first, HIST_A, CONV_A - 1, seg)
    _load_history(cb_ext, bufb_ref, first, HIST_B, CONV_B - 1, seg)
    a_ext[:, HIST_A:HIST_A + seg, :] = a.reshape(n_seq, seg, D_HALF)
    cb_ext[:, HIST_B:HIST_B + seg, :] = cb.reshape(n_seq, seg, D_HALF)
    nbufa_ref[...] = a_ext[:, seg + HIST_A - (CONV_A - 1):seg + HIST_A, :]
    nbufb_ref[...] = cb_ext[:, seg + HIST_B - (CONV_B - 1):seg + HIST_B, :]

    _depthwise_conv(a_ext, caw_ref, conv_a, n_seq=n_seq, seg=seg, taps=CONV_A, hist=HIST_A, shifted_ref=a_shift)
    _depthwise_conv(cb_ext, cbw_ref, conv_b, n_seq=n_seq, seg=seg, taps=CONV_B, hist=HIST_B)

    a2 = _layer_norm(conv_a[...] + cab_ref[...], nag_ref[...], nab_ref[...])
    a2 = a2 * jax.nn.sigmoid(a2)
    b2 = h_scr[:, 2 * D_HALF:3 * D_HALF] * conv_b[...]
    project(lambda: (_dot3(a2, w_out_ref[0:D_HALF, :], w_out_lo_ref[0:D_HALF, :])
                     + _dot3(b2, w_out_ref[D_HALF:D_MODEL, :], w_out_lo_ref[D_HALF:D_MODEL, :])),
            lambda: _bdot(a2, w_out_ref[0:D_HALF, :]) + _bdot(b2, w_out_ref[D_HALF:D_MODEL, :]), y_scr)
    y_ref[...] = _layer_norm(ALPHA * x + y_scr[...], lng_ref[...], lnb_ref[...])


def _mixer_ab(x, buf_a, buf_b, w_in, w_in_lo, caw, cab, nag, nab, cbw, w_out, w_out_lo, lng, lnb, *,
              n_seq, seg, precise_tiles):
    batch = buf_a.shape[0]
    length = x.shape[0] // batch
    d_in = w_in.shape[1]
    n_l = length // seg
    rows = n_seq * seg
    lo_in = _full((D_MODEL, d_in)) if precise_tiles else _full((SUBLANES, LANES))
    lo_out = _full((D_MODEL, D_MODEL)) if precise_tiles else _full((SUBLANES, LANES))
    return pl.pallas_call(
        functools.partial(_mixer_ab_kernel, n_seq=n_seq, seg=seg, precise_tiles=precise_tiles),
        grid=(batch // n_seq, n_l),
        in_specs=[pl.BlockSpec((rows, D_MODEL), lambda b, l: (b * n_l + l, 0)),
                  pl.BlockSpec((n_seq, CONV_A - 1, D_HALF), lambda b, l: (b, 0, 0)),
                  pl.BlockSpec((n_seq, CONV_B - 1, D_HALF), lambda b, l: (b, 0, 0)),
                  _full((D_MODEL, d_in)), lo_in, _full((CONV_A, D_HALF)), _full((1, D_HALF)), _full((1, D_HALF)),
                  _full((1, D_HALF)), _full((CONV_B, D_HALF)), _full((D_MODEL, D_MODEL)), lo_out,
                  _full((1, D_MODEL)), _full((1, D_MODEL))],
        out_specs=[pl.BlockSpec((rows, D_MODEL), lambda b, l: (b * n_l + l, 0)),
                   pl.BlockSpec((n_seq, CONV_A - 1, D_HALF), lambda b, l: (b, 0, 0)),
                   pl.BlockSpec((n_seq, CONV_B - 1, D_HALF), lambda b, l: (b, 0, 0))],
        out_shape=[jax.ShapeDtypeStruct((batch * length, D_MODEL), F32),
                   jax.ShapeDtypeStruct((batch, CONV_A - 1, D_HALF), F32),
                   jax.ShapeDtypeStruct((batch, CONV_B - 1, D_HALF), F32)],
        scratch_shapes=[pltpu.VMEM((rows, d_in), F32),
                        pltpu.VMEM((rows, D_MODEL), F32),
                        pltpu.VMEM((n_seq, HIST_A + seg, D_HALF), F32),
                        pltpu.VMEM((SUBLANES - 1, n_seq, HIST_A + seg - SUBLANES, D_HALF), F32),
                        pltpu.VMEM((n_seq, HIST_B + seg, D_HALF), F32),
                        pltpu.VMEM((rows, D_HALF), F32),
                        pltpu.VMEM((rows, D_HALF), F32)],
        compiler_params=_params(("arbitrary", "arbitrary")),
        name="mixer_ab",
    )(x, buf_a, buf_b, w_in, w_in_lo, caw, _row2(cab), _row2(nag), _row2(nab), cbw, w_out, w_out_lo,
      _row2(lng), _row2(lnb))


def _mixer_cd_kernel(x_ref, bufp_ref, w_in_ref, pw_ref, ps_ref, vg_ref, vb_ref, ws_ref, bs_ref, w_out_ref,
                     lng_ref, lnb_ref, *refs, n_seq, seg, pos0, n_mix, emit_v):
    if emit_v:
        y_ref, nbufp_ref, v_ref, c_ext, pooled, mixed = refs
    else:
        y_ref, nbufp_ref, c_ext, pooled, mixed = refs
    rows = n_seq * seg
    li = pl.program_id(1)
    x = x_ref[...]
    h = _bdot(x, w_in_ref[...])
    c_in = h[:, 0:D_HALF]

    _load_history(c_ext, bufp_ref, li == 0, HIST_P, POOL_BUF, seg)
    c_ext[:, HIST_P:HIST_P + seg, :] = c_in.reshape(n_seq, seg, D_HALF)
    nbufp_ref[...] = c_ext[:, seg + HIST_P - POOL_BUF:seg + HIST_P, :]

    rc = min(CONV_ROWS, seg)
    for g, win in enumerate(POOL_WINDOWS):
        lanes = slice(g * LANES, (g + 1) * LANES)
        for s in range(n_seq):
            for r0 in range(0, seg, rc):
                cur = c_ext[s, HIST_P + r0:HIST_P + r0 + rc, lanes]
                acc = cur
                for j in range(1, win):
                    acc = acc + c_ext[s, HIST_P + r0 - j:HIST_P + r0 - j + rc, lanes]
                pos = pos0 + li * seg + r0 + lax.broadcasted_iota(jnp.int32, (rc, LANES), 0)
                cnt = jnp.minimum(pos + 1, win).astype(F32)
                pooled[s * seg + r0:s * seg + r0 + rc, lanes] = acc / cnt - cur
    c = _bdot(pooled[...], pw_ref[...]) * ps_ref[...]

    z = jax.nn.gelu(h[:, D_HALF:3 * D_HALF], approximate=True)
    u = z[:, 0:D_HALF]
    v = _layer_norm(z[:, D_HALF:2 * D_HALF], vg_ref[...], vb_ref[...])
    if emit_v:
        v_ref[...] = v.reshape(n_seq, seg, D_HALF)
    vb16 = v.astype(BF16)

    ri = lax.broadcasted_iota(jnp.int32, (n_mix, n_mix), 0) >> CHUNK_SHIFT
    ci = lax.broadcasted_iota(jnp.int32, (n_mix, n_mix), 1) >> CHUNK_SHIFT
    for g in range(N_HEAD_D):
        lanes = slice(g * LANES, (g + 1) * LANES)
        ws = jnp.where(ci <= ri, ws_ref[g], 0.0).astype(BF16)
        bias = bs_ref[:, g:g + 1]
        for r0 in range(0, rows, n_mix):
            mixed[r0:r0 + n_mix, lanes] = jnp.dot(ws, vb16[r0:r0 + n_mix, lanes], preferred_element_type=F32) + bias
    d = u * mixed[...]
    y = _bdot(c, w_out_ref[0:D_HALF, :]) + _bdot(d, w_out_ref[D_HALF:D_MODEL, :])
    y_ref[...] = _layer_norm(ALPHA * x + y, lng_ref[...], lnb_ref[...])


def _mixer_cd(x, row_start, buf_p, length, w_in, pw, ps, vg, vb, ws, bs, w_out, lng, lnb, *,
              n_seq, seg, pos0, emit_v):
    batch = buf_p.shape[0]
    d_in = w_in.shape[1]
    n_mix = min(length, GMLP_CHUNK)
    assert seg % n_mix == 0
    n_l = length // seg
    rows = n_seq * seg
    blk0 = row_start // rows
    ws_n = ws[:, :n_mix, :n_mix]
    bs_t = bs[:, :n_mix].T
    out_specs = [pl.BlockSpec((rows, D_MODEL), lambda b, l: (b * n_l + l, 0)),
                 pl.BlockSpec((n_seq, POOL_BUF, D_HALF), lambda b, l: (b, 0, 0))]
    out_shape = [jax.ShapeDtypeStruct((batch * length, D_MODEL), F32),
                 jax.ShapeDtypeStruct((batch, POOL_BUF, D_HALF), F32)]
    if emit_v:
        out_specs.append(pl.BlockSpec((n_seq, seg, D_HALF), lambda b, l: (b, l, 0)))
        out_shape.append(jax.ShapeDtypeStruct((batch, length, D_HALF), F32))
    return pl.pallas_call(
        functools.partial(_mixer_cd_kernel, n_seq=n_seq, seg=seg, pos0=pos0, n_mix=n_mix, emit_v=emit_v),
        grid=(batch // n_seq, n_l),
        in_specs=[pl.BlockSpec((rows, D_MODEL), lambda b, l: (blk0 + b * n_l + l, 0)),
                  pl.BlockSpec((n_seq, POOL_BUF, D_HALF), lambda b, l: (b, 0, 0)),
                  _full((D_MODEL, d_in)), _full((D_HALF, D_HALF)), _full((1, D_HALF)), _full((1, D_HALF)),
                  _full((1, D_HALF)), _full((N_HEAD_D, n_mix, n_mix)), _full((n_mix, N_HEAD_D)),
                  _full((D_MODEL, D_MODEL)), _full((1, D_MODEL)), _full((1, D_MODEL))],
        out_specs=out_specs,
        out_shape=out_shape,
        scratch_shapes=[pltpu.VMEM((n_seq, HIST_P + seg, D_HALF), F32),
                        pltpu.VMEM((rows, D_HALF), F32),
                        pltpu.VMEM((rows, D_HALF), F32)],
        compiler_params=_params(("arbitrary", "arbitrary")),
        name="mixer_cd",
    )(x, buf_p, w_in, pw, _row2(ps), _row2(vg), _row2(vb), ws_n, bs_t, w_out, _row2(lng), _row2(lnb))


def _attn_router_kernel(xp_ref, xs_ref, kp_ref, vp_ref, ks_ref, vs_ref, wq_ref, wo_ref, lng_ref, lnb_ref, rw_ref, rb_ref,
                        rows_ref, meta_ref, counts_ref,
                        q_scr, o_scr, carry, *, n_prompt_tiles, sample_seg):
    rows = ROW_TILE
    step = pl.program_id(0)
    is_prompt = step < n_prompt_tiles
    x = jnp.where(is_prompt, xp_ref[...], xs_ref[...])
    q_scr[...] = (_bdot(x, wq_ref[...]) * (MEM_HEAD_DIM ** -0.5)).astype(BF16)

    def heads(k_ref, v_ref, n_seq, seg):
        for s in range(n_seq):
            for hd in range(MEM_HEADS):
                cols = slice(hd * MEM_HEAD_DIM, (hd + 1) * MEM_HEAD_DIM)
                sc = lax.dot_general(q_scr[s * seg:(s + 1) * seg, cols], k_ref[s, :, cols], (((1,), (1,)), ((), ())),
                                     preferred_element_type=F32)
                p = jnp.exp(sc - jnp.max(sc, axis=-1, keepdims=True))
                den = jnp.sum(p, axis=-1, keepdims=True)
                o_scr[s * seg:(s + 1) * seg, cols] = _bdot(p, v_ref[s, :, cols]) / den

    @pl.when(is_prompt)
    def _():
        heads(kp_ref, vp_ref, 1, rows)

    @pl.when(jnp.logical_not(is_prompt))
    def _():
        heads(ks_ref, vs_ref, rows // sample_seg, sample_seg)

    x2 = _layer_norm(ALPHA * x + _bdot(o_scr[...], wo_ref[...]), lng_ref[...], lnb_ref[...])
    rows_ref[:, 0:D_MODEL] = x2

    nt = (((1,), (1,)), ((), ()))
    x2_hi, x2_lo = _split(x2)
    both = lax.dot_general(rw_ref[...], x2_hi, nt, preferred_element_type=F32)
    cross = lax.dot_general(rw_ref[0:N_EXPERTS, :], x2_lo, nt, preferred_element_type=F32)
    logits = both[0:N_EXPERTS] + both[N_EXPERTS:2 * N_EXPERTS] + cross + rb_ref[...]
    e = jnp.exp(logits - jnp.max(logits, axis=0, keepdims=True))
    scores = e / jnp.sum(e, axis=0, keepdims=True)
    eid = lax.broadcasted_iota(jnp.int32, (N_EXPERTS, rows), 0)
    egrp = eid >> GROUP_SHIFT
    best = jnp.max(jnp.where(egrp == 0, scores, -1.0), axis=0, keepdims=True)
    g_sel = jnp.zeros((1, rows), jnp.int32)
    for g in range(1, N_EXPERT_GROUPS):
        gs = jnp.max(jnp.where(egrp == g, scores, -1.0), axis=0, keepdims=True)
        upd = gs > best
        g_sel = jnp.where(upd, g, g_sel)
        best = jnp.where(upd, gs, best)
    masked = jnp.where(egrp == g_sel, scores, -1.0)
    m1 = jnp.max(masked, axis=0, keepdims=True)
    i1 = jnp.min(jnp.where(masked == m1, eid, N_EXPERTS), axis=0, keepdims=True)
    masked2 = jnp.where(eid == i1, -2.0, masked)
    m2 = jnp.max(masked2, axis=0, keepdims=True)
    i2 = jnp.min(jnp.where(masked2 == m2, eid, N_EXPERTS), axis=0, keepdims=True)
    tot = m1 + m2
    g1 = m1 / tot
    g2 = m2 / tot
    first_low = i1 < i2
    ea = jnp.where(first_low, i1, i2) & (GROUP_SIZE - 1)
    eb = jnp.where(first_low, i2, i1) & (GROUP_SIZE - 1)
    gate_a = jnp.where(first_low, g1, g2)
    gate_b = jnp.where(first_low, g2, g1)
    pair = eb - 1 + jnp.where(ea == 1, 2, 0) + jnp.where(ea == 2, 3, 0)
    cls = g_sel * PAIRS_PER_GROUP + pair

    lane_row = lax.broadcasted_iota(jnp.int32, (LANES, rows), 0)
    gates_t = jnp.where(lane_row == 0, gate_a, jnp.where(lane_row == 1, gate_b, 0.0))
    rows_ref[:, D_MODEL:ROW_W] = gates_t.T

    @pl.when(step == 0)
    def _():
        carry[...] = jnp.zeros_like(carry)

    onehot = (lax.broadcasted_iota(jnp.int32, (CLASS_ROWS, rows), 0) == cls).astype(F32)
    tri = (lax.broadcasted_iota(jnp.int32, (rows, rows), 0) <= lax.broadcasted_iota(jnp.int32, (rows, rows), 1))
    cum = jnp.dot(onehot.astype(BF16), tri.astype(F32).astype(BF16), preferred_element_type=F32)
    before = carry[:, 0:1]
    rank = jnp.sum(onehot * (before + cum), axis=0, keepdims=True) - 1.0
    sub = lax.broadcasted_iota(jnp.int32, (SUBLANES, rows), 0)
    meta_ref[...] = jnp.where(sub == 0, cls, jnp.where(sub == 1, rank.astype(jnp.int32), 0))
    carry[...] = carry[...] + cum[:, rows - 1:rows]
    counts_ref[...] = carry[...].astype(jnp.int32)


def _attn_router(x_p, x_s, k_p, v_p, k_s, v_s, wq, wo, lng, lnb, rw_t, rb, *, prompt_len, sample_len):
    rows = ROW_TILE
    n_p = x_p.shape[0] // rows
    n_s = x_s.shape[0] // rows
    tiles_per_seq = prompt_len // rows
    seq_per_tile = rows // sample_len
    total = x_p.shape[0] + x_s.shape[0]
    p_idx = lambda i: jnp.minimum(i, n_p - 1)
    s_idx = lambda i: jnp.maximum(i - n_p, 0)
    return pl.pallas_call(
        functools.partial(_attn_router_kernel, n_prompt_tiles=n_p, sample_seg=sample_len),
        grid=(n_p + n_s,),
        in_specs=[pl.BlockSpec((rows, D_MODEL), lambda i: (p_idx(i), 0)),
                  pl.BlockSpec((rows, D_MODEL), lambda i: (s_idx(i), 0)),
                  pl.BlockSpec((1, N_MEM, D_MODEL), lambda i: (p_idx(i) // tiles_per_seq, 0, 0)),
                  pl.BlockSpec((1, N_MEM, D_MODEL), lambda i: (p_idx(i) // tiles_per_seq, 0, 0)),
                  pl.BlockSpec((seq_per_tile, N_MEM, D_MODEL), lambda i: (s_idx(i), 0, 0)),
                  pl.BlockSpec((seq_per_tile, N_MEM, D_MODEL), lambda i: (s_idx(i), 0, 0)),
                  _full((D_MODEL, D_MODEL)), _full((D_MODEL, D_MODEL)), _full((1, D_MODEL)), _full((1, D_MODEL)),
                  _full((2 * N_EXPERTS, D_MODEL)), _full((N_EXPERTS, 1))],
        out_specs=[pl.BlockSpec((rows, ROW_W), lambda i: (i, 0)),
                   pl.BlockSpec((SUBLANES, rows), lambda i: (0, i)),
                   pl.BlockSpec((CLASS_ROWS, LANES), lambda i: (0, 0))],
        out_shape=[jax.ShapeDtypeStruct((total, ROW_W), F32),
                   jax.ShapeDtypeStruct((SUBLANES, total), jnp.int32),
                   jax.ShapeDtypeStruct((CLASS_ROWS, LANES), jnp.int32)],
        scratch_shapes=[pltpu.VMEM((rows, D_MODEL), BF16),
                        pltpu.VMEM((rows, D_MODEL), F32),
                        pltpu.VMEM((CLASS_ROWS, LANES), F32)],
        compiler_params=_params(("arbitrary",)),
        name="attn_router",
    )(x_p, x_s, k_p, v_p, k_s, v_s, wq, wo, _row2(lng), _row2(lnb), rw_t, rb.reshape(N_EXPERTS, 1))


def _dispatch_kernel(cls_ref, rank_ref, offs_ref, cnt_ref, end_ref, used_ref, rows_ref, xs_hbm, tok_ref,
                     zeros, sem, zsem, *, tile, n_tiles):
    step = pl.program_id(0)

    def class_padding(c, go):
        start = offs_ref[c] + cnt_ref[c]
        head = (-start) & (SUBLANES - 1)
        bulk_start = pl.multiple_of(start + head, SUBLANES)
        bulk = pl.multiple_of(end_ref[c] - bulk_start, SUBLANES)

        def one(r, carry_):
            go(pltpu.make_async_copy(zeros.at[pl.ds(0, 1)], xs_hbm.at[pl.ds(r, 1)], zsem))
            return carry_
        lax.fori_loop(start, start + head, one, 0)

        @pl.when(bulk > 0)
        def _():
            go(pltpu.make_async_copy(zeros.at[pl.ds(0, bulk)], xs_hbm.at[pl.ds(bulk_start, bulk)], zsem))

    def unused_tiles(go):
        def one(j, carry_):
            go(pltpu.make_async_copy(zeros, xs_hbm.at[pl.ds(pl.multiple_of(j * MOE_TILE, MOE_TILE), MOE_TILE)], zsem))
            return carry_
        lax.fori_loop(used_ref[0], n_tiles, one, 0)

    @pl.when(step == 0)
    def _():
        zeros[...] = jnp.zeros_like(zeros)

        def no_token(p, carry_):
            tok_ref[p] = -1
            return carry_
        for c in range(N_CLASSES):
            lax.fori_loop(offs_ref[c] + cnt_ref[c], end_ref[c], no_token, 0)
            class_padding(c, lambda copy: copy.start())
        lax.fori_loop(used_ref[0] * MOE_TILE, n_tiles * MOE_TILE, no_token, 0)
        unused_tiles(lambda copy: copy.start())

    def body(r, carry_):
        t = step * tile + r
        p = offs_ref[cls_ref[t]] + rank_ref[t]
        tok_ref[p] = t
        pltpu.make_async_copy(rows_ref.at[pl.ds(r, 1)], xs_hbm.at[pl.ds(p, 1)], sem).start()
        return carry_
    lax.fori_loop(0, tile, body, 0, unroll=8)
    pltpu.make_async_copy(rows_ref, xs_hbm.at[pl.ds(0, tile)], sem).wait()

    @pl.when(step == 0)
    def _():
        for c in range(N_CLASSES):
            class_padding(c, lambda copy: copy.wait())
        unused_tiles(lambda copy: copy.wait())


def _dispatch(cls, rank, offs, cnt, end, used, rows, n_tiles):
    n_rows = rows.shape[0]
    n_sorted = n_tiles * MOE_TILE
    any_spec = pl.BlockSpec(memory_space=pl.ANY)
    return pl.pallas_call(
        functools.partial(_dispatch_kernel, tile=DISPATCH_TILE, n_tiles=n_tiles),
        grid_spec=pltpu.PrefetchScalarGridSpec(
            num_scalar_prefetch=6, grid=(n_rows // DISPATCH_TILE,),
            in_specs=[pl.BlockSpec((DISPATCH_TILE, ROW_W), lambda i, *_: (i, 0))],
            out_specs=[any_spec, pl.BlockSpec(memory_space=pltpu.SMEM)],
            scratch_shapes=[pltpu.VMEM((MOE_TILE, ROW_W), F32), pltpu.SemaphoreType.DMA(()),
                            pltpu.SemaphoreType.DMA(())]),
        out_shape=[jax.ShapeDtypeStruct((n_sorted, ROW_W), F32), jax.ShapeDtypeStruct((n_sorted,), jnp.int32)],
        compiler_params=_params(("arbitrary",)),
        name="dispatch",
    )(cls, rank, offs, cnt, end, used, rows)


def _experts_kernel(ea_ref, eb_ref, blk_ref, nvalid_ref, tok_ref, xs_ref, wga_ref, wua_ref, wda_ref, wgb_ref, wub_ref,
                    wdb_ref, lng_ref, lnb_ref, out_hbm, ybuf, sems):
    del ea_ref, eb_ref, blk_ref
    i = pl.program_id(0)
    slot = i % 2
    n = nvalid_ref[i]

    def wait_rows(s, count):
        aligned = pl.multiple_of((count // SUBLANES) * SUBLANES, SUBLANES)

        @pl.when(aligned > 0)
        def _():
            pltpu.make_async_copy(ybuf.at[s, pl.ds(0, aligned)], out_hbm.at[pl.ds(0, aligned)], sems.at[s]).wait()

        def one(r, carry_):
            pltpu.make_async_copy(ybuf.at[s, pl.ds(0, 1)], out_hbm.at[pl.ds(0, 1)], sems.at[s]).wait()
            return carry_
        lax.fori_loop(aligned, count, one, 0)

    @pl.when(n > 0)
    def _():
        x = xs_ref[:, 0:D_MODEL]
        xb = x.astype(BF16)
        y = None
        for lane, (wg, wu, wd) in enumerate(((wga_ref, wua_ref, wda_ref), (wgb_ref, wub_ref, wdb_ref))):
            gate = xs_ref[:, D_MODEL + lane:D_MODEL + lane + 1]
            hg = jnp.dot(xb, wg[0], preferred_element_type=F32)
            hu = jnp.dot(xb, wu[0], preferred_element_type=F32)
            hid = hg * jax.nn.sigmoid(hg) * hu * gate
            part = _bdot(hid, wd[0])
            y = part if y is None else y + part
        ybuf[slot] = _layer_norm(ALPHA * x + y, lng_ref[...], lnb_ref[...])

        def start_row(r):
            t = tok_ref[i * MOE_TILE + r]
            pltpu.make_async_copy(ybuf.at[slot, pl.ds(r, 1)], out_hbm.at[pl.ds(t, 1)], sems.at[slot]).start()

        def group(j, carry_):
            for u in range(SUBLANES):
                start_row(j * SUBLANES + u)
            return carry_

        def single(r, carry_):
            start_row(r)
            return carry_
        n_groups = n // SUBLANES
        lax.fori_loop(0, n_groups, group, 0)
        lax.fori_loop(n_groups * SUBLANES, n, single, 0)

    @pl.when(i > 0)
    def _():
        wait_rows(1 - slot, nvalid_ref[jnp.maximum(i - 1, 0)])

    @pl.when(i == pl.num_programs(0) - 1)
    def _():
        wait_rows(slot, n)


def _experts(tile_a, tile_b, tile_blk, tile_nvalid, tok, xs, w_gate, w_up, w_down, lng, lnb, n_rows):
    n_tiles = xs.shape[0] // MOE_TILE
    up_a = pl.BlockSpec((1, D_MODEL, D_EXPERT), lambda i, ea, eb, *_: (ea[i], 0, 0))
    up_b = pl.BlockSpec((1, D_MODEL, D_EXPERT), lambda i, ea, eb, *_: (eb[i], 0, 0))
    dn_a = pl.BlockSpec((1, D_EXPERT, D_MODEL), lambda i, ea, eb, *_: (ea[i], 0, 0))
    dn_b = pl.BlockSpec((1, D_EXPERT, D_MODEL), lambda i, ea, eb, *_: (eb[i], 0, 0))
    vec = pl.BlockSpec((1, D_MODEL), lambda i, *_: (0, 0))
    return pl.pallas_call(
        _experts_kernel,
        grid_spec=pltpu.PrefetchScalarGridSpec(
            num_scalar_prefetch=5, grid=(n_tiles,),
            in_specs=[pl.BlockSpec((MOE_TILE, ROW_W), lambda i, ea, eb, blk, *_: (blk[i], 0)),
                      up_a, up_a, dn_a, up_b, up_b, dn_b, vec, vec],
            out_specs=pl.BlockSpec(memory_space=pl.ANY),
            scratch_shapes=[pltpu.VMEM((2, MOE_TILE, D_MODEL), F32), pltpu.SemaphoreType.DMA((2,))]),
        out_shape=jax.ShapeDtypeStruct((n_rows, D_MODEL), F32),
        compiler_params=_params(("arbitrary",)),
        name="experts",
    )(tile_a, tile_b, tile_blk, tile_nvalid, tok, xs, w_gate, w_up, w_down, w_gate, w_up, w_down,
      _row2(lng), _row2(lnb))


def _lookup(table, idx):
    pick = idx[:, None] == jnp.arange(table.shape[0], dtype=jnp.int32)[None, :]
    return jnp.sum(jnp.where(pick, table[None, :], 0), axis=1)


def _moe(rows, meta, counts, w_gate, w_up, w_down, lng, lnb):
    n_rows = rows.shape[0]
    n_tiles = n_rows // MOE_TILE + N_CLASSES
    cnt = counts[:N_CLASSES, 0]
    tiles_per = (cnt + MOE_TILE - 1) // MOE_TILE
    cls_id = jnp.arange(N_CLASSES, dtype=jnp.int32)
    tile_end = jnp.sum(jnp.where(cls_id[None, :] <= cls_id[:, None], tiles_per[None, :], 0), axis=1)
    tile_start = tile_end - tiles_per
    offs = tile_start * MOE_TILE
    end = tile_end * MOE_TILE
    used = tile_end[N_CLASSES - 1:N_CLASSES]
    tile_id = jnp.arange(n_tiles, dtype=jnp.int32)
    tile_blk = jnp.minimum(tile_id, used - 1)
    tile_cls = jnp.sum((tile_blk[:, None] >= tile_end[None, :]).astype(jnp.int32), axis=1)
    tile_a = _lookup(jnp.asarray(_CLASS_A), tile_cls)
    tile_b = _lookup(jnp.asarray(_CLASS_B), tile_cls)
    tile_nvalid = jnp.clip(_lookup(cnt, tile_cls) - (tile_id - _lookup(tile_start, tile_cls)) * MOE_TILE, 0, MOE_TILE)
    xs, tok = _dispatch(meta[0], meta[1], offs, cnt, end, used, rows, n_tiles)
    return _experts(tile_a, tile_b, tile_blk, tile_nvalid, tok, xs, w_gate, w_up, w_down, lng, lnb, n_rows)


def kernel(x_prompt, x_sample, mem_prompt, cache_mem_k, cache_mem_v, state_conv_a, state_conv_b, state_pool,
           ln_g, ln_b, ab_w_in, ab_conv_a_w, ab_conv_a_b, ab_norm_a_g, ab_norm_a_b, ab_conv_b_w, ab_w_out,
           cd_w_in, cd_pool_w, cd_pool_scale, cd_v_norm_g, cd_v_norm_b, cd_w_s, cd_b_s, cd_w_out,
           ca_wq, ca_wk, ca_wv, ca_wo, router_w, router_b, moe_w_gate, moe_w_up, moe_w_down):
    assert ln_g.shape[0] == DEPTH and ab_w_in.shape[0] == 1 and cd_w_in.shape[0] == 1
    bsz, seq, _ = x_prompt.shape
    dec_b, dec_seq, _ = x_sample.shape
    n_prompt = bsz * seq
    n_sample = dec_b * dec_seq
    sample_seqs = ROW_TILE // dec_seq

    n_pool = len(POOL_WINDOWS)
    pool_c = D_HALF // n_pool
    pool_bd = jnp.zeros((D_HALF, D_HALF), F32)
    for g in range(n_pool):
        pool_bd = pool_bd.at[g * pool_c:(g + 1) * pool_c, g * pool_c:(g + 1) * pool_c].set(cd_pool_w[0, g])
    pool_bd = pool_bd.astype(BF16)
    ab_w_in_hi, ab_w_in_lo = _split_weight(ab_w_in[0])
    ab_w_out_hi, ab_w_out_lo = _split_weight(ab_w_out[0])
    rw_hi, rw_lo = _split_weight(router_w.T)
    rw_t = jnp.concatenate([rw_hi, rw_lo], axis=0)
    cd_w_in_b, cd_w_out_b = cd_w_in[0].astype(BF16), cd_w_out[0].astype(BF16)
    wq_b, wo_b = ca_wq.astype(BF16), ca_wo.astype(BF16)
    wg_b, wu_b, wd_b = moe_w_gate.astype(BF16), moe_w_up.astype(BF16), moe_w_down.astype(BF16)

    kv, kv16 = _mem_projection(mem_prompt.reshape(bsz * N_MEM, D_MODEL),
                               jnp.concatenate([ca_wk, ca_wv], axis=0).astype(BF16))
    kv_shape = (DEPTH, bsz, N_MEM, MEM_HEADS, MEM_HEAD_DIM)
    mem_k_prompt, mem_v_prompt = kv[0:DEPTH].reshape(kv_shape), kv[DEPTH:2 * DEPTH].reshape(kv_shape)
    kv16 = kv16.reshape(2, DEPTH, bsz, N_MEM, D_MODEL)
    k_s = cache_mem_k.reshape(DEPTH, dec_b, N_MEM, D_MODEL).astype(BF16)
    v_s = cache_mem_v.reshape(DEPTH, dec_b, N_MEM, D_MODEL).astype(BF16)

    def mixer_ab(x, buf_a, buf_b, n_seq, seg, precise_tiles):
        return _mixer_ab(x, buf_a, buf_b, ab_w_in_hi, ab_w_in_lo, ab_conv_a_w[0], ab_conv_a_b[0], ab_norm_a_g[0],
                         ab_norm_a_b[0], ab_conv_b_w[0], ab_w_out_hi, ab_w_out_lo, ln_g[0, 0], ln_b[0, 0],
                         n_seq=n_seq, seg=seg, precise_tiles=precise_tiles)

    def mixer_cd(x, row_start, buf_p, length, n_seq, seg, pos0, emit_v):
        return _mixer_cd(x, row_start, buf_p, length, cd_w_in_b, pool_bd, cd_pool_scale[0], cd_v_norm_g[0],
                         cd_v_norm_b[0], cd_w_s[0], cd_b_s[0], cd_w_out_b, ln_g[1, 0], ln_b[1, 0],
                         n_seq=n_seq, seg=seg, pos0=pos0, emit_v=emit_v)

    def attn_moe(layer, x_p, x_s):
        rows, meta, counts = _attn_router(x_p, x_s, kv16[0, layer], kv16[1, layer], k_s[layer], v_s[layer],
                                          wq_b[layer], wo_b[layer], ln_g[layer, 1], ln_b[layer, 1], rw_t, router_b,
                                          prompt_len=seq, sample_len=dec_seq)
        return _moe(rows, meta, counts, wg_b[layer], wu_b[layer], wd_b[layer], ln_g[layer, 2], ln_b[layer, 2])

    x_p, conv_a_p, conv_b_p = mixer_ab(x_prompt.reshape(n_prompt, D_MODEL),
                                       jnp.zeros((bsz, CONV_A - 1, D_HALF), F32),
                                       jnp.zeros((bsz, CONV_B - 1, D_HALF), F32), 1, ROW_TILE, STATE_TILES)
    x_s, conv_a_s, conv_b_s = mixer_ab(x_sample.reshape(n_sample, D_MODEL), state_conv_a[0], state_conv_b[0],
                                       sample_seqs, dec_seq, 0)
    x_all = attn_moe(0, x_p, x_s)

    x_p, pool_p = mixer_cd(x_all, 0, jnp.zeros((bsz, POOL_BUF, D_HALF), F32), seq, 1, ROW_TILE, 0, False)
    x_s, pool_s, v_s_rows = mixer_cd(x_all, n_prompt, state_pool[0], dec_seq, sample_seqs, dec_seq, PAST_LEN, True)
    x_all = attn_moe(1, x_p, x_s)

    y_prompt = x_all[0:n_prompt].reshape(bsz, seq, D_MODEL)
    y_sample = x_all[n_prompt:n_prompt + n_sample].reshape(dec_b, dec_seq, D_MODEL)
    return (y_prompt, y_sample, mem_k_prompt, mem_v_prompt, conv_a_p[None], conv_b_p[None], pool_p[None],
            conv_a_s[None], conv_b_s[None], pool_s[None], v_s_rows[None])
```

```python
import functools

import jax
import jax.numpy as jnp
import numpy as np
from jax import lax
from jax.experimental import pallas as pl
from jax.experimental.pallas import tpu as pltpu

F32 = jnp.float32
BF16 = jnp.bfloat16

D_MODEL = 1024
D_HALF = D_MODEL // 2
DEPTH = 2
PAST_LEN = 4096
CHUNK = 64
CHUNK_SHIFT = CHUNK.bit_length() - 1
CONV_A = 31
CONV_B = 3
POOL_WINDOWS = (2, 4, 8, 16)
POOL_BUF = max(POOL_WINDOWS) - 1
N_HEAD_D = 4
GMLP_CHUNK = 128
N_MEM = 256
MEM_HEADS = 4
MEM_HEAD_DIM = D_MODEL // MEM_HEADS
N_EXPERTS = 16
N_EXPERT_GROUPS = 4
GROUP_SIZE = N_EXPERTS // N_EXPERT_GROUPS
GROUP_SHIFT = GROUP_SIZE.bit_length() - 1
PAIRS_PER_GROUP = GROUP_SIZE * (GROUP_SIZE - 1) // 2
N_CLASSES = N_EXPERT_GROUPS * PAIRS_PER_GROUP
D_EXPERT = D_MODEL // 2
ALPHA = (2 * DEPTH) ** 0.25
LN_EPS = 1e-5

LANES = 128
SUBLANES = 8
ROW_W = D_MODEL + LANES
HIST_A = 32
HIST_B = 8
HIST_P = 16
CONV_ROWS = 64
ROW_TILE = 512
MOE_TILE = 256
CLASS_ROWS = 32
DISPATCH_TILE = 1024
SPLIT_ROWS = 256
STATE_TILES = 1
VMEM_LIMIT = 56 * 1024 * 1024

_PAIR_AB = [(a, b) for a in range(GROUP_SIZE) for b in range(a + 1, GROUP_SIZE)]
_CLASS_A = np.array([g * GROUP_SIZE + a for g in range(N_EXPERT_GROUPS) for a, _ in _PAIR_AB], np.int32)
_CLASS_B = np.array([g * GROUP_SIZE + b for g in range(N_EXPERT_GROUPS) for _, b in _PAIR_AB], np.int32)


def _layer_norm(x, g, b):
    mu = jnp.mean(x, axis=-1, keepdims=True)
    xc = x - mu
    var = jnp.mean(xc * xc, axis=-1, keepdims=True)
    return xc * lax.rsqrt(var + LN_EPS) * g + b


def _bdot(a, w):
    return jnp.dot(a.astype(BF16), w, preferred_element_type=F32)


def _split(a):
    hi = a.astype(BF16)
    return hi, (a - hi.astype(F32)).astype(BF16)


def _dot3(a, w_hi, w_lo):
    a_hi, a_lo = _split(a)
    return (jnp.dot(a_hi, w_hi, preferred_element_type=F32) + jnp.dot(a_lo, w_hi, preferred_element_type=F32)
            + jnp.dot(a_hi, w_lo, preferred_element_type=F32))


def _params(sem):
    return pltpu.CompilerParams(dimension_semantics=sem, vmem_limit_bytes=VMEM_LIMIT)


def _full(shape):
    return pl.BlockSpec(shape, lambda *_: (0,) * len(shape), pipeline_mode=pl.Buffered(1))


def _row2(v):
    return v.reshape(1, -1)


def _split_weight_kernel(w_ref, hi_ref, lo_ref):
    hi, lo = _split(w_ref[...])
    hi_ref[...] = hi
    lo_ref[...] = lo


def _split_weight(w):
    rows, cols = w.shape
    blk = min(rows, SPLIT_ROWS)
    spec = pl.BlockSpec((blk, cols), lambda i: (i, 0))
    return pl.pallas_call(
        _split_weight_kernel,
        grid=(rows // blk,),
        in_specs=[spec],
        out_specs=[spec, spec],
        out_shape=[jax.ShapeDtypeStruct(w.shape, BF16)] * 2,
        compiler_params=_params(("arbitrary",)),
        name="split_weight",
    )(w)


def _proj_kernel(x_ref, w_ref, o_ref, o16_ref):
    res = _bdot(x_ref[...], w_ref[0])
    o_ref[0] = res
    o16_ref[0] = res.astype(BF16)


def _mem_projection(mem, w):
    n, rows = w.shape[0], mem.shape[0]
    out = pl.BlockSpec((1, rows, D_MODEL), lambda j: (j, 0, 0))
    return pl.pallas_call(
        _proj_kernel,
        grid=(n,),
        in_specs=[pl.BlockSpec((rows, D_MODEL), lambda j: (0, 0)),
                  pl.BlockSpec((1, D_MODEL, D_MODEL), lambda j: (j, 0, 0))],
        out_specs=[out, out],
        out_shape=[jax.ShapeDtypeStruct((n, rows, D_MODEL), F32), jax.ShapeDtypeStruct((n, rows, D_MODEL), BF16)],
        compiler_params=_params(("arbitrary",)),
        name="mem_projection",
    )(mem, w)


def _load_history(ext_ref, buf_ref, first, hist, keep, seg):
    @pl.when(first)
    def _():
        ext_ref[:, hist - keep:hist, :] = buf_ref[...]

    @pl.when(jnp.logical_not(first))
    def _():
        ext_ref[:, hist - keep:hist, :] = ext_ref[:, seg + hist - keep:seg + hist, :]


def _depthwise_conv(ext_ref, w_ref, out_ref, *, n_seq, seg, taps, hist, shifted_ref=None):
    rc = min(CONV_ROWS, seg)
    off0 = hist - (taps - 1)
    length = hist + seg
    if shifted_ref is not None:
        for r in range(1, SUBLANES):
            shifted_ref[r - 1, :, 0:length - SUBLANES, :] = ext_ref[:, r:r + length - SUBLANES, :]
    for s in range(n_seq):
        for r0 in range(0, seg, rc):
            for lb in range(0, D_HALF, LANES):
                acc = None
                for k in range(taps):
                    lo = off0 + k + r0
                    shift = (off0 + k) % SUBLANES
                    if shifted_ref is None or shift == 0:
                        win = ext_ref[s, lo:lo + rc, lb:lb + LANES]
                    else:
                        win = shifted_ref[shift - 1, s, lo - shift:lo - shift + rc, lb:lb + LANES]
                    term = w_ref[k:k + 1, lb:lb + LANES] * win
                    acc = term if acc is None else acc + term
                out_ref[s * seg + r0:s * seg + r0 + rc, lb:lb + LANES] = acc


def _mixer_ab_kernel(x_ref, bufa_ref, bufb_ref, w_in_ref, w_in_lo_ref, caw_ref, cab_ref, nag_ref, nab_ref, cbw_ref,
                     w_out_ref, w_out_lo_ref, lng_ref, lnb_ref,
                     y_ref, nbufa_ref, nbufb_ref,
                     h_scr, y_scr, a_ext, a_shift, cb_ext, conv_a, conv_b, *, n_seq, seg, precise_tiles):
    first = pl.program_id(1) == 0
    x = x_ref[...]

    def project(compute3, compute1, out_ref):
        if precise_tiles == 0:
            out_ref[...] = compute1()
            return
        is_precise = pl.program_id(1) >= pl.num_programs(1) - precise_tiles

        @pl.when(is_precise)
        def _():
            out_ref[...] = compute3()

        @pl.when(jnp.logical_not(is_precise))
        def _():
            out_ref[...] = compute1()

    project(lambda: _dot3(x, w_in_ref[...], w_in_lo_ref[...]), lambda: _bdot(x, w_in_ref[...]), h_scr)
    a = h_scr[:, 0:D_HALF] * jax.nn.sigmoid(h_scr[:, D_HALF:2 * D_HALF])
    cb = h_scr[:, 3 * D_HALF:4 * D_HALF] * h_scr[:, 4 * D_HALF:5 * D_HALF]

    _load_history(a_ext, bufa_ref, first, HIST_A, CONV_A - 1, seg)
    _load_history(cb_ext, bufb_ref, first, HIST_B, CONV_B - 1, seg)
    a_ext[:, HIST_A:HIST_A + seg, :] = a.reshape(n_seq, seg, D_HALF)
    cb_ext[:, HIST_B:HIST_B + seg, :] = cb.reshape(n_seq, seg, D_HALF)
    nbufa_ref[...] = a_ext[:, seg + HIST_A - (CONV_A - 1):seg + HIST_A, :]
    nbufb_ref[...] = cb_ext[:, seg + HIST_B - (CONV_B - 1):seg + HIST_B, :]

    _depthwise_conv(a_ext, caw_ref, conv_a, n_seq=n_seq, seg=seg, taps=CONV_A, hist=HIST_A, shifted_ref=a_shift)
    _depthwise_conv(cb_ext, cbw_ref, conv_b, n_seq=n_seq, seg=seg, taps=CONV_B, hist=HIST_B)

    a2 = _layer_norm(conv_a[...] + cab_ref[...], nag_ref[...], nab_ref[...])
    a2 = a2 * jax.nn.sigmoid(a2)
    b2 = h_scr[:, 2 * D_HALF:3 * D_HALF] * conv_b[...]
    project(lambda: (_dot3(a2, w_out_ref[0:D_HALF, :], w_out_lo_ref[0:D_HALF, :])
                     + _dot3(b2, w_out_ref[D_HALF:D_MODEL, :], w_out_lo_ref[D_HALF:D_MODEL, :])),
            lambda: _bdot(a2, w_out_ref[0:D_HALF, :]) + _bdot(b2, w_out_ref[D_HALF:D_MODEL, :]), y_scr)
    y_ref[...] = _layer_norm(ALPHA * x + y_scr[...], lng_ref[...], lnb_ref[...])


def _mixer_ab(x, buf_a, buf_b, w_in, w_in_lo, caw, cab, nag, nab, cbw, w_out, w_out_lo, lng, lnb, *,
              n_seq, seg, precise_tiles):
    batch = buf_a.shape[0]
    length = x.shape[0] // batch
    d_in = w_in.shape[1]
    n_l = length // seg
    rows = n_seq * seg
    lo_in = _full((D_MODEL, d_in)) if precise_tiles else _full((SUBLANES, LANES))
    lo_out = _full((D_MODEL, D_MODEL)) if precise_tiles else _full((SUBLANES, LANES))
    return pl.pallas_call(
        functools.partial(_mixer_ab_kernel, n_seq=n_seq, seg=seg, precise_tiles=precise_tiles),
        grid=(batch // n_seq, n_l),
        in_specs=[pl.BlockSpec((rows, D_MODEL), lambda b, l: (b * n_l + l, 0)),
                  pl.BlockSpec((n_seq, CONV_A - 1, D_HALF), lambda b, l: (b, 0, 0)),
                  pl.BlockSpec((n_seq, CONV_B - 1, D_HALF), lambda b, l: (b, 0, 0)),
                  _full((D_MODEL, d_in)), lo_in, _full((CONV_A, D_HALF)), _full((1, D_HALF)), _full((1, D_HALF)),
                  _full((1, D_HALF)), _full((CONV_B, D_HALF)), _full((D_MODEL, D_MODEL)), lo_out,
                  _full((1, D_MODEL)), _full((1, D_MODEL))],
        out_specs=[pl.BlockSpec((rows, D_MODEL), lambda b, l: (b * n_l + l, 0)),
                   pl.BlockSpec((n_seq, CONV_A - 1, D_HALF), lambda b, l: (b, 0, 0)),
                   pl.BlockSpec((n_seq, CONV_B - 1, D_HALF), lambda b, l: (b, 0, 0))],
        out_shape=[jax.ShapeDtypeStruct((batch * length, D_MODEL), F32),
                   jax.ShapeDtypeStruct((batch, CONV_A - 1, D_HALF), F32),
                   jax.ShapeDtypeStruct((batch, CONV_B - 1, D_HALF), F32)],
        scratch_shapes=[pltpu.VMEM((rows, d_in), F32),
                        pltpu.VMEM((rows, D_MODEL), F32),
                        pltpu.VMEM((n_seq, HIST_A + seg, D_HALF), F32),
                        pltpu.VMEM((SUBLANES - 1, n_seq, HIST_A + seg - SUBLANES, D_HALF), F32),
                        pltpu.VMEM((n_seq, HIST_B + seg, D_HALF), F32),
                        pltpu.VMEM((rows, D_HALF), F32),
                        pltpu.VMEM((rows, D_HALF), F32)],
        compiler_params=_params(("arbitrary", "arbitrary")),
        name="mixer_ab",
    )(x, buf_a, buf_b, w_in, w_in_lo, caw, _row2(cab), _row2(nag), _row2(nab), cbw, w_out, w_out_lo,
      _row2(lng), _row2(lnb))


def _mixer_cd_kernel(x_ref, bufp_ref, w_in_ref, pw_ref, ps_ref, vg_ref, vb_ref, ws_ref, bs_ref, w_out_ref,
                     lng_ref, lnb_ref, *refs, n_seq, seg, pos0, n_mix, emit_v):
    if emit_v:
        y_ref, nbufp_ref, v_ref, c_ext, pooled, mixed = refs
    else:
        y_ref, nbufp_ref, c_ext, pooled, mixed = refs
    rows = n_seq * seg
    li = pl.program_id(1)
    x = x_ref[...]
    h = _bdot(x, w_in_ref[...])
    c_in = h[:, 0:D_HALF]

    _load_history(c_ext, bufp_ref, li == 0, HIST_P, POOL_BUF, seg)
    c_ext[:, HIST_P:HIST_P + seg, :] = c_in.reshape(n_seq, seg, D_HALF)
    nbufp_ref[...] = c_ext[:, seg + HIST_P - POOL_BUF:seg + HIST_P, :]

    rc = min(CONV_ROWS, seg)
    for g, win in enumerate(POOL_WINDOWS):
        lanes = slice(g * LANES, (g + 1) * LANES)
        for s in range(n_seq):
            for r0 in range(0, seg, rc):
                cur = c_ext[s, HIST_P + r0:HIST_P + r0 + rc, lanes]
                acc = cur
                for j in range(1, win):
                    acc = acc + c_ext[s, HIST_P + r0 - j:HIST_P + r0 - j + rc, lanes]
                pos = pos0 + li * seg + r0 + lax.broadcasted_iota(jnp.int32, (rc, LANES), 0)
                cnt = jnp.minimum(pos + 1, win).astype(F32)
                pooled[s * seg + r0:s * seg + r0 + rc, lanes] = acc / cnt - cur
    c = _bdot(pooled[...], pw_ref[...]) * ps_ref[...]

    z = jax.nn.gelu(h[:, D_HALF:3 * D_HALF], approximate=True)
    u = z[:, 0:D_HALF]
    v = _layer_norm(z[:, D_HALF:2 * D_HALF], vg_ref[...], vb_ref[...])
    if emit_v:
        v_ref[...] = v.reshape(n_seq, seg, D_HALF)
    vb16 = v.astype(BF16)

    ri = lax.broadcasted_iota(jnp.int32, (n_mix, n_mix), 0) >> CHUNK_SHIFT
    ci = lax.broadcasted_iota(jnp.int32, (n_mix, n_mix), 1) >> CHUNK_SHIFT
    for g in range(N_HEAD_D):
        lanes = slice(g * LANES, (g + 1) * LANES)
        ws = jnp.where(ci <= ri, ws_ref[g], 0.0).astype(BF16)
        bias = bs_ref[:, g:g + 1]
        for r0 in range(0, rows, n_mix):
            mixed[r0:r0 + n_mix, lanes] = jnp.dot(ws, vb16[r0:r0 + n_mix, lanes], preferred_element_type=F32) + bias
    d = u * mixed[...]
    y = _bdot(c, w_out_ref[0:D_HALF, :]) + _bdot(d, w_out_ref[D_HALF:D_MODEL, :])
    y_ref[...] = _layer_norm(ALPHA * x + y, lng_ref[...], lnb_ref[...])


def _mixer_cd(x, row_start, buf_p, length, w_in, pw, ps, vg, vb, ws, bs, w_out, lng, lnb, *,
              n_seq, seg, pos0, emit_v):
    batch = buf_p.shape[0]
    d_in = w_in.shape[1]
    n_mix = min(length, GMLP_CHUNK)
    assert seg % n_mix == 0
    n_l = length // seg
    rows = n_seq * seg
    blk0 = row_start // rows
    ws_n = ws[:, :n_mix, :n_mix]
    bs_t = bs[:, :n_mix].T
    out_specs = [pl.BlockSpec((rows, D_MODEL), lambda b, l: (b * n_l + l, 0)),
                 pl.BlockSpec((n_seq, POOL_BUF, D_HALF), lambda b, l: (b, 0, 0))]
    out_shape = [jax.ShapeDtypeStruct((batch * length, D_MODEL), F32),
                 jax.ShapeDtypeStruct((batch, POOL_BUF, D_HALF), F32)]
    if emit_v:
        out_specs.append(pl.BlockSpec((n_seq, seg, D_HALF), lambda b, l: (b, l, 0)))
        out_shape.append(jax.ShapeDtypeStruct((batch, length, D_HALF), F32))
    return pl.pallas_call(
        functools.partial(_mixer_cd_kernel, n_seq=n_seq, seg=seg, pos0=pos0, n_mix=n_mix, emit_v=emit_v),
        grid=(batch // n_seq, n_l),
        in_specs=[pl.BlockSpec((rows, D_MODEL), lambda b, l: (blk0 + b * n_l + l, 0)),
                  pl.BlockSpec((n_seq, POOL_BUF, D_HALF), lambda b, l: (b, 0, 0)),
                  _full((D_MODEL, d_in)), _full((D_HALF, D_HALF)), _full((1, D_HALF)), _full((1, D_HALF)),
                  _full((1, D_HALF)), _full((N_HEAD_D, n_mix, n_mix)), _full((n_mix, N_HEAD_D)),
                  _full((D_MODEL, D_MODEL)), _full((1, D_MODEL)), _full((1, D_MODEL))],
        out_specs=out_specs,
        out_shape=out_shape,
        scratch_shapes=[pltpu.VMEM((n_seq, HIST_P + seg, D_HALF), F32),
                        pltpu.VMEM((rows, D_HALF), F32),
                        pltpu.VMEM((rows, D_HALF), F32)],
        compiler_params=_params(("arbitrary", "arbitrary")),
        name="mixer_cd",
    )(x, buf_p, w_in, pw, _row2(ps), _row2(vg), _row2(vb), ws_n, bs_t, w_out, _row2(lng), _row2(lnb))


def _attn_router_kernel(xp_ref, xs_ref, kp_ref, vp_ref, ks_ref, vs_ref, wq_ref, wo_ref, lng_ref, lnb_ref, rw_ref, rb_ref,
                        rows_ref, meta_ref, counts_ref,
                        q_scr, o_scr, carry, *, n_prompt_tiles, sample_seg):
    rows = ROW_TILE
    step = pl.program_id(0)
    is_prompt = step < n_prompt_tiles
    x = jnp.where(is_prompt, xp_ref[...], xs_ref[...])
    q_scr[...] = (_bdot(x, wq_ref[...]) * (MEM_HEAD_DIM ** -0.5)).astype(BF16)

    def heads(k_ref, v_ref, n_seq, seg):
        for s in range(n_seq):
            for hd in range(MEM_HEADS):
                cols = slice(hd * MEM_HEAD_DIM, (hd + 1) * MEM_HEAD_DIM)
                sc = lax.dot_general(q_scr[s * seg:(s + 1) * seg, cols], k_ref[s, :, cols], (((1,), (1,)), ((), ())),
                                     preferred_element_type=F32)
                p = jnp.exp(sc - jnp.max(sc, axis=-1, keepdims=True))
                den = jnp.sum(p, axis=-1, keepdims=True)
                o_scr[s * seg:(s + 1) * seg, cols] = _bdot(p, v_ref[s, :, cols]) / den

    @pl.when(is_prompt)
    def _():
        heads(kp_ref, vp_ref, 1, rows)

    @pl.when(jnp.logical_not(is_prompt))
    def _():
        heads(ks_ref, vs_ref, rows // sample_seg, sample_seg)

    x2 = _layer_norm(ALPHA * x + _bdot(o_scr[...], wo_ref[...]), lng_ref[...], lnb_ref[...])
    rows_ref[:, 0:D_MODEL] = x2

    nt = (((1,), (1,)), ((), ()))
    x2_hi, x2_lo = _split(x2)
    both = lax.dot_general(rw_ref[...], x2_hi, nt, preferred_element_type=F32)
    cross = lax.dot_general(rw_ref[0:N_EXPERTS, :], x2_lo, nt, preferred_element_type=F32)
    logits = both[0:N_EXPERTS] + both[N_EXPERTS:2 * N_EXPERTS] + cross + rb_ref[...]
    e = jnp.exp(logits - jnp.max(logits, axis=0, keepdims=True))
    scores = e / jnp.sum(e, axis=0, keepdims=True)
    eid = lax.broadcasted_iota(jnp.int32, (N_EXPERTS, rows), 0)
    egrp = eid >> GROUP_SHIFT
    best = jnp.max(jnp.where(egrp == 0, scores, -1.0), axis=0, keepdims=True)
    g_sel = jnp.zeros((1, rows), jnp.int32)
    for g in range(1, N_EXPERT_GROUPS):
        gs = jnp.max(jnp.where(egrp == g, scores, -1.0), axis=0, keepdims=True)
        upd = gs > best
        g_sel = jnp.where(upd, g, g_sel)
        best = jnp.where(upd, gs, best)
    masked = jnp.where(egrp == g_sel, scores, -1.0)
    m1 = jnp.max(masked, axis=0, keepdims=True)
    i1 = jnp.min(jnp.where(masked == m1, eid, N_EXPERTS), axis=0, keepdims=True)
    masked2 = jnp.where(eid == i1, -2.0, masked)
    m2 = jnp.max(masked2, axis=0, keepdims=True)
    i2 = jnp.min(jnp.where(masked2 == m2, eid, N_EXPERTS), axis=0, keepdims=True)
    tot = m1 + m2
    g1 = m1 / tot
    g2 = m2 / tot
    first_low = i1 < i2
    ea = jnp.where(first_low, i1, i2) & (GROUP_SIZE - 1)
    eb = jnp.where(first_low, i2, i1) & (GROUP_SIZE - 1)
    gate_a = jnp.where(first_low, g1, g2)
    gate_b = jnp.where(first_low, g2, g1)
    pair = eb - 1 + jnp.where(ea == 1, 2, 0) + jnp.where(ea == 2, 3, 0)
    cls = g_sel * PAIRS_PER_GROUP + pair

    lane_row = lax.broadcasted_iota(jnp.int32, (LANES, rows), 0)
    gates_t = jnp.where(lane_row == 0, gate_a, jnp.where(lane_row == 1, gate_b, 0.0))
    rows_ref[:, D_MODEL:ROW_W] = gates_t.T

    @pl.when(step == 0)
    def _():
        carry[...] = jnp.zeros_like(carry)

    onehot = (lax.broadcasted_iota(jnp.int32, (CLASS_ROWS, rows), 0) == cls).astype(F32)
    tri = (lax.broadcasted_iota(jnp.int32, (rows, rows), 0) <= lax.broadcasted_iota(jnp.int32, (rows, rows), 1))
    cum = jnp.dot(onehot.astype(BF16), tri.astype(F32).astype(BF16), preferred_element_type=F32)
    before = carry[:, 0:1]
    rank = jnp.sum(onehot * (before + cum), axis=0, keepdims=True) - 1.0
    sub = lax.broadcasted_iota(jnp.int32, (SUBLANES, rows), 0)
    meta_ref[...] = jnp.where(sub == 0, cls, jnp.where(sub == 1, rank.astype(jnp.int32), 0))
    carry[...] = carry[...] + cum[:, rows - 1:rows]
    counts_ref[...] = carry[...].astype(jnp.int32)


def _attn_router(x_p, x_s, k_p, v_p, k_s, v_s, wq, wo, lng, lnb, rw_t, rb, *, prompt_len, sample_len):
    rows = ROW_TILE
    n_p = x_p.shape[0] // rows
    n_s = x_s.shape[0] // rows
    tiles_per_seq = prompt_len // rows
    seq_per_tile = rows // sample_len
    total = x_p.shape[0] + x_s.shape[0]
    p_idx = lambda i: jnp.minimum(i, n_p - 1)
    s_idx = lambda i: jnp.maximum(i - n_p, 0)
    return pl.pallas_call(
        functools.partial(_attn_router_kernel, n_prompt_tiles=n_p, sample_seg=sample_len),
        grid=(n_p + n_s,),
        in_specs=[pl.BlockSpec((rows, D_MODEL), lambda i: (p_idx(i), 0)),
                  pl.BlockSpec((rows, D_MODEL), lambda i: (s_idx(i), 0)),
                  pl.BlockSpec((1, N_MEM, D_MODEL), lambda i: (p_idx(i) // tiles_per_seq, 0, 0)),
                  pl.BlockSpec((1, N_MEM, D_MODEL), lambda i: (p_idx(i) // tiles_per_seq, 0, 0)),
                  pl.BlockSpec((seq_per_tile, N_MEM, D_MODEL), lambda i: (s_idx(i), 0, 0)),
                  pl.BlockSpec((seq_per_tile, N_MEM, D_MODEL), lambda i: (s_idx(i), 0, 0)),
                  _full((D_MODEL, D_MODEL)), _full((D_MODEL, D_MODEL)), _full((1, D_MODEL)), _full((1, D_MODEL)),
                  _full((2 * N_EXPERTS, D_MODEL)), _full((N_EXPERTS, 1))],
        out_specs=[pl.BlockSpec((rows, ROW_W), lambda i: (i, 0)),
                   pl.BlockSpec((SUBLANES, rows), lambda i: (0, i)),
                   pl.BlockSpec((CLASS_ROWS, LANES), lambda i: (0, 0))],
        out_shape=[jax.ShapeDtypeStruct((total, ROW_W), F32),
                   jax.ShapeDtypeStruct((SUBLANES, total), jnp.int32),
                   jax.ShapeDtypeStruct((CLASS_ROWS, LANES), jnp.int32)],
        scratch_shapes=[pltpu.VMEM((rows, D_MODEL), BF16),
                        pltpu.VMEM((rows, D_MODEL), F32),
                        pltpu.VMEM((CLASS_ROWS, LANES), F32)],
        compiler_params=_params(("arbitrary",)),
        name="attn_router",
    )(x_p, x_s, k_p, v_p, k_s, v_s, wq, wo, _row2(lng), _row2(lnb), rw_t, rb.reshape(N_EXPERTS, 1))


def _dispatch_kernel(cls_ref, rank_ref, offs_ref, cnt_ref, end_ref, used_ref, rows_ref, xs_hbm, tok_ref,
                     zeros, sem, zsem, *, tile, n_tiles):
    step = pl.program_id(0)

    def class_padding(c, go):
        start = offs_ref[c] + cnt_ref[c]
        head = (-start) & (SUBLANES - 1)
        bulk_start = pl.multiple_of(start + head, SUBLANES)
        bulk = pl.multiple_of(end_ref[c] - bulk_start, SUBLANES)

        def one(r, carry_):
            go(pltpu.make_async_copy(zeros.at[pl.ds(0, 1)], xs_hbm.at[pl.ds(r, 1)], zsem))
            return carry_
        lax.fori_loop(start, start + head, one, 0)

        @pl.when(bulk > 0)
        def _():
            go(pltpu.make_async_copy(zeros.at[pl.ds(0, bulk)], xs_hbm.at[pl.ds(bulk_start, bulk)], zsem))

    def unused_tiles(go):
        def one(j, carry_):
            go(pltpu.make_async_copy(zeros, xs_hbm.at[pl.ds(pl.multiple_of(j * MOE_TILE, MOE_TILE), MOE_TILE)], zsem))
            return carry_
        lax.fori_loop(used_ref[0], n_tiles, one, 0)

    @pl.when(step == 0)
    def _():
        zeros[...] = jnp.zeros_like(zeros)

        def no_token(p, carry_):
            tok_ref[p] = -1
            return carry_
        for c in range(N_CLASSES):
            lax.fori_loop(offs_ref[c] + cnt_ref[c], end_ref[c], no_token, 0)
            class_padding(c, lambda copy: copy.start())
        lax.fori_loop(used_ref[0] * MOE_TILE, n_tiles * MOE_TILE, no_token, 0)
        unused_tiles(lambda copy: copy.start())

    for r in range(tile):
        t = step * tile + r
        p = offs_ref[cls_ref[t]] + rank_ref[t]
        tok_ref[p] = t
        pltpu.make_async_copy(rows_ref.at[pl.ds(r, 1)], xs_hbm.at[pl.ds(p, 1)], sem).start()
    pltpu.make_async_copy(rows_ref, xs_hbm.at[pl.ds(0, tile)], sem).wait()

    @pl.when(step == 0)
    def _():
        for c in range(N_CLASSES):
            class_padding(c, lambda copy: copy.wait())
        unused_tiles(lambda copy: copy.wait())


def _dispatch(cls, rank, offs, cnt, end, used, rows, n_tiles):
    n_rows = rows.shape[0]
    n_sorted = n_tiles * MOE_TILE
    any_spec = pl.BlockSpec(memory_space=pl.ANY)
    return pl.pallas_call(
        functools.partial(_dispatch_kernel, tile=DISPATCH_TILE, n_tiles=n_tiles),
        grid_spec=pltpu.PrefetchScalarGridSpec(
            num_scalar_prefetch=6, grid=(n_rows // DISPATCH_TILE,),
            in_specs=[pl.BlockSpec((DISPATCH_TILE, ROW_W), lambda i, *_: (i, 0))],
            out_specs=[any_spec, pl.BlockSpec(memory_space=pltpu.SMEM)],
            scratch_shapes=[pltpu.VMEM((MOE_TILE, ROW_W), F32), pltpu.SemaphoreType.DMA(()),
                            pltpu.SemaphoreType.DMA(())]),
        out_shape=[jax.ShapeDtypeStruct((n_sorted, ROW_W), F32), jax.ShapeDtypeStruct((n_sorted,), jnp.int32)],
        compiler_params=_params(("arbitrary",)),
        name="dispatch",
    )(cls, rank, offs, cnt, end, used, rows)


def _experts_kernel(ea_ref, eb_ref, blk_ref, nvalid_ref, tok_ref, xs_ref, wga_ref, wua_ref, wda_ref, wgb_ref, wub_ref,
                    wdb_ref, lng_ref, lnb_ref, out_hbm, ybuf, sems, *, n_rows):
    del ea_ref, eb_ref, blk_ref
    i = pl.program_id(0)
    n_cur = nvalid_ref[i]
    n_prev = jnp.where(i >= 1, nvalid_ref[jnp.maximum(i - 1, 0)], 0)
    n_prev2 = jnp.where(i >= 2, nvalid_ref[jnp.maximum(i - 2, 0)], 0)
    has_cur = n_cur > 0
    has_prev = n_prev > 0

    def tile_copy(slot):
        return pltpu.make_async_copy(ybuf.at[slot], out_hbm.at[pl.ds(0, MOE_TILE)], sems.at[slot])

    def compute(slot):
        x = xs_ref[:, 0:D_MODEL]
        xb = x.astype(BF16)
        y = None
        for lane, (wg, wu, wd) in enumerate(((wga_ref, wua_ref, wda_ref), (wgb_ref, wub_ref, wdb_ref))):
            gate = xs_ref[:, D_MODEL + lane:D_MODEL + lane + 1]
            hg = jnp.dot(xb, wg[0], preferred_element_type=F32)
            hu = jnp.dot(xb, wu[0], preferred_element_type=F32)
            hid = hg * jax.nn.sigmoid(hg) * hu * gate
            part = _bdot(hid, wd[0])
            y = part if y is None else y + part
        ybuf[slot] = _layer_norm(ALPHA * x + y, lng_ref[...], lnb_ref[...])

    def start_rows(tile, slot):
        base = tile * MOE_TILE
        spare = n_rows + (tile & 1) * MOE_TILE
        for r in range(MOE_TILE):
            t = tok_ref[base + r]
            dst = jnp.where(t < 0, spare + r, t)
            pltpu.make_async_copy(ybuf.at[slot, pl.ds(r, 1)], out_hbm.at[pl.ds(dst, 1)], sems.at[slot]).start()

    @pl.when(i == 0)
    def _():
        ybuf[2] = jnp.zeros((MOE_TILE, D_MODEL), F32)
        for half in range(2):
            fill = pltpu.make_async_copy(ybuf.at[2], out_hbm.at[pl.ds(n_rows + half * MOE_TILE, MOE_TILE)], sems.at[2])
            fill.start()
            fill.wait()

    slot_cur = i % 3
    slot_prev = (i + 2) % 3

    @pl.when(has_prev & has_cur)
    def _():
        start_rows(i - 1, slot_prev)
        compute(slot_cur)

    @pl.when(has_prev & jnp.logical_not(has_cur))
    def _():
        start_rows(i - 1, slot_prev)

    @pl.when(jnp.logical_not(has_prev) & has_cur)
    def _():
        compute(slot_cur)

    @pl.when(n_prev2 > 0)
    def _():
        tile_copy((i + 1) % 3).wait()


def _experts(tile_a, tile_b, tile_blk, tile_nvalid, tok, xs, w_gate, w_up, w_down, lng, lnb, n_rows):
    n_steps = tile_nvalid.shape[0]
    up_a = pl.BlockSpec((1, D_MODEL, D_EXPERT), lambda i, ea, eb, *_: (ea[i], 0, 0))
    up_b = pl.BlockSpec((1, D_MODEL, D_EXPERT), lambda i, ea, eb, *_: (eb[i], 0, 0))
    dn_a = pl.BlockSpec((1, D_EXPERT, D_MODEL), lambda i, ea, eb, *_: (ea[i], 0, 0))
    dn_b = pl.BlockSpec((1, D_EXPERT, D_MODEL), lambda i, ea, eb, *_: (eb[i], 0, 0))
    vec = pl.BlockSpec((1, D_MODEL), lambda i, *_: (0, 0))
    return pl.pallas_call(
        functools.partial(_experts_kernel, n_rows=n_rows),
        grid_spec=pltpu.PrefetchScalarGridSpec(
            num_scalar_prefetch=5, grid=(n_steps,),
            in_specs=[pl.BlockSpec((MOE_TILE, ROW_W), lambda i, ea, eb, blk, *_: (blk[i], 0)),
                      up_a, up_a, dn_a, up_b, up_b, dn_b, vec, vec],
            out_specs=pl.BlockSpec(memory_space=pl.ANY),
            scratch_shapes=[pltpu.VMEM((3, MOE_TILE, D_MODEL), F32), pltpu.SemaphoreType.DMA((3,))]),
        out_shape=jax.ShapeDtypeStruct((n_rows + 2 * MOE_TILE, D_MODEL), F32),
        compiler_params=_params(("arbitrary",)),
        name="experts",
    )(tile_a, tile_b, tile_blk, tile_nvalid, tok, xs, w_gate, w_up, w_down, w_gate, w_up, w_down,
      _row2(lng), _row2(lnb))


def _lookup(table, idx):
    pick = idx[:, None] == jnp.arange(table.shape[0], dtype=jnp.int32)[None, :]
    return jnp.sum(jnp.where(pick, table[None, :], 0), axis=1)


def _moe(rows, meta, counts, w_gate, w_up, w_down, lng, lnb):
    n_rows = rows.shape[0]
    n_tiles = n_rows // MOE_TILE + N_CLASSES
    cnt = counts[:N_CLASSES, 0]
    tiles_per = (cnt + MOE_TILE - 1) // MOE_TILE
    cls_id = jnp.arange(N_CLASSES, dtype=jnp.int32)
    tile_end = jnp.sum(jnp.where(cls_id[None, :] <= cls_id[:, None], tiles_per[None, :], 0), axis=1)
    tile_start = tile_end - tiles_per
    offs = tile_start * MOE_TILE
    end = tile_end * MOE_TILE
    used = tile_end[N_CLASSES - 1:N_CLASSES]
    tile_id = jnp.arange(n_tiles + 2, dtype=jnp.int32)
    tile_blk = jnp.minimum(tile_id, used - 1)
    tile_cls = jnp.sum((tile_blk[:, None] >= tile_end[None, :]).astype(jnp.int32), axis=1)
    tile_a = _lookup(jnp.asarray(_CLASS_A), tile_cls)
    tile_b = _lookup(jnp.asarray(_CLASS_B), tile_cls)
    tile_nvalid = jnp.clip(_lookup(cnt, tile_cls) - (tile_id - _lookup(tile_start, tile_cls)) * MOE_TILE, 0, MOE_TILE)
    xs, tok = _dispatch(meta[0], meta[1], offs, cnt, end, used, rows, n_tiles)
    return _experts(tile_a, tile_b, tile_blk, tile_nvalid, tok, xs, w_gate, w_up, w_down, lng, lnb, n_rows)


def kernel(x_prompt, x_sample, mem_prompt, cache_mem_k, cache_mem_v, state_conv_a, state_conv_b, state_pool,
           ln_g, ln_b, ab_w_in, ab_conv_a_w, ab_conv_a_b, ab_norm_a_g, ab_norm_a_b, ab_conv_b_w, ab_w_out,
           cd_w_in, cd_pool_w, cd_pool_scale, cd_v_norm_g, cd_v_norm_b, cd_w_s, cd_b_s, cd_w_out,
           ca_wq, ca_wk, ca_wv, ca_wo, router_w, router_b, moe_w_gate, moe_w_up, moe_w_down):
    assert ln_g.shape[0] == DEPTH and ab_w_in.shape[0] == 1 and cd_w_in.shape[0] == 1
    bsz, seq, _ = x_prompt.shape
    dec_b, dec_seq, _ = x_sample.shape
    n_prompt = bsz * seq
    n_sample = dec_b * dec_seq
    sample_seqs = ROW_TILE // dec_seq

    n_pool = len(POOL_WINDOWS)
    pool_c = D_HALF // n_pool
    pool_bd = jnp.zeros((D_HALF, D_HALF), F32)
    for g in range(n_pool):
        pool_bd = pool_bd.at[g * pool_c:(g + 1) * pool_c, g * pool_c:(g + 1) * pool_c].set(cd_pool_w[0, g])
    pool_bd = pool_bd.astype(BF16)
    ab_w_in_hi, ab_w_in_lo = _split_weight(ab_w_in[0])
    ab_w_out_hi, ab_w_out_lo = _split_weight(ab_w_out[0])
    rw_hi, rw_lo = _split_weight(router_w.T)
    rw_t = jnp.concatenate([rw_hi, rw_lo], axis=0)
    cd_w_in_b, cd_w_out_b = cd_w_in[0].astype(BF16), cd_w_out[0].astype(BF16)
    wq_b, wo_b = ca_wq.astype(BF16), ca_wo.astype(BF16)
    wg_b, wu_b, wd_b = moe_w_gate.astype(BF16), moe_w_up.astype(BF16), moe_w_down.astype(BF16)

    kv, kv16 = _mem_projection(mem_prompt.reshape(bsz * N_MEM, D_MODEL),
                               jnp.concatenate([ca_wk, ca_wv], axis=0).astype(BF16))
    kv_shape = (DEPTH, bsz, N_MEM, MEM_HEADS, MEM_HEAD_DIM)
    mem_k_prompt, mem_v_prompt = kv[0:DEPTH].reshape(kv_shape), kv[DEPTH:2 * DEPTH].reshape(kv_shape)
    kv16 = kv16.reshape(2, DEPTH, bsz, N_MEM, D_MODEL)
    k_s = cache_mem_k.reshape(DEPTH, dec_b, N_MEM, D_MODEL).astype(BF16)
    v_s = cache_mem_v.reshape(DEPTH, dec_b, N_MEM, D_MODEL).astype(BF16)

    def mixer_ab(x, buf_a, buf_b, n_seq, seg, precise_tiles):
        return _mixer_ab(x, buf_a, buf_b, ab_w_in_hi, ab_w_in_lo, ab_conv_a_w[0], ab_conv_a_b[0], ab_norm_a_g[0],
                         ab_norm_a_b[0], ab_conv_b_w[0], ab_w_out_hi, ab_w_out_lo, ln_g[0, 0], ln_b[0, 0],
                         n_seq=n_seq, seg=seg, precise_tiles=precise_tiles)

    def mixer_cd(x, row_start, buf_p, length, n_seq, seg, pos0, emit_v):
        return _mixer_cd(x, row_start, buf_p, length, cd_w_in_b, pool_bd, cd_pool_scale[0], cd_v_norm_g[0],
                         cd_v_norm_b[0], cd_w_s[0], cd_b_s[0], cd_w_out_b, ln_g[1, 0], ln_b[1, 0],
                         n_seq=n_seq, seg=seg, pos0=pos0, emit_v=emit_v)

    def attn_moe(layer, x_p, x_s):
        rows, meta, counts = _attn_router(x_p, x_s, kv16[0, layer], kv16[1, layer], k_s[layer], v_s[layer],
                                          wq_b[layer], wo_b[layer], ln_g[layer, 1], ln_b[layer, 1], rw_t, router_b,
                                          prompt_len=seq, sample_len=dec_seq)
        return _moe(rows, meta, counts, wg_b[layer], wu_b[layer], wd_b[layer], ln_g[layer, 2], ln_b[layer, 2])

    x_p, conv_a_p, conv_b_p = mixer_ab(x_prompt.reshape(n_prompt, D_MODEL),
                                       jnp.zeros((bsz, CONV_A - 1, D_HALF), F32),
                                       jnp.zeros((bsz, CONV_B - 1, D_HALF), F32), 1, ROW_TILE, STATE_TILES)
    x_s, conv_a_s, conv_b_s = mixer_ab(x_sample.reshape(n_sample, D_MODEL), state_conv_a[0], state_conv_b[0],
                                       sample_seqs, dec_seq, 0)
    x_all = attn_moe(0, x_p, x_s)

    x_p, pool_p = mixer_cd(x_all, 0, jnp.zeros((bsz, POOL_BUF, D_HALF), F32), seq, 1, ROW_TILE, 0, False)
    x_s, pool_s, v_s_rows = mixer_cd(x_all, n_prompt, state_pool[0], dec_seq, sample_seqs, dec_seq, PAST_LEN, True)
    x_all = attn_moe(1, x_p, x_s)

    y_prompt = x_all[0:n_prompt].reshape(bsz, seq, D_MODEL)
    y_sample = x_all[n_prompt:n_prompt + n_sample].reshape(dec_b, dec_seq, D_MODEL)
    return (y_prompt, y_sample, mem_k_prompt, mem_v_prompt, conv_a_p[None], conv_b_p[None], pool_p[None],
            conv_a_s[None], conv_b_s[None], pool_s[None], v_s_rows[None])
```

```python
import functools

import jax
import jax.numpy as jnp
import numpy as np
from jax import lax
from jax.experimental import pallas as pl
from jax.experimental.pallas import tpu as pltpu

F32 = jnp.float32
BF16 = jnp.bfloat16

D_MODEL = 1024
D_HALF = D_MODEL // 2
DEPTH = 2
PAST_LEN = 4096
CHUNK = 64
CHUNK_SHIFT = CHUNK.bit_length() - 1
CONV_A = 31
CONV_B = 3
POOL_WINDOWS = (2, 4, 8, 16)
POOL_BUF = max(POOL_WINDOWS) - 1
N_HEAD_D = 4
GMLP_CHUNK = 128
N_MEM = 256
MEM_HEADS = 4
MEM_HEAD_DIM = D_MODEL // MEM_HEADS
N_EXPERTS = 16
N_EXPERT_GROUPS = 4
GROUP_SIZE = N_EXPERTS // N_EXPERT_GROUPS
GROUP_SHIFT = GROUP_SIZE.bit_length() - 1
PAIRS_PER_GROUP = GROUP_SIZE * (GROUP_SIZE - 1) // 2
N_CLASSES = N_EXPERT_GROUPS * PAIRS_PER_GROUP
D_EXPERT = D_MODEL // 2
ALPHA = (2 * DEPTH) ** 0.25
LN_EPS = 1e-5

LANES = 128
SUBLANES = 8
ROW_W = D_MODEL + LANES
HIST_A = 32
HIST_B = 8
HIST_P = 16
CONV_ROWS = 64
ROW_TILE = 512
MOE_TILE = 256
CLASS_ROWS = 32
DISPATCH_TILE = 1024
SPLIT_ROWS = 256
STATE_TILES = 1
VMEM_LIMIT = 56 * 1024 * 1024

_PAIR_AB = [(a, b) for a in range(GROUP_SIZE) for b in range(a + 1, GROUP_SIZE)]
_CLASS_A = np.array([g * GROUP_SIZE + a for g in range(N_EXPERT_GROUPS) for a, _ in _PAIR_AB], np.int32)
_CLASS_B = np.array([g * GROUP_SIZE + b for g in range(N_EXPERT_GROUPS) for _, b in _PAIR_AB], np.int32)


def _layer_norm(x, g, b):
    mu = jnp.mean(x, axis=-1, keepdims=True)
    xc = x - mu
    var = jnp.mean(xc * xc, axis=-1, keepdims=True)
    return xc * lax.rsqrt(var + LN_EPS) * g + b


def _bdot(a, w):
    return jnp.dot(a.astype(BF16), w, preferred_element_type=F32)


def _split(a):
    hi = a.astype(BF16)
    return hi, (a - hi.astype(F32)).astype(BF16)


def _dot3(a, w_hi, w_lo):
    a_hi, a_lo = _split(a)
    return (jnp.dot(a_hi, w_hi, preferred_element_type=F32) + jnp.dot(a_lo, w_hi, preferred_element_type=F32)
            + jnp.dot(a_hi, w_lo, preferred_element_type=F32))


def _params(sem):
    return pltpu.CompilerParams(dimension_semantics=sem, vmem_limit_bytes=VMEM_LIMIT)


def _full(shape):
    return pl.BlockSpec(shape, lambda *_: (0,) * len(shape), pipeline_mode=pl.Buffered(1))


def _row2(v):
    return v.reshape(1, -1)


def _split_weight_kernel(w_ref, hi_ref, lo_ref):
    hi, lo = _split(w_ref[...])
    hi_ref[...] = hi
    lo_ref[...] = lo


def _split_weight(w):
    rows, cols = w.shape
    blk = min(rows, SPLIT_ROWS)
    spec = pl.BlockSpec((blk, cols), lambda i: (i, 0))
    return pl.pallas_call(
        _split_weight_kernel,
        grid=(rows // blk,),
        in_specs=[spec],
        out_specs=[spec, spec],
        out_shape=[jax.ShapeDtypeStruct(w.shape, BF16)] * 2,
        compiler_params=_params(("arbitrary",)),
        name="split_weight",
    )(w)


def _proj_kernel(x_ref, w_ref, o_ref, o16_ref):
    res = _bdot(x_ref[...], w_ref[0])
    o_ref[0] = res
    o16_ref[0] = res.astype(BF16)


def _mem_projection(mem, w):
    n, rows = w.shape[0], mem.shape[0]
    out = pl.BlockSpec((1, rows, D_MODEL), lambda j: (j, 0, 0))
    return pl.pallas_call(
        _proj_kernel,
        grid=(n,),
        in_specs=[pl.BlockSpec((rows, D_MODEL), lambda j: (0, 0)),
                  pl.BlockSpec((1, D_MODEL, D_MODEL), lambda j: (j, 0, 0))],
        out_specs=[out, out],
        out_shape=[jax.ShapeDtypeStruct((n, rows, D_MODEL), F32), jax.ShapeDtypeStruct((n, rows, D_MODEL), BF16)],
        compiler_params=_params(("arbitrary",)),
        name="mem_projection",
    )(mem, w)


def _load_history(ext_ref, buf_ref, first, hist, keep, seg):
    @pl.when(first)
    def _():
        ext_ref[:, hist - keep:hist, :] = buf_ref[...]

    @pl.when(jnp.logical_not(first))
    def _():
        ext_ref[:, hist - keep:hist, :] = ext_ref[:, seg + hist - keep:seg + hist, :]


def _depthwise_conv(ext_ref, w_ref, out_ref, *, n_seq, seg, taps, hist, shifted_ref=None):
    rc = min(CONV_ROWS, seg)
    off0 = hist - (taps - 1)
    length = hist + seg
    if shifted_ref is not None:
        for r in range(1, SUBLANES):
            shifted_ref[r - 1, :, 0:length - SUBLANES, :] = ext_ref[:, r:r + length - SUBLANES, :]
    for s in range(n_seq):
        for r0 in range(0, seg, rc):
            for lb in range(0, D_HALF, LANES):
                acc = None
                for k in range(taps):
                    lo = off0 + k + r0
                    shift = (off0 + k) % SUBLANES
                    if shifted_ref is None or shift == 0:
                        win = ext_ref[s, lo:lo + rc, lb:lb + LANES]
                    else:
                        win = shifted_ref[shift - 1, s, lo - shift:lo - shift + rc, lb:lb + LANES]
                    term = w_ref[k:k + 1, lb:lb + LANES] * win
                    acc = term if acc is None else acc + term
                out_ref[s * seg + r0:s * seg + r0 + rc, lb:lb + LANES] = acc


def _mixer_ab_kernel(x_ref, bufa_ref, bufb_ref, w_in_ref, w_in_lo_ref, caw_ref, cab_ref, nag_ref, nab_ref, cbw_ref,
                     w_out_ref, w_out_lo_ref, lng_ref, lnb_ref,
                     y_ref, nbufa_ref, nbufb_ref,
                     h_scr, y_scr, a_ext, a_shift, cb_ext, conv_a, conv_b, *, n_seq, seg, precise_tiles):
    first = pl.program_id(1) == 0
    x = x_ref[...]

    def project(compute3, compute1, out_ref):
        if precise_tiles == 0:
            out_ref[...] = compute1()
            return
        is_precise = pl.program_id(1) >= pl.num_programs(1) - precise_tiles

        @pl.when(is_precise)
        def _():
            out_ref[...] = compute3()

        @pl.when(jnp.logical_not(is_precise))
        def _():
            out_ref[...] = compute1()

    project(lambda: _dot3(x, w_in_ref[...], w_in_lo_ref[...]), lambda: _bdot(x, w_in_ref[...]), h_scr)
    a = h_scr[:, 0:D_HALF] * jax.nn.sigmoid(h_scr[:, D_HALF:2 * D_HALF])
    cb = h_scr[:, 3 * D_HALF:4 * D_HALF] * h_scr[:, 4 * D_HALF:5 * D_HALF]

    _load_history(a_ext, bufa_ref, first, HIST_A, CONV_A - 1, seg)
    _load_history(cb_ext, bufb_ref, first, HIST_B, CONV_B - 1, seg)
    a_ext[:, HIST_A:HIST_A + seg, :] = a.reshape(n_seq, seg, D_HALF)
    cb_ext[:, HIST_B:HIST_B + seg, :] = cb.reshape(n_seq, seg, D_HALF)
    nbufa_ref[...] = a_ext[:, seg + HIST_A - (CONV_A - 1):seg + HIST_A, :]
    nbufb_ref[...] = cb_ext[:, seg + HIST_B - (CONV_B - 1):seg + HIST_B, :]

    _depthwise_conv(a_ext, caw_ref, conv_a, n_seq=n_seq, seg=seg, taps=CONV_A, hist=HIST_A, shifted_ref=a_shift)
    _depthwise_conv(cb_ext, cbw_ref, conv_b, n_seq=n_seq, seg=seg, taps=CONV_B, hist=HIST_B)

    a2 = _layer_norm(conv_a[...] + cab_ref[...], nag_ref[...], nab_ref[...])
    a2 = a2 * jax.nn.sigmoid(a2)
    b2 = h_scr[:, 2 * D_HALF:3 * D_HALF] * conv_b[...]
    project(lambda: (_dot3(a2, w_out_ref[0:D_HALF, :], w_out_lo_ref[0:D_HALF, :])
                     + _dot3(b2, w_out_ref[D_HALF:D_MODEL, :], w_out_lo_ref[D_HALF:D_MODEL, :])),
            lambda: _bdot(a2, w_out_ref[0:D_HALF, :]) + _bdot(b2, w_out_ref[D_HALF:D_MODEL, :]), y_scr)
    y_ref[...] = _layer_norm(ALPHA * x + y_scr[...], lng_ref[...], lnb_ref[...])


def _mixer_ab(x, buf_a, buf_b, w_in, w_in_lo, caw, cab, nag, nab, cbw, w_out, w_out_lo, lng, lnb, *,
              n_seq, seg, precise_tiles):
    batch = buf_a.shape[0]
    length = x.shape[0] // batch
    d_in = w_in.shape[1]
    n_l = length // seg
    rows = n_seq * seg
    lo_in = _full((D_MODEL, d_in)) if precise_tiles else _full((SUBLANES, LANES))
    lo_out = _full((D_MODEL, D_MODEL)) if precise_tiles else _full((SUBLANES, LANES))
    return pl.pallas_call(
        functools.partial(_mixer_ab_kernel, n_seq=n_seq, seg=seg, precise_tiles=precise_tiles),
        grid=(batch // n_seq, n_l),
        in_specs=[pl.BlockSpec((rows, D_MODEL), lambda b, l: (b * n_l + l, 0)),
                  pl.BlockSpec((n_seq, CONV_A - 1, D_HALF), lambda b, l: (b, 0, 0)),
                  pl.BlockSpec((n_seq, CONV_B - 1, D_HALF), lambda b, l: (b, 0, 0)),
                  _full((D_MODEL, d_in)), lo_in, _full((CONV_A, D_HALF)), _full((1, D_HALF)), _full((1, D_HALF)),
                  _full((1, D_HALF)), _full((CONV_B, D_HALF)), _full((D_MODEL, D_MODEL)), lo_out,
                  _full((1, D_MODEL)), _full((1, D_MODEL))],
        out_specs=[pl.BlockSpec((rows, D_MODEL), lambda b, l: (b * n_l + l, 0)),
                   pl.BlockSpec((n_seq, CONV_A - 1, D_HALF), lambda b, l: (b, 0, 0)),
                   pl.BlockSpec((n_seq, CONV_B - 1, D_HALF), lambda b, l: (b, 0, 0))],
        out_shape=[jax.ShapeDtypeStruct((batch * length, D_MODEL), F32),
                   jax.ShapeDtypeStruct((batch, CONV_A - 1, D_HALF), F32),
                   jax.ShapeDtypeStruct((batch, CONV_B - 1, D_HALF), F32)],
        scratch_shapes=[pltpu.VMEM((rows, d_in), F32),
                        pltpu.VMEM((rows, D_MODEL), F32),
                        pltpu.VMEM((n_seq, HIST_A + seg, D_HALF), F32),
                        pltpu.VMEM((SUBLANES - 1, n_seq, HIST_A + seg - SUBLANES, D_HALF), F32),
                        pltpu.VMEM((n_seq, HIST_B + seg, D_HALF), F32),
                        pltpu.VMEM((rows, D_HALF), F32),
                        pltpu.VMEM((rows, D_HALF), F32)],
        compiler_params=_params(("arbitrary", "arbitrary")),
        name="mixer_ab",
    )(x, buf_a, buf_b, w_in, w_in_lo, caw, _row2(cab), _row2(nag), _row2(nab), cbw, w_out, w_out_lo,
      _row2(lng), _row2(lnb))


def _mixer_cd_kernel(x_ref, bufp_ref, w_in_ref, pw_ref, ps_ref, vg_ref, vb_ref, ws_ref, bs_ref, w_out_ref,
                     lng_ref, lnb_ref, *refs, n_seq, seg, pos0, n_mix, emit_v):
    if emit_v:
        y_ref, nbufp_ref, v_ref, c_ext, pooled, mixed = refs
    else:
        y_ref, nbufp_ref, c_ext, pooled, mixed = refs
    rows = n_seq * seg
    li = pl.program_id(1)
    x = x_ref[...]
    h = _bdot(x, w_in_ref[...])
    c_in = h[:, 0:D_HALF]

    _load_history(c_ext, bufp_ref, li == 0, HIST_P, POOL_BUF, seg)
    c_ext[:, HIST_P:HIST_P + seg, :] = c_in.reshape(n_seq, seg, D_HALF)
    nbufp_ref[...] = c_ext[:, seg + HIST_P - POOL_BUF:seg + HIST_P, :]

    rc = min(CONV_ROWS, seg)
    for g, win in enumerate(POOL_WINDOWS):
        lanes = slice(g * LANES, (g + 1) * LANES)
        for s in range(n_seq):
            for r0 in range(0, seg, rc):
                cur = c_ext[s, HIST_P + r0:HIST_P + r0 + rc, lanes]
                acc = cur
                for j in range(1, win):
                    acc = acc + c_ext[s, HIST_P + r0 - j:HIST_P + r0 - j + rc, lanes]
                pos = pos0 + li * seg + r0 + lax.broadcasted_iota(jnp.int32, (rc, LANES), 0)
                cnt = jnp.minimum(pos + 1, win).astype(F32)
                pooled[s * seg + r0:s * seg + r0 + rc, lanes] = acc / cnt - cur
    c = _bdot(pooled[...], pw_ref[...]) * ps_ref[...]

    z = jax.nn.gelu(h[:, D_HALF:3 * D_HALF], approximate=True)
    u = z[:, 0:D_HALF]
    v = _layer_norm(z[:, D_HALF:2 * D_HALF], vg_ref[...], vb_ref[...])
    if emit_v:
        v_ref[...] = v.reshape(n_seq, seg, D_HALF)
    vb16 = v.astype(BF16)

    ri = lax.broadcasted_iota(jnp.int32, (n_mix, n_mix), 0) >> CHUNK_SHIFT
    ci = lax.broadcasted_iota(jnp.int32, (n_mix, n_mix), 1) >> CHUNK_SHIFT
    for g in range(N_HEAD_D):
        lanes = slice(g * LANES, (g + 1) * LANES)
        ws = jnp.where(ci <= ri, ws_ref[g], 0.0).astype(BF16)
        bias = bs_ref[:, g:g + 1]
        for r0 in range(0, rows, n_mix):
            mixed[r0:r0 + n_mix, lanes] = jnp.dot(ws, vb16[r0:r0 + n_mix, lanes], preferred_element_type=F32) + bias
    d = u * mixed[...]
    y = _bdot(c, w_out_ref[0:D_HALF, :]) + _bdot(d, w_out_ref[D_HALF:D_MODEL, :])
    y_ref[...] = _layer_norm(ALPHA * x + y, lng_ref[...], lnb_ref[...])


def _mixer_cd(x, row_start, buf_p, length, w_in, pw, ps, vg, vb, ws, bs, w_out, lng, lnb, *,
              n_seq, seg, pos0, emit_v):
    batch = buf_p.shape[0]
    d_in = w_in.shape[1]
    n_mix = min(length, GMLP_CHUNK)
    assert seg % n_mix == 0
    n_l = length // seg
    rows = n_seq * seg
    blk0 = row_start // rows
    ws_n = ws[:, :n_mix, :n_mix]
    bs_t = bs[:, :n_mix].T
    out_specs = [pl.BlockSpec((rows, D_MODEL), lambda b, l: (b * n_l + l, 0)),
                 pl.BlockSpec((n_seq, POOL_BUF, D_HALF), lambda b, l: (b, 0, 0))]
    out_shape = [jax.ShapeDtypeStruct((batch * length, D_MODEL), F32),
                 jax.ShapeDtypeStruct((batch, POOL_BUF, D_HALF), F32)]
    if emit_v:
        out_specs.append(pl.BlockSpec((n_seq, seg, D_HALF), lambda b, l: (b, l, 0)))
        out_shape.append(jax.ShapeDtypeStruct((batch, length, D_HALF), F32))
    return pl.pallas_call(
        functools.partial(_mixer_cd_kernel, n_seq=n_seq, seg=seg, pos0=pos0, n_mix=n_mix, emit_v=emit_v),
        grid=(batch // n_seq, n_l),
        in_specs=[pl.BlockSpec((rows, D_MODEL), lambda b, l: (blk0 + b * n_l + l, 0)),
                  pl.BlockSpec((n_seq, POOL_BUF, D_HALF), lambda b, l: (b, 0, 0)),
                  _full((D_MODEL, d_in)), _full((D_HALF, D_HALF)), _full((1, D_HALF)), _full((1, D_HALF)),
                  _full((1, D_HALF)), _full((N_HEAD_D, n_mix, n_mix)), _full((n_mix, N_HEAD_D)),
                  _full((D_MODEL, D_MODEL)), _full((1, D_MODEL)), _full((1, D_MODEL))],
        out_specs=out_specs,
        out_shape=out_shape,
        scratch_shapes=[pltpu.VMEM((n_seq, HIST_P + seg, D_HALF), F32),
                        pltpu.VMEM((rows, D_HALF), F32),
                        pltpu.VMEM((rows, D_HALF), F32)],
        compiler_params=_params(("arbitrary", "arbitrary")),
        name="mixer_cd",
    )(x, buf_p, w_in, pw, _row2(ps), _row2(vg), _row2(vb), ws_n, bs_t, w_out, _row2(lng), _row2(lnb))


def _attn_router_kernel(xp_ref, xs_ref, kp_ref, vp_ref, ks_ref, vs_ref, wq_ref, wo_ref, lng_ref, lnb_ref, rw_ref, rb_ref,
                        rows_ref, meta_ref, counts_ref,
                        q_scr, o_scr, carry, *, n_prompt_tiles, sample_seg):
    rows = ROW_TILE
    step = pl.program_id(0)
    is_prompt = step < n_prompt_tiles
    x = jnp.where(is_prompt, xp_ref[...], xs_ref[...])
    q_scr[...] = (_bdot(x, wq_ref[0]) * (MEM_HEAD_DIM ** -0.5)).astype(BF16)

    def heads(k_ref, v_ref, n_seq, seg):
        for s in range(n_seq):
            for hd in range(MEM_HEADS):
                cols = slice(hd * MEM_HEAD_DIM, (hd + 1) * MEM_HEAD_DIM)
                sc = lax.dot_general(q_scr[s * seg:(s + 1) * seg, cols], k_ref[s, :, cols], (((1,), (1,)), ((), ())),
                                     preferred_element_type=F32)
                p = jnp.exp(sc - jnp.max(sc, axis=-1, keepdims=True))
                den = jnp.sum(p, axis=-1, keepdims=True)
                o_scr[s * seg:(s + 1) * seg, cols] = _bdot(p, v_ref[s, :, cols]) / den

    @pl.when(is_prompt)
    def _():
        heads(kp_ref, vp_ref, 1, rows)

    @pl.when(jnp.logical_not(is_prompt))
    def _():
        heads(ks_ref, vs_ref, rows // sample_seg, sample_seg)

    x2 = _layer_norm(ALPHA * x + _bdot(o_scr[...], wo_ref[0]), lng_ref[...], lnb_ref[...])
    rows_ref[:, 0:D_MODEL] = x2

    nt = (((1,), (1,)), ((), ()))
    x2_hi, x2_lo = _split(x2)
    both = lax.dot_general(rw_ref[...], x2_hi, nt, preferred_element_type=F32)
    cross = lax.dot_general(rw_ref[0:N_EXPERTS, :], x2_lo, nt, preferred_element_type=F32)
    logits = both[0:N_EXPERTS] + both[N_EXPERTS:2 * N_EXPERTS] + cross + rb_ref[...]
    e = jnp.exp(logits - jnp.max(logits, axis=0, keepdims=True))
    scores = e / jnp.sum(e, axis=0, keepdims=True)
    eid = lax.broadcasted_iota(jnp.int32, (N_EXPERTS, rows), 0)
    egrp = eid >> GROUP_SHIFT
    best = jnp.max(jnp.where(egrp == 0, scores, -1.0), axis=0, keepdims=True)
    g_sel = jnp.zeros((1, rows), jnp.int32)
    for g in range(1, N_EXPERT_GROUPS):
        gs = jnp.max(jnp.where(egrp == g, scores, -1.0), axis=0, keepdims=True)
        upd = gs > best
        g_sel = jnp.where(upd, g, g_sel)
        best = jnp.where(upd, gs, best)
    masked = jnp.where(egrp == g_sel, scores, -1.0)
    m1 = jnp.max(masked, axis=0, keepdims=True)
    i1 = jnp.min(jnp.where(masked == m1, eid, N_EXPERTS), axis=0, keepdims=True)
    masked2 = jnp.where(eid == i1, -2.0, masked)
    m2 = jnp.max(masked2, axis=0, keepdims=True)
    i2 = jnp.min(jnp.where(masked2 == m2, eid, N_EXPERTS), axis=0, keepdims=True)
    tot = m1 + m2
    g1 = m1 / tot
    g2 = m2 / tot
    first_low = i1 < i2
    ea = jnp.where(first_low, i1, i2) & (GROUP_SIZE - 1)
    eb = jnp.where(first_low, i2, i1) & (GROUP_SIZE - 1)
    gate_a = jnp.where(first_low, g1, g2)
    gate_b = jnp.where(first_low, g2, g1)
    pair = eb - 1 + jnp.where(ea == 1, 2, 0) + jnp.where(ea == 2, 3, 0)
    cls = g_sel * PAIRS_PER_GROUP + pair

    lane_row = lax.broadcasted_iota(jnp.int32, (LANES, rows), 0)
    gates_t = jnp.where(lane_row == 0, gate_a, jnp.where(lane_row == 1, gate_b, 0.0))
    rows_ref[:, D_MODEL:ROW_W] = gates_t.T

    @pl.when(step == 0)
    def _():
        carry[...] = jnp.zeros_like(carry)

    onehot = (lax.broadcasted_iota(jnp.int32, (CLASS_ROWS, rows), 0) == cls).astype(F32)
    tri = (lax.broadcasted_iota(jnp.int32, (rows, rows), 0) <= lax.broadcasted_iota(jnp.int32, (rows, rows), 1))
    cum = jnp.dot(onehot.astype(BF16), tri.astype(F32).astype(BF16), preferred_element_type=F32)
    before = carry[:, 0:1]
    rank = jnp.sum(onehot * (before + cum), axis=0, keepdims=True) - 1.0
    sub = lax.broadcasted_iota(jnp.int32, (SUBLANES, rows), 0)
    meta_ref[...] = jnp.where(sub == 0, cls, jnp.where(sub == 1, rank.astype(jnp.int32), 0))
    carry[...] = carry[...] + cum[:, rows - 1:rows]
    counts_ref[...] = carry[...].astype(jnp.int32)


def _attn_router(x_p, x_s, kv_p, k_s, v_s, wq, wo, lng, lnb, rw_t, rb, *, layer, prompt_len, sample_len):
    rows = ROW_TILE
    n_p = x_p.shape[0] // rows
    n_s = x_s.shape[0] // rows
    tiles_per_seq = prompt_len // rows
    seq_per_tile = rows // sample_len
    total = x_p.shape[0] + x_s.shape[0]
    n_prompt_seq = n_p // tiles_per_seq
    kp0 = layer * n_prompt_seq
    vp0 = (DEPTH + layer) * n_prompt_seq
    s0 = layer * n_s
    p_idx = lambda i: jnp.minimum(i, n_p - 1)
    s_idx = lambda i: jnp.maximum(i - n_p, 0)
    layer_w = pl.BlockSpec((1, D_MODEL, D_MODEL), lambda i: (layer, 0, 0), pipeline_mode=pl.Buffered(1))
    return pl.pallas_call(
        functools.partial(_attn_router_kernel, n_prompt_tiles=n_p, sample_seg=sample_len),
        grid=(n_p + n_s,),
        in_specs=[pl.BlockSpec((rows, D_MODEL), lambda i: (p_idx(i), 0)),
                  pl.BlockSpec((rows, D_MODEL), lambda i: (s_idx(i), 0)),
                  pl.BlockSpec((1, N_MEM, D_MODEL), lambda i: (kp0 + p_idx(i) // tiles_per_seq, 0, 0)),
                  pl.BlockSpec((1, N_MEM, D_MODEL), lambda i: (vp0 + p_idx(i) // tiles_per_seq, 0, 0)),
                  pl.BlockSpec((seq_per_tile, N_MEM, D_MODEL), lambda i: (s0 + s_idx(i), 0, 0)),
                  pl.BlockSpec((seq_per_tile, N_MEM, D_MODEL), lambda i: (s0 + s_idx(i), 0, 0)),
                  layer_w, layer_w, _full((1, D_MODEL)), _full((1, D_MODEL)),
                  _full((2 * N_EXPERTS, D_MODEL)), _full((N_EXPERTS, 1))],
        out_specs=[pl.BlockSpec((rows, ROW_W), lambda i: (i, 0)),
                   pl.BlockSpec((SUBLANES, rows), lambda i: (0, i)),
                   pl.BlockSpec((CLASS_ROWS, LANES), lambda i: (0, 0))],
        out_shape=[jax.ShapeDtypeStruct((total, ROW_W), F32),
                   jax.ShapeDtypeStruct((SUBLANES, total), jnp.int32),
                   jax.ShapeDtypeStruct((CLASS_ROWS, LANES), jnp.int32)],
        scratch_shapes=[pltpu.VMEM((rows, D_MODEL), BF16),
                        pltpu.VMEM((rows, D_MODEL), F32),
                        pltpu.VMEM((CLASS_ROWS, LANES), F32)],
        compiler_params=_params(("arbitrary",)),
        name="attn_router",
    )(x_p, x_s, kv_p, kv_p, k_s, v_s, wq, wo, _row2(lng), _row2(lnb), rw_t, rb.reshape(N_EXPERTS, 1))


def _dispatch_kernel(cls_ref, rank_ref, offs_ref, cnt_ref, end_ref, used_ref, rows_ref, xs_hbm, tok_ref,
                     zeros, sem, zsem, *, tile, n_tiles):
    step = pl.program_id(0)

    def class_padding(c, go):
        start = offs_ref[c] + cnt_ref[c]
        head = (-start) & (SUBLANES - 1)
        bulk_start = pl.multiple_of(start + head, SUBLANES)
        bulk = pl.multiple_of(end_ref[c] - bulk_start, SUBLANES)

        def one(r, carry_):
            go(pltpu.make_async_copy(zeros.at[pl.ds(0, 1)], xs_hbm.at[pl.ds(r, 1)], zsem))
            return carry_
        lax.fori_loop(start, start + head, one, 0)

        @pl.when(bulk > 0)
        def _():
            go(pltpu.make_async_copy(zeros.at[pl.ds(0, bulk)], xs_hbm.at[pl.ds(bulk_start, bulk)], zsem))

    def unused_tiles(go):
        def one(j, carry_):
            go(pltpu.make_async_copy(zeros, xs_hbm.at[pl.ds(pl.multiple_of(j * MOE_TILE, MOE_TILE), MOE_TILE)], zsem))
            return carry_
        lax.fori_loop(used_ref[0], n_tiles, one, 0)

    @pl.when(step == 0)
    def _():
        zeros[...] = jnp.zeros_like(zeros)

        def no_token(p, carry_):
            tok_ref[p] = -1
            return carry_
        for c in range(N_CLASSES):
            lax.fori_loop(offs_ref[c] + cnt_ref[c], end_ref[c], no_token, 0)
            class_padding(c, lambda copy: copy.start())
        lax.fori_loop(used_ref[0] * MOE_TILE, n_tiles * MOE_TILE, no_token, 0)
        unused_tiles(lambda copy: copy.start())

    for r in range(tile):
        t = step * tile + r
        p = offs_ref[cls_ref[t]] + rank_ref[t]
        tok_ref[p] = t
        pltpu.make_async_copy(rows_ref.at[pl.ds(r, 1)], xs_hbm.at[pl.ds(p, 1)], sem).start()
    pltpu.make_async_copy(rows_ref, xs_hbm.at[pl.ds(0, tile)], sem).wait()

    @pl.when(step == 0)
    def _():
        for c in range(N_CLASSES):
            class_padding(c, lambda copy: copy.wait())
        unused_tiles(lambda copy: copy.wait())


def _dispatch(cls, rank, offs, cnt, end, used, rows, n_tiles):
    n_rows = rows.shape[0]
    n_sorted = n_tiles * MOE_TILE
    any_spec = pl.BlockSpec(memory_space=pl.ANY)
    return pl.pallas_call(
        functools.partial(_dispatch_kernel, tile=DISPATCH_TILE, n_tiles=n_tiles),
        grid_spec=pltpu.PrefetchScalarGridSpec(
            num_scalar_prefetch=6, grid=(n_rows // DISPATCH_TILE,),
            in_specs=[pl.BlockSpec((DISPATCH_TILE, ROW_W), lambda i, *_: (i, 0))],
            out_specs=[any_spec, pl.BlockSpec(memory_space=pltpu.SMEM)],
            scratch_shapes=[pltpu.VMEM((MOE_TILE, ROW_W), F32), pltpu.SemaphoreType.DMA(()),
                            pltpu.SemaphoreType.DMA(())]),
        out_shape=[jax.ShapeDtypeStruct((n_sorted, ROW_W), F32), jax.ShapeDtypeStruct((n_sorted,), jnp.int32)],
        compiler_params=_params(("arbitrary",)),
        name="dispatch",
    )(cls, rank, offs, cnt, end, used, rows)


def _experts_kernel(ea_ref, eb_ref, blk_ref, nvalid_ref, tok_ref, xs_ref, wga_ref, wua_ref, wda_ref, wgb_ref, wub_ref,
                    wdb_ref, lng_ref, lnb_ref, out_hbm, ybuf, sems, *, n_rows):
    del ea_ref, eb_ref, blk_ref
    i = pl.program_id(0)
    n_cur = nvalid_ref[i]
    n_prev = jnp.where(i >= 1, nvalid_ref[jnp.maximum(i - 1, 0)], 0)
    n_prev2 = jnp.where(i >= 2, nvalid_ref[jnp.maximum(i - 2, 0)], 0)
    has_cur = n_cur > 0
    has_prev = n_prev > 0

    def tile_copy(slot):
        return pltpu.make_async_copy(ybuf.at[slot], out_hbm.at[pl.ds(0, MOE_TILE)], sems.at[slot])

    def compute(slot):
        x = xs_ref[:, 0:D_MODEL]
        xb = x.astype(BF16)
        y = None
        for lane, (wg, wu, wd) in enumerate(((wga_ref, wua_ref, wda_ref), (wgb_ref, wub_ref, wdb_ref))):
            gate = xs_ref[:, D_MODEL + lane:D_MODEL + lane + 1]
            hg = jnp.dot(xb, wg[0], preferred_element_type=F32)
            hu = jnp.dot(xb, wu[0], preferred_element_type=F32)
            hid = hg * jax.nn.sigmoid(hg) * hu * gate
            part = _bdot(hid, wd[0])
            y = part if y is None else y + part
        ybuf[slot] = _layer_norm(ALPHA * x + y, lng_ref[...], lnb_ref[...])

    def start_rows(tile, slot):
        base = tile * MOE_TILE
        spare = n_rows + (tile & 1) * MOE_TILE
        for r in range(MOE_TILE):
            t = tok_ref[base + r]
            dst = jnp.where(t < 0, spare + r, t)
            pltpu.make_async_copy(ybuf.at[slot, pl.ds(r, 1)], out_hbm.at[pl.ds(dst, 1)], sems.at[slot]).start()

    @pl.when(i == 0)
    def _():
        ybuf[2] = jnp.zeros((MOE_TILE, D_MODEL), F32)
        for half in range(2):
            fill = pltpu.make_async_copy(ybuf.at[2], out_hbm.at[pl.ds(n_rows + half * MOE_TILE, MOE_TILE)], sems.at[2])
            fill.start()
            fill.wait()

    slot_cur = i % 3
    slot_prev = (i + 2) % 3

    @pl.when(has_prev & has_cur)
    def _():
        start_rows(i - 1, slot_prev)
        compute(slot_cur)

    @pl.when(has_prev & jnp.logical_not(has_cur))
    def _():
        start_rows(i - 1, slot_prev)

    @pl.when(jnp.logical_not(has_prev) & has_cur)
    def _():
        compute(slot_cur)

    @pl.when(n_prev2 > 0)
    def _():
        tile_copy((i + 1) % 3).wait()


def _experts(tile_a, tile_b, tile_blk, tile_nvalid, tok, xs, w_gate, w_up, w_down, lng, lnb, n_rows, layer):
    n_steps = tile_nvalid.shape[0]
    e0 = layer * N_EXPERTS
    up_a = pl.BlockSpec((1, D_MODEL, D_EXPERT), lambda i, ea, eb, *_: (e0 + ea[i], 0, 0))
    up_b = pl.BlockSpec((1, D_MODEL, D_EXPERT), lambda i, ea, eb, *_: (e0 + eb[i], 0, 0))
    dn_a = pl.BlockSpec((1, D_EXPERT, D_MODEL), lambda i, ea, eb, *_: (e0 + ea[i], 0, 0))
    dn_b = pl.BlockSpec((1, D_EXPERT, D_MODEL), lambda i, ea, eb, *_: (e0 + eb[i], 0, 0))
    vec = pl.BlockSpec((1, D_MODEL), lambda i, *_: (0, 0))
    return pl.pallas_call(
        functools.partial(_experts_kernel, n_rows=n_rows),
        grid_spec=pltpu.PrefetchScalarGridSpec(
            num_scalar_prefetch=5, grid=(n_steps,),
            in_specs=[pl.BlockSpec((MOE_TILE, ROW_W), lambda i, ea, eb, blk, *_: (blk[i], 0)),
                      up_a, up_a, dn_a, up_b, up_b, dn_b, vec, vec],
            out_specs=pl.BlockSpec(memory_space=pl.ANY),
            scratch_shapes=[pltpu.VMEM((3, MOE_TILE, D_MODEL), F32), pltpu.SemaphoreType.DMA((3,))]),
        out_shape=jax.ShapeDtypeStruct((n_rows + 2 * MOE_TILE, D_MODEL), F32),
        compiler_params=_params(("arbitrary",)),
        name="experts",
    )(tile_a, tile_b, tile_blk, tile_nvalid, tok, xs, w_gate, w_up, w_down, w_gate, w_up, w_down,
      _row2(lng), _row2(lnb))


def _lookup(table, idx):
    pick = idx[:, None] == jnp.arange(table.shape[0], dtype=jnp.int32)[None, :]
    return jnp.sum(jnp.where(pick, table[None, :], 0), axis=1)


def _moe(rows, meta, counts, w_gate, w_up, w_down, lng, lnb, layer):
    n_rows = rows.shape[0]
    n_tiles = n_rows // MOE_TILE + N_CLASSES
    cnt = counts[:N_CLASSES, 0]
    tiles_per = (cnt + MOE_TILE - 1) // MOE_TILE
    cls_id = jnp.arange(N_CLASSES, dtype=jnp.int32)
    tile_end = jnp.sum(jnp.where(cls_id[None, :] <= cls_id[:, None], tiles_per[None, :], 0), axis=1)
    tile_start = tile_end - tiles_per
    offs = tile_start * MOE_TILE
    end = tile_end * MOE_TILE
    used = tile_end[N_CLASSES - 1:N_CLASSES]
    tile_id = jnp.arange(n_tiles + 2, dtype=jnp.int32)
    tile_blk = jnp.minimum(tile_id, used - 1)
    tile_cls = jnp.sum((tile_blk[:, None] >= tile_end[None, :]).astype(jnp.int32), axis=1)
    tile_a = _lookup(jnp.asarray(_CLASS_A), tile_cls)
    tile_b = _lookup(jnp.asarray(_CLASS_B), tile_cls)
    tile_nvalid = jnp.clip(_lookup(cnt, tile_cls) - (tile_id - _lookup(tile_start, tile_cls)) * MOE_TILE, 0, MOE_TILE)
    xs, tok = _dispatch(meta[0], meta[1], offs, cnt, end, used, rows, n_tiles)
    return _experts(tile_a, tile_b, tile_blk, tile_nvalid, tok, xs, w_gate, w_up, w_down, lng, lnb, n_rows, layer)


def kernel(x_prompt, x_sample, mem_prompt, cache_mem_k, cache_mem_v, state_conv_a, state_conv_b, state_pool,
           ln_g, ln_b, ab_w_in, ab_conv_a_w, ab_conv_a_b, ab_norm_a_g, ab_norm_a_b, ab_conv_b_w, ab_w_out,
           cd_w_in, cd_pool_w, cd_pool_scale, cd_v_norm_g, cd_v_norm_b, cd_w_s, cd_b_s, cd_w_out,
           ca_wq, ca_wk, ca_wv, ca_wo, router_w, router_b, moe_w_gate, moe_w_up, moe_w_down):
    assert ln_g.shape[0] == DEPTH and ab_w_in.shape[0] == 1 and cd_w_in.shape[0] == 1
    bsz, seq, _ = x_prompt.shape
    dec_b, dec_seq, _ = x_sample.shape
    n_prompt = bsz * seq
    n_sample = dec_b * dec_seq
    sample_seqs = ROW_TILE // dec_seq

    n_pool = len(POOL_WINDOWS)
    pool_c = D_HALF // n_pool
    pool_bd = jnp.zeros((D_HALF, D_HALF), F32)
    for g in range(n_pool):
        pool_bd = pool_bd.at[g * pool_c:(g + 1) * pool_c, g * pool_c:(g + 1) * pool_c].set(cd_pool_w[0, g])
    pool_bd = pool_bd.astype(BF16)
    ab_w_in_hi, ab_w_in_lo = _split_weight(ab_w_in[0])
    ab_w_out_hi, ab_w_out_lo = _split_weight(ab_w_out[0])
    rw_hi, rw_lo = _split_weight(router_w.T)
    rw_t = jnp.concatenate([rw_hi, rw_lo], axis=0)
    cd_w_in_b, cd_w_out_b = cd_w_in[0].astype(BF16), cd_w_out[0].astype(BF16)
    wq_b, wo_b = ca_wq.astype(BF16), ca_wo.astype(BF16)
    wg_b = moe_w_gate.astype(BF16).reshape(DEPTH * N_EXPERTS, D_MODEL, D_EXPERT)
    wu_b = moe_w_up.astype(BF16).reshape(DEPTH * N_EXPERTS, D_MODEL, D_EXPERT)
    wd_b = moe_w_down.astype(BF16).reshape(DEPTH * N_EXPERTS, D_EXPERT, D_MODEL)

    kv, kv16 = _mem_projection(mem_prompt.reshape(bsz * N_MEM, D_MODEL),
                               jnp.concatenate([ca_wk, ca_wv], axis=0).astype(BF16))
    kv_shape = (DEPTH, bsz, N_MEM, MEM_HEADS, MEM_HEAD_DIM)
    mem_k_prompt, mem_v_prompt = kv[0:DEPTH].reshape(kv_shape), kv[DEPTH:2 * DEPTH].reshape(kv_shape)
    kv16 = kv16.reshape(2 * DEPTH * bsz, N_MEM, D_MODEL)
    k_s = cache_mem_k.astype(BF16).reshape(DEPTH * dec_b, N_MEM, D_MODEL)
    v_s = cache_mem_v.astype(BF16).reshape(DEPTH * dec_b, N_MEM, D_MODEL)

    def mixer_ab(x, buf_a, buf_b, n_seq, seg, precise_tiles):
        return _mixer_ab(x, buf_a, buf_b, ab_w_in_hi, ab_w_in_lo, ab_conv_a_w[0], ab_conv_a_b[0], ab_norm_a_g[0],
                         ab_norm_a_b[0], ab_conv_b_w[0], ab_w_out_hi, ab_w_out_lo, ln_g[0, 0], ln_b[0, 0],
                         n_seq=n_seq, seg=seg, precise_tiles=precise_tiles)

    def mixer_cd(x, row_start, buf_p, length, n_seq, seg, pos0, emit_v):
        return _mixer_cd(x, row_start, buf_p, length, cd_w_in_b, pool_bd, cd_pool_scale[0], cd_v_norm_g[0],
                         cd_v_norm_b[0], cd_w_s[0], cd_b_s[0], cd_w_out_b, ln_g[1, 0], ln_b[1, 0],
                         n_seq=n_seq, seg=seg, pos0=pos0, emit_v=emit_v)

    def attn_moe(layer, x_p, x_s):
        rows, meta, counts = _attn_router(x_p, x_s, kv16, k_s, v_s, wq_b, wo_b, ln_g[layer, 1], ln_b[layer, 1],
                                          rw_t, router_b, layer=layer, prompt_len=seq, sample_len=dec_seq)
        return _moe(rows, meta, counts, wg_b, wu_b, wd_b, ln_g[layer, 2], ln_b[layer, 2], layer)

    x_p, conv_a_p, conv_b_p = mixer_ab(x_prompt.reshape(n_prompt, D_MODEL),
                                       jnp.zeros((bsz, CONV_A - 1, D_HALF), F32),
                                       jnp.zeros((bsz, CONV_B - 1, D_HALF), F32), 1, ROW_TILE, STATE_TILES)
    x_s, conv_a_s, conv_b_s = mixer_ab(x_sample.reshape(n_sample, D_MODEL), state_conv_a[0], state_conv_b[0],
                                       sample_seqs, dec_seq, 0)
    x_all = attn_moe(0, x_p, x_s)

    x_p, pool_p = mixer_cd(x_all, 0, jnp.zeros((bsz, POOL_BUF, D_HALF), F32), seq, 1, ROW_TILE, 0, False)
    x_s, pool_s, v_s_rows = mixer_cd(x_all, n_prompt, state_pool[0], dec_seq, sample_seqs, dec_seq, PAST_LEN, True)
    x_all = attn_moe(1, x_p, x_s)

    y_prompt = x_all[0:n_prompt].reshape(bsz, seq, D_MODEL)
    y_sample = x_all[n_prompt:n_prompt + n_sample].reshape(dec_b, dec_seq, D_MODEL)
    return (y_prompt, y_sample, mem_k_prompt, mem_v_prompt, conv_a_p[None], conv_b_p[None], pool_p[None],
            conv_a_s[None], conv_b_s[None], pool_s[None], v_s_rows[None])
```

```python
import functools

import jax
import jax.numpy as jnp
import numpy as np
from jax import lax
from jax.experimental import pallas as pl
from jax.experimental.pallas import tpu as pltpu

F32 = jnp.float32
BF16 = jnp.bfloat16

D_MODEL = 1024
D_HALF = D_MODEL // 2
DEPTH = 2
PAST_LEN = 4096
CHUNK = 64
CHUNK_SHIFT = CHUNK.bit_length() - 1
CONV_A = 31
CONV_B = 3
POOL_WINDOWS = (2, 4, 8, 16)
POOL_BUF = max(POOL_WINDOWS) - 1
N_HEAD_D = 4
GMLP_CHUNK = 128
N_MEM = 256
MEM_HEADS = 4
MEM_HEAD_DIM = D_MODEL // MEM_HEADS
N_EXPERTS = 16
N_EXPERT_GROUPS = 4
GROUP_SIZE = N_EXPERTS // N_EXPERT_GROUPS
GROUP_SHIFT = GROUP_SIZE.bit_length() - 1
PAIRS_PER_GROUP = GROUP_SIZE * (GROUP_SIZE - 1) // 2
N_CLASSES = N_EXPERT_GROUPS * PAIRS_PER_GROUP
D_EXPERT = D_MODEL // 2
ALPHA = (2 * DEPTH) ** 0.25
LN_EPS = 1e-5

LANES = 128
SUBLANES = 8
ROW_W = D_MODEL + LANES
HIST_A = 32
HIST_B = 8
HIST_P = 16
CONV_ROWS = 64
ROW_TILE = 512
MOE_TILE = 256
CLASS_ROWS = 32
DISPATCH_TILE = 1024
SPLIT_ROWS = 256
STATE_TILES = 1
VMEM_LIMIT = 56 * 1024 * 1024

_PAIR_AB = [(a, b) for a in range(GROUP_SIZE) for b in range(a + 1, GROUP_SIZE)]
_CLASS_A = np.array([g * GROUP_SIZE + a for g in range(N_EXPERT_GROUPS) for a, _ in _PAIR_AB], np.int32)
_CLASS_B = np.array([g * GROUP_SIZE + b for g in range(N_EXPERT_GROUPS) for _, b in _PAIR_AB], np.int32)


def _layer_norm(x, g, b):
    mu = jnp.mean(x, axis=-1, keepdims=True)
    xc = x - mu
    var = jnp.mean(xc * xc, axis=-1, keepdims=True)
    return xc * lax.rsqrt(var + LN_EPS) * g + b


def _bdot(a, w):
    return jnp.dot(a.astype(BF16), w, preferred_element_type=F32)


def _split(a):
    hi = a.astype(BF16)
    return hi, (a - hi.astype(F32)).astype(BF16)


def _dot3(a, w_hi, w_lo):
    a_hi, a_lo = _split(a)
    return (jnp.dot(a_hi, w_hi, preferred_element_type=F32) + jnp.dot(a_lo, w_hi, preferred_element_type=F32)
            + jnp.dot(a_hi, w_lo, preferred_element_type=F32))


def _params(sem):
    return pltpu.CompilerParams(dimension_semantics=sem, vmem_limit_bytes=VMEM_LIMIT)


def _full(shape):
    return pl.BlockSpec(shape, lambda *_: (0,) * len(shape), pipeline_mode=pl.Buffered(1))


def _row2(v):
    return v.reshape(1, -1)


def _split_weight_kernel(w_ref, hi_ref, lo_ref):
    hi, lo = _split(w_ref[...])
    hi_ref[...] = hi
    lo_ref[...] = lo


def _split_weight(w):
    rows, cols = w.shape
    blk = min(rows, SPLIT_ROWS)
    spec = pl.BlockSpec((blk, cols), lambda i: (i, 0))
    return pl.pallas_call(
        _split_weight_kernel,
        grid=(rows // blk,),
        in_specs=[spec],
        out_specs=[spec, spec],
        out_shape=[jax.ShapeDtypeStruct(w.shape, BF16)] * 2,
        compiler_params=_params(("arbitrary",)),
        name="split_weight",
    )(w)


def _proj_kernel(x_ref, w_ref, o_ref, o16_ref):
    res = _bdot(x_ref[...], w_ref[0])
    o_ref[0] = res
    o16_ref[0] = res.astype(BF16)


def _mem_projection(mem, w):
    n, rows = w.shape[0], mem.shape[0]
    out = pl.BlockSpec((1, rows, D_MODEL), lambda j: (j, 0, 0))
    return pl.pallas_call(
        _proj_kernel,
        grid=(n,),
        in_specs=[pl.BlockSpec((rows, D_MODEL), lambda j: (0, 0)),
                  pl.BlockSpec((1, D_MODEL, D_MODEL), lambda j: (j, 0, 0))],
        out_specs=[out, out],
        out_shape=[jax.ShapeDtypeStruct((n, rows, D_MODEL), F32), jax.ShapeDtypeStruct((n, rows, D_MODEL), BF16)],
        compiler_params=_params(("arbitrary",)),
        name="mem_projection",
    )(mem, w)


def _load_history(ext_ref, buf_ref, first, hist, keep, seg):
    @pl.when(first)
    def _():
        ext_ref[:, hist - keep:hist, :] = buf_ref[...]

    @pl.when(jnp.logical_not(first))
    def _():
        ext_ref[:, hist - keep:hist, :] = ext_ref[:, seg + hist - keep:seg + hist, :]


def _depthwise_conv(ext_ref, w_ref, out_ref, *, n_seq, seg, taps, hist, shifted_ref=None):
    rc = min(CONV_ROWS, seg)
    off0 = hist - (taps - 1)
    length = hist + seg
    if shifted_ref is not None:
        for r in range(1, SUBLANES):
            shifted_ref[r - 1, :, 0:length - SUBLANES, :] = ext_ref[:, r:r + length - SUBLANES, :]
    for s in range(n_seq):
        for r0 in range(0, seg, rc):
            for lb in range(0, D_HALF, LANES):
                acc = None
                for k in range(taps):
                    lo = off0 + k + r0
                    shift = (off0 + k) % SUBLANES
                    if shifted_ref is None or shift == 0:
                        win = ext_ref[s, lo:lo + rc, lb:lb + LANES]
                    else:
                        win = shifted_ref[shift - 1, s, lo - shift:lo - shift + rc, lb:lb + LANES]
                    term = w_ref[k:k + 1, lb:lb + LANES] * win
                    acc = term if acc is None else acc + term
                out_ref[s * seg + r0:s * seg + r0 + rc, lb:lb + LANES] = acc


def _mixer_ab_kernel(x_ref, bufa_ref, bufb_ref, w_in_ref, w_in_lo_ref, caw_ref, cab_ref, nag_ref, nab_ref, cbw_ref,
                     w_out_ref, w_out_lo_ref, lng_ref, lnb_ref, *refs, n_seq, seg, precise, skip_tail, has_prev):
    if has_prev:
        refs = refs[1:]
    y_ref, nbufa_ref, nbufb_ref, a_ext, a_shift, cb_ext, conv_a, conv_b = refs

    def body():
        first = pl.program_id(1) == 0
        _load_history(a_ext, bufa_ref, first, HIST_A, CONV_A - 1, seg)
        _load_history(cb_ext, bufb_ref, first, HIST_B, CONV_B - 1, seg)
        x = x_ref[...]
        h = _dot3(x, w_in_ref[...], w_in_lo_ref[...]) if precise else _bdot(x, w_in_ref[...])
        a = h[:, 0:D_HALF] * jax.nn.sigmoid(h[:, D_HALF:2 * D_HALF])
        cb = h[:, 3 * D_HALF:4 * D_HALF] * h[:, 4 * D_HALF:5 * D_HALF]
        a_ext[:, HIST_A:HIST_A + seg, :] = a.reshape(n_seq, seg, D_HALF)
        cb_ext[:, HIST_B:HIST_B + seg, :] = cb.reshape(n_seq, seg, D_HALF)
        nbufa_ref[...] = a_ext[:, seg + HIST_A - (CONV_A - 1):seg + HIST_A, :]
        nbufb_ref[...] = cb_ext[:, seg + HIST_B - (CONV_B - 1):seg + HIST_B, :]

        _depthwise_conv(a_ext, caw_ref, conv_a, n_seq=n_seq, seg=seg, taps=CONV_A, hist=HIST_A, shifted_ref=a_shift)
        _depthwise_conv(cb_ext, cbw_ref, conv_b, n_seq=n_seq, seg=seg, taps=CONV_B, hist=HIST_B)

        a2 = _layer_norm(conv_a[...] + cab_ref[...], nag_ref[...], nab_ref[...])
        a2 = a2 * jax.nn.sigmoid(a2)
        b2 = h[:, 2 * D_HALF:3 * D_HALF] * conv_b[...]
        if precise:
            y = (_dot3(a2, w_out_ref[0:D_HALF, :], w_out_lo_ref[0:D_HALF, :])
                 + _dot3(b2, w_out_ref[D_HALF:D_MODEL, :], w_out_lo_ref[D_HALF:D_MODEL, :]))
        else:
            y = _bdot(a2, w_out_ref[0:D_HALF, :]) + _bdot(b2, w_out_ref[D_HALF:D_MODEL, :])
        y_ref[...] = _layer_norm(ALPHA * x + y, lng_ref[...], lnb_ref[...])

    if skip_tail == 0:
        body()
    else:
        live = pl.program_id(1) < pl.num_programs(1) - skip_tail
        pl.when(live)(body)

        @pl.when(jnp.logical_not(live))
        def _():
            y_ref[...] = jnp.zeros_like(y_ref)


def _mixer_ab(x, buf_a, buf_b, w_in, w_in_lo, caw, cab, nag, nab, cbw, w_out, w_out_lo, lng, lnb, *,
              n_seq, seg, precise, tile_lo=0, tiles=None, skip_tail=0, y_prev=None):
    batch = buf_a.shape[0]
    length = x.shape[0] // batch
    d_in = w_in.shape[1]
    n_l = length // seg
    tiles = n_l if tiles is None else tiles
    rows = n_seq * seg
    lo_in = _full((D_MODEL, d_in)) if precise else _full((SUBLANES, LANES))
    lo_out = _full((D_MODEL, D_MODEL)) if precise else _full((SUBLANES, LANES))
    row_blk = pl.BlockSpec((rows, D_MODEL), lambda b, l: (b * n_l + tile_lo + l, 0))
    in_specs = [row_blk,
                pl.BlockSpec((n_seq, CONV_A - 1, D_HALF), lambda b, l: (b, 0, 0)),
                pl.BlockSpec((n_seq, CONV_B - 1, D_HALF), lambda b, l: (b, 0, 0)),
                _full((D_MODEL, d_in)), lo_in, _full((CONV_A, D_HALF)), _full((1, D_HALF)), _full((1, D_HALF)),
                _full((1, D_HALF)), _full((CONV_B, D_HALF)), _full((D_MODEL, D_MODEL)), lo_out,
                _full((1, D_MODEL)), _full((1, D_MODEL))]
    args = [x, buf_a, buf_b, w_in, w_in_lo, caw, _row2(cab), _row2(nag), _row2(nab), cbw, w_out, w_out_lo,
            _row2(lng), _row2(lnb)]
    aliases = {}
    if y_prev is not None:
        aliases = {len(args): 0}
        in_specs.append(pl.BlockSpec(memory_space=pl.ANY))
        args.append(y_prev)
    return pl.pallas_call(
        functools.partial(_mixer_ab_kernel, n_seq=n_seq, seg=seg, precise=precise, skip_tail=skip_tail,
                          has_prev=y_prev is not None),
        grid=(batch // n_seq, tiles),
        in_specs=in_specs,
        out_specs=[row_blk,
                   pl.BlockSpec((n_seq, CONV_A - 1, D_HALF), lambda b, l: (b, 0, 0)),
                   pl.BlockSpec((n_seq, CONV_B - 1, D_HALF), lambda b, l: (b, 0, 0))],
        out_shape=[jax.ShapeDtypeStruct((batch * length, D_MODEL), F32),
                   jax.ShapeDtypeStruct((batch, CONV_A - 1, D_HALF), F32),
                   jax.ShapeDtypeStruct((batch, CONV_B - 1, D_HALF), F32)],
        scratch_shapes=[pltpu.VMEM((n_seq, HIST_A + seg, D_HALF), F32),
                        pltpu.VMEM((SUBLANES - 1, n_seq, HIST_A + seg - SUBLANES, D_HALF), F32),
                        pltpu.VMEM((n_seq, HIST_B + seg, D_HALF), F32),
                        pltpu.VMEM((rows, D_HALF), F32),
                        pltpu.VMEM((rows, D_HALF), F32)],
        input_output_aliases=aliases,
        compiler_params=_params(("arbitrary", "arbitrary")),
        name="mixer_ab",
    )(*args)


def _mixer_cd_kernel(x_ref, bufp_ref, w_in_ref, pw_ref, ps_ref, vg_ref, vb_ref, ws_ref, bs_ref, w_out_ref,
                     lng_ref, lnb_ref, *refs, n_seq, seg, pos0, n_mix, emit_v):
    if emit_v:
        y_ref, nbufp_ref, v_ref, c_ext, pooled, mixed = refs
    else:
        y_ref, nbufp_ref, c_ext, pooled, mixed = refs
    rows = n_seq * seg
    li = pl.program_id(1)
    x = x_ref[...]
    h = _bdot(x, w_in_ref[...])
    c_in = h[:, 0:D_HALF]

    _load_history(c_ext, bufp_ref, li == 0, HIST_P, POOL_BUF, seg)
    c_ext[:, HIST_P:HIST_P + seg, :] = c_in.reshape(n_seq, seg, D_HALF)
    nbufp_ref[...] = c_ext[:, seg + HIST_P - POOL_BUF:seg + HIST_P, :]

    rc = min(CONV_ROWS, seg)
    for g, win in enumerate(POOL_WINDOWS):
        lanes = slice(g * LANES, (g + 1) * LANES)
        for s in range(n_seq):
            for r0 in range(0, seg, rc):
                cur = c_ext[s, HIST_P + r0:HIST_P + r0 + rc, lanes]
                acc = cur
                for j in range(1, win):
                    acc = acc + c_ext[s, HIST_P + r0 - j:HIST_P + r0 - j + rc, lanes]
                pos = pos0 + li * seg + r0 + lax.broadcasted_iota(jnp.int32, (rc, LANES), 0)
                cnt = jnp.minimum(pos + 1, win).astype(F32)
                pooled[s * seg + r0:s * seg + r0 + rc, lanes] = acc / cnt - cur
    c = _bdot(pooled[...], pw_ref[...]) * ps_ref[...]

    z = jax.nn.gelu(h[:, D_HALF:3 * D_HALF], approximate=True)
    u = z[:, 0:D_HALF]
    v = _layer_norm(z[:, D_HALF:2 * D_HALF], vg_ref[...], vb_ref[...])
    if emit_v:
        v_ref[...] = v.reshape(n_seq, seg, D_HALF)
    vb16 = v.astype(BF16)

    ri = lax.broadcasted_iota(jnp.int32, (n_mix, n_mix), 0) >> CHUNK_SHIFT
    ci = lax.broadcasted_iota(jnp.int32, (n_mix, n_mix), 1) >> CHUNK_SHIFT
    for g in range(N_HEAD_D):
        lanes = slice(g * LANES, (g + 1) * LANES)
        ws = jnp.where(ci <= ri, ws_ref[g], 0.0).astype(BF16)
        bias = bs_ref[:, g:g + 1]
        for r0 in range(0, rows, n_mix):
            mixed[r0:r0 + n_mix, lanes] = jnp.dot(ws, vb16[r0:r0 + n_mix, lanes], preferred_element_type=F32) + bias
    d = u * mixed[...]
    y = _bdot(c, w_out_ref[0:D_HALF, :]) + _bdot(d, w_out_ref[D_HALF:D_MODEL, :])
    y_ref[...] = _layer_norm(ALPHA * x + y, lng_ref[...], lnb_ref[...])


def _mixer_cd(x, row_start, buf_p, length, w_in, pw, ps, vg, vb, ws, bs, w_out, lng, lnb, *,
              n_seq, seg, pos0, emit_v):
    batch = buf_p.shape[0]
    d_in = w_in.shape[1]
    n_mix = min(length, GMLP_CHUNK)
    assert seg % n_mix == 0
    n_l = length // seg
    rows = n_seq * seg
    blk0 = row_start // rows
    ws_n = ws[:, :n_mix, :n_mix]
    bs_t = bs[:, :n_mix].T
    out_specs = [pl.BlockSpec((rows, D_MODEL), lambda b, l: (b * n_l + l, 0)),
                 pl.BlockSpec((n_seq, POOL_BUF, D_HALF), lambda b, l: (b, 0, 0))]
    out_shape = [jax.ShapeDtypeStruct((batch * length, D_MODEL), F32),
                 jax.ShapeDtypeStruct((batch, POOL_BUF, D_HALF), F32)]
    if emit_v:
        out_specs.append(pl.BlockSpec((n_seq, seg, D_HALF), lambda b, l: (b, l, 0)))
        out_shape.append(jax.ShapeDtypeStruct((batch, length, D_HALF), F32))
    return pl.pallas_call(
        functools.partial(_mixer_cd_kernel, n_seq=n_seq, seg=seg, pos0=pos0, n_mix=n_mix, emit_v=emit_v),
        grid=(batch // n_seq, n_l),
        in_specs=[pl.BlockSpec((rows, D_MODEL), lambda b, l: (blk0 + b * n_l + l, 0)),
                  pl.BlockSpec((n_seq, POOL_BUF, D_HALF), lambda b, l: (b, 0, 0)),
                  _full((D_MODEL, d_in)), _full((D_HALF, D_HALF)), _full((1, D_HALF)), _full((1, D_HALF)),
                  _full((1, D_HALF)), _full((N_HEAD_D, n_mix, n_mix)), _full((n_mix, N_HEAD_D)),
                  _full((D_MODEL, D_MODEL)), _full((1, D_MODEL)), _full((1, D_MODEL))],
        out_specs=out_specs,
        out_shape=out_shape,
        scratch_shapes=[pltpu.VMEM((n_seq, HIST_P + seg, D_HALF), F32),
                        pltpu.VMEM((rows, D_HALF), F32),
                        pltpu.VMEM((rows, D_HALF), F32)],
        compiler_params=_params(("arbitrary", "arbitrary")),
        name="mixer_cd",
    )(x, buf_p, w_in, pw, _row2(ps), _row2(vg), _row2(vb), ws_n, bs_t, w_out, _row2(lng), _row2(lnb))


def _attn_router_kernel(xp_ref, xs_ref, kp_ref, vp_ref, ks_ref, vs_ref, wq_ref, wo_ref, lng_ref, lnb_ref, rw_ref, rb_ref,
                        rows_ref, meta_ref, counts_ref,
                        q_scr, o_scr, carry, *, n_prompt_tiles, sample_seg):
    rows = ROW_TILE
    step = pl.program_id(0)
    is_prompt = step < n_prompt_tiles
    x = jnp.where(is_prompt, xp_ref[...], xs_ref[...])
    q_scr[...] = (_bdot(x, wq_ref[0]) * (MEM_HEAD_DIM ** -0.5)).astype(BF16)

    def heads(k_ref, v_ref, n_seq, seg):
        for s in range(n_seq):
            for hd in range(MEM_HEADS):
                cols = slice(hd * MEM_HEAD_DIM, (hd + 1) * MEM_HEAD_DIM)
                sc = lax.dot_general(q_scr[s * seg:(s + 1) * seg, cols], k_ref[s, :, cols], (((1,), (1,)), ((), ())),
                                     preferred_element_type=F32)
                p = jnp.exp(sc - jnp.max(sc, axis=-1, keepdims=True))
                den = jnp.sum(p, axis=-1, keepdims=True)
                o_scr[s * seg:(s + 1) * seg, cols] = _bdot(p, v_ref[s, :, cols]) / den

    @pl.when(is_prompt)
    def _():
        heads(kp_ref, vp_ref, 1, rows)

    @pl.when(jnp.logical_not(is_prompt))
    def _():
        heads(ks_ref, vs_ref, rows // sample_seg, sample_seg)

    x2 = _layer_norm(ALPHA * x + _bdot(o_scr[...], wo_ref[0]), lng_ref[...], lnb_ref[...])
    rows_ref[:, 0:D_MODEL] = x2

    nt = (((1,), (1,)), ((), ()))
    x2_hi, x2_lo = _split(x2)
    both = lax.dot_general(rw_ref[...], x2_hi, nt, preferred_element_type=F32)
    cross = lax.dot_general(rw_ref[0:N_EXPERTS, :], x2_lo, nt, preferred_element_type=F32)
    logits = both[0:N_EXPERTS] + both[N_EXPERTS:2 * N_EXPERTS] + cross + rb_ref[...]
    e = jnp.exp(logits - jnp.max(logits, axis=0, keepdims=True))
    scores = e / jnp.sum(e, axis=0, keepdims=True)
    eid = lax.broadcasted_iota(jnp.int32, (N_EXPERTS, rows), 0)
    egrp = eid >> GROUP_SHIFT
    best = jnp.max(jnp.where(egrp == 0, scores, -1.0), axis=0, keepdims=True)
    g_sel = jnp.zeros((1, rows), jnp.int32)
    for g in range(1, N_EXPERT_GROUPS):
        gs = jnp.max(jnp.where(egrp == g, scores, -1.0), axis=0, keepdims=True)
        upd = gs > best
        g_sel = jnp.where(upd, g, g_sel)
        best = jnp.where(upd, gs, best)
    masked = jnp.where(egrp == g_sel, scores, -1.0)
    m1 = jnp.max(masked, axis=0, keepdims=True)
    i1 = jnp.min(jnp.where(masked == m1, eid, N_EXPERTS), axis=0, keepdims=True)
    masked2 = jnp.where(eid == i1, -2.0, masked)
    m2 = jnp.max(masked2, axis=0, keepdims=True)
    i2 = jnp.min(jnp.where(masked2 == m2, eid, N_EXPERTS), axis=0, keepdims=True)
    tot = m1 + m2
    g1 = m1 / tot
    g2 = m2 / tot
    first_low = i1 < i2
    ea = jnp.where(first_low, i1, i2) & (GROUP_SIZE - 1)
    eb = jnp.where(first_low, i2, i1) & (GROUP_SIZE - 1)
    gate_a = jnp.where(first_low, g1, g2)
    gate_b = jnp.where(first_low, g2, g1)
    pair = eb - 1 + jnp.where(ea == 1, 2, 0) + jnp.where(ea == 2, 3, 0)
    cls = g_sel * PAIRS_PER_GROUP + pair

    lane_row = lax.broadcasted_iota(jnp.int32, (LANES, rows), 0)
    gates_t = jnp.where(lane_row == 0, gate_a, jnp.where(lane_row == 1, gate_b, 0.0))
    rows_ref[:, D_MODEL:ROW_W] = gates_t.T

    @pl.when(step == 0)
    def _():
        carry[...] = jnp.zeros_like(carry)

    onehot = (lax.broadcasted_iota(jnp.int32, (CLASS_ROWS, rows), 0) == cls).astype(F32)
    tri = (lax.broadcasted_iota(jnp.int32, (rows, rows), 0) <= lax.broadcasted_iota(jnp.int32, (rows, rows), 1))
    cum = jnp.dot(onehot.astype(BF16), tri.astype(F32).astype(BF16), preferred_element_type=F32)
    before = carry[:, 0:1]
    rank = jnp.sum(onehot * (before + cum), axis=0, keepdims=True) - 1.0
    sub = lax.broadcasted_iota(jnp.int32, (SUBLANES, rows), 0)
    meta_ref[...] = jnp.where(sub == 0, cls, jnp.where(sub == 1, rank.astype(jnp.int32), 0))
    carry[...] = carry[...] + cum[:, rows - 1:rows]
    counts_ref[...] = carry[...].astype(jnp.int32)


def _attn_router(x_p, x_s, kv_p, k_s, v_s, wq, wo, lng, lnb, rw_t, rb, *, layer, prompt_len, sample_len):
    rows = ROW_TILE
    n_p = x_p.shape[0] // rows
    n_s = x_s.shape[0] // rows
    tiles_per_seq = prompt_len // rows
    seq_per_tile = rows // sample_len
    total = x_p.shape[0] + x_s.shape[0]
    n_prompt_seq = n_p // tiles_per_seq
    kp0 = layer * n_prompt_seq
    vp0 = (DEPTH + layer) * n_prompt_seq
    s0 = layer * n_s
    p_idx = lambda i: jnp.minimum(i, n_p - 1)
    s_idx = lambda i: jnp.maximum(i - n_p, 0)
    layer_w = pl.BlockSpec((1, D_MODEL, D_MODEL), lambda i: (layer, 0, 0), pipeline_mode=pl.Buffered(1))
    return pl.pallas_call(
        functools.partial(_attn_router_kernel, n_prompt_tiles=n_p, sample_seg=sample_len),
        grid=(n_p + n_s,),
        in_specs=[pl.BlockSpec((rows, D_MODEL), lambda i: (p_idx(i), 0)),
                  pl.BlockSpec((rows, D_MODEL), lambda i: (s_idx(i), 0)),
                  pl.BlockSpec((1, N_MEM, D_MODEL), lambda i: (kp0 + p_idx(i) // tiles_per_seq, 0, 0)),
                  pl.BlockSpec((1, N_MEM, D_MODEL), lambda i: (vp0 + p_idx(i) // tiles_per_seq, 0, 0)),
                  pl.BlockSpec((seq_per_tile, N_MEM, D_MODEL), lambda i: (s0 + s_idx(i), 0, 0)),
                  pl.BlockSpec((seq_per_tile, N_MEM, D_MODEL), lambda i: (s0 + s_idx(i), 0, 0)),
                  layer_w, layer_w, _full((1, D_MODEL)), _full((1, D_MODEL)),
                  _full((2 * N_EXPERTS, D_MODEL)), _full((N_EXPERTS, 1))],
        out_specs=[pl.BlockSpec((rows, ROW_W), lambda i: (i, 0)),
                   pl.BlockSpec((SUBLANES, rows), lambda i: (0, i)),
                   pl.BlockSpec((CLASS_ROWS, LANES), lambda i: (0, 0))],
        out_shape=[jax.ShapeDtypeStruct((total, ROW_W), F32),
                   jax.ShapeDtypeStruct((SUBLANES, total), jnp.int32),
                   jax.ShapeDtypeStruct((CLASS_ROWS, LANES), jnp.int32)],
        scratch_shapes=[pltpu.VMEM((rows, D_MODEL), BF16),
                        pltpu.VMEM((rows, D_MODEL), F32),
                        pltpu.VMEM((CLASS_ROWS, LANES), F32)],
        compiler_params=_params(("arbitrary",)),
        name="attn_router",
    )(x_p, x_s, kv_p, kv_p, k_s, v_s, wq, wo, _row2(lng), _row2(lnb), rw_t, rb.reshape(N_EXPERTS, 1))


def _dispatch_kernel(cls_ref, rank_ref, offs_ref, cnt_ref, end_ref, used_ref, rows_ref, xs_hbm, tok_ref,
                     zeros, sem, zsem, *, tile, n_tiles):
    step = pl.program_id(0)

    def class_padding(c, go):
        start = offs_ref[c] + cnt_ref[c]
        head = (-start) & (SUBLANES - 1)
        bulk_start = pl.multiple_of(start + head, SUBLANES)
        bulk = pl.multiple_of(end_ref[c] - bulk_start, SUBLANES)

        def one(r, carry_):
            go(pltpu.make_async_copy(zeros.at[pl.ds(0, 1)], xs_hbm.at[pl.ds(r, 1)], zsem))
            return carry_
        lax.fori_loop(start, start + head, one, 0)

        @pl.when(bulk > 0)
        def _():
            go(pltpu.make_async_copy(zeros.at[pl.ds(0, bulk)], xs_hbm.at[pl.ds(bulk_start, bulk)], zsem))

    def unused_tiles(go):
        def one(j, carry_):
            go(pltpu.make_async_copy(zeros, xs_hbm.at[pl.ds(pl.multiple_of(j * MOE_TILE, MOE_TILE), MOE_TILE)], zsem))
            return carry_
        lax.fori_loop(used_ref[0], n_tiles, one, 0)

    @pl.when(step == 0)
    def _():
        zeros[...] = jnp.zeros_like(zeros)

        def no_token(p, carry_):
            tok_ref[p] = -1
            return carry_
        for c in range(N_CLASSES):
            lax.fori_loop(offs_ref[c] + cnt_ref[c], end_ref[c], no_token, 0)
            class_padding(c, lambda copy: copy.start())
        lax.fori_loop(used_ref[0] * MOE_TILE, n_tiles * MOE_TILE, no_token, 0)
        unused_tiles(lambda copy: copy.start())

    for r in range(tile):
        t = step * tile + r
        p = offs_ref[cls_ref[t]] + rank_ref[t]
        tok_ref[p] = t
        pltpu.make_async_copy(rows_ref.at[pl.ds(r, 1)], xs_hbm.at[pl.ds(p, 1)], sem).start()
    pltpu.make_async_copy(rows_ref, xs_hbm.at[pl.ds(0, tile)], sem).wait()

    @pl.when(step == 0)
    def _():
        for c in range(N_CLASSES):
            class_padding(c, lambda copy: copy.wait())
        unused_tiles(lambda copy: copy.wait())


def _dispatch(cls, rank, offs, cnt, end, used, rows, n_tiles):
    n_rows = rows.shape[0]
    n_sorted = n_tiles * MOE_TILE
    any_spec = pl.BlockSpec(memory_space=pl.ANY)
    return pl.pallas_call(
        functools.partial(_dispatch_kernel, tile=DISPATCH_TILE, n_tiles=n_tiles),
        grid_spec=pltpu.PrefetchScalarGridSpec(
            num_scalar_prefetch=6, grid=(n_rows // DISPATCH_TILE,),
            in_specs=[pl.BlockSpec((DISPATCH_TILE, ROW_W), lambda i, *_: (i, 0))],
            out_specs=[any_spec, pl.BlockSpec(memory_space=pltpu.SMEM)],
            scratch_shapes=[pltpu.VMEM((MOE_TILE, ROW_W), F32), pltpu.SemaphoreType.DMA(()),
                            pltpu.SemaphoreType.DMA(())]),
        out_shape=[jax.ShapeDtypeStruct((n_sorted, ROW_W), F32), jax.ShapeDtypeStruct((n_sorted,), jnp.int32)],
        compiler_params=_params(("arbitrary",)),
        name="dispatch",
    )(cls, rank, offs, cnt, end, used, rows)


def _experts_kernel(ea_ref, eb_ref, blk_ref, nvalid_ref, tok_ref, xs_ref, wga_ref, wua_ref, wda_ref, wgb_ref, wub_ref,
                    wdb_ref, lng_ref, lnb_ref, out_hbm, ybuf, sems, *, n_rows):
    del ea_ref, eb_ref, blk_ref
    i = pl.program_id(0)
    n_cur = nvalid_ref[i]
    n_prev = jnp.where(i >= 1, nvalid_ref[jnp.maximum(i - 1, 0)], 0)
    n_prev2 = jnp.where(i >= 2, nvalid_ref[jnp.maximum(i - 2, 0)], 0)
    has_cur = n_cur > 0
    has_prev = n_prev > 0

    def tile_copy(slot):
        return pltpu.make_async_copy(ybuf.at[slot], out_hbm.at[pl.ds(0, MOE_TILE)], sems.at[slot])

    def compute(slot):
        x = xs_ref[:, 0:D_MODEL]
        xb = x.astype(BF16)
        y = None
        for lane, (wg, wu, wd) in enumerate(((wga_ref, wua_ref, wda_ref), (wgb_ref, wub_ref, wdb_ref))):
            gate = xs_ref[:, D_MODEL + lane:D_MODEL + lane + 1]
            hg = jnp.dot(xb, wg[0], preferred_element_type=F32)
            hu = jnp.dot(xb, wu[0], preferred_element_type=F32)
            hid = hg * jax.nn.sigmoid(hg) * hu * gate
            part = _bdot(hid, wd[0])
            y = part if y is None else y + part
        ybuf[slot] = _layer_norm(ALPHA * x + y, lng_ref[...], lnb_ref[...])

    def start_rows(tile, slot):
        base = tile * MOE_TILE
        spare = n_rows + (tile & 1) * MOE_TILE
        for r in range(MOE_TILE):
            t = tok_ref[base + r]
            dst = jnp.where(t < 0, spare + r, t)
            pltpu.make_async_copy(ybuf.at[slot, pl.ds(r, 1)], out_hbm.at[pl.ds(dst, 1)], sems.at[slot]).start()

    @pl.when(i == 0)
    def _():
        ybuf[2] = jnp.zeros((MOE_TILE, D_MODEL), F32)
        for half in range(2):
            fill = pltpu.make_async_copy(ybuf.at[2], out_hbm.at[pl.ds(n_rows + half * MOE_TILE, MOE_TILE)], sems.at[2])
            fill.start()
            fill.wait()

    slot_cur = i % 3
    slot_prev = (i + 2) % 3

    @pl.when(has_prev & has_cur)
    def _():
        start_rows(i - 1, slot_prev)
        compute(slot_cur)

    @pl.when(has_prev & jnp.logical_not(has_cur))
    def _():
        start_rows(i - 1, slot_prev)

    @pl.when(jnp.logical_not(has_prev) & has_cur)
    def _():
        compute(slot_cur)

    @pl.when(n_prev2 > 0)
    def _():
        tile_copy((i + 1) % 3).wait()


def _experts(tile_a, tile_b, tile_blk, tile_nvalid, tok, xs, w_gate, w_up, w_down, lng, lnb, n_rows, layer):
    n_steps = tile_nvalid.shape[0]
    e0 = layer * N_EXPERTS
    up_a = pl.BlockSpec((1, D_MODEL, D_EXPERT), lambda i, ea, eb, *_: (e0 + ea[i], 0, 0))
    up_b = pl.BlockSpec((1, D_MODEL, D_EXPERT), lambda i, ea, eb, *_: (e0 + eb[i], 0, 0))
    dn_a = pl.BlockSpec((1, D_EXPERT, D_MODEL), lambda i, ea, eb, *_: (e0 + ea[i], 0, 0))
    dn_b = pl.BlockSpec((1, D_EXPERT, D_MODEL), lambda i, ea, eb, *_: (e0 + eb[i], 0, 0))
    vec = pl.BlockSpec((1, D_MODEL), lambda i, *_: (0, 0))
    return pl.pallas_call(
        functools.partial(_experts_kernel, n_rows=n_rows),
        grid_spec=pltpu.PrefetchScalarGridSpec(
            num_scalar_prefetch=5, grid=(n_steps,),
            in_specs=[pl.BlockSpec((MOE_TILE, ROW_W), lambda i, ea, eb, blk, *_: (blk[i], 0)),
                      up_a, up_a, dn_a, up_b, up_b, dn_b, vec, vec],
            out_specs=pl.BlockSpec(memory_space=pl.ANY),
            scratch_shapes=[pltpu.VMEM((3, MOE_TILE, D_MODEL), F32), pltpu.SemaphoreType.DMA((3,))]),
        out_shape=jax.ShapeDtypeStruct((n_rows + 2 * MOE_TILE, D_MODEL), F32),
        compiler_params=_params(("arbitrary",)),
        name="experts",
    )(tile_a, tile_b, tile_blk, tile_nvalid, tok, xs, w_gate, w_up, w_down, w_gate, w_up, w_down,
      _row2(lng), _row2(lnb))


def _lookup(table, idx):
    pick = idx[:, None] == jnp.arange(table.shape[0], dtype=jnp.int32)[None, :]
    return jnp.sum(jnp.where(pick, table[None, :], 0), axis=1)


def _moe(rows, meta, counts, w_gate, w_up, w_down, lng, lnb, layer):
    n_rows = rows.shape[0]
    n_tiles = n_rows // MOE_TILE + N_CLASSES
    cnt = counts[:N_CLASSES, 0]
    tiles_per = (cnt + MOE_TILE - 1) // MOE_TILE
    cls_id = jnp.arange(N_CLASSES, dtype=jnp.int32)
    tile_end = jnp.sum(jnp.where(cls_id[None, :] <= cls_id[:, None], tiles_per[None, :], 0), axis=1)
    tile_start = tile_end - tiles_per
    offs = tile_start * MOE_TILE
    end = tile_end * MOE_TILE
    used = tile_end[N_CLASSES - 1:N_CLASSES]
    tile_id = jnp.arange(n_tiles + 2, dtype=jnp.int32)
    tile_blk = jnp.minimum(tile_id, used - 1)
    tile_cls = jnp.sum((tile_blk[:, None] >= tile_end[None, :]).astype(jnp.int32), axis=1)
    tile_a = _lookup(jnp.asarray(_CLASS_A), tile_cls)
    tile_b = _lookup(jnp.asarray(_CLASS_B), tile_cls)
    tile_nvalid = jnp.clip(_lookup(cnt, tile_cls) - (tile_id - _lookup(tile_start, tile_cls)) * MOE_TILE, 0, MOE_TILE)
    xs, tok = _dispatch(meta[0], meta[1], offs, cnt, end, used, rows, n_tiles)
    return _experts(tile_a, tile_b, tile_blk, tile_nvalid, tok, xs, w_gate, w_up, w_down, lng, lnb, n_rows, layer)


def kernel(x_prompt, x_sample, mem_prompt, cache_mem_k, cache_mem_v, state_conv_a, state_conv_b, state_pool,
           ln_g, ln_b, ab_w_in, ab_conv_a_w, ab_conv_a_b, ab_norm_a_g, ab_norm_a_b, ab_conv_b_w, ab_w_out,
           cd_w_in, cd_pool_w, cd_pool_scale, cd_v_norm_g, cd_v_norm_b, cd_w_s, cd_b_s, cd_w_out,
           ca_wq, ca_wk, ca_wv, ca_wo, router_w, router_b, moe_w_gate, moe_w_up, moe_w_down):
    assert ln_g.shape[0] == DEPTH and ab_w_in.shape[0] == 1 and cd_w_in.shape[0] == 1
    bsz, seq, _ = x_prompt.shape
    dec_b, dec_seq, _ = x_sample.shape
    n_prompt = bsz * seq
    n_sample = dec_b * dec_seq
    sample_seqs = ROW_TILE // dec_seq

    n_pool = len(POOL_WINDOWS)
    pool_c = D_HALF // n_pool
    pool_bd = jnp.zeros((D_HALF, D_HALF), F32)
    for g in range(n_pool):
        pool_bd = pool_bd.at[g * pool_c:(g + 1) * pool_c, g * pool_c:(g + 1) * pool_c].set(cd_pool_w[0, g])
    pool_bd = pool_bd.astype(BF16)
    ab_w_in_hi, ab_w_in_lo = _split_weight(ab_w_in[0])
    ab_w_out_hi, ab_w_out_lo = _split_weight(ab_w_out[0])
    rw_hi, rw_lo = _split_weight(router_w.T)
    rw_t = jnp.concatenate([rw_hi, rw_lo], axis=0)
    cd_w_in_b, cd_w_out_b = cd_w_in[0].astype(BF16), cd_w_out[0].astype(BF16)
    wq_b, wo_b = ca_wq.astype(BF16), ca_wo.astype(BF16)
    wg_b = moe_w_gate.astype(BF16).reshape(DEPTH * N_EXPERTS, D_MODEL, D_EXPERT)
    wu_b = moe_w_up.astype(BF16).reshape(DEPTH * N_EXPERTS, D_MODEL, D_EXPERT)
    wd_b = moe_w_down.astype(BF16).reshape(DEPTH * N_EXPERTS, D_EXPERT, D_MODEL)

    kv, kv16 = _mem_projection(mem_prompt.reshape(bsz * N_MEM, D_MODEL),
                               jnp.concatenate([ca_wk, ca_wv], axis=0).astype(BF16))
    kv_shape = (DEPTH, bsz, N_MEM, MEM_HEADS, MEM_HEAD_DIM)
    mem_k_prompt, mem_v_prompt = kv[0:DEPTH].reshape(kv_shape), kv[DEPTH:2 * DEPTH].reshape(kv_shape)
    kv16 = kv16.reshape(2 * DEPTH * bsz, N_MEM, D_MODEL)
    k_s = cache_mem_k.astype(BF16).reshape(DEPTH * dec_b, N_MEM, D_MODEL)
    v_s = cache_mem_v.astype(BF16).reshape(DEPTH * dec_b, N_MEM, D_MODEL)

    def mixer_ab(x, buf_a, buf_b, n_seq, seg, **kw):
        return _mixer_ab(x, buf_a, buf_b, ab_w_in_hi, ab_w_in_lo, ab_conv_a_w[0], ab_conv_a_b[0], ab_norm_a_g[0],
                         ab_norm_a_b[0], ab_conv_b_w[0], ab_w_out_hi, ab_w_out_lo, ln_g[0, 0], ln_b[0, 0],
                         n_seq=n_seq, seg=seg, **kw)

    def mixer_cd(x, row_start, buf_p, length, n_seq, seg, pos0, emit_v):
        return _mixer_cd(x, row_start, buf_p, length, cd_w_in_b, pool_bd, cd_pool_scale[0], cd_v_norm_g[0],
                         cd_v_norm_b[0], cd_w_s[0], cd_b_s[0], cd_w_out_b, ln_g[1, 0], ln_b[1, 0],
                         n_seq=n_seq, seg=seg, pos0=pos0, emit_v=emit_v)

    def attn_moe(layer, x_p, x_s):
        rows, meta, counts = _attn_router(x_p, x_s, kv16, k_s, v_s, wq_b, wo_b, ln_g[layer, 1], ln_b[layer, 1],
                                          rw_t, router_b, layer=layer, prompt_len=seq, sample_len=dec_seq)
        return _moe(rows, meta, counts, wg_b, wu_b, wd_b, ln_g[layer, 2], ln_b[layer, 2], layer)

    xp_flat = x_prompt.reshape(n_prompt, D_MODEL)
    tiles_p = seq // ROW_TILE
    x_p, conv_a_p, conv_b_p = mixer_ab(xp_flat, jnp.zeros((bsz, CONV_A - 1, D_HALF), F32),
                                       jnp.zeros((bsz, CONV_B - 1, D_HALF), F32), 1, ROW_TILE,
                                       precise=False, skip_tail=STATE_TILES)
    x_p, conv_a_p, conv_b_p = mixer_ab(xp_flat, conv_a_p, conv_b_p, 1, ROW_TILE, precise=True,
                                       tile_lo=tiles_p - STATE_TILES, tiles=STATE_TILES, y_prev=x_p)
    x_s, conv_a_s, conv_b_s = mixer_ab(x_sample.reshape(n_sample, D_MODEL), state_conv_a[0], state_conv_b[0],
                                       sample_seqs, dec_seq, precise=False)
    x_all = attn_moe(0, x_p, x_s)

    x_p, pool_p = mixer_cd(x_all, 0, jnp.zeros((bsz, POOL_BUF, D_HALF), F32), seq, 1, ROW_TILE, 0, False)
    x_s, pool_s, v_s_rows = mixer_cd(x_all, n_prompt, state_pool[0], dec_seq, sample_seqs, dec_seq, PAST_LEN, True)
    x_all = attn_moe(1, x_p, x_s)

    y_prompt = x_all[0:n_prompt].reshape(bsz, seq, D_MODEL)
    y_sample = x_all[n_prompt:n_prompt + n_sample].reshape(dec_b, dec_seq, D_MODEL)
    return (y_prompt, y_sample, mem_k_prompt, mem_v_prompt, conv_a_p[None], conv_b_p[None], pool_p[None],
            conv_a_s[None], conv_b_s[None], pool_s[None], v_s_rows[None])
```

```python
import functools

import jax
import jax.numpy as jnp
import numpy as np
from jax import lax
from jax.experimental import pallas as pl
from jax.experimental.pallas import tpu as pltpu

F32 = jnp.float32
BF16 = jnp.bfloat16

D_MODEL = 1024
D_HALF = D_MODEL // 2
DEPTH = 2
PAST_LEN = 4096
CHUNK = 64
CHUNK_SHIFT = CHUNK.bit_length() - 1
CONV_A = 31
CONV_B = 3
POOL_WINDOWS = (2, 4, 8, 16)
POOL_BUF = max(POOL_WINDOWS) - 1
N_HEAD_D = 4
GMLP_CHUNK = 128
N_MEM = 256
MEM_HEADS = 4
MEM_HEAD_DIM = D_MODEL // MEM_HEADS
N_EXPERTS = 16
N_EXPERT_GROUPS = 4
GROUP_SIZE = N_EXPERTS // N_EXPERT_GROUPS
GROUP_SHIFT = GROUP_SIZE.bit_length() - 1
PAIRS_PER_GROUP = GROUP_SIZE * (GROUP_SIZE - 1) // 2
N_CLASSES = N_EXPERT_GROUPS * PAIRS_PER_GROUP
D_EXPERT = D_MODEL // 2
ALPHA = (2 * DEPTH) ** 0.25
LN_EPS = 1e-5

LANES = 128
SUBLANES = 8
ROW_W = D_MODEL + LANES
HIST_A = 32
HIST_B = 8
HIST_P = 16
CONV_ROWS = 64
ROW_TILE = 512
MOE_TILE = 256
CLASS_ROWS = 32
DISPATCH_TILE = 1024
SPLIT_ROWS = 256
STATE_TILES = 1
VMEM_LIMIT = 56 * 1024 * 1024

_PAIR_AB = [(a, b) for a in range(GROUP_SIZE) for b in range(a + 1, GROUP_SIZE)]
_CLASS_A = np.array([g * GROUP_SIZE + a for g in range(N_EXPERT_GROUPS) for a, _ in _PAIR_AB], np.int32)
_CLASS_B = np.array([g * GROUP_SIZE + b for g in range(N_EXPERT_GROUPS) for _, b in _PAIR_AB], np.int32)


def _layer_norm(x, g, b):
    mu = jnp.mean(x, axis=-1, keepdims=True)
    xc = x - mu
    var = jnp.mean(xc * xc, axis=-1, keepdims=True)
    return xc * lax.rsqrt(var + LN_EPS) * g + b


def _bdot(a, w):
    return jnp.dot(a.astype(BF16), w, preferred_element_type=F32)


def _split(a):
    hi = a.astype(BF16)
    return hi, (a - hi.astype(F32)).astype(BF16)


def _dot3(a, w_hi, w_lo):
    a_hi, a_lo = _split(a)
    return (jnp.dot(a_hi, w_hi, preferred_element_type=F32) + jnp.dot(a_lo, w_hi, preferred_element_type=F32)
            + jnp.dot(a_hi, w_lo, preferred_element_type=F32))


def _params(sem):
    return pltpu.CompilerParams(dimension_semantics=sem, vmem_limit_bytes=VMEM_LIMIT)


def _full(shape):
    return pl.BlockSpec(shape, lambda *_: (0,) * len(shape), pipeline_mode=pl.Buffered(1))


def _row2(v):
    return v.reshape(1, -1)


def _split_weight_kernel(w_ref, hi_ref, lo_ref):
    hi, lo = _split(w_ref[...])
    hi_ref[...] = hi
    lo_ref[...] = lo


def _split_weight(w):
    rows, cols = w.shape
    blk = min(rows, SPLIT_ROWS)
    spec = pl.BlockSpec((blk, cols), lambda i: (i, 0))
    return pl.pallas_call(
        _split_weight_kernel,
        grid=(rows // blk,),
        in_specs=[spec],
        out_specs=[spec, spec],
        out_shape=[jax.ShapeDtypeStruct(w.shape, BF16)] * 2,
        compiler_params=_params(("arbitrary",)),
        name="split_weight",
    )(w)


def _proj_kernel(x_ref, w_ref, o_ref, o16_ref):
    res = _bdot(x_ref[...], w_ref[0])
    o_ref[0] = res
    o16_ref[0] = res.astype(BF16)


def _mem_projection(mem, w):
    n, rows = w.shape[0], mem.shape[0]
    out = pl.BlockSpec((1, rows, D_MODEL), lambda j: (j, 0, 0))
    return pl.pallas_call(
        _proj_kernel,
        grid=(n,),
        in_specs=[pl.BlockSpec((rows, D_MODEL), lambda j: (0, 0)),
                  pl.BlockSpec((1, D_MODEL, D_MODEL), lambda j: (j, 0, 0))],
        out_specs=[out, out],
        out_shape=[jax.ShapeDtypeStruct((n, rows, D_MODEL), F32), jax.ShapeDtypeStruct((n, rows, D_MODEL), BF16)],
        compiler_params=_params(("arbitrary",)),
        name="mem_projection",
    )(mem, w)


def _load_history(ext_ref, buf_ref, first, hist, keep, seg):
    @pl.when(first)
    def _():
        ext_ref[:, hist - keep:hist, :] = buf_ref[...]

    @pl.when(jnp.logical_not(first))
    def _():
        ext_ref[:, hist - keep:hist, :] = ext_ref[:, seg + hist - keep:seg + hist, :]


def _depthwise_conv(ext_ref, w_ref, out_ref, *, n_seq, seg, taps, hist, shifted_ref=None):
    rc = min(CONV_ROWS, seg)
    off0 = hist - (taps - 1)
    length = hist + seg
    if shifted_ref is not None:
        for r in range(1, SUBLANES):
            shifted_ref[r - 1, :, 0:length - SUBLANES, :] = ext_ref[:, r:r + length - SUBLANES, :]
    for s in range(n_seq):
        for r0 in range(0, seg, rc):
            for lb in range(0, D_HALF, LANES):
                acc = None
                for k in range(taps):
                    lo = off0 + k + r0
                    shift = (off0 + k) % SUBLANES
                    if shifted_ref is None or shift == 0:
                        win = ext_ref[s, lo:lo + rc, lb:lb + LANES]
                    else:
                        win = shifted_ref[shift - 1, s, lo - shift:lo - shift + rc, lb:lb + LANES]
                    term = w_ref[k:k + 1, lb:lb + LANES] * win
                    acc = term if acc is None else acc + term
                out_ref[s * seg + r0:s * seg + r0 + rc, lb:lb + LANES] = acc


def _mixer_ab_kernel(x_ref, bufa_ref, bufb_ref, w_in_ref, w_in_lo_ref, caw_ref, cab_ref, nag_ref, nab_ref, cbw_ref,
                     w_out_ref, w_out_lo_ref, lng_ref, lnb_ref, *refs, n_seq, seg, precise, skip_tail, has_prev):
    if has_prev:
        refs = refs[1:]
    y_ref, nbufa_ref, nbufb_ref, a_ext, a_shift, cb_ext, conv_a, conv_b = refs

    def body():
        first = pl.program_id(1) == 0
        _load_history(a_ext, bufa_ref, first, HIST_A, CONV_A - 1, seg)
        _load_history(cb_ext, bufb_ref, first, HIST_B, CONV_B - 1, seg)
        x = x_ref[...]
        h = _dot3(x, w_in_ref[...], w_in_lo_ref[...]) if precise else _bdot(x, w_in_ref[...])
        a = h[:, 0:D_HALF] * jax.nn.sigmoid(h[:, D_HALF:2 * D_HALF])
        cb = h[:, 3 * D_HALF:4 * D_HALF] * h[:, 4 * D_HALF:5 * D_HALF]
        a_ext[:, HIST_A:HIST_A + seg, :] = a.reshape(n_seq, seg, D_HALF)
        cb_ext[:, HIST_B:HIST_B + seg, :] = cb.reshape(n_seq, seg, D_HALF)
        nbufa_ref[...] = a_ext[:, seg + HIST_A - (CONV_A - 1):seg + HIST_A, :]
        nbufb_ref[...] = cb_ext[:, seg + HIST_B - (CONV_B - 1):seg + HIST_B, :]

        _depthwise_conv(a_ext, caw_ref, conv_a, n_seq=n_seq, seg=seg, taps=CONV_A, hist=HIST_A, shifted_ref=a_shift)
        _depthwise_conv(cb_ext, cbw_ref, conv_b, n_seq=n_seq, seg=seg, taps=CONV_B, hist=HIST_B)

        a2 = _layer_norm(conv_a[...] + cab_ref[...], nag_ref[...], nab_ref[...])
        a2 = a2 * jax.nn.sigmoid(a2)
        b2 = h[:, 2 * D_HALF:3 * D_HALF] * conv_b[...]
        if precise:
            y = (_dot3(a2, w_out_ref[0:D_HALF, :], w_out_lo_ref[0:D_HALF, :])
                 + _dot3(b2, w_out_ref[D_HALF:D_MODEL, :], w_out_lo_ref[D_HALF:D_MODEL, :]))
        else:
            y = _bdot(a2, w_out_ref[0:D_HALF, :]) + _bdot(b2, w_out_ref[D_HALF:D_MODEL, :])
        y_ref[...] = _layer_norm(ALPHA * x + y, lng_ref[...], lnb_ref[...])

    if skip_tail == 0:
        body()
    else:
        live = pl.program_id(1) < pl.num_programs(1) - skip_tail
        pl.when(live)(body)

        @pl.when(jnp.logical_not(live))
        def _():
            y_ref[...] = jnp.zeros_like(y_ref)


def _mixer_ab(x, buf_a, buf_b, w_in, w_in_lo, caw, cab, nag, nab, cbw, w_out, w_out_lo, lng, lnb, *,
              n_seq, seg, precise, tile_lo=0, tiles=None, skip_tail=0, y_prev=None):
    batch = buf_a.shape[0]
    length = x.shape[0] // batch
    d_in = w_in.shape[1]
    n_l = length // seg
    tiles = n_l if tiles is None else tiles
    rows = n_seq * seg
    lo_in = _full((D_MODEL, d_in)) if precise else _full((SUBLANES, LANES))
    lo_out = _full((D_MODEL, D_MODEL)) if precise else _full((SUBLANES, LANES))
    row_blk = pl.BlockSpec((rows, D_MODEL), lambda b, l: (b * n_l + tile_lo + l, 0))
    in_specs = [row_blk,
                pl.BlockSpec((n_seq, CONV_A - 1, D_HALF), lambda b, l: (b, 0, 0)),
                pl.BlockSpec((n_seq, CONV_B - 1, D_HALF), lambda b, l: (b, 0, 0)),
                _full((D_MODEL, d_in)), lo_in, _full((CONV_A, D_HALF)), _full((1, D_HALF)), _full((1, D_HALF)),
                _full((1, D_HALF)), _full((CONV_B, D_HALF)), _full((D_MODEL, D_MODEL)), lo_out,
                _full((1, D_MODEL)), _full((1, D_MODEL))]
    args = [x, buf_a, buf_b, w_in, w_in_lo, caw, _row2(cab), _row2(nag), _row2(nab), cbw, w_out, w_out_lo,
            _row2(lng), _row2(lnb)]
    aliases = {}
    if y_prev is not None:
        aliases = {len(args): 0}
        in_specs.append(pl.BlockSpec(memory_space=pl.ANY))
        args.append(y_prev)
    return pl.pallas_call(
        functools.partial(_mixer_ab_kernel, n_seq=n_seq, seg=seg, precise=precise, skip_tail=skip_tail,
                          has_prev=y_prev is not None),
        grid=(batch // n_seq, tiles),
        in_specs=in_specs,
        out_specs=[row_blk,
                   pl.BlockSpec((n_seq, CONV_A - 1, D_HALF), lambda b, l: (b, 0, 0)),
                   pl.BlockSpec((n_seq, CONV_B - 1, D_HALF), lambda b, l: (b, 0, 0))],
        out_shape=[jax.ShapeDtypeStruct((batch * length, D_MODEL), F32),
                   jax.ShapeDtypeStruct((batch, CONV_A - 1, D_HALF), F32),
                   jax.ShapeDtypeStruct((batch, CONV_B - 1, D_HALF), F32)],
        scratch_shapes=[pltpu.VMEM((n_seq, HIST_A + seg, D_HALF), F32),
                        pltpu.VMEM((SUBLANES - 1, n_seq, HIST_A + seg - SUBLANES, D_HALF), F32),
                        pltpu.VMEM((n_seq, HIST_B + seg, D_HALF), F32),
                        pltpu.VMEM((rows, D_HALF), F32),
                        pltpu.VMEM((rows, D_HALF), F32)],
        input_output_aliases=aliases,
        compiler_params=_params(("arbitrary", "arbitrary")),
        name="mixer_ab",
    )(*args)


def _mixer_cd_kernel(x_ref, bufp_ref, w_in_ref, pw_ref, ps_ref, vg_ref, vb_ref, ws_ref, bs_ref, w_out_ref,
                     lng_ref, lnb_ref, *refs, n_seq, seg, pos0, n_mix, emit_v):
    if emit_v:
        y_ref, nbufp_ref, v_ref, c_ext, pooled, mixed = refs
    else:
        y_ref, nbufp_ref, c_ext, pooled, mixed = refs
    rows = n_seq * seg
    li = pl.program_id(1)
    x = x_ref[...]
    h = _bdot(x, w_in_ref[...])
    c_in = h[:, 0:D_HALF]

    _load_history(c_ext, bufp_ref, li == 0, HIST_P, POOL_BUF, seg)
    c_ext[:, HIST_P:HIST_P + seg, :] = c_in.reshape(n_seq, seg, D_HALF)
    nbufp_ref[...] = c_ext[:, seg + HIST_P - POOL_BUF:seg + HIST_P, :]

    rc = min(CONV_ROWS, seg)
    for g, win in enumerate(POOL_WINDOWS):
        lanes = slice(g * LANES, (g + 1) * LANES)
        for s in range(n_seq):
            for r0 in range(0, seg, rc):
                cur = c_ext[s, HIST_P + r0:HIST_P + r0 + rc, lanes]
                acc = cur
                for j in range(1, win):
                    acc = acc + c_ext[s, HIST_P + r0 - j:HIST_P + r0 - j + rc, lanes]
                pos = pos0 + li * seg + r0 + lax.broadcasted_iota(jnp.int32, (rc, LANES), 0)
                cnt = jnp.minimum(pos + 1, win).astype(F32)
                pooled[s * seg + r0:s * seg + r0 + rc, lanes] = acc / cnt - cur
    c = _bdot(pooled[...], pw_ref[...]) * ps_ref[...]

    z = jax.nn.gelu(h[:, D_HALF:3 * D_HALF], approximate=True)
    u = z[:, 0:D_HALF]
    v = _layer_norm(z[:, D_HALF:2 * D_HALF], vg_ref[...], vb_ref[...])
    if emit_v:
        v_ref[...] = v.reshape(n_seq, seg, D_HALF)
    vb16 = v.astype(BF16)

    ri = lax.broadcasted_iota(jnp.int32, (n_mix, n_mix), 0) >> CHUNK_SHIFT
    ci = lax.broadcasted_iota(jnp.int32, (n_mix, n_mix), 1) >> CHUNK_SHIFT
    for g in range(N_HEAD_D):
        lanes = slice(g * LANES, (g + 1) * LANES)
        ws = jnp.where(ci <= ri, ws_ref[g], 0.0).astype(BF16)
        bias = bs_ref[:, g:g + 1]
        for r0 in range(0, rows, n_mix):
            mixed[r0:r0 + n_mix, lanes] = jnp.dot(ws, vb16[r0:r0 + n_mix, lanes], preferred_element_type=F32) + bias
    d = u * mixed[...]
    y = _bdot(c, w_out_ref[0:D_HALF, :]) + _bdot(d, w_out_ref[D_HALF:D_MODEL, :])
    y_ref[...] = _layer_norm(ALPHA * x + y, lng_ref[...], lnb_ref[...])


def _mixer_cd(x, row_start, buf_p, length, w_in, pw, ps, vg, vb, ws, bs, w_out, lng, lnb, *,
              n_seq, seg, pos0, emit_v):
    batch = buf_p.shape[0]
    d_in = w_in.shape[1]
    n_mix = min(length, GMLP_CHUNK)
    assert seg % n_mix == 0
    n_l = length // seg
    rows = n_seq * seg
    blk0 = row_start // rows
    ws_n = ws[:, :n_mix, :n_mix]
    bs_t = bs[:, :n_mix].T
    out_specs = [pl.BlockSpec((rows, D_MODEL), lambda b, l: (b * n_l + l, 0)),
                 pl.BlockSpec((n_seq, POOL_BUF, D_HALF), lambda b, l: (b, 0, 0))]
    out_shape = [jax.ShapeDtypeStruct((batch * length, D_MODEL), F32),
                 jax.ShapeDtypeStruct((batch, POOL_BUF, D_HALF), F32)]
    if emit_v:
        out_specs.append(pl.BlockSpec((n_seq, seg, D_HALF), lambda b, l: (b, l, 0)))
        out_shape.append(jax.ShapeDtypeStruct((batch, length, D_HALF), F32))
    return pl.pallas_call(
        functools.partial(_mixer_cd_kernel, n_seq=n_seq, seg=seg, pos0=pos0, n_mix=n_mix, emit_v=emit_v),
        grid=(batch // n_seq, n_l),
        in_specs=[pl.BlockSpec((rows, D_MODEL), lambda b, l: (blk0 + b * n_l + l, 0)),
                  pl.BlockSpec((n_seq, POOL_BUF, D_HALF), lambda b, l: (b, 0, 0)),
                  _full((D_MODEL, d_in)), _full((D_HALF, D_HALF)), _full((1, D_HALF)), _full((1, D_HALF)),
                  _full((1, D_HALF)), _full((N_HEAD_D, n_mix, n_mix)), _full((n_mix, N_HEAD_D)),
                  _full((D_MODEL, D_MODEL)), _full((1, D_MODEL)), _full((1, D_MODEL))],
        out_specs=out_specs,
        out_shape=out_shape,
        scratch_shapes=[pltpu.VMEM((n_seq, HIST_P + seg, D_HALF), F32),
                        pltpu.VMEM((rows, D_HALF), F32),
                        pltpu.VMEM((rows, D_HALF), F32)],
        compiler_params=_params(("arbitrary", "arbitrary")),
        name="mixer_cd",
    )(x, buf_p, w_in, pw, _row2(ps), _row2(vg), _row2(vb), ws_n, bs_t, w_out, _row2(lng), _row2(lnb))


def _attn_router_kernel(xp_ref, xs_ref, kp_ref, vp_ref, ks_ref, vs_ref, wq_ref, wo_ref, lng_ref, lnb_ref, rw_ref, rb_ref,
                        rows_ref, meta_ref, counts_ref,
                        q_scr, o_scr, carry, *, n_prompt_tiles, sample_seg):
    rows = ROW_TILE
    step = pl.program_id(0)
    is_prompt = step < n_prompt_tiles
    x = jnp.where(is_prompt, xp_ref[...], xs_ref[...])
    q_scr[...] = (_bdot(x, wq_ref[0]) * (MEM_HEAD_DIM ** -0.5)).astype(BF16)

    def heads(k_ref, v_ref, n_seq, seg):
        for s in range(n_seq):
            for hd in range(MEM_HEADS):
                cols = slice(hd * MEM_HEAD_DIM, (hd + 1) * MEM_HEAD_DIM)
                sc = lax.dot_general(q_scr[s * seg:(s + 1) * seg, cols], k_ref[s, :, cols], (((1,), (1,)), ((), ())),
                                     preferred_element_type=F32)
                p = jnp.exp(sc - jnp.max(sc, axis=-1, keepdims=True))
                den = jnp.sum(p, axis=-1, keepdims=True)
                o_scr[s * seg:(s + 1) * seg, cols] = _bdot(p, v_ref[s, :, cols]) / den

    @pl.when(is_prompt)
    def _():
        heads(kp_ref, vp_ref, 1, rows)

    @pl.when(jnp.logical_not(is_prompt))
    def _():
        heads(ks_ref, vs_ref, rows // sample_seg, sample_seg)

    x2 = _layer_norm(ALPHA * x + _bdot(o_scr[...], wo_ref[0]), lng_ref[...], lnb_ref[...])
    rows_ref[:, 0:D_MODEL] = x2

    nt = (((1,), (1,)), ((), ()))
    x2_hi, x2_lo = _split(x2)
    both = lax.dot_general(rw_ref[...], x2_hi, nt, preferred_element_type=F32)
    cross = lax.dot_general(rw_ref[0:N_EXPERTS, :], x2_lo, nt, preferred_element_type=F32)
    logits = both[0:N_EXPERTS] + both[N_EXPERTS:2 * N_EXPERTS] + cross + rb_ref[...]
    e = jnp.exp(logits - jnp.max(logits, axis=0, keepdims=True))
    scores = e / jnp.sum(e, axis=0, keepdims=True)
    eid = lax.broadcasted_iota(jnp.int32, (N_EXPERTS, rows), 0)
    egrp = eid >> GROUP_SHIFT
    best = jnp.max(jnp.where(egrp == 0, scores, -1.0), axis=0, keepdims=True)
    g_sel = jnp.zeros((1, rows), jnp.int32)
    for g in range(1, N_EXPERT_GROUPS):
        gs = jnp.max(jnp.where(egrp == g, scores, -1.0), axis=0, keepdims=True)
        upd = gs > best
        g_sel = jnp.where(upd, g, g_sel)
        best = jnp.where(upd, gs, best)
    masked = jnp.where(egrp == g_sel, scores, -1.0)
    m1 = jnp.max(masked, axis=0, keepdims=True)
    i1 = jnp.min(jnp.where(masked == m1, eid, N_EXPERTS), axis=0, keepdims=True)
    masked2 = jnp.where(eid == i1, -2.0, masked)
    m2 = jnp.max(masked2, axis=0, keepdims=True)
    i2 = jnp.min(jnp.where(masked2 == m2, eid, N_EXPERTS), axis=0, keepdims=True)
    tot = m1 + m2
    g1 = m1 / tot
    g2 = m2 / tot
    first_low = i1 < i2
    ea = jnp.where(first_low, i1, i2) & (GROUP_SIZE - 1)
    eb = jnp.where(first_low, i2, i1) & (GROUP_SIZE - 1)
    gate_a = jnp.where(first_low, g1, g2)
    gate_b = jnp.where(first_low, g2, g1)
    pair = eb - 1 + jnp.where(ea == 1, 2, 0) + jnp.where(ea == 2, 3, 0)
    cls = g_sel * PAIRS_PER_GROUP + pair

    lane_row = lax.broadcasted_iota(jnp.int32, (LANES, rows), 0)
    gates_t = jnp.where(lane_row == 0, gate_a, jnp.where(lane_row == 1, gate_b, 0.0))
    rows_ref[:, D_MODEL:ROW_W] = gates_t.T

    @pl.when(step == 0)
    def _():
        carry[...] = jnp.zeros_like(carry)

    onehot = (lax.broadcasted_iota(jnp.int32, (CLASS_ROWS, rows), 0) == cls).astype(F32)
    tri = (lax.broadcasted_iota(jnp.int32, (rows, rows), 0) <= lax.broadcasted_iota(jnp.int32, (rows, rows), 1))
    cum = jnp.dot(onehot.astype(BF16), tri.astype(F32).astype(BF16), preferred_element_type=F32)
    before = carry[:, 0:1]
    rank = jnp.sum(onehot * (before + cum), axis=0, keepdims=True) - 1.0
    sub = lax.broadcasted_iota(jnp.int32, (SUBLANES, rows), 0)
    meta_ref[...] = jnp.where(sub == 0, cls, jnp.where(sub == 1, rank.astype(jnp.int32), 0))
    carry[...] = carry[...] + cum[:, rows - 1:rows]
    counts_ref[...] = carry[...].astype(jnp.int32)


def _attn_router(x_p, x_s, kv_p, k_s, v_s, wq, wo, lng, lnb, rw_t, rb, *, layer, prompt_len, sample_len):
    rows = ROW_TILE
    n_p = x_p.shape[0] // rows
    n_s = x_s.shape[0] // rows
    tiles_per_seq = prompt_len // rows
    seq_per_tile = rows // sample_len
    total = x_p.shape[0] + x_s.shape[0]
    n_prompt_seq = n_p // tiles_per_seq
    kp0 = layer * n_prompt_seq
    vp0 = (DEPTH + layer) * n_prompt_seq
    s0 = layer * n_s
    p_idx = lambda i: jnp.minimum(i, n_p - 1)
    s_idx = lambda i: jnp.maximum(i - n_p, 0)
    layer_w = pl.BlockSpec((1, D_MODEL, D_MODEL), lambda i: (layer, 0, 0), pipeline_mode=pl.Buffered(1))
    return pl.pallas_call(
        functools.partial(_attn_router_kernel, n_prompt_tiles=n_p, sample_seg=sample_len),
        grid=(n_p + n_s,),
        in_specs=[pl.BlockSpec((rows, D_MODEL), lambda i: (p_idx(i), 0)),
                  pl.BlockSpec((rows, D_MODEL), lambda i: (s_idx(i), 0)),
                  pl.BlockSpec((1, N_MEM, D_MODEL), lambda i: (kp0 + p_idx(i) // tiles_per_seq, 0, 0)),
                  pl.BlockSpec((1, N_MEM, D_MODEL), lambda i: (vp0 + p_idx(i) // tiles_per_seq, 0, 0)),
                  pl.BlockSpec((seq_per_tile, N_MEM, D_MODEL), lambda i: (s0 + s_idx(i), 0, 0)),
                  pl.BlockSpec((seq_per_tile, N_MEM, D_MODEL), lambda i: (s0 + s_idx(i), 0, 0)),
                  layer_w, layer_w, _full((1, D_MODEL)), _full((1, D_MODEL)),
                  _full((2 * N_EXPERTS, D_MODEL)), _full((N_EXPERTS, 1))],
        out_specs=[pl.BlockSpec((rows, ROW_W), lambda i: (i, 0)),
                   pl.BlockSpec((SUBLANES, rows), lambda i: (0, i)),
                   pl.BlockSpec((CLASS_ROWS, LANES), lambda i: (0, 0))],
        out_shape=[jax.ShapeDtypeStruct((total, ROW_W), F32),
                   jax.ShapeDtypeStruct((SUBLANES, total), jnp.int32),
                   jax.ShapeDtypeStruct((CLASS_ROWS, LANES), jnp.int32)],
        scratch_shapes=[pltpu.VMEM((rows, D_MODEL), BF16),
                        pltpu.VMEM((rows, D_MODEL), F32),
                        pltpu.VMEM((CLASS_ROWS, LANES), F32)],
        compiler_params=_params(("arbitrary",)),
        name="attn_router",
    )(x_p, x_s, kv_p, kv_p, k_s, v_s, wq, wo, _row2(lng), _row2(lnb), rw_t, rb.reshape(N_EXPERTS, 1))


def _dispatch_kernel(cls_ref, rank_ref, offs_ref, cnt_ref, end_ref, used_ref, rows_ref, xs_hbm, tok_ref,
                     zeros, sem, zsem, *, tile, n_tiles):
    step = pl.program_id(0)

    def class_padding(c, go):
        start = offs_ref[c] + cnt_ref[c]
        head = (-start) & (SUBLANES - 1)
        bulk_start = pl.multiple_of(start + head, SUBLANES)
        bulk = pl.multiple_of(end_ref[c] - bulk_start, SUBLANES)

        def one(r, carry_):
            go(pltpu.make_async_copy(zeros.at[pl.ds(0, 1)], xs_hbm.at[pl.ds(r, 1)], zsem))
            return carry_
        lax.fori_loop(start, start + head, one, 0)

        @pl.when(bulk > 0)
        def _():
            go(pltpu.make_async_copy(zeros.at[pl.ds(0, bulk)], xs_hbm.at[pl.ds(bulk_start, bulk)], zsem))

    def unused_tiles(go):
        def one(j, carry_):
            go(pltpu.make_async_copy(zeros, xs_hbm.at[pl.ds(pl.multiple_of(j * MOE_TILE, MOE_TILE), MOE_TILE)], zsem))
            return carry_
        lax.fori_loop(used_ref[0], n_tiles, one, 0)

    @pl.when(step == 0)
    def _():
        zeros[...] = jnp.zeros_like(zeros)

        def no_token(p, carry_):
            tok_ref[p] = -1
            return carry_
        for c in range(N_CLASSES):
            lax.fori_loop(offs_ref[c] + cnt_ref[c], end_ref[c], no_token, 0)
            class_padding(c, lambda copy: copy.start())
        lax.fori_loop(used_ref[0] * MOE_TILE, n_tiles * MOE_TILE, no_token, 0)
        unused_tiles(lambda copy: copy.start())

    for r in range(tile):
        t = step * tile + r
        p = offs_ref[cls_ref[t]] + rank_ref[t]
        tok_ref[p] = t
        pltpu.make_async_copy(rows_ref.at[pl.ds(r, 1)], xs_hbm.at[pl.ds(p, 1)], sem).start()
    pltpu.make_async_copy(rows_ref, xs_hbm.at[pl.ds(0, tile)], sem).wait()

    @pl.when(step == 0)
    def _():
        for c in range(N_CLASSES):
            class_padding(c, lambda copy: copy.wait())
        unused_tiles(lambda copy: copy.wait())


def _dispatch(cls, rank, offs, cnt, end, used, rows, n_tiles):
    n_rows = rows.shape[0]
    n_sorted = n_tiles * MOE_TILE
    any_spec = pl.BlockSpec(memory_space=pl.ANY)
    return pl.pallas_call(
        functools.partial(_dispatch_kernel, tile=DISPATCH_TILE, n_tiles=n_tiles),
        grid_spec=pltpu.PrefetchScalarGridSpec(
            num_scalar_prefetch=6, grid=(n_rows // DISPATCH_TILE,),
            in_specs=[pl.BlockSpec((DISPATCH_TILE, ROW_W), lambda i, *_: (i, 0))],
            out_specs=[any_spec, pl.BlockSpec(memory_space=pltpu.SMEM)],
            scratch_shapes=[pltpu.VMEM((MOE_TILE, ROW_W), F32), pltpu.SemaphoreType.DMA(()),
                            pltpu.SemaphoreType.DMA(())]),
        out_shape=[jax.ShapeDtypeStruct((n_sorted, ROW_W), F32), jax.ShapeDtypeStruct((n_sorted,), jnp.int32)],
        compiler_params=_params(("arbitrary",)),
        name="dispatch",
    )(cls, rank, offs, cnt, end, used, rows)


def _experts_kernel(ea_ref, eb_ref, blk_ref, nvalid_ref, tok_ref, xs_ref, wga_ref, wua_ref, wda_ref, wgb_ref, wub_ref,
                    wdb_ref, lng_ref, lnb_ref, out_hbm, ybuf, sems, *, n_rows):
    del ea_ref, eb_ref, blk_ref
    i = pl.program_id(0)
    n_cur = nvalid_ref[i]
    n_prev = jnp.where(i >= 1, nvalid_ref[jnp.maximum(i - 1, 0)], 0)
    n_prev2 = jnp.where(i >= 2, nvalid_ref[jnp.maximum(i - 2, 0)], 0)
    has_cur = n_cur > 0
    has_prev = n_prev > 0

    def tile_copy(slot):
        return pltpu.make_async_copy(ybuf.at[slot], out_hbm.at[pl.ds(0, MOE_TILE)], sems.at[slot])

    def compute(slot):
        x = xs_ref[:, 0:D_MODEL]
        xb = x.astype(BF16)
        y = None
        for lane, (wg, wu, wd) in enumerate(((wga_ref, wua_ref, wda_ref), (wgb_ref, wub_ref, wdb_ref))):
            gate = xs_ref[:, D_MODEL + lane:D_MODEL + lane + 1]
            hg = jnp.dot(xb, wg[0].astype(BF16), preferred_element_type=F32)
            hu = jnp.dot(xb, wu[0].astype(BF16), preferred_element_type=F32)
            hid = hg * jax.nn.sigmoid(hg) * hu * gate
            part = _bdot(hid, wd[0].astype(BF16))
            y = part if y is None else y + part
        ybuf[slot] = _layer_norm(ALPHA * x + y, lng_ref[...], lnb_ref[...])

    def start_rows(tile, slot):
        base = tile * MOE_TILE
        spare = n_rows + (tile & 1) * MOE_TILE
        for r in range(MOE_TILE):
            t = tok_ref[base + r]
            dst = jnp.where(t < 0, spare + r, t)
            pltpu.make_async_copy(ybuf.at[slot, pl.ds(r, 1)], out_hbm.at[pl.ds(dst, 1)], sems.at[slot]).start()

    @pl.when(i == 0)
    def _():
        ybuf[2] = jnp.zeros((MOE_TILE, D_MODEL), F32)
        for half in range(2):
            fill = pltpu.make_async_copy(ybuf.at[2], out_hbm.at[pl.ds(n_rows + half * MOE_TILE, MOE_TILE)], sems.at[2])
            fill.start()
            fill.wait()

    slot_cur = i % 3
    slot_prev = (i + 2) % 3

    @pl.when(has_prev & has_cur)
    def _():
        start_rows(i - 1, slot_prev)
        compute(slot_cur)

    @pl.when(has_prev & jnp.logical_not(has_cur))
    def _():
        start_rows(i - 1, slot_prev)

    @pl.when(jnp.logical_not(has_prev) & has_cur)
    def _():
        compute(slot_cur)

    @pl.when(n_prev2 > 0)
    def _():
        tile_copy((i + 1) % 3).wait()


def _experts(tile_a, tile_b, tile_blk, tile_nvalid, tok, xs, w_gate, w_up, w_down, lng, lnb, n_rows, layer):
    n_steps = tile_nvalid.shape[0]
    e0 = layer * N_EXPERTS
    up_a = pl.BlockSpec((1, D_MODEL, D_EXPERT), lambda i, ea, eb, *_: (e0 + ea[i], 0, 0))
    up_b = pl.BlockSpec((1, D_MODEL, D_EXPERT), lambda i, ea, eb, *_: (e0 + eb[i], 0, 0))
    dn_a = pl.BlockSpec((1, D_EXPERT, D_MODEL), lambda i, ea, eb, *_: (e0 + ea[i], 0, 0))
    dn_b = pl.BlockSpec((1, D_EXPERT, D_MODEL), lambda i, ea, eb, *_: (e0 + eb[i], 0, 0))
    vec = pl.BlockSpec((1, D_MODEL), lambda i, *_: (0, 0))
    return pl.pallas_call(
        functools.partial(_experts_kernel, n_rows=n_rows),
        grid_spec=pltpu.PrefetchScalarGridSpec(
            num_scalar_prefetch=5, grid=(n_steps,),
            in_specs=[pl.BlockSpec((MOE_TILE, ROW_W), lambda i, ea, eb, blk, *_: (blk[i], 0)),
                      up_a, up_a, dn_a, up_b, up_b, dn_b, vec, vec],
            out_specs=pl.BlockSpec(memory_space=pl.ANY),
            scratch_shapes=[pltpu.VMEM((3, MOE_TILE, D_MODEL), F32), pltpu.SemaphoreType.DMA((3,))]),
        out_shape=jax.ShapeDtypeStruct((n_rows + 2 * MOE_TILE, D_MODEL), F32),
        compiler_params=_params(("arbitrary",)),
        name="experts",
    )(tile_a, tile_b, tile_blk, tile_nvalid, tok, xs, w_gate, w_up, w_down, w_gate, w_up, w_down,
      _row2(lng), _row2(lnb))


def _lookup(table, idx):
    pick = idx[:, None] == jnp.arange(table.shape[0], dtype=jnp.int32)[None, :]
    return jnp.sum(jnp.where(pick, table[None, :], 0), axis=1)


def _moe(rows, meta, counts, w_gate, w_up, w_down, lng, lnb, layer):
    n_rows = rows.shape[0]
    n_tiles = n_rows // MOE_TILE + N_CLASSES
    cnt = counts[:N_CLASSES, 0]
    tiles_per = (cnt + MOE_TILE - 1) // MOE_TILE
    cls_id = jnp.arange(N_CLASSES, dtype=jnp.int32)
    tile_end = jnp.sum(jnp.where(cls_id[None, :] <= cls_id[:, None], tiles_per[None, :], 0), axis=1)
    tile_start = tile_end - tiles_per
    offs = tile_start * MOE_TILE
    end = tile_end * MOE_TILE
    used = tile_end[N_CLASSES - 1:N_CLASSES]
    tile_id = jnp.arange(n_tiles + 2, dtype=jnp.int32)
    tile_blk = jnp.minimum(tile_id, used - 1)
    tile_cls = jnp.sum((tile_blk[:, None] >= tile_end[None, :]).astype(jnp.int32), axis=1)
    tile_a = _lookup(jnp.asarray(_CLASS_A), tile_cls)
    tile_b = _lookup(jnp.asarray(_CLASS_B), tile_cls)
    tile_nvalid = jnp.clip(_lookup(cnt, tile_cls) - (tile_id - _lookup(tile_start, tile_cls)) * MOE_TILE, 0, MOE_TILE)
    xs, tok = _dispatch(meta[0], meta[1], offs, cnt, end, used, rows, n_tiles)
    return _experts(tile_a, tile_b, tile_blk, tile_nvalid, tok, xs, w_gate, w_up, w_down, lng, lnb, n_rows, layer)


def kernel(x_prompt, x_sample, mem_prompt, cache_mem_k, cache_mem_v, state_conv_a, state_conv_b, state_pool,
           ln_g, ln_b, ab_w_in, ab_conv_a_w, ab_conv_a_b, ab_norm_a_g, ab_norm_a_b, ab_conv_b_w, ab_w_out,
           cd_w_in, cd_pool_w, cd_pool_scale, cd_v_norm_g, cd_v_norm_b, cd_w_s, cd_b_s, cd_w_out,
           ca_wq, ca_wk, ca_wv, ca_wo, router_w, router_b, moe_w_gate, moe_w_up, moe_w_down):
    assert ln_g.shape[0] == DEPTH and ab_w_in.shape[0] == 1 and cd_w_in.shape[0] == 1
    bsz, seq, _ = x_prompt.shape
    dec_b, dec_seq, _ = x_sample.shape
    n_prompt = bsz * seq
    n_sample = dec_b * dec_seq
    sample_seqs = ROW_TILE // dec_seq

    n_pool = len(POOL_WINDOWS)
    pool_c = D_HALF // n_pool
    pool_bd = jnp.zeros((D_HALF, D_HALF), F32)
    for g in range(n_pool):
        pool_bd = pool_bd.at[g * pool_c:(g + 1) * pool_c, g * pool_c:(g + 1) * pool_c].set(cd_pool_w[0, g])
    pool_bd = pool_bd.astype(BF16)
    ab_w_in_hi, ab_w_in_lo = _split_weight(ab_w_in[0])
    ab_w_out_hi, ab_w_out_lo = _split_weight(ab_w_out[0])
    rw_hi, rw_lo = _split_weight(router_w.T)
    rw_t = jnp.concatenate([rw_hi, rw_lo], axis=0)
    cd_w_in_b, cd_w_out_b = cd_w_in[0].astype(BF16), cd_w_out[0].astype(BF16)
    wq_b, wo_b = ca_wq.astype(BF16), ca_wo.astype(BF16)
    wg_b = moe_w_gate.reshape(DEPTH * N_EXPERTS, D_MODEL, D_EXPERT)
    wu_b = moe_w_up.reshape(DEPTH * N_EXPERTS, D_MODEL, D_EXPERT)
    wd_b = moe_w_down.reshape(DEPTH * N_EXPERTS, D_EXPERT, D_MODEL)

    kv, kv16 = _mem_projection(mem_prompt.reshape(bsz * N_MEM, D_MODEL),
                               jnp.concatenate([ca_wk, ca_wv], axis=0).astype(BF16))
    kv_shape = (DEPTH, bsz, N_MEM, MEM_HEADS, MEM_HEAD_DIM)
    mem_k_prompt, mem_v_prompt = kv[0:DEPTH].reshape(kv_shape), kv[DEPTH:2 * DEPTH].reshape(kv_shape)
    kv16 = kv16.reshape(2 * DEPTH * bsz, N_MEM, D_MODEL)
    k_s = cache_mem_k.astype(BF16).reshape(DEPTH * dec_b, N_MEM, D_MODEL)
    v_s = cache_mem_v.astype(BF16).reshape(DEPTH * dec_b, N_MEM, D_MODEL)

    def mixer_ab(x, buf_a, buf_b, n_seq, seg, **kw):
        return _mixer_ab(x, buf_a, buf_b, ab_w_in_hi, ab_w_in_lo, ab_conv_a_w[0], ab_conv_a_b[0], ab_norm_a_g[0],
                         ab_norm_a_b[0], ab_conv_b_w[0], ab_w_out_hi, ab_w_out_lo, ln_g[0, 0], ln_b[0, 0],
                         n_seq=n_seq, seg=seg, **kw)

    def mixer_cd(x, row_start, buf_p, length, n_seq, seg, pos0, emit_v):
        return _mixer_cd(x, row_start, buf_p, length, cd_w_in_b, pool_bd, cd_pool_scale[0], cd_v_norm_g[0],
                         cd_v_norm_b[0], cd_w_s[0], cd_b_s[0], cd_w_out_b, ln_g[1, 0], ln_b[1, 0],
                         n_seq=n_seq, seg=seg, pos0=pos0, emit_v=emit_v)

    def attn_moe(layer, x_p, x_s):
        rows, meta, counts = _attn_router(x_p, x_s, kv16, k_s, v_s, wq_b, wo_b, ln_g[layer, 1], ln_b[layer, 1],
                                          rw_t, router_b, layer=layer, prompt_len=seq, sample_len=dec_seq)
        return _moe(rows, meta, counts, wg_b, wu_b, wd_b, ln_g[layer, 2], ln_b[layer, 2], layer)

    xp_flat = x_prompt.reshape(n_prompt, D_MODEL)
    tiles_p = seq // ROW_TILE
    x_p, conv_a_p, conv_b_p = mixer_ab(xp_flat, jnp.zeros((bsz, CONV_A - 1, D_HALF), F32),
                                       jnp.zeros((bsz, CONV_B - 1, D_HALF), F32), 1, ROW_TILE,
                                       precise=False, skip_tail=STATE_TILES)
    x_p, conv_a_p, conv_b_p = mixer_ab(xp_flat, conv_a_p, conv_b_p, 1, ROW_TILE, precise=True,
                                       tile_lo=tiles_p - STATE_TILES, tiles=STATE_TILES, y_prev=x_p)
    x_s, conv_a_s, conv_b_s = mixer_ab(x_sample.reshape(n_sample, D_MODEL), state_conv_a[0], state_conv_b[0],
                                       sample_seqs, dec_seq, precise=False)
    x_all = attn_moe(0, x_p, x_s)

    x_p, pool_p = mixer_cd(x_all, 0, jnp.zeros((bsz, POOL_BUF, D_HALF), F32), seq, 1, ROW_TILE, 0, False)
    x_s, pool_s, v_s_rows = mixer_cd(x_all, n_prompt, state_pool[0], dec_seq, sample_seqs, dec_seq, PAST_LEN, True)
    x_all = attn_moe(1, x_p, x_s)

    y_prompt = x_all[0:n_prompt].reshape(bsz, seq, D_MODEL)
    y_sample = x_all[n_prompt:n_prompt + n_sample].reshape(dec_b, dec_seq, D_MODEL)
    return (y_prompt, y_sample, mem_k_prompt, mem_v_prompt, conv_a_p[None], conv_b_p[None], pool_p[None],
            conv_a_s[None], conv_b_s[None], pool_s[None], v_s_rows[None])
```

```python
import functools

import jax
import jax.numpy as jnp
import numpy as np
from jax import lax
from jax.experimental import pallas as pl
from jax.experimental.pallas import tpu as pltpu

F32 = jnp.float32
BF16 = jnp.bfloat16

D_MODEL = 1024
D_HALF = D_MODEL // 2
DEPTH = 2
PAST_LEN = 4096
CHUNK = 64
CHUNK_SHIFT = CHUNK.bit_length() - 1
CONV_A = 31
CONV_B = 3
POOL_WINDOWS = (2, 4, 8, 16)
POOL_BUF = max(POOL_WINDOWS) - 1
N_HEAD_D = 4
GMLP_CHUNK = 128
N_MEM = 256
MEM_HEADS = 4
MEM_HEAD_DIM = D_MODEL // MEM_HEADS
N_EXPERTS = 16
N_EXPERT_GROUPS = 4
GROUP_SIZE = N_EXPERTS // N_EXPERT_GROUPS
GROUP_SHIFT = GROUP_SIZE.bit_length() - 1
PAIRS_PER_GROUP = GROUP_SIZE * (GROUP_SIZE - 1) // 2
N_CLASSES = N_EXPERT_GROUPS * PAIRS_PER_GROUP
D_EXPERT = D_MODEL // 2
ALPHA = (2 * DEPTH) ** 0.25
LN_EPS = 1e-5

LANES = 128
SUBLANES = 8
ROW_W = D_MODEL + LANES
HIST_A = 32
HIST_B = 8
HIST_P = 16
CONV_ROWS = 64
ROW_TILE = 512
MOE_TILE = 256
CLASS_ROWS = 32
DISPATCH_TILE = 1024
SPLIT_ROWS = 256
CACHE_SEQS = 4
STATE_TILES = 1
VMEM_LIMIT = 56 * 1024 * 1024

_PAIR_AB = [(a, b) for a in range(GROUP_SIZE) for b in range(a + 1, GROUP_SIZE)]
_CLASS_A = np.array([g * GROUP_SIZE + a for g in range(N_EXPERT_GROUPS) for a, _ in _PAIR_AB], np.int32)
_CLASS_B = np.array([g * GROUP_SIZE + b for g in range(N_EXPERT_GROUPS) for _, b in _PAIR_AB], np.int32)


def _layer_norm(x, g, b):
    mu = jnp.mean(x, axis=-1, keepdims=True)
    xc = x - mu
    var = jnp.mean(xc * xc, axis=-1, keepdims=True)
    return xc * lax.rsqrt(var + LN_EPS) * g + b


def _bdot(a, w):
    return jnp.dot(a.astype(BF16), w, preferred_element_type=F32)


def _split(a):
    hi = a.astype(BF16)
    return hi, (a - hi.astype(F32)).astype(BF16)


def _dot3(a, w_hi, w_lo):
    a_hi, a_lo = _split(a)
    return (jnp.dot(a_hi, w_hi, preferred_element_type=F32) + jnp.dot(a_lo, w_hi, preferred_element_type=F32)
            + jnp.dot(a_hi, w_lo, preferred_element_type=F32))


def _params(sem):
    return pltpu.CompilerParams(dimension_semantics=sem, vmem_limit_bytes=VMEM_LIMIT)


def _full(shape):
    return pl.BlockSpec(shape, lambda *_: (0,) * len(shape), pipeline_mode=pl.Buffered(1))


def _row2(v):
    return v.reshape(1, -1)


def _split_weight_kernel(w_ref, hi_ref, lo_ref):
    hi, lo = _split(w_ref[...])
    hi_ref[...] = hi
    lo_ref[...] = lo


def _split_weight(w):
    rows, cols = w.shape
    blk = min(rows, SPLIT_ROWS)
    spec = pl.BlockSpec((blk, cols), lambda i: (i, 0))
    return pl.pallas_call(
        _split_weight_kernel,
        grid=(rows // blk,),
        in_specs=[spec],
        out_specs=[spec, spec],
        out_shape=[jax.ShapeDtypeStruct(w.shape, BF16)] * 2,
        compiler_params=_params(("arbitrary",)),
        name="split_weight",
    )(w)


def _proj_kernel(x_ref, w_ref, o_ref, o16_ref, *, batch):
    res = _bdot(x_ref[...], w_ref[0].astype(BF16))
    o16_ref[...] = res.astype(BF16).reshape(batch, N_MEM, D_MODEL)
    for hd in range(MEM_HEADS):
        cols = slice(hd * MEM_HEAD_DIM, (hd + 1) * MEM_HEAD_DIM)
        o_ref[0, :, :, hd, :] = res[:, cols].reshape(batch, N_MEM, MEM_HEAD_DIM)


def _mem_projection(mem, w):
    batch, n = mem.shape[0], w.shape[0]
    return pl.pallas_call(
        functools.partial(_proj_kernel, batch=batch),
        grid=(n,),
        in_specs=[pl.BlockSpec((batch * N_MEM, D_MODEL), lambda j: (0, 0)),
                  pl.BlockSpec((1, D_MODEL, D_MODEL), lambda j: (j, 0, 0))],
        out_specs=[pl.BlockSpec((1, batch, N_MEM, MEM_HEADS, MEM_HEAD_DIM), lambda j: (j, 0, 0, 0, 0)),
                   pl.BlockSpec((batch, N_MEM, D_MODEL), lambda j: (j, 0, 0))],
        out_shape=[jax.ShapeDtypeStruct((n, batch, N_MEM, MEM_HEADS, MEM_HEAD_DIM), F32),
                   jax.ShapeDtypeStruct((n * batch, N_MEM, D_MODEL), BF16)],
        compiler_params=_params(("arbitrary",)),
        name="mem_projection",
    )(mem.reshape(batch * N_MEM, D_MODEL), w)


def _heads_to_columns_kernel(c_hbm, o_ref, buf, sem):
    step = pl.program_id(0)

    def head_copies(at_step, slot):
        return [pltpu.make_async_copy(c_hbm.at[at_step * CACHE_SEQS + s, :, hd, :], buf.at[slot, s, hd], sem.at[slot])
                for s in range(CACHE_SEQS) for hd in range(MEM_HEADS)]

    @pl.when(step == 0)
    def _():
        for copy in head_copies(0, 0):
            copy.start()

    @pl.when(step + 1 < pl.num_programs(0))
    def _():
        for copy in head_copies(step + 1, (step + 1) % 2):
            copy.start()

    slot = step % 2
    for copy in head_copies(step, slot):
        copy.wait()
    for s in range(CACHE_SEQS):
        for hd in range(MEM_HEADS):
            o_ref[s, :, hd * MEM_HEAD_DIM:(hd + 1) * MEM_HEAD_DIM] = buf[slot, s, hd].astype(BF16)


def _heads_to_columns(cache):
    n = cache.shape[0]
    return pl.pallas_call(
        _heads_to_columns_kernel,
        grid=(n // CACHE_SEQS,),
        in_specs=[pl.BlockSpec(memory_space=pl.ANY)],
        out_specs=pl.BlockSpec((CACHE_SEQS, N_MEM, D_MODEL), lambda i: (i, 0, 0)),
        out_shape=jax.ShapeDtypeStruct((n, N_MEM, D_MODEL), BF16),
        scratch_shapes=[pltpu.VMEM((2, CACHE_SEQS, MEM_HEADS, N_MEM, MEM_HEAD_DIM), F32),
                        pltpu.SemaphoreType.DMA((2,))],
        compiler_params=_params(("arbitrary",)),
        name="heads_to_columns",
    )(cache)


def _load_history(ext_ref, buf_ref, first, hist, keep, seg):
    @pl.when(first)
    def _():
        ext_ref[:, hist - keep:hist, :] = buf_ref[...]

    @pl.when(jnp.logical_not(first))
    def _():
        ext_ref[:, hist - keep:hist, :] = ext_ref[:, seg + hist - keep:seg + hist, :]


def _depthwise_conv(ext_ref, w_ref, out_ref, *, n_seq, seg, taps, hist, shifted_ref=None):
    rc = min(CONV_ROWS, seg)
    off0 = hist - (taps - 1)
    length = hist + seg
    if shifted_ref is not None:
        for r in range(1, SUBLANES):
            shifted_ref[r - 1, :, 0:length - SUBLANES, :] = ext_ref[:, r:r + length - SUBLANES, :]
    for s in range(n_seq):
        for r0 in range(0, seg, rc):
            for lb in range(0, D_HALF, LANES):
                acc = None
                for k in range(taps):
                    lo = off0 + k + r0
                    shift = (off0 + k) % SUBLANES
                    if shifted_ref is None or shift == 0:
                        win = ext_ref[s, lo:lo + rc, lb:lb + LANES]
                    else:
                        win = shifted_ref[shift - 1, s, lo - shift:lo - shift + rc, lb:lb + LANES]
                    term = w_ref[k:k + 1, lb:lb + LANES] * win
                    acc = term if acc is None else acc + term
                out_ref[s * seg + r0:s * seg + r0 + rc, lb:lb + LANES] = acc


def _mixer_ab_kernel(x_ref, bufa_ref, bufb_ref, w_in_ref, w_in_lo_ref, caw_ref, cab_ref, nag_ref, nab_ref, cbw_ref,
                     w_out_ref, w_out_lo_ref, lng_ref, lnb_ref, *refs, n_seq, seg, precise, skip_tail, has_prev):
    if has_prev:
        refs = refs[1:]
    y_ref, nbufa_ref, nbufb_ref, a_ext, a_shift, cb_ext, conv_a, conv_b = refs

    def body():
        first = pl.program_id(1) == 0
        _load_history(a_ext, bufa_ref, first, HIST_A, CONV_A - 1, seg)
        _load_history(cb_ext, bufb_ref, first, HIST_B, CONV_B - 1, seg)
        x = x_ref[...]
        h = _dot3(x, w_in_ref[...], w_in_lo_ref[...]) if precise else _bdot(x, w_in_ref[...])
        a = h[:, 0:D_HALF] * jax.nn.sigmoid(h[:, D_HALF:2 * D_HALF])
        cb = h[:, 3 * D_HALF:4 * D_HALF] * h[:, 4 * D_HALF:5 * D_HALF]
        a_ext[:, HIST_A:HIST_A + seg, :] = a.reshape(n_seq, seg, D_HALF)
        cb_ext[:, HIST_B:HIST_B + seg, :] = cb.reshape(n_seq, seg, D_HALF)
        nbufa_ref[...] = a_ext[:, seg + HIST_A - (CONV_A - 1):seg + HIST_A, :]
        nbufb_ref[...] = cb_ext[:, seg + HIST_B - (CONV_B - 1):seg + HIST_B, :]

        _depthwise_conv(a_ext, caw_ref, conv_a, n_seq=n_seq, seg=seg, taps=CONV_A, hist=HIST_A, shifted_ref=a_shift)
        _depthwise_conv(cb_ext, cbw_ref, conv_b, n_seq=n_seq, seg=seg, taps=CONV_B, hist=HIST_B)

        a2 = _layer_norm(conv_a[...] + cab_ref[...], nag_ref[...], nab_ref[...])
        a2 = a2 * jax.nn.sigmoid(a2)
        b2 = h[:, 2 * D_HALF:3 * D_HALF] * conv_b[...]
        if precise:
            y = (_dot3(a2, w_out_ref[0:D_HALF, :], w_out_lo_ref[0:D_HALF, :])
                 + _dot3(b2, w_out_ref[D_HALF:D_MODEL, :], w_out_lo_ref[D_HALF:D_MODEL, :]))
        else:
            y = _bdot(a2, w_out_ref[0:D_HALF, :]) + _bdot(b2, w_out_ref[D_HALF:D_MODEL, :])
        y_ref[...] = _layer_norm(ALPHA * x + y, lng_ref[...], lnb_ref[...])

    if skip_tail == 0:
        body()
    else:
        live = pl.program_id(1) < pl.num_programs(1) - skip_tail
        pl.when(live)(body)

        @pl.when(jnp.logical_not(live))
        def _():
            y_ref[...] = jnp.zeros_like(y_ref)


def _mixer_ab(x, buf_a, buf_b, w_in, w_in_lo, caw, cab, nag, nab, cbw, w_out, w_out_lo, lng, lnb, *,
              n_seq, seg, precise, tile_lo=0, tiles=None, skip_tail=0, y_prev=None):
    batch = buf_a.shape[0]
    length = x.shape[0] // batch
    d_in = w_in.shape[1]
    n_l = length // seg
    tiles = n_l if tiles is None else tiles
    rows = n_seq * seg
    lo_in = _full((D_MODEL, d_in)) if precise else _full((SUBLANES, LANES))
    lo_out = _full((D_MODEL, D_MODEL)) if precise else _full((SUBLANES, LANES))
    row_blk = pl.BlockSpec((rows, D_MODEL), lambda b, l: (b * n_l + tile_lo + l, 0))
    in_specs = [row_blk,
                pl.BlockSpec((n_seq, CONV_A - 1, D_HALF), lambda b, l: (b, 0, 0)),
                pl.BlockSpec((n_seq, CONV_B - 1, D_HALF), lambda b, l: (b, 0, 0)),
                _full((D_MODEL, d_in)), lo_in, _full((CONV_A, D_HALF)), _full((1, D_HALF)), _full((1, D_HALF)),
                _full((1, D_HALF)), _full((CONV_B, D_HALF)), _full((D_MODEL, D_MODEL)), lo_out,
                _full((1, D_MODEL)), _full((1, D_MODEL))]
    args = [x, buf_a, buf_b, w_in, w_in_lo, caw, _row2(cab), _row2(nag), _row2(nab), cbw, w_out, w_out_lo,
            _row2(lng), _row2(lnb)]
    aliases = {}
    if y_prev is not None:
        aliases = {len(args): 0}
        in_specs.append(pl.BlockSpec(memory_space=pl.ANY))
        args.append(y_prev)
    return pl.pallas_call(
        functools.partial(_mixer_ab_kernel, n_seq=n_seq, seg=seg, precise=precise, skip_tail=skip_tail,
                          has_prev=y_prev is not None),
        grid=(batch // n_seq, tiles),
        in_specs=in_specs,
        out_specs=[row_blk,
                   pl.BlockSpec((n_seq, CONV_A - 1, D_HALF), lambda b, l: (b, 0, 0)),
                   pl.BlockSpec((n_seq, CONV_B - 1, D_HALF), lambda b, l: (b, 0, 0))],
        out_shape=[jax.ShapeDtypeStruct((batch * length, D_MODEL), F32),
                   jax.ShapeDtypeStruct((batch, CONV_A - 1, D_HALF), F32),
                   jax.ShapeDtypeStruct((batch, CONV_B - 1, D_HALF), F32)],
        scratch_shapes=[pltpu.VMEM((n_seq, HIST_A + seg, D_HALF), F32),
                        pltpu.VMEM((SUBLANES - 1, n_seq, HIST_A + seg - SUBLANES, D_HALF), F32),
                        pltpu.VMEM((n_seq, HIST_B + seg, D_HALF), F32),
                        pltpu.VMEM((rows, D_HALF), F32),
                        pltpu.VMEM((rows, D_HALF), F32)],
        input_output_aliases=aliases,
        compiler_params=_params(("arbitrary", "arbitrary")),
        name="mixer_ab",
    )(*args)


def _mixer_cd_kernel(x_ref, bufp_ref, w_in_ref, pw_ref, ps_ref, vg_ref, vb_ref, ws_ref, bs_ref, w_out_ref,
                     lng_ref, lnb_ref, *refs, n_seq, seg, pos0, n_mix, emit_v):
    if emit_v:
        y_ref, nbufp_ref, v_ref, c_ext, pooled, mixed = refs
    else:
        y_ref, nbufp_ref, c_ext, pooled, mixed = refs
    rows = n_seq * seg
    li = pl.program_id(1)
    x = x_ref[...]
    h = _bdot(x, w_in_ref[...])
    c_in = h[:, 0:D_HALF]

    _load_history(c_ext, bufp_ref, li == 0, HIST_P, POOL_BUF, seg)
    c_ext[:, HIST_P:HIST_P + seg, :] = c_in.reshape(n_seq, seg, D_HALF)
    nbufp_ref[...] = c_ext[:, seg + HIST_P - POOL_BUF:seg + HIST_P, :]

    rc = min(CONV_ROWS, seg)
    for g, win in enumerate(POOL_WINDOWS):
        lanes = slice(g * LANES, (g + 1) * LANES)
        for s in range(n_seq):
            for r0 in range(0, seg, rc):
                cur = c_ext[s, HIST_P + r0:HIST_P + r0 + rc, lanes]
                acc = cur
                for j in range(1, win):
                    acc = acc + c_ext[s, HIST_P + r0 - j:HIST_P + r0 - j + rc, lanes]
                pos = pos0 + li * seg + r0 + lax.broadcasted_iota(jnp.int32, (rc, LANES), 0)
                cnt = jnp.minimum(pos + 1, win).astype(F32)
                pooled[s * seg + r0:s * seg + r0 + rc, lanes] = acc / cnt - cur
    c = _bdot(pooled[...], pw_ref[...]) * ps_ref[...]

    z = jax.nn.gelu(h[:, D_HALF:3 * D_HALF], approximate=True)
    u = z[:, 0:D_HALF]
    v = _layer_norm(z[:, D_HALF:2 * D_HALF], vg_ref[...], vb_ref[...])
    if emit_v:
        v_ref[...] = v.reshape(n_seq, seg, D_HALF)
    vb16 = v.astype(BF16)

    ri = lax.broadcasted_iota(jnp.int32, (n_mix, n_mix), 0) >> CHUNK_SHIFT
    ci = lax.broadcasted_iota(jnp.int32, (n_mix, n_mix), 1) >> CHUNK_SHIFT
    for g in range(N_HEAD_D):
        lanes = slice(g * LANES, (g + 1) * LANES)
        ws = jnp.where(ci <= ri, ws_ref[g], 0.0).astype(BF16)
        bias = bs_ref[:, g:g + 1]
        for r0 in range(0, rows, n_mix):
            mixed[r0:r0 + n_mix, lanes] = jnp.dot(ws, vb16[r0:r0 + n_mix, lanes], preferred_element_type=F32) + bias
    d = u * mixed[...]
    y = _bdot(c, w_out_ref[0:D_HALF, :]) + _bdot(d, w_out_ref[D_HALF:D_MODEL, :])
    y_ref[...] = _layer_norm(ALPHA * x + y, lng_ref[...], lnb_ref[...])


def _mixer_cd(x, row_start, buf_p, length, w_in, pw, ps, vg, vb, ws, bs, w_out, lng, lnb, *,
              n_seq, seg, pos0, emit_v):
    batch = buf_p.shape[0]
    d_in = w_in.shape[1]
    n_mix = min(length, GMLP_CHUNK)
    assert seg % n_mix == 0
    n_l = length // seg
    rows = n_seq * seg
    blk0 = row_start // rows
    ws_n = ws[:, :n_mix, :n_mix]
    bs_t = bs[:, :n_mix].T
    out_specs = [pl.BlockSpec((rows, D_MODEL), lambda b, l: (b * n_l + l, 0)),
                 pl.BlockSpec((n_seq, POOL_BUF, D_HALF), lambda b, l: (b, 0, 0))]
    out_shape = [jax.ShapeDtypeStruct((batch * length, D_MODEL), F32),
                 jax.ShapeDtypeStruct((batch, POOL_BUF, D_HALF), F32)]
    if emit_v:
        out_specs.append(pl.BlockSpec((n_seq, seg, D_HALF), lambda b, l: (b, l, 0)))
        out_shape.append(jax.ShapeDtypeStruct((batch, length, D_HALF), F32))
    return pl.pallas_call(
        functools.partial(_mixer_cd_kernel, n_seq=n_seq, seg=seg, pos0=pos0, n_mix=n_mix, emit_v=emit_v),
        grid=(batch // n_seq, n_l),
        in_specs=[pl.BlockSpec((rows, D_MODEL), lambda b, l: (blk0 + b * n_l + l, 0)),
                  pl.BlockSpec((n_seq, POOL_BUF, D_HALF), lambda b, l: (b, 0, 0)),
                  _full((D_MODEL, d_in)), _full((D_HALF, D_HALF)), _full((1, D_HALF)), _full((1, D_HALF)),
                  _full((1, D_HALF)), _full((N_HEAD_D, n_mix, n_mix)), _full((n_mix, N_HEAD_D)),
                  _full((D_MODEL, D_MODEL)), _full((1, D_MODEL)), _full((1, D_MODEL))],
        out_specs=out_specs,
        out_shape=out_shape,
        scratch_shapes=[pltpu.VMEM((n_seq, HIST_P + seg, D_HALF), F32),
                        pltpu.VMEM((rows, D_HALF), F32),
                        pltpu.VMEM((rows, D_HALF), F32)],
        compiler_params=_params(("arbitrary", "arbitrary")),
        name="mixer_cd",
    )(x, buf_p, w_in, pw, _row2(ps), _row2(vg), _row2(vb), ws_n, bs_t, w_out, _row2(lng), _row2(lnb))


def _attn_router_kernel(xp_ref, xs_ref, kp_ref, vp_ref, ks_ref, vs_ref, wq_ref, wo_ref, lng_ref, lnb_ref, rw_ref, rb_ref,
                        rows_ref, meta_ref, counts_ref,
                        q_scr, o_scr, carry, *, n_prompt_tiles, sample_seg):
    rows = ROW_TILE
    step = pl.program_id(0)
    is_prompt = step < n_prompt_tiles
    x = jnp.where(is_prompt, xp_ref[...], xs_ref[...])
    q_scr[...] = (_bdot(x, wq_ref[0]) * (MEM_HEAD_DIM ** -0.5)).astype(BF16)

    def heads(k_ref, v_ref, n_seq, seg):
        for s in range(n_seq):
            for hd in range(MEM_HEADS):
                cols = slice(hd * MEM_HEAD_DIM, (hd + 1) * MEM_HEAD_DIM)
                sc = lax.dot_general(q_scr[s * seg:(s + 1) * seg, cols], k_ref[s, :, cols], (((1,), (1,)), ((), ())),
                                     preferred_element_type=F32)
                p = jnp.exp(sc - jnp.max(sc, axis=-1, keepdims=True))
                den = jnp.sum(p, axis=-1, keepdims=True)
                o_scr[s * seg:(s + 1) * seg, cols] = _bdot(p, v_ref[s, :, cols]) / den

    @pl.when(is_prompt)
    def _():
        heads(kp_ref, vp_ref, 1, rows)

    @pl.when(jnp.logical_not(is_prompt))
    def _():
        heads(ks_ref, vs_ref, rows // sample_seg, sample_seg)

    x2 = _layer_norm(ALPHA * x + _bdot(o_scr[...], wo_ref[0]), lng_ref[...], lnb_ref[...])
    rows_ref[:, 0:D_MODEL] = x2

    nt = (((1,), (1,)), ((), ()))
    x2_hi, x2_lo = _split(x2)
    both = lax.dot_general(rw_ref[...], x2_hi, nt, preferred_element_type=F32)
    cross = lax.dot_general(rw_ref[0:N_EXPERTS, :], x2_lo, nt, preferred_element_type=F32)
    logits = both[0:N_EXPERTS] + both[N_EXPERTS:2 * N_EXPERTS] + cross + rb_ref[...]
    e = jnp.exp(logits - jnp.max(logits, axis=0, keepdims=True))
    scores = e / jnp.sum(e, axis=0, keepdims=True)
    eid = lax.broadcasted_iota(jnp.int32, (N_EXPERTS, rows), 0)
    egrp = eid >> GROUP_SHIFT
    best = jnp.max(jnp.where(egrp == 0, scores, -1.0), axis=0, keepdims=True)
    g_sel = jnp.zeros((1, rows), jnp.int32)
    for g in range(1, N_EXPERT_GROUPS):
        gs = jnp.max(jnp.where(egrp == g, scores, -1.0), axis=0, keepdims=True)
        upd = gs > best
        g_sel = jnp.where(upd, g, g_sel)
        best = jnp.where(upd, gs, best)
    masked = jnp.where(egrp == g_sel, scores, -1.0)
    m1 = jnp.max(masked, axis=0, keepdims=True)
    i1 = jnp.min(jnp.where(masked == m1, eid, N_EXPERTS), axis=0, keepdims=True)
    masked2 = jnp.where(eid == i1, -2.0, masked)
    m2 = jnp.max(masked2, axis=0, keepdims=True)
    i2 = jnp.min(jnp.where(masked2 == m2, eid, N_EXPERTS), axis=0, keepdims=True)
    tot = m1 + m2
    g1 = m1 / tot
    g2 = m2 / tot
    first_low = i1 < i2
    ea = jnp.where(first_low, i1, i2) & (GROUP_SIZE - 1)
    eb = jnp.where(first_low, i2, i1) & (GROUP_SIZE - 1)
    gate_a = jnp.where(first_low, g1, g2)
    gate_b = jnp.where(first_low, g2, g1)
    pair = eb - 1 + jnp.where(ea == 1, 2, 0) + jnp.where(ea == 2, 3, 0)
    cls = g_sel * PAIRS_PER_GROUP + pair

    lane_row = lax.broadcasted_iota(jnp.int32, (LANES, rows), 0)
    gates_t = jnp.where(lane_row == 0, gate_a, jnp.where(lane_row == 1, gate_b, 0.0))
    rows_ref[:, D_MODEL:ROW_W] = gates_t.T

    @pl.when(step == 0)
    def _():
        carry[...] = jnp.zeros_like(carry)

    onehot = (lax.broadcasted_iota(jnp.int32, (CLASS_ROWS, rows), 0) == cls).astype(F32)
    tri = (lax.broadcasted_iota(jnp.int32, (rows, rows), 0) <= lax.broadcasted_iota(jnp.int32, (rows, rows), 1))
    cum = jnp.dot(onehot.astype(BF16), tri.astype(F32).astype(BF16), preferred_element_type=F32)
    before = carry[:, 0:1]
    rank = jnp.sum(onehot * (before + cum), axis=0, keepdims=True) - 1.0
    sub = lax.broadcasted_iota(jnp.int32, (SUBLANES, rows), 0)
    meta_ref[...] = jnp.where(sub == 0, cls, jnp.where(sub == 1, rank.astype(jnp.int32), 0))
    carry[...] = carry[...] + cum[:, rows - 1:rows]
    counts_ref[...] = carry[...].astype(jnp.int32)


def _attn_router(x_p, x_s, k_p, v_p, k_s, v_s, wq, wo, lng, lnb, rw_t, rb, *, layer, prompt_len, sample_len):
    rows = ROW_TILE
    n_p = x_p.shape[0] // rows
    n_s = x_s.shape[0] // rows
    tiles_per_seq = prompt_len // rows
    seq_per_tile = rows // sample_len
    total = x_p.shape[0] + x_s.shape[0]
    n_prompt_seq = n_p // tiles_per_seq
    p0 = layer * n_prompt_seq
    s0 = layer * n_s
    p_idx = lambda i: jnp.minimum(i, n_p - 1)
    s_idx = lambda i: jnp.maximum(i - n_p, 0)
    layer_w = pl.BlockSpec((1, D_MODEL, D_MODEL), lambda i: (layer, 0, 0), pipeline_mode=pl.Buffered(1))
    return pl.pallas_call(
        functools.partial(_attn_router_kernel, n_prompt_tiles=n_p, sample_seg=sample_len),
        grid=(n_p + n_s,),
        in_specs=[pl.BlockSpec((rows, D_MODEL), lambda i: (p_idx(i), 0)),
                  pl.BlockSpec((rows, D_MODEL), lambda i: (s_idx(i), 0)),
                  pl.BlockSpec((1, N_MEM, D_MODEL), lambda i: (p0 + p_idx(i) // tiles_per_seq, 0, 0)),
                  pl.BlockSpec((1, N_MEM, D_MODEL), lambda i: (p0 + p_idx(i) // tiles_per_seq, 0, 0)),
                  pl.BlockSpec((seq_per_tile, N_MEM, D_MODEL), lambda i: (s0 + s_idx(i), 0, 0)),
                  pl.BlockSpec((seq_per_tile, N_MEM, D_MODEL), lambda i: (s0 + s_idx(i), 0, 0)),
                  layer_w, layer_w, _full((1, D_MODEL)), _full((1, D_MODEL)),
                  _full((2 * N_EXPERTS, D_MODEL)), _full((N_EXPERTS, 1))],
        out_specs=[pl.BlockSpec((rows, ROW_W), lambda i: (i, 0)),
                   pl.BlockSpec((SUBLANES, rows), lambda i: (0, i)),
                   pl.BlockSpec((CLASS_ROWS, LANES), lambda i: (0, 0))],
        out_shape=[jax.ShapeDtypeStruct((total, ROW_W), F32),
                   jax.ShapeDtypeStruct((SUBLANES, total), jnp.int32),
                   jax.ShapeDtypeStruct((CLASS_ROWS, LANES), jnp.int32)],
        scratch_shapes=[pltpu.VMEM((rows, D_MODEL), BF16),
                        pltpu.VMEM((rows, D_MODEL), F32),
                        pltpu.VMEM((CLASS_ROWS, LANES), F32)],
        compiler_params=_params(("arbitrary",)),
        name="attn_router",
    )(x_p, x_s, k_p, v_p, k_s, v_s, wq, wo, _row2(lng), _row2(lnb), rw_t, rb.reshape(N_EXPERTS, 1))


def _dispatch_kernel(cls_ref, rank_ref, offs_ref, cnt_ref, end_ref, used_ref, rows_ref, xs_hbm, tok_ref,
                     zeros, sem, zsem, *, tile, n_tiles):
    step = pl.program_id(0)

    def class_padding(c, go):
        start = offs_ref[c] + cnt_ref[c]
        head = (-start) & (SUBLANES - 1)
        bulk_start = pl.multiple_of(start + head, SUBLANES)
        bulk = pl.multiple_of(end_ref[c] - bulk_start, SUBLANES)

        def one(r, carry_):
            go(pltpu.make_async_copy(zeros.at[pl.ds(0, 1)], xs_hbm.at[pl.ds(r, 1)], zsem))
            return carry_
        lax.fori_loop(start, start + head, one, 0)

        @pl.when(bulk > 0)
        def _():
            go(pltpu.make_async_copy(zeros.at[pl.ds(0, bulk)], xs_hbm.at[pl.ds(bulk_start, bulk)], zsem))

    def unused_tiles(go):
        def one(j, carry_):
            go(pltpu.make_async_copy(zeros, xs_hbm.at[pl.ds(pl.multiple_of(j * MOE_TILE, MOE_TILE), MOE_TILE)], zsem))
            return carry_
        lax.fori_loop(used_ref[0], n_tiles, one, 0)

    @pl.when(step == 0)
    def _():
        zeros[...] = jnp.zeros_like(zeros)

        def no_token(p, carry_):
            tok_ref[p] = -1
            return carry_
        for c in range(N_CLASSES):
            lax.fori_loop(offs_ref[c] + cnt_ref[c], end_ref[c], no_token, 0)
            class_padding(c, lambda copy: copy.start())
        lax.fori_loop(used_ref[0] * MOE_TILE, n_tiles * MOE_TILE, no_token, 0)
        unused_tiles(lambda copy: copy.start())

    for r in range(tile):
        t = step * tile + r
        p = offs_ref[cls_ref[t]] + rank_ref[t]
        tok_ref[p] = t
        pltpu.make_async_copy(rows_ref.at[pl.ds(r, 1)], xs_hbm.at[pl.ds(p, 1)], sem).start()
    pltpu.make_async_copy(rows_ref, xs_hbm.at[pl.ds(0, tile)], sem).wait()

    @pl.when(step == 0)
    def _():
        for c in range(N_CLASSES):
            class_padding(c, lambda copy: copy.wait())
        unused_tiles(lambda copy: copy.wait())


def _dispatch(cls, rank, offs, cnt, end, used, rows, n_tiles):
    n_rows = rows.shape[0]
    n_sorted = n_tiles * MOE_TILE
    any_spec = pl.BlockSpec(memory_space=pl.ANY)
    return pl.pallas_call(
        functools.partial(_dispatch_kernel, tile=DISPATCH_TILE, n_tiles=n_tiles),
        grid_spec=pltpu.PrefetchScalarGridSpec(
            num_scalar_prefetch=6, grid=(n_rows // DISPATCH_TILE,),
            in_specs=[pl.BlockSpec((DISPATCH_TILE, ROW_W), lambda i, *_: (i, 0))],
            out_specs=[any_spec, pl.BlockSpec(memory_space=pltpu.SMEM)],
            scratch_shapes=[pltpu.VMEM((MOE_TILE, ROW_W), F32), pltpu.SemaphoreType.DMA(()),
                            pltpu.SemaphoreType.DMA(())]),
        out_shape=[jax.ShapeDtypeStruct((n_sorted, ROW_W), F32), jax.ShapeDtypeStruct((n_sorted,), jnp.int32)],
        compiler_params=_params(("arbitrary",)),
        name="dispatch",
    )(cls, rank, offs, cnt, end, used, rows)


def _experts_kernel(ea_ref, eb_ref, blk_ref, nvalid_ref, tok_ref, xs_ref, wga_ref, wua_ref, wda_ref, wgb_ref, wub_ref,
                    wdb_ref, lng_ref, lnb_ref, out_hbm, ybuf, sems, *, n_rows):
    del ea_ref, eb_ref, blk_ref
    i = pl.program_id(0)
    n_cur = nvalid_ref[i]
    n_prev = jnp.where(i >= 1, nvalid_ref[jnp.maximum(i - 1, 0)], 0)
    n_prev2 = jnp.where(i >= 2, nvalid_ref[jnp.maximum(i - 2, 0)], 0)
    has_cur = n_cur > 0
    has_prev = n_prev > 0

    def tile_copy(slot):
        return pltpu.make_async_copy(ybuf.at[slot], out_hbm.at[pl.ds(0, MOE_TILE)], sems.at[slot])

    def compute(slot):
        x = xs_ref[:, 0:D_MODEL]
        xb = x.astype(BF16)
        y = None
        for lane, (wg, wu, wd) in enumerate(((wga_ref, wua_ref, wda_ref), (wgb_ref, wub_ref, wdb_ref))):
            gate = xs_ref[:, D_MODEL + lane:D_MODEL + lane + 1]
            hg = jnp.dot(xb, wg[0].astype(BF16), preferred_element_type=F32)
            hu = jnp.dot(xb, wu[0].astype(BF16), preferred_element_type=F32)
            hid = hg * jax.nn.sigmoid(hg) * hu * gate
            part = _bdot(hid, wd[0].astype(BF16))
            y = part if y is None else y + part
        ybuf[slot] = _layer_norm(ALPHA * x + y, lng_ref[...], lnb_ref[...])

    def start_rows(tile, slot):
        base = tile * MOE_TILE
        spare = n_rows + (tile & 1) * MOE_TILE
        for r in range(MOE_TILE):
            t = tok_ref[base + r]
            dst = jnp.where(t < 0, spare + r, t)
            pltpu.make_async_copy(ybuf.at[slot, pl.ds(r, 1)], out_hbm.at[pl.ds(dst, 1)], sems.at[slot]).start()

    @pl.when(i == 0)
    def _():
        ybuf[2] = jnp.zeros((MOE_TILE, D_MODEL), F32)
        for half in range(2):
            fill = pltpu.make_async_copy(ybuf.at[2], out_hbm.at[pl.ds(n_rows + half * MOE_TILE, MOE_TILE)], sems.at[2])
            fill.start()
            fill.wait()

    slot_cur = i % 3
    slot_prev = (i + 2) % 3

    @pl.when(has_prev & has_cur)
    def _():
        start_rows(i - 1, slot_prev)
        compute(slot_cur)

    @pl.when(has_prev & jnp.logical_not(has_cur))
    def _():
        start_rows(i - 1, slot_prev)

    @pl.when(jnp.logical_not(has_prev) & has_cur)
    def _():
        compute(slot_cur)

    @pl.when(n_prev2 > 0)
    def _():
        tile_copy((i + 1) % 3).wait()


def _experts(tile_a, tile_b, tile_blk, tile_nvalid, tok, xs, w_gate, w_up, w_down, lng, lnb, n_rows, layer):
    n_steps = tile_nvalid.shape[0]
    e0 = layer * N_EXPERTS
    up_a = pl.BlockSpec((1, D_MODEL, D_EXPERT), lambda i, ea, eb, *_: (e0 + ea[i], 0, 0))
    up_b = pl.BlockSpec((1, D_MODEL, D_EXPERT), lambda i, ea, eb, *_: (e0 + eb[i], 0, 0))
    dn_a = pl.BlockSpec((1, D_EXPERT, D_MODEL), lambda i, ea, eb, *_: (e0 + ea[i], 0, 0))
    dn_b = pl.BlockSpec((1, D_EXPERT, D_MODEL), lambda i, ea, eb, *_: (e0 + eb[i], 0, 0))
    vec = pl.BlockSpec((1, D_MODEL), lambda i, *_: (0, 0))
    return pl.pallas_call(
        functools.partial(_experts_kernel, n_rows=n_rows),
        grid_spec=pltpu.PrefetchScalarGridSpec(
            num_scalar_prefetch=5, grid=(n_steps,),
            in_specs=[pl.BlockSpec((MOE_TILE, ROW_W), lambda i, ea, eb, blk, *_: (blk[i], 0)),
                      up_a, up_a, dn_a, up_b, up_b, dn_b, vec, vec],
            out_specs=pl.BlockSpec(memory_space=pl.ANY),
            scratch_shapes=[pltpu.VMEM((3, MOE_TILE, D_MODEL), F32), pltpu.SemaphoreType.DMA((3,))]),
        out_shape=jax.ShapeDtypeStruct((n_rows + 2 * MOE_TILE, D_MODEL), F32),
        compiler_params=_params(("arbitrary",)),
        name="experts",
    )(tile_a, tile_b, tile_blk, tile_nvalid, tok, xs, w_gate, w_up, w_down, w_gate, w_up, w_down,
      _row2(lng), _row2(lnb))


def _lookup(table, idx):
    pick = idx[:, None] == jnp.arange(table.shape[0], dtype=jnp.int32)[None, :]
    return jnp.sum(jnp.where(pick, table[None, :], 0), axis=1)


def _moe(rows, meta, counts, w_gate, w_up, w_down, lng, lnb, layer):
    n_rows = rows.shape[0]
    n_tiles = n_rows // MOE_TILE + N_CLASSES
    cnt = counts[:N_CLASSES, 0]
    tiles_per = (cnt + MOE_TILE - 1) // MOE_TILE
    cls_id = jnp.arange(N_CLASSES, dtype=jnp.int32)
    tile_end = jnp.sum(jnp.where(cls_id[None, :] <= cls_id[:, None], tiles_per[None, :], 0), axis=1)
    tile_start = tile_end - tiles_per
    offs = tile_start * MOE_TILE
    end = tile_end * MOE_TILE
    used = tile_end[N_CLASSES - 1:N_CLASSES]
    tile_id = jnp.arange(n_tiles + 2, dtype=jnp.int32)
    tile_blk = jnp.minimum(tile_id, used - 1)
    tile_cls = jnp.sum((tile_blk[:, None] >= tile_end[None, :]).astype(jnp.int32), axis=1)
    tile_a = _lookup(jnp.asarray(_CLASS_A), tile_cls)
    tile_b = _lookup(jnp.asarray(_CLASS_B), tile_cls)
    tile_nvalid = jnp.clip(_lookup(cnt, tile_cls) - (tile_id - _lookup(tile_start, tile_cls)) * MOE_TILE, 0, MOE_TILE)
    xs, tok = _dispatch(meta[0], meta[1], offs, cnt, end, used, rows, n_tiles)
    return _experts(tile_a, tile_b, tile_blk, tile_nvalid, tok, xs, w_gate, w_up, w_down, lng, lnb, n_rows, layer)


def kernel(x_prompt, x_sample, mem_prompt, cache_mem_k, cache_mem_v, state_conv_a, state_conv_b, state_pool,
           ln_g, ln_b, ab_w_in, ab_conv_a_w, ab_conv_a_b, ab_norm_a_g, ab_norm_a_b, ab_conv_b_w, ab_w_out,
           cd_w_in, cd_pool_w, cd_pool_scale, cd_v_norm_g, cd_v_norm_b, cd_w_s, cd_b_s, cd_w_out,
           ca_wq, ca_wk, ca_wv, ca_wo, router_w, router_b, moe_w_gate, moe_w_up, moe_w_down):
    assert ln_g.shape[0] == DEPTH and ab_w_in.shape[0] == 1 and cd_w_in.shape[0] == 1
    bsz, seq, _ = x_prompt.shape
    dec_b, dec_seq, _ = x_sample.shape
    n_prompt = bsz * seq
    n_sample = dec_b * dec_seq
    sample_seqs = ROW_TILE // dec_seq

    n_pool = len(POOL_WINDOWS)
    pool_c = D_HALF // n_pool
    pool_bd = jnp.zeros((D_HALF, D_HALF), F32)
    for g in range(n_pool):
        pool_bd = pool_bd.at[g * pool_c:(g + 1) * pool_c, g * pool_c:(g + 1) * pool_c].set(cd_pool_w[0, g])
    pool_bd = pool_bd.astype(BF16)
    ab_w_in_hi, ab_w_in_lo = _split_weight(ab_w_in[0])
    ab_w_out_hi, ab_w_out_lo = _split_weight(ab_w_out[0])
    rw_hi, rw_lo = _split_weight(router_w.T)
    rw_t = jnp.concatenate([rw_hi, rw_lo], axis=0)
    cd_w_in_b, cd_w_out_b = cd_w_in[0].astype(BF16), cd_w_out[0].astype(BF16)
    wq_b, wo_b = ca_wq.astype(BF16), ca_wo.astype(BF16)
    wg_b = moe_w_gate.reshape(DEPTH * N_EXPERTS, D_MODEL, D_EXPERT)
    wu_b = moe_w_up.reshape(DEPTH * N_EXPERTS, D_MODEL, D_EXPERT)
    wd_b = moe_w_down.reshape(DEPTH * N_EXPERTS, D_EXPERT, D_MODEL)

    mem_k_prompt, k_p = _mem_projection(mem_prompt, ca_wk)
    mem_v_prompt, v_p = _mem_projection(mem_prompt, ca_wv)
    k_s = _heads_to_columns(cache_mem_k.reshape(DEPTH * dec_b, N_MEM, MEM_HEADS, MEM_HEAD_DIM))
    v_s = _heads_to_columns(cache_mem_v.reshape(DEPTH * dec_b, N_MEM, MEM_HEADS, MEM_HEAD_DIM))

    def mixer_ab(x, buf_a, buf_b, n_seq, seg, **kw):
        return _mixer_ab(x, buf_a, buf_b, ab_w_in_hi, ab_w_in_lo, ab_conv_a_w[0], ab_conv_a_b[0], ab_norm_a_g[0],
                         ab_norm_a_b[0], ab_conv_b_w[0], ab_w_out_hi, ab_w_out_lo, ln_g[0, 0], ln_b[0, 0],
                         n_seq=n_seq, seg=seg, **kw)

    def mixer_cd(x, row_start, buf_p, length, n_seq, seg, pos0, emit_v):
        return _mixer_cd(x, row_start, buf_p, length, cd_w_in_b, pool_bd, cd_pool_scale[0], cd_v_norm_g[0],
                         cd_v_norm_b[0], cd_w_s[0], cd_b_s[0], cd_w_out_b, ln_g[1, 0], ln_b[1, 0],
                         n_seq=n_seq, seg=seg, pos0=pos0, emit_v=emit_v)

    def attn_moe(layer, x_p, x_s):
        rows, meta, counts = _attn_router(x_p, x_s, k_p, v_p, k_s, v_s, wq_b, wo_b, ln_g[layer, 1], ln_b[layer, 1],
                                          rw_t, router_b, layer=layer, prompt_len=seq, sample_len=dec_seq)
        return _moe(rows, meta, counts, wg_b, wu_b, wd_b, ln_g[layer, 2], ln_b[layer, 2], layer)

    xp_flat = x_prompt.reshape(n_prompt, D_MODEL)
    tiles_p = seq // ROW_TILE
    x_p, conv_a_p, conv_b_p = mixer_ab(xp_flat, jnp.zeros((bsz, CONV_A - 1, D_HALF), F32),
                                       jnp.zeros((bsz, CONV_B - 1, D_HALF), F32), 1, ROW_TILE,
                                       precise=False, skip_tail=STATE_TILES)
    x_p, conv_a_p, conv_b_p = mixer_ab(xp_flat, conv_a_p, conv_b_p, 1, ROW_TILE, precise=True,
                                       tile_lo=tiles_p - STATE_TILES, tiles=STATE_TILES, y_prev=x_p)
    x_s, conv_a_s, conv_b_s = mixer_ab(x_sample.reshape(n_sample, D_MODEL), state_conv_a[0], state_conv_b[0],
                                       sample_seqs, dec_seq, precise=False)
    x_all = attn_moe(0, x_p, x_s)

    x_p, pool_p = mixer_cd(x_all, 0, jnp.zeros((bsz, POOL_BUF, D_HALF), F32), seq, 1, ROW_TILE, 0, False)
    x_s, pool_s, v_s_rows = mixer_cd(x_all, n_prompt, state_pool[0], dec_seq, sample_seqs, dec_seq, PAST_LEN, True)
    x_all = attn_moe(1, x_p, x_s)

    y_prompt = x_all[0:n_prompt].reshape(bsz, seq, D_MODEL)
    y_sample = x_all[n_prompt:n_prompt + n_sample].reshape(dec_b, dec_seq, D_MODEL)
    return (y_prompt, y_sample, mem_k_prompt, mem_v_prompt, conv_a_p[None], conv_b_p[None], pool_p[None],
            conv_a_s[None], conv_b_s[None], pool_s[None], v_s_rows[None])
```

```python
import functools

import jax
import jax.numpy as jnp
import numpy as np
from jax import lax
from jax.experimental import pallas as pl
from jax.experimental.pallas import tpu as pltpu

F32 = jnp.float32
BF16 = jnp.bfloat16

D_MODEL = 1024
D_HALF = D_MODEL // 2
DEPTH = 2
PAST_LEN = 4096
CHUNK = 64
CHUNK_SHIFT = CHUNK.bit_length() - 1
CONV_A = 31
CONV_B = 3
POOL_WINDOWS = (2, 4, 8, 16)
POOL_BUF = max(POOL_WINDOWS) - 1
N_HEAD_D = 4
GMLP_CHUNK = 128
N_MEM = 256
MEM_HEADS = 4
MEM_HEAD_DIM = D_MODEL // MEM_HEADS
N_EXPERTS = 16
N_EXPERT_GROUPS = 4
GROUP_SIZE = N_EXPERTS // N_EXPERT_GROUPS
GROUP_SHIFT = GROUP_SIZE.bit_length() - 1
PAIRS_PER_GROUP = GROUP_SIZE * (GROUP_SIZE - 1) // 2
N_CLASSES = N_EXPERT_GROUPS * PAIRS_PER_GROUP
D_EXPERT = D_MODEL // 2
ALPHA = (2 * DEPTH) ** 0.25
LN_EPS = 1e-5

LANES = 128
SUBLANES = 8
ROW_W = D_MODEL + LANES
HIST_A = 32
HIST_B = 8
HIST_P = 16
CONV_ROWS = 64
ROW_TILE = 512
MOE_TILE = 256
MOE_SHIFT = MOE_TILE.bit_length() - 1
CLASS_ROWS = 32
DISPATCH_TILE = 1024
SPLIT_ROWS = 256
CACHE_SEQS = 4
STATE_TILES = 1
VMEM_LIMIT = 56 * 1024 * 1024

_PAIR_AB = [(a, b) for a in range(GROUP_SIZE) for b in range(a + 1, GROUP_SIZE)]
_CLASS_A = np.array([g * GROUP_SIZE + a for g in range(N_EXPERT_GROUPS) for a, _ in _PAIR_AB], np.int32)
_CLASS_B = np.array([g * GROUP_SIZE + b for g in range(N_EXPERT_GROUPS) for _, b in _PAIR_AB], np.int32)


def _layer_norm(x, g, b):
    mu = jnp.mean(x, axis=-1, keepdims=True)
    xc = x - mu
    var = jnp.mean(xc * xc, axis=-1, keepdims=True)
    return xc * lax.rsqrt(var + LN_EPS) * g + b


def _bdot(a, w):
    return jnp.dot(a.astype(BF16), w, preferred_element_type=F32)


def _split(a):
    hi = a.astype(BF16)
    return hi, (a - hi.astype(F32)).astype(BF16)


def _dot3(a, w_hi, w_lo):
    a_hi, a_lo = _split(a)
    return (jnp.dot(a_hi, w_hi, preferred_element_type=F32) + jnp.dot(a_lo, w_hi, preferred_element_type=F32)
            + jnp.dot(a_hi, w_lo, preferred_element_type=F32))


def _params(sem):
    return pltpu.CompilerParams(dimension_semantics=sem, vmem_limit_bytes=VMEM_LIMIT)


def _full(shape):
    return pl.BlockSpec(shape, lambda *_: (0,) * len(shape), pipeline_mode=pl.Buffered(1))


def _row2(v):
    return v.reshape(1, -1)


def _split_weight_kernel(w_ref, hi_ref, lo_ref):
    hi, lo = _split(w_ref[...])
    hi_ref[...] = hi
    lo_ref[...] = lo


def _split_weight(w):
    rows, cols = w.shape
    blk = min(rows, SPLIT_ROWS)
    spec = pl.BlockSpec((blk, cols), lambda i: (i, 0))
    return pl.pallas_call(
        _split_weight_kernel,
        grid=(rows // blk,),
        in_specs=[spec],
        out_specs=[spec, spec],
        out_shape=[jax.ShapeDtypeStruct(w.shape, BF16)] * 2,
        compiler_params=_params(("arbitrary",)),
        name="split_weight",
    )(w)


def _proj_kernel(x_ref, w_ref, o_ref, o16_ref, *, batch):
    res = _bdot(x_ref[...], w_ref[0].astype(BF16))
    o16_ref[...] = res.astype(BF16).reshape(batch, N_MEM, D_MODEL)
    for hd in range(MEM_HEADS):
        cols = slice(hd * MEM_HEAD_DIM, (hd + 1) * MEM_HEAD_DIM)
        o_ref[0, :, :, hd, :] = res[:, cols].reshape(batch, N_MEM, MEM_HEAD_DIM)


def _mem_projection(mem, w):
    batch, n = mem.shape[0], w.shape[0]
    return pl.pallas_call(
        functools.partial(_proj_kernel, batch=batch),
        grid=(n,),
        in_specs=[pl.BlockSpec((batch * N_MEM, D_MODEL), lambda j: (0, 0)),
                  pl.BlockSpec((1, D_MODEL, D_MODEL), lambda j: (j, 0, 0))],
        out_specs=[pl.BlockSpec((1, batch, N_MEM, MEM_HEADS, MEM_HEAD_DIM), lambda j: (j, 0, 0, 0, 0)),
                   pl.BlockSpec((batch, N_MEM, D_MODEL), lambda j: (j, 0, 0))],
        out_shape=[jax.ShapeDtypeStruct((n, batch, N_MEM, MEM_HEADS, MEM_HEAD_DIM), F32),
                   jax.ShapeDtypeStruct((n * batch, N_MEM, D_MODEL), BF16)],
        compiler_params=_params(("arbitrary",)),
        name="mem_projection",
    )(mem.reshape(batch * N_MEM, D_MODEL), w)


def _heads_to_columns_kernel(c_hbm, o_ref, buf, sem):
    step = pl.program_id(0)

    def head_copies(at_step, slot):
        return [pltpu.make_async_copy(c_hbm.at[at_step * CACHE_SEQS + s, :, hd, :], buf.at[slot, s, hd], sem.at[slot])
                for s in range(CACHE_SEQS) for hd in range(MEM_HEADS)]

    @pl.when(step == 0)
    def _():
        for copy in head_copies(0, 0):
            copy.start()

    @pl.when(step + 1 < pl.num_programs(0))
    def _():
        for copy in head_copies(step + 1, (step + 1) % 2):
            copy.start()

    slot = step % 2
    for copy in head_copies(step, slot):
        copy.wait()
    for s in range(CACHE_SEQS):
        for hd in range(MEM_HEADS):
            o_ref[s, :, hd * MEM_HEAD_DIM:(hd + 1) * MEM_HEAD_DIM] = buf[slot, s, hd].astype(BF16)


def _heads_to_columns(cache):
    n = cache.shape[0]
    return pl.pallas_call(
        _heads_to_columns_kernel,
        grid=(n // CACHE_SEQS,),
        in_specs=[pl.BlockSpec(memory_space=pl.ANY)],
        out_specs=pl.BlockSpec((CACHE_SEQS, N_MEM, D_MODEL), lambda i: (i, 0, 0)),
        out_shape=jax.ShapeDtypeStruct((n, N_MEM, D_MODEL), BF16),
        scratch_shapes=[pltpu.VMEM((2, CACHE_SEQS, MEM_HEADS, N_MEM, MEM_HEAD_DIM), F32),
                        pltpu.SemaphoreType.DMA((2,))],
        compiler_params=_params(("arbitrary",)),
        name="heads_to_columns",
    )(cache)


def _load_history(ext_ref, buf_ref, first, hist, keep, seg):
    @pl.when(first)
    def _():
        ext_ref[:, hist - keep:hist, :] = buf_ref[...]

    @pl.when(jnp.logical_not(first))
    def _():
        ext_ref[:, hist - keep:hist, :] = ext_ref[:, seg + hist - keep:seg + hist, :]


def _depthwise_conv(ext_ref, w_ref, out_ref, *, n_seq, seg, taps, hist, shifted_ref=None):
    rc = min(CONV_ROWS, seg)
    off0 = hist - (taps - 1)
    length = hist + seg
    if shifted_ref is not None:
        for r in range(1, SUBLANES):
            shifted_ref[r - 1, :, 0:length - SUBLANES, :] = ext_ref[:, r:r + length - SUBLANES, :]
    for s in range(n_seq):
        for r0 in range(0, seg, rc):
            for lb in range(0, D_HALF, LANES):
                acc = None
                for k in range(taps):
                    lo = off0 + k + r0
                    shift = (off0 + k) % SUBLANES
                    if shifted_ref is None or shift == 0:
                        win = ext_ref[s, lo:lo + rc, lb:lb + LANES]
                    else:
                        win = shifted_ref[shift - 1, s, lo - shift:lo - shift + rc, lb:lb + LANES]
                    term = w_ref[k:k + 1, lb:lb + LANES] * win
                    acc = term if acc is None else acc + term
                out_ref[s * seg + r0:s * seg + r0 + rc, lb:lb + LANES] = acc


def _mixer_ab_kernel(x_ref, bufa_ref, bufb_ref, w_in_ref, w_in_lo_ref, caw_ref, cab_ref, nag_ref, nab_ref, cbw_ref,
                     w_out_ref, w_out_lo_ref, lng_ref, lnb_ref, *refs, n_seq, seg, precise, skip_tail, has_prev):
    if has_prev:
        refs = refs[1:]
    y_ref, nbufa_ref, nbufb_ref, a_ext, a_shift, cb_ext, conv_a, conv_b = refs

    def body():
        first = pl.program_id(1) == 0
        _load_history(a_ext, bufa_ref, first, HIST_A, CONV_A - 1, seg)
        _load_history(cb_ext, bufb_ref, first, HIST_B, CONV_B - 1, seg)
        x = x_ref[...]
        h = _dot3(x, w_in_ref[...], w_in_lo_ref[...]) if precise else _bdot(x, w_in_ref[...])
        a = h[:, 0:D_HALF] * jax.nn.sigmoid(h[:, D_HALF:2 * D_HALF])
        cb = h[:, 3 * D_HALF:4 * D_HALF] * h[:, 4 * D_HALF:5 * D_HALF]
        a_ext[:, HIST_A:HIST_A + seg, :] = a.reshape(n_seq, seg, D_HALF)
        cb_ext[:, HIST_B:HIST_B + seg, :] = cb.reshape(n_seq, seg, D_HALF)
        nbufa_ref[...] = a_ext[:, seg + HIST_A - (CONV_A - 1):seg + HIST_A, :]
        nbufb_ref[...] = cb_ext[:, seg + HIST_B - (CONV_B - 1):seg + HIST_B, :]

        _depthwise_conv(a_ext, caw_ref, conv_a, n_seq=n_seq, seg=seg, taps=CONV_A, hist=HIST_A, shifted_ref=a_shift)
        _depthwise_conv(cb_ext, cbw_ref, conv_b, n_seq=n_seq, seg=seg, taps=CONV_B, hist=HIST_B)

        a2 = _layer_norm(conv_a[...] + cab_ref[...], nag_ref[...], nab_ref[...])
        a2 = a2 * jax.nn.sigmoid(a2)
        b2 = h[:, 2 * D_HALF:3 * D_HALF] * conv_b[...]
        if precise:
            y = (_dot3(a2, w_out_ref[0:D_HALF, :], w_out_lo_ref[0:D_HALF, :])
                 + _dot3(b2, w_out_ref[D_HALF:D_MODEL, :], w_out_lo_ref[D_HALF:D_MODEL, :]))
        else:
            y = _bdot(a2, w_out_ref[0:D_HALF, :]) + _bdot(b2, w_out_ref[D_HALF:D_MODEL, :])
        y_ref[...] = _layer_norm(ALPHA * x + y, lng_ref[...], lnb_ref[...])

    if skip_tail == 0:
        body()
    else:
        live = pl.program_id(1) < pl.num_programs(1) - skip_tail
        pl.when(live)(body)

        @pl.when(jnp.logical_not(live))
        def _():
            y_ref[...] = jnp.zeros_like(y_ref)


def _mixer_ab(x, buf_a, buf_b, w_in, w_in_lo, caw, cab, nag, nab, cbw, w_out, w_out_lo, lng, lnb, *,
              n_seq, seg, precise, tile_lo=0, tiles=None, skip_tail=0, y_prev=None):
    batch = buf_a.shape[0]
    length = x.shape[0] // batch
    d_in = w_in.shape[1]
    n_l = length // seg
    tiles = n_l if tiles is None else tiles
    rows = n_seq * seg
    lo_in = _full((D_MODEL, d_in)) if precise else _full((SUBLANES, LANES))
    lo_out = _full((D_MODEL, D_MODEL)) if precise else _full((SUBLANES, LANES))
    row_blk = pl.BlockSpec((rows, D_MODEL), lambda b, l: (b * n_l + tile_lo + l, 0))
    in_specs = [row_blk,
                pl.BlockSpec((n_seq, CONV_A - 1, D_HALF), lambda b, l: (b, 0, 0)),
                pl.BlockSpec((n_seq, CONV_B - 1, D_HALF), lambda b, l: (b, 0, 0)),
                _full((D_MODEL, d_in)), lo_in, _full((CONV_A, D_HALF)), _full((1, D_HALF)), _full((1, D_HALF)),
                _full((1, D_HALF)), _full((CONV_B, D_HALF)), _full((D_MODEL, D_MODEL)), lo_out,
                _full((1, D_MODEL)), _full((1, D_MODEL))]
    args = [x, buf_a, buf_b, w_in, w_in_lo, caw, _row2(cab), _row2(nag), _row2(nab), cbw, w_out, w_out_lo,
            _row2(lng), _row2(lnb)]
    aliases = {}
    if y_prev is not None:
        aliases = {len(args): 0}
        in_specs.append(pl.BlockSpec(memory_space=pl.ANY))
        args.append(y_prev)
    return pl.pallas_call(
        functools.partial(_mixer_ab_kernel, n_seq=n_seq, seg=seg, precise=precise, skip_tail=skip_tail,
                          has_prev=y_prev is not None),
        grid=(batch // n_seq, tiles),
        in_specs=in_specs,
        out_specs=[row_blk,
                   pl.BlockSpec((n_seq, CONV_A - 1, D_HALF), lambda b, l: (b, 0, 0)),
                   pl.BlockSpec((n_seq, CONV_B - 1, D_HALF), lambda b, l: (b, 0, 0))],
        out_shape=[jax.ShapeDtypeStruct((batch * length, D_MODEL), F32),
                   jax.ShapeDtypeStruct((batch, CONV_A - 1, D_HALF), F32),
                   jax.ShapeDtypeStruct((batch, CONV_B - 1, D_HALF), F32)],
        scratch_shapes=[pltpu.VMEM((n_seq, HIST_A + seg, D_HALF), F32),
                        pltpu.VMEM((SUBLANES - 1, n_seq, HIST_A + seg - SUBLANES, D_HALF), F32),
                        pltpu.VMEM((n_seq, HIST_B + seg, D_HALF), F32),
                        pltpu.VMEM((rows, D_HALF), F32),
                        pltpu.VMEM((rows, D_HALF), F32)],
        input_output_aliases=aliases,
        compiler_params=_params(("arbitrary", "arbitrary")),
        name="mixer_ab",
    )(*args)


def _mixer_cd_kernel(x_ref, bufp_ref, w_in_ref, pw_ref, ps_ref, vg_ref, vb_ref, ws_ref, bs_ref, w_out_ref,
                     lng_ref, lnb_ref, *refs, n_seq, seg, pos0, n_mix, emit_v):
    if emit_v:
        y_ref, nbufp_ref, v_ref, c_ext, pooled, mixed = refs
    else:
        y_ref, nbufp_ref, c_ext, pooled, mixed = refs
    rows = n_seq * seg
    li = pl.program_id(1)
    x = x_ref[...]
    h = _bdot(x, w_in_ref[...])
    c_in = h[:, 0:D_HALF]

    _load_history(c_ext, bufp_ref, li == 0, HIST_P, POOL_BUF, seg)
    c_ext[:, HIST_P:HIST_P + seg, :] = c_in.reshape(n_seq, seg, D_HALF)
    nbufp_ref[...] = c_ext[:, seg + HIST_P - POOL_BUF:seg + HIST_P, :]

    rc = min(CONV_ROWS, seg)
    for g, win in enumerate(POOL_WINDOWS):
        lanes = slice(g * LANES, (g + 1) * LANES)
        for s in range(n_seq):
            for r0 in range(0, seg, rc):
                cur = c_ext[s, HIST_P + r0:HIST_P + r0 + rc, lanes]
                acc = cur
                for j in range(1, win):
                    acc = acc + c_ext[s, HIST_P + r0 - j:HIST_P + r0 - j + rc, lanes]
                pos = pos0 + li * seg + r0 + lax.broadcasted_iota(jnp.int32, (rc, LANES), 0)
                cnt = jnp.minimum(pos + 1, win).astype(F32)
                pooled[s * seg + r0:s * seg + r0 + rc, lanes] = acc / cnt - cur
    c = _bdot(pooled[...], pw_ref[...]) * ps_ref[...]

    z = jax.nn.gelu(h[:, D_HALF:3 * D_HALF], approximate=True)
    u = z[:, 0:D_HALF]
    v = _layer_norm(z[:, D_HALF:2 * D_HALF], vg_ref[...], vb_ref[...])
    if emit_v:
        v_ref[...] = v.reshape(n_seq, seg, D_HALF)
    vb16 = v.astype(BF16)

    ri = lax.broadcasted_iota(jnp.int32, (n_mix, n_mix), 0) >> CHUNK_SHIFT
    ci = lax.broadcasted_iota(jnp.int32, (n_mix, n_mix), 1) >> CHUNK_SHIFT
    for g in range(N_HEAD_D):
        lanes = slice(g * LANES, (g + 1) * LANES)
        ws = jnp.where(ci <= ri, ws_ref[g], 0.0).astype(BF16)
        bias = bs_ref[:, g:g + 1]
        for r0 in range(0, rows, n_mix):
            mixed[r0:r0 + n_mix, lanes] = jnp.dot(ws, vb16[r0:r0 + n_mix, lanes], preferred_element_type=F32) + bias
    d = u * mixed[...]
    y = _bdot(c, w_out_ref[0:D_HALF, :]) + _bdot(d, w_out_ref[D_HALF:D_MODEL, :])
    y_ref[...] = _layer_norm(ALPHA * x + y, lng_ref[...], lnb_ref[...])


def _mixer_cd(x, row_start, buf_p, length, w_in, pw, ps, vg, vb, ws, bs, w_out, lng, lnb, *,
              n_seq, seg, pos0, emit_v):
    batch = buf_p.shape[0]
    d_in = w_in.shape[1]
    n_mix = min(length, GMLP_CHUNK)
    assert seg % n_mix == 0
    n_l = length // seg
    rows = n_seq * seg
    blk0 = row_start // rows
    ws_n = ws[:, :n_mix, :n_mix]
    bs_t = bs[:, :n_mix].T
    out_specs = [pl.BlockSpec((rows, D_MODEL), lambda b, l: (b * n_l + l, 0)),
                 pl.BlockSpec((n_seq, POOL_BUF, D_HALF), lambda b, l: (b, 0, 0))]
    out_shape = [jax.ShapeDtypeStruct((batch * length, D_MODEL), F32),
                 jax.ShapeDtypeStruct((batch, POOL_BUF, D_HALF), F32)]
    if emit_v:
        out_specs.append(pl.BlockSpec((n_seq, seg, D_HALF), lambda b, l: (b, l, 0)))
        out_shape.append(jax.ShapeDtypeStruct((batch, length, D_HALF), F32))
    return pl.pallas_call(
        functools.partial(_mixer_cd_kernel, n_seq=n_seq, seg=seg, pos0=pos0, n_mix=n_mix, emit_v=emit_v),
        grid=(batch // n_seq, n_l),
        in_specs=[pl.BlockSpec((rows, D_MODEL), lambda b, l: (blk0 + b * n_l + l, 0)),
                  pl.BlockSpec((n_seq, POOL_BUF, D_HALF), lambda b, l: (b, 0, 0)),
                  _full((D_MODEL, d_in)), _full((D_HALF, D_HALF)), _full((1, D_HALF)), _full((1, D_HALF)),
                  _full((1, D_HALF)), _full((N_HEAD_D, n_mix, n_mix)), _full((n_mix, N_HEAD_D)),
                  _full((D_MODEL, D_MODEL)), _full((1, D_MODEL)), _full((1, D_MODEL))],
        out_specs=out_specs,
        out_shape=out_shape,
        scratch_shapes=[pltpu.VMEM((n_seq, HIST_P + seg, D_HALF), F32),
                        pltpu.VMEM((rows, D_HALF), F32),
                        pltpu.VMEM((rows, D_HALF), F32)],
        compiler_params=_params(("arbitrary", "arbitrary")),
        name="mixer_cd",
    )(x, buf_p, w_in, pw, _row2(ps), _row2(vg), _row2(vb), ws_n, bs_t, w_out, _row2(lng), _row2(lnb))


def _attn_router_kernel(xp_ref, xs_ref, kp_ref, vp_ref, ks_ref, vs_ref, wq_ref, wo_ref, lng_ref, lnb_ref, rw_ref, rb_ref,
                        rows_ref, meta_ref, counts_ref,
                        q_scr, o_scr, carry, *, n_prompt_tiles, sample_seg):
    rows = ROW_TILE
    step = pl.program_id(0)
    is_prompt = step < n_prompt_tiles
    x = jnp.where(is_prompt, xp_ref[...], xs_ref[...])
    q_scr[...] = (_bdot(x, wq_ref[0]) * (MEM_HEAD_DIM ** -0.5)).astype(BF16)

    def heads(k_ref, v_ref, n_seq, seg):
        for s in range(n_seq):
            for hd in range(MEM_HEADS):
                cols = slice(hd * MEM_HEAD_DIM, (hd + 1) * MEM_HEAD_DIM)
                sc = lax.dot_general(q_scr[s * seg:(s + 1) * seg, cols], k_ref[s, :, cols], (((1,), (1,)), ((), ())),
                                     preferred_element_type=F32)
                p = jnp.exp(sc - jnp.max(sc, axis=-1, keepdims=True))
                den = jnp.sum(p, axis=-1, keepdims=True)
                o_scr[s * seg:(s + 1) * seg, cols] = _bdot(p, v_ref[s, :, cols]) / den

    @pl.when(is_prompt)
    def _():
        heads(kp_ref, vp_ref, 1, rows)

    @pl.when(jnp.logical_not(is_prompt))
    def _():
        heads(ks_ref, vs_ref, rows // sample_seg, sample_seg)

    x2 = _layer_norm(ALPHA * x + _bdot(o_scr[...], wo_ref[0]), lng_ref[...], lnb_ref[...])
    rows_ref[:, 0:D_MODEL] = x2

    nt = (((1,), (1,)), ((), ()))
    x2_hi, x2_lo = _split(x2)
    both = lax.dot_general(rw_ref[...], x2_hi, nt, preferred_element_type=F32)
    cross = lax.dot_general(rw_ref[0:N_EXPERTS, :], x2_lo, nt, preferred_element_type=F32)
    logits = both[0:N_EXPERTS] + both[N_EXPERTS:2 * N_EXPERTS] + cross + rb_ref[...]
    e = jnp.exp(logits - jnp.max(logits, axis=0, keepdims=True))
    scores = e / jnp.sum(e, axis=0, keepdims=True)
    eid = lax.broadcasted_iota(jnp.int32, (N_EXPERTS, rows), 0)
    egrp = eid >> GROUP_SHIFT
    best = jnp.max(jnp.where(egrp == 0, scores, -1.0), axis=0, keepdims=True)
    g_sel = jnp.zeros((1, rows), jnp.int32)
    for g in range(1, N_EXPERT_GROUPS):
        gs = jnp.max(jnp.where(egrp == g, scores, -1.0), axis=0, keepdims=True)
        upd = gs > best
        g_sel = jnp.where(upd, g, g_sel)
        best = jnp.where(upd, gs, best)
    masked = jnp.where(egrp == g_sel, scores, -1.0)
    m1 = jnp.max(masked, axis=0, keepdims=True)
    i1 = jnp.min(jnp.where(masked == m1, eid, N_EXPERTS), axis=0, keepdims=True)
    masked2 = jnp.where(eid == i1, -2.0, masked)
    m2 = jnp.max(masked2, axis=0, keepdims=True)
    i2 = jnp.min(jnp.where(masked2 == m2, eid, N_EXPERTS), axis=0, keepdims=True)
    tot = m1 + m2
    g1 = m1 / tot
    g2 = m2 / tot
    first_low = i1 < i2
    ea = jnp.where(first_low, i1, i2) & (GROUP_SIZE - 1)
    eb = jnp.where(first_low, i2, i1) & (GROUP_SIZE - 1)
    gate_a = jnp.where(first_low, g1, g2)
    gate_b = jnp.where(first_low, g2, g1)
    pair = eb - 1 + jnp.where(ea == 1, 2, 0) + jnp.where(ea == 2, 3, 0)
    cls = g_sel * PAIRS_PER_GROUP + pair

    lane_row = lax.broadcasted_iota(jnp.int32, (LANES, rows), 0)
    gates_t = jnp.where(lane_row == 0, gate_a, jnp.where(lane_row == 1, gate_b, 0.0))
    rows_ref[:, D_MODEL:ROW_W] = gates_t.T

    @pl.when(step == 0)
    def _():
        carry[...] = jnp.zeros_like(carry)

    onehot = (lax.broadcasted_iota(jnp.int32, (CLASS_ROWS, rows), 0) == cls).astype(F32)
    tri = (lax.broadcasted_iota(jnp.int32, (rows, rows), 0) <= lax.broadcasted_iota(jnp.int32, (rows, rows), 1))
    cum = jnp.dot(onehot.astype(BF16), tri.astype(F32).astype(BF16), preferred_element_type=F32)
    before = carry[:, 0:1]
    rank = jnp.sum(onehot * (before + cum), axis=0, keepdims=True) - 1.0
    sub = lax.broadcasted_iota(jnp.int32, (SUBLANES, rows), 0)
    meta_ref[...] = jnp.where(sub == 0, cls, jnp.where(sub == 1, rank.astype(jnp.int32), 0))
    carry[...] = carry[...] + cum[:, rows - 1:rows]
    counts_ref[...] = carry[...].astype(jnp.int32)


def _attn_router(x_p, x_s, k_p, v_p, k_s, v_s, wq, wo, lng, lnb, rw_t, rb, *, layer, prompt_len, sample_len):
    rows = ROW_TILE
    n_p = x_p.shape[0] // rows
    n_s = x_s.shape[0] // rows
    tiles_per_seq = prompt_len // rows
    seq_per_tile = rows // sample_len
    total = x_p.shape[0] + x_s.shape[0]
    n_prompt_seq = n_p // tiles_per_seq
    p0 = layer * n_prompt_seq
    s0 = layer * n_s
    p_idx = lambda i: jnp.minimum(i, n_p - 1)
    s_idx = lambda i: jnp.maximum(i - n_p, 0)
    layer_w = pl.BlockSpec((1, D_MODEL, D_MODEL), lambda i: (layer, 0, 0), pipeline_mode=pl.Buffered(1))
    return pl.pallas_call(
        functools.partial(_attn_router_kernel, n_prompt_tiles=n_p, sample_seg=sample_len),
        grid=(n_p + n_s,),
        in_specs=[pl.BlockSpec((rows, D_MODEL), lambda i: (p_idx(i), 0)),
                  pl.BlockSpec((rows, D_MODEL), lambda i: (s_idx(i), 0)),
                  pl.BlockSpec((1, N_MEM, D_MODEL), lambda i: (p0 + p_idx(i) // tiles_per_seq, 0, 0)),
                  pl.BlockSpec((1, N_MEM, D_MODEL), lambda i: (p0 + p_idx(i) // tiles_per_seq, 0, 0)),
                  pl.BlockSpec((seq_per_tile, N_MEM, D_MODEL), lambda i: (s0 + s_idx(i), 0, 0)),
                  pl.BlockSpec((seq_per_tile, N_MEM, D_MODEL), lambda i: (s0 + s_idx(i), 0, 0)),
                  layer_w, layer_w, _full((1, D_MODEL)), _full((1, D_MODEL)),
                  _full((2 * N_EXPERTS, D_MODEL)), _full((N_EXPERTS, 1))],
        out_specs=[pl.BlockSpec((rows, ROW_W), lambda i: (i, 0)),
                   pl.BlockSpec((SUBLANES, rows), lambda i: (0, i)),
                   pl.BlockSpec((CLASS_ROWS, LANES), lambda i: (0, 0))],
        out_shape=[jax.ShapeDtypeStruct((total, ROW_W), F32),
                   jax.ShapeDtypeStruct((SUBLANES, total), jnp.int32),
                   jax.ShapeDtypeStruct((CLASS_ROWS, LANES), jnp.int32)],
        scratch_shapes=[pltpu.VMEM((rows, D_MODEL), BF16),
                        pltpu.VMEM((rows, D_MODEL), F32),
                        pltpu.VMEM((CLASS_ROWS, LANES), F32)],
        compiler_params=_params(("arbitrary",)),
        name="attn_router",
    )(x_p, x_s, k_p, v_p, k_s, v_s, wq, wo, _row2(lng), _row2(lnb), rw_t, rb.reshape(N_EXPERTS, 1))


def _dispatch_kernel(pos_ref, offs_ref, cnt_ref, end_ref, used_ref, rows_ref, xs_hbm, dst_ref,
                     zeros, sem, zsem, *, tile, n_tiles, n_rows):
    step = pl.program_id(0)

    def class_padding(c, go):
        start = offs_ref[c] + cnt_ref[c]
        head = (-start) & (SUBLANES - 1)
        bulk_start = pl.multiple_of(start + head, SUBLANES)
        bulk = pl.multiple_of(end_ref[c] - bulk_start, SUBLANES)

        def one(r, carry_):
            go(pltpu.make_async_copy(zeros.at[pl.ds(0, 1)], xs_hbm.at[pl.ds(r, 1)], zsem))
            return carry_
        lax.fori_loop(start, start + head, one, 0)

        @pl.when(bulk > 0)
        def _():
            go(pltpu.make_async_copy(zeros.at[pl.ds(0, bulk)], xs_hbm.at[pl.ds(bulk_start, bulk)], zsem))

    def unused_tiles(go):
        def one(j, carry_):
            go(pltpu.make_async_copy(zeros, xs_hbm.at[pl.ds(pl.multiple_of(j * MOE_TILE, MOE_TILE), MOE_TILE)], zsem))
            return carry_
        lax.fori_loop(used_ref[0], n_tiles, one, 0)

    @pl.when(step == 0)
    def _():
        zeros[...] = jnp.zeros_like(zeros)

        def no_token(p, carry_):
            dst_ref[p] = n_rows + ((p >> MOE_SHIFT) & 1) * MOE_TILE + (p & (MOE_TILE - 1))
            return carry_
        for c in range(N_CLASSES):
            lax.fori_loop(offs_ref[c] + cnt_ref[c], end_ref[c], no_token, 0)
            class_padding(c, lambda copy: copy.start())
        lax.fori_loop(used_ref[0] * MOE_TILE, n_tiles * MOE_TILE, no_token, 0)
        unused_tiles(lambda copy: copy.start())

    for r in range(tile):
        t = step * tile + r
        p = pos_ref[t]
        dst_ref[p] = t
        pltpu.make_async_copy(rows_ref.at[pl.ds(r, 1)], xs_hbm.at[pl.ds(p, 1)], sem).start()
    pltpu.make_async_copy(rows_ref, xs_hbm.at[pl.ds(0, tile)], sem).wait()

    @pl.when(step == 0)
    def _():
        for c in range(N_CLASSES):
            class_padding(c, lambda copy: copy.wait())
        unused_tiles(lambda copy: copy.wait())


def _dispatch(pos, offs, cnt, end, used, rows, n_tiles):
    n_rows = rows.shape[0]
    n_sorted = n_tiles * MOE_TILE
    any_spec = pl.BlockSpec(memory_space=pl.ANY)
    return pl.pallas_call(
        functools.partial(_dispatch_kernel, tile=DISPATCH_TILE, n_tiles=n_tiles, n_rows=n_rows),
        grid_spec=pltpu.PrefetchScalarGridSpec(
            num_scalar_prefetch=5, grid=(n_rows // DISPATCH_TILE,),
            in_specs=[pl.BlockSpec((DISPATCH_TILE, ROW_W), lambda i, *_: (i, 0))],
            out_specs=[any_spec, pl.BlockSpec(memory_space=pltpu.SMEM)],
            scratch_shapes=[pltpu.VMEM((MOE_TILE, ROW_W), F32), pltpu.SemaphoreType.DMA(()),
                            pltpu.SemaphoreType.DMA(())]),
        out_shape=[jax.ShapeDtypeStruct((n_sorted, ROW_W), F32), jax.ShapeDtypeStruct((n_sorted,), jnp.int32)],
        compiler_params=_params(("arbitrary",)),
        name="dispatch",
    )(pos, offs, cnt, end, used, rows)


def _experts_kernel(ea_ref, eb_ref, blk_ref, nvalid_ref, dst_ref, xs_ref, wga_ref, wua_ref, wda_ref, wgb_ref, wub_ref,
                    wdb_ref, lng_ref, lnb_ref, out_hbm, ybuf, sems, *, n_rows):
    del ea_ref, eb_ref, blk_ref
    i = pl.program_id(0)
    n_cur = nvalid_ref[i]
    n_prev = jnp.where(i >= 1, nvalid_ref[jnp.maximum(i - 1, 0)], 0)
    n_prev2 = jnp.where(i >= 2, nvalid_ref[jnp.maximum(i - 2, 0)], 0)
    has_cur = n_cur > 0
    has_prev = n_prev > 0

    def tile_copy(slot):
        return pltpu.make_async_copy(ybuf.at[slot], out_hbm.at[pl.ds(0, MOE_TILE)], sems.at[slot])

    def compute(slot):
        x = xs_ref[:, 0:D_MODEL]
        xb = x.astype(BF16)
        y = None
        for lane, (wg, wu, wd) in enumerate(((wga_ref, wua_ref, wda_ref), (wgb_ref, wub_ref, wdb_ref))):
            gate = xs_ref[:, D_MODEL + lane:D_MODEL + lane + 1]
            hg = jnp.dot(xb, wg[0].astype(BF16), preferred_element_type=F32)
            hu = jnp.dot(xb, wu[0].astype(BF16), preferred_element_type=F32)
            hid = hg * jax.nn.sigmoid(hg) * hu * gate
            part = _bdot(hid, wd[0].astype(BF16))
            y = part if y is None else y + part
        ybuf[slot] = _layer_norm(ALPHA * x + y, lng_ref[...], lnb_ref[...])

    def start_rows(tile, slot):
        base = tile * MOE_TILE
        for r in range(MOE_TILE):
            dst = dst_ref[base + r]
            pltpu.make_async_copy(ybuf.at[slot, pl.ds(r, 1)], out_hbm.at[pl.ds(dst, 1)], sems.at[slot]).start()

    @pl.when(i == 0)
    def _():
        ybuf[2] = jnp.zeros((MOE_TILE, D_MODEL), F32)
        for half in range(2):
            fill = pltpu.make_async_copy(ybuf.at[2], out_hbm.at[pl.ds(n_rows + half * MOE_TILE, MOE_TILE)], sems.at[2])
            fill.start()
            fill.wait()

    slot_cur = i % 3
    slot_prev = (i + 2) % 3

    @pl.when(has_prev & has_cur)
    def _():
        start_rows(i - 1, slot_prev)
        compute(slot_cur)

    @pl.when(has_prev & jnp.logical_not(has_cur))
    def _():
        start_rows(i - 1, slot_prev)

    @pl.when(jnp.logical_not(has_prev) & has_cur)
    def _():
        compute(slot_cur)

    @pl.when(n_prev2 > 0)
    def _():
        tile_copy((i + 1) % 3).wait()


def _experts(tile_a, tile_b, tile_blk, tile_nvalid, tok, xs, w_gate, w_up, w_down, lng, lnb, n_rows, layer):
    n_steps = tile_nvalid.shape[0]
    e0 = layer * N_EXPERTS
    up_a = pl.BlockSpec((1, D_MODEL, D_EXPERT), lambda i, ea, eb, *_: (e0 + ea[i], 0, 0))
    up_b = pl.BlockSpec((1, D_MODEL, D_EXPERT), lambda i, ea, eb, *_: (e0 + eb[i], 0, 0))
    dn_a = pl.BlockSpec((1, D_EXPERT, D_MODEL), lambda i, ea, eb, *_: (e0 + ea[i], 0, 0))
    dn_b = pl.BlockSpec((1, D_EXPERT, D_MODEL), lambda i, ea, eb, *_: (e0 + eb[i], 0, 0))
    vec = pl.BlockSpec((1, D_MODEL), lambda i, *_: (0, 0))
    return pl.pallas_call(
        functools.partial(_experts_kernel, n_rows=n_rows),
        grid_spec=pltpu.PrefetchScalarGridSpec(
            num_scalar_prefetch=5, grid=(n_steps,),
            in_specs=[pl.BlockSpec((MOE_TILE, ROW_W), lambda i, ea, eb, blk, *_: (blk[i], 0)),
                      up_a, up_a, dn_a, up_b, up_b, dn_b, vec, vec],
            out_specs=pl.BlockSpec(memory_space=pl.ANY),
            scratch_shapes=[pltpu.VMEM((3, MOE_TILE, D_MODEL), F32), pltpu.SemaphoreType.DMA((3,))]),
        out_shape=jax.ShapeDtypeStruct((n_rows + 2 * MOE_TILE, D_MODEL), F32),
        compiler_params=_params(("arbitrary",)),
        name="experts",
    )(tile_a, tile_b, tile_blk, tile_nvalid, tok, xs, w_gate, w_up, w_down, w_gate, w_up, w_down,
      _row2(lng), _row2(lnb))


def _lookup(table, idx):
    pick = idx[:, None] == jnp.arange(table.shape[0], dtype=jnp.int32)[None, :]
    return jnp.sum(jnp.where(pick, table[None, :], 0), axis=1)


def _moe(rows, meta, counts, w_gate, w_up, w_down, lng, lnb, layer):
    n_rows = rows.shape[0]
    n_tiles = n_rows // MOE_TILE + N_CLASSES
    cnt = counts[:N_CLASSES, 0]
    tiles_per = (cnt + MOE_TILE - 1) // MOE_TILE
    cls_id = jnp.arange(N_CLASSES, dtype=jnp.int32)
    tile_end = jnp.sum(jnp.where(cls_id[None, :] <= cls_id[:, None], tiles_per[None, :], 0), axis=1)
    tile_start = tile_end - tiles_per
    offs = tile_start * MOE_TILE
    end = tile_end * MOE_TILE
    used = tile_end[N_CLASSES - 1:N_CLASSES]
    tile_id = jnp.arange(n_tiles + 2, dtype=jnp.int32)
    tile_blk = jnp.minimum(tile_id, used - 1)
    tile_cls = jnp.sum((tile_blk[:, None] >= tile_end[None, :]).astype(jnp.int32), axis=1)
    tile_a = _lookup(jnp.asarray(_CLASS_A), tile_cls)
    tile_b = _lookup(jnp.asarray(_CLASS_B), tile_cls)
    tile_nvalid = jnp.clip(_lookup(cnt, tile_cls) - (tile_id - _lookup(tile_start, tile_cls)) * MOE_TILE, 0, MOE_TILE)
    pos = _lookup(offs, meta[0]) + meta[1]
    xs, dst = _dispatch(pos, offs, cnt, end, used, rows, n_tiles)
    return _experts(tile_a, tile_b, tile_blk, tile_nvalid, dst, xs, w_gate, w_up, w_down, lng, lnb, n_rows, layer)


def kernel(x_prompt, x_sample, mem_prompt, cache_mem_k, cache_mem_v, state_conv_a, state_conv_b, state_pool,
           ln_g, ln_b, ab_w_in, ab_conv_a_w, ab_conv_a_b, ab_norm_a_g, ab_norm_a_b, ab_conv_b_w, ab_w_out,
           cd_w_in, cd_pool_w, cd_pool_scale, cd_v_norm_g, cd_v_norm_b, cd_w_s, cd_b_s, cd_w_out,
           ca_wq, ca_wk, ca_wv, ca_wo, router_w, router_b, moe_w_gate, moe_w_up, moe_w_down):
    assert ln_g.shape[0] == DEPTH and ab_w_in.shape[0] == 1 and cd_w_in.shape[0] == 1
    bsz, seq, _ = x_prompt.shape
    dec_b, dec_seq, _ = x_sample.shape
    n_prompt = bsz * seq
    n_sample = dec_b * dec_seq
    sample_seqs = ROW_TILE // dec_seq

    n_pool = len(POOL_WINDOWS)
    pool_c = D_HALF // n_pool
    pool_bd = jnp.zeros((D_HALF, D_HALF), F32)
    for g in range(n_pool):
        pool_bd = pool_bd.at[g * pool_c:(g + 1) * pool_c, g * pool_c:(g + 1) * pool_c].set(cd_pool_w[0, g])
    pool_bd = pool_bd.astype(BF16)
    ab_w_in_hi, ab_w_in_lo = _split_weight(ab_w_in[0])
    ab_w_out_hi, ab_w_out_lo = _split_weight(ab_w_out[0])
    rw_hi, rw_lo = _split_weight(router_w.T)
    rw_t = jnp.concatenate([rw_hi, rw_lo], axis=0)
    cd_w_in_b, cd_w_out_b = cd_w_in[0].astype(BF16), cd_w_out[0].astype(BF16)
    wq_b, wo_b = ca_wq.astype(BF16), ca_wo.astype(BF16)
    wg_b = moe_w_gate.reshape(DEPTH * N_EXPERTS, D_MODEL, D_EXPERT)
    wu_b = moe_w_up.reshape(DEPTH * N_EXPERTS, D_MODEL, D_EXPERT)
    wd_b = moe_w_down.reshape(DEPTH * N_EXPERTS, D_EXPERT, D_MODEL)

    mem_k_prompt, k_p = _mem_projection(mem_prompt, ca_wk)
    mem_v_prompt, v_p = _mem_projection(mem_prompt, ca_wv)
    k_s = _heads_to_columns(cache_mem_k.reshape(DEPTH * dec_b, N_MEM, MEM_HEADS, MEM_HEAD_DIM))
    v_s = _heads_to_columns(cache_mem_v.reshape(DEPTH * dec_b, N_MEM, MEM_HEADS, MEM_HEAD_DIM))

    def mixer_ab(x, buf_a, buf_b, n_seq, seg, **kw):
        return _mixer_ab(x, buf_a, buf_b, ab_w_in_hi, ab_w_in_lo, ab_conv_a_w[0], ab_conv_a_b[0], ab_norm_a_g[0],
                         ab_norm_a_b[0], ab_conv_b_w[0], ab_w_out_hi, ab_w_out_lo, ln_g[0, 0], ln_b[0, 0],
                         n_seq=n_seq, seg=seg, **kw)

    def mixer_cd(x, row_start, buf_p, length, n_seq, seg, pos0, emit_v):
        return _mixer_cd(x, row_start, buf_p, length, cd_w_in_b, pool_bd, cd_pool_scale[0], cd_v_norm_g[0],
                         cd_v_norm_b[0], cd_w_s[0], cd_b_s[0], cd_w_out_b, ln_g[1, 0], ln_b[1, 0],
                         n_seq=n_seq, seg=seg, pos0=pos0, emit_v=emit_v)

    def attn_moe(layer, x_p, x_s):
        rows, meta, counts = _attn_router(x_p, x_s, k_p, v_p, k_s, v_s, wq_b, wo_b, ln_g[layer, 1], ln_b[layer, 1],
                                          rw_t, router_b, layer=layer, prompt_len=seq, sample_len=dec_seq)
        return _moe(rows, meta, counts, wg_b, wu_b, wd_b, ln_g[layer, 2], ln_b[layer, 2], layer)

    xp_flat = x_prompt.reshape(n_prompt, D_MODEL)
    tiles_p = seq // ROW_TILE
    x_p, conv_a_p, conv_b_p = mixer_ab(xp_flat, jnp.zeros((bsz, CONV_A - 1, D_HALF), F32),
                                       jnp.zeros((bsz, CONV_B - 1, D_HALF), F32), 1, ROW_TILE,
                                       precise=False, skip_tail=STATE_TILES)
    x_p, conv_a_p, conv_b_p = mixer_ab(xp_flat, conv_a_p, conv_b_p, 1, ROW_TILE, precise=True,
                                       tile_lo=tiles_p - STATE_TILES, tiles=STATE_TILES, y_prev=x_p)
    x_s, conv_a_s, conv_b_s = mixer_ab(x_sample.reshape(n_sample, D_MODEL), state_conv_a[0], state_conv_b[0],
                                       sample_seqs, dec_seq, precise=False)
    x_all = attn_moe(0, x_p, x_s)

    x_p, pool_p = mixer_cd(x_all, 0, jnp.zeros((bsz, POOL_BUF, D_HALF), F32), seq, 1, ROW_TILE, 0, False)
    x_s, pool_s, v_s_rows = mixer_cd(x_all, n_prompt, state_pool[0], dec_seq, sample_seqs, dec_seq, PAST_LEN, True)
    x_all = attn_moe(1, x_p, x_s)

    y_prompt = x_all[0:n_prompt].reshape(bsz, seq, D_MODEL)
    y_sample = x_all[n_prompt:n_prompt + n_sample].reshape(dec_b, dec_seq, D_MODEL)
    return (y_prompt, y_sample, mem_k_prompt, mem_v_prompt, conv_a_p[None], conv_b_p[None], pool_p[None],
            conv_a_s[None], conv_b_s[None], pool_s[None], v_s_rows[None])
```

```python
import functools

import jax
import jax.numpy as jnp
import numpy as np
from jax import lax
from jax.experimental import pallas as pl
from jax.experimental.pallas import tpu as pltpu

F32 = jnp.float32
BF16 = jnp.bfloat16

D_MODEL = 1024
D_HALF = D_MODEL // 2
DEPTH = 2
PAST_LEN = 4096
CHUNK = 64
CHUNK_SHIFT = CHUNK.bit_length() - 1
CONV_A = 31
CONV_B = 3
POOL_WINDOWS = (2, 4, 8, 16)
POOL_BUF = max(POOL_WINDOWS) - 1
N_HEAD_D = 4
GMLP_CHUNK = 128
N_MEM = 256
MEM_HEADS = 4
MEM_HEAD_DIM = D_MODEL // MEM_HEADS
N_EXPERTS = 16
N_EXPERT_GROUPS = 4
GROUP_SIZE = N_EXPERTS // N_EXPERT_GROUPS
GROUP_SHIFT = GROUP_SIZE.bit_length() - 1
PAIRS_PER_GROUP = GROUP_SIZE * (GROUP_SIZE - 1) // 2
N_CLASSES = N_EXPERT_GROUPS * PAIRS_PER_GROUP
D_EXPERT = D_MODEL // 2
ALPHA = (2 * DEPTH) ** 0.25
LN_EPS = 1e-5

LANES = 128
SUBLANES = 8
ROW_W = D_MODEL + LANES
HIST_A = 32
HIST_B = 8
HIST_P = 16
CONV_ROWS = 64
ROW_TILE = 512
MOE_TILE = 256
MOE_SHIFT = MOE_TILE.bit_length() - 1
CLASS_ROWS = 32
DISPATCH_TILE = 1024
SPLIT_ROWS = 256
CACHE_SEQS = 4
STATE_TILES = 1
VMEM_LIMIT = 56 * 1024 * 1024

_PAIR_AB = [(a, b) for a in range(GROUP_SIZE) for b in range(a + 1, GROUP_SIZE)]
_CLASS_A = np.array([g * GROUP_SIZE + a for g in range(N_EXPERT_GROUPS) for a, _ in _PAIR_AB], np.int32)
_CLASS_B = np.array([g * GROUP_SIZE + b for g in range(N_EXPERT_GROUPS) for _, b in _PAIR_AB], np.int32)


def _layer_norm(x, g, b):
    mu = jnp.mean(x, axis=-1, keepdims=True)
    xc = x - mu
    var = jnp.mean(xc * xc, axis=-1, keepdims=True)
    return xc * lax.rsqrt(var + LN_EPS) * g + b


def _bdot(a, w):
    return jnp.dot(a.astype(BF16), w, preferred_element_type=F32)


def _split(a):
    hi = a.astype(BF16)
    return hi, (a - hi.astype(F32)).astype(BF16)


def _dot3(a, w_hi, w_lo):
    a_hi, a_lo = _split(a)
    return (jnp.dot(a_hi, w_hi, preferred_element_type=F32) + jnp.dot(a_lo, w_hi, preferred_element_type=F32)
            + jnp.dot(a_hi, w_lo, preferred_element_type=F32))


def _params(sem):
    return pltpu.CompilerParams(dimension_semantics=sem, vmem_limit_bytes=VMEM_LIMIT)


def _full(shape):
    return pl.BlockSpec(shape, lambda *_: (0,) * len(shape), pipeline_mode=pl.Buffered(1))


def _row2(v):
    return v.reshape(1, -1)


def _split_weight_kernel(w_ref, hi_ref, lo_ref):
    hi, lo = _split(w_ref[...])
    hi_ref[...] = hi
    lo_ref[...] = lo


def _split_weight(w):
    rows, cols = w.shape
    blk = min(rows, SPLIT_ROWS)
    spec = pl.BlockSpec((blk, cols), lambda i: (i, 0))
    return pl.pallas_call(
        _split_weight_kernel,
        grid=(rows // blk,),
        in_specs=[spec],
        out_specs=[spec, spec],
        out_shape=[jax.ShapeDtypeStruct(w.shape, BF16)] * 2,
        compiler_params=_params(("arbitrary",)),
        name="split_weight",
    )(w)


def _proj_kernel(x_ref, w_ref, o_ref, o16_ref, *, batch):
    res = _bdot(x_ref[...], w_ref[0].astype(BF16))
    o16_ref[...] = res.astype(BF16).reshape(batch, N_MEM, D_MODEL)
    for hd in range(MEM_HEADS):
        cols = slice(hd * MEM_HEAD_DIM, (hd + 1) * MEM_HEAD_DIM)
        o_ref[0, :, :, hd, :] = res[:, cols].reshape(batch, N_MEM, MEM_HEAD_DIM)


def _mem_projection(mem, w):
    batch, n = mem.shape[0], w.shape[0]
    return pl.pallas_call(
        functools.partial(_proj_kernel, batch=batch),
        grid=(n,),
        in_specs=[pl.BlockSpec((batch * N_MEM, D_MODEL), lambda j: (0, 0)),
                  pl.BlockSpec((1, D_MODEL, D_MODEL), lambda j: (j, 0, 0))],
        out_specs=[pl.BlockSpec((1, batch, N_MEM, MEM_HEADS, MEM_HEAD_DIM), lambda j: (j, 0, 0, 0, 0)),
                   pl.BlockSpec((batch, N_MEM, D_MODEL), lambda j: (j, 0, 0))],
        out_shape=[jax.ShapeDtypeStruct((n, batch, N_MEM, MEM_HEADS, MEM_HEAD_DIM), F32),
                   jax.ShapeDtypeStruct((n * batch, N_MEM, D_MODEL), BF16)],
        compiler_params=_params(("arbitrary",)),
        name="mem_projection",
    )(mem.reshape(batch * N_MEM, D_MODEL), w)


def _heads_to_columns_kernel(c_hbm, o_ref, buf, sem):
    step = pl.program_id(0)

    def head_copies(at_step, slot):
        return [pltpu.make_async_copy(c_hbm.at[at_step * CACHE_SEQS + s, :, hd, :], buf.at[slot, s, hd], sem.at[slot])
                for s in range(CACHE_SEQS) for hd in range(MEM_HEADS)]

    @pl.when(step == 0)
    def _():
        for copy in head_copies(0, 0):
            copy.start()

    @pl.when(step + 1 < pl.num_programs(0))
    def _():
        for copy in head_copies(step + 1, (step + 1) % 2):
            copy.start()

    slot = step % 2
    for copy in head_copies(step, slot):
        copy.wait()
    for s in range(CACHE_SEQS):
        for hd in range(MEM_HEADS):
            o_ref[s, :, hd * MEM_HEAD_DIM:(hd + 1) * MEM_HEAD_DIM] = buf[slot, s, hd].astype(BF16)


def _heads_to_columns(cache):
    n = cache.shape[0]
    return pl.pallas_call(
        _heads_to_columns_kernel,
        grid=(n // CACHE_SEQS,),
        in_specs=[pl.BlockSpec(memory_space=pl.ANY)],
        out_specs=pl.BlockSpec((CACHE_SEQS, N_MEM, D_MODEL), lambda i: (i, 0, 0)),
        out_shape=jax.ShapeDtypeStruct((n, N_MEM, D_MODEL), BF16),
        scratch_shapes=[pltpu.VMEM((2, CACHE_SEQS, MEM_HEADS, N_MEM, MEM_HEAD_DIM), F32),
                        pltpu.SemaphoreType.DMA((2,))],
        compiler_params=_params(("arbitrary",)),
        name="heads_to_columns",
    )(cache)


def _load_history(ext_ref, buf_ref, first, hist, keep, seg):
    @pl.when(first)
    def _():
        ext_ref[:, hist - keep:hist, :] = buf_ref[...]

    @pl.when(jnp.logical_not(first))
    def _():
        ext_ref[:, hist - keep:hist, :] = ext_ref[:, seg + hist - keep:seg + hist, :]


def _depthwise_conv(ext_ref, w_ref, out_ref, *, n_seq, seg, taps, hist, shifted_ref=None):
    rc = min(CONV_ROWS, seg)
    off0 = hist - (taps - 1)
    length = hist + seg
    if shifted_ref is not None:
        for r in range(1, SUBLANES):
            shifted_ref[r - 1, :, 0:length - SUBLANES, :] = ext_ref[:, r:r + length - SUBLANES, :]
    for s in range(n_seq):
        for r0 in range(0, seg, rc):
            for lb in range(0, D_HALF, LANES):
                acc = None
                for k in range(taps):
                    lo = off0 + k + r0
                    shift = (off0 + k) % SUBLANES
                    if shifted_ref is None or shift == 0:
                        win = ext_ref[s, lo:lo + rc, lb:lb + LANES]
                    else:
                        win = shifted_ref[shift - 1, s, lo - shift:lo - shift + rc, lb:lb + LANES]
                    term = w_ref[k:k + 1, lb:lb + LANES] * win
                    acc = term if acc is None else acc + term
                out_ref[s * seg + r0:s * seg + r0 + rc, lb:lb + LANES] = acc


def _mixer_ab_kernel(x_ref, bufa_ref, bufb_ref, w_in_ref, w_in_lo_ref, caw_ref, cab_ref, nag_ref, nab_ref, cbw_ref,
                     w_out_ref, w_out_lo_ref, lng_ref, lnb_ref, *refs, n_seq, seg, precise, skip_tail, has_prev):
    if has_prev:
        refs = refs[1:]
    y_ref, nbufa_ref, nbufb_ref, a_ext, a_shift, cb_ext, conv_a, conv_b = refs

    def body():
        first = pl.program_id(1) == 0
        _load_history(a_ext, bufa_ref, first, HIST_A, CONV_A - 1, seg)
        _load_history(cb_ext, bufb_ref, first, HIST_B, CONV_B - 1, seg)
        x = x_ref[...]
        h = _dot3(x, w_in_ref[...], w_in_lo_ref[...]) if precise else _bdot(x, w_in_ref[...])
        a = h[:, 0:D_HALF] * jax.nn.sigmoid(h[:, D_HALF:2 * D_HALF])
        cb = h[:, 3 * D_HALF:4 * D_HALF] * h[:, 4 * D_HALF:5 * D_HALF]
        a_ext[:, HIST_A:HIST_A + seg, :] = a.reshape(n_seq, seg, D_HALF)
        cb_ext[:, HIST_B:HIST_B + seg, :] = cb.reshape(n_seq, seg, D_HALF)
        nbufa_ref[...] = a_ext[:, seg + HIST_A - (CONV_A - 1):seg + HIST_A, :]
        nbufb_ref[...] = cb_ext[:, seg + HIST_B - (CONV_B - 1):seg + HIST_B, :]

        _depthwise_conv(a_ext, caw_ref, conv_a, n_seq=n_seq, seg=seg, taps=CONV_A, hist=HIST_A, shifted_ref=a_shift)
        _depthwise_conv(cb_ext, cbw_ref, conv_b, n_seq=n_seq, seg=seg, taps=CONV_B, hist=HIST_B)

        a2 = _layer_norm(conv_a[...] + cab_ref[...], nag_ref[...], nab_ref[...])
        a2 = a2 * jax.nn.sigmoid(a2)
        b2 = h[:, 2 * D_HALF:3 * D_HALF] * conv_b[...]
        if precise:
            y = (_dot3(a2, w_out_ref[0:D_HALF, :], w_out_lo_ref[0:D_HALF, :])
                 + _dot3(b2, w_out_ref[D_HALF:D_MODEL, :], w_out_lo_ref[D_HALF:D_MODEL, :]))
        else:
            y = _bdot(a2, w_out_ref[0:D_HALF, :]) + _bdot(b2, w_out_ref[D_HALF:D_MODEL, :])
        y_ref[...] = _layer_norm(ALPHA * x + y, lng_ref[...], lnb_ref[...])

    if skip_tail == 0:
        body()
    else:
        live = pl.program_id(1) < pl.num_programs(1) - skip_tail
        pl.when(live)(body)

        @pl.when(jnp.logical_not(live))
        def _():
            y_ref[...] = jnp.zeros_like(y_ref)


def _mixer_ab(x, buf_a, buf_b, w_in, w_in_lo, caw, cab, nag, nab, cbw, w_out, w_out_lo, lng, lnb, *,
              n_seq, seg, precise, tile_lo=0, tiles=None, skip_tail=0, y_prev=None):
    batch = buf_a.shape[0]
    length = x.shape[0] // batch
    d_in = w_in.shape[1]
    n_l = length // seg
    tiles = n_l if tiles is None else tiles
    rows = n_seq * seg
    lo_in = _full((D_MODEL, d_in)) if precise else _full((SUBLANES, LANES))
    lo_out = _full((D_MODEL, D_MODEL)) if precise else _full((SUBLANES, LANES))
    row_blk = pl.BlockSpec((rows, D_MODEL), lambda b, l: (b * n_l + tile_lo + l, 0))
    in_specs = [row_blk,
                pl.BlockSpec((n_seq, CONV_A - 1, D_HALF), lambda b, l: (b, 0, 0)),
                pl.BlockSpec((n_seq, CONV_B - 1, D_HALF), lambda b, l: (b, 0, 0)),
                _full((D_MODEL, d_in)), lo_in, _full((CONV_A, D_HALF)), _full((1, D_HALF)), _full((1, D_HALF)),
                _full((1, D_HALF)), _full((CONV_B, D_HALF)), _full((D_MODEL, D_MODEL)), lo_out,
                _full((1, D_MODEL)), _full((1, D_MODEL))]
    args = [x, buf_a, buf_b, w_in, w_in_lo, caw, _row2(cab), _row2(nag), _row2(nab), cbw, w_out, w_out_lo,
            _row2(lng), _row2(lnb)]
    aliases = {}
    if y_prev is not None:
        aliases = {len(args): 0}
        in_specs.append(pl.BlockSpec(memory_space=pl.ANY))
        args.append(y_prev)
    return pl.pallas_call(
        functools.partial(_mixer_ab_kernel, n_seq=n_seq, seg=seg, precise=precise, skip_tail=skip_tail,
                          has_prev=y_prev is not None),
        grid=(batch // n_seq, tiles),
        in_specs=in_specs,
        out_specs=[row_blk,
                   pl.BlockSpec((n_seq, CONV_A - 1, D_HALF), lambda b, l: (b, 0, 0)),
                   pl.BlockSpec((n_seq, CONV_B - 1, D_HALF), lambda b, l: (b, 0, 0))],
        out_shape=[jax.ShapeDtypeStruct((batch * length, D_MODEL), F32),
                   jax.ShapeDtypeStruct((batch, CONV_A - 1, D_HALF), F32),
                   jax.ShapeDtypeStruct((batch, CONV_B - 1, D_HALF), F32)],
        scratch_shapes=[pltpu.VMEM((n_seq, HIST_A + seg, D_HALF), F32),
                        pltpu.VMEM((SUBLANES - 1, n_seq, HIST_A + seg - SUBLANES, D_HALF), F32),
                        pltpu.VMEM((n_seq, HIST_B + seg, D_HALF), F32),
                        pltpu.VMEM((rows, D_HALF), F32),
                        pltpu.VMEM((rows, D_HALF), F32)],
        input_output_aliases=aliases,
        compiler_params=_params(("arbitrary", "arbitrary")),
        name="mixer_ab",
    )(*args)


def _mixer_cd_kernel(x_ref, bufp_ref, w_in_ref, pw_ref, ps_ref, vg_ref, vb_ref, ws_ref, bs_ref, w_out_ref,
                     lng_ref, lnb_ref, *refs, n_seq, seg, pos0, n_mix, emit_v):
    if emit_v:
        y_ref, nbufp_ref, v_ref, c_ext, pooled, mixed = refs
    else:
        y_ref, nbufp_ref, c_ext, pooled, mixed = refs
    rows = n_seq * seg
    li = pl.program_id(1)
    _load_history(c_ext, bufp_ref, li == 0, HIST_P, POOL_BUF, seg)
    x = x_ref[...]
    h = _bdot(x, w_in_ref[...])
    c_in = h[:, 0:D_HALF]
    c_ext[:, HIST_P:HIST_P + seg, :] = c_in.reshape(n_seq, seg, D_HALF)
    nbufp_ref[...] = c_ext[:, seg + HIST_P - POOL_BUF:seg + HIST_P, :]

    rc = min(CONV_ROWS, seg)
    for g, win in enumerate(POOL_WINDOWS):
        lanes = slice(g * LANES, (g + 1) * LANES)
        for s in range(n_seq):
            for r0 in range(0, seg, rc):
                cur = c_ext[s, HIST_P + r0:HIST_P + r0 + rc, lanes]
                acc = cur
                for j in range(1, win):
                    acc = acc + c_ext[s, HIST_P + r0 - j:HIST_P + r0 - j + rc, lanes]
                pos = pos0 + li * seg + r0 + lax.broadcasted_iota(jnp.int32, (rc, LANES), 0)
                cnt = jnp.minimum(pos + 1, win).astype(F32)
                pooled[s * seg + r0:s * seg + r0 + rc, lanes] = acc / cnt - cur
    c = _bdot(pooled[...], pw_ref[...]) * ps_ref[...]

    z = jax.nn.gelu(h[:, D_HALF:3 * D_HALF], approximate=True)
    u = z[:, 0:D_HALF]
    v = _layer_norm(z[:, D_HALF:2 * D_HALF], vg_ref[...], vb_ref[...])
    if emit_v:
        v_ref[...] = v.reshape(n_seq, seg, D_HALF)
    vb16 = v.astype(BF16)

    ri = lax.broadcasted_iota(jnp.int32, (n_mix, n_mix), 0) >> CHUNK_SHIFT
    ci = lax.broadcasted_iota(jnp.int32, (n_mix, n_mix), 1) >> CHUNK_SHIFT
    for g in range(N_HEAD_D):
        lanes = slice(g * LANES, (g + 1) * LANES)
        ws = jnp.where(ci <= ri, ws_ref[g], 0.0).astype(BF16)
        bias = bs_ref[:, g:g + 1]
        for r0 in range(0, rows, n_mix):
            mixed[r0:r0 + n_mix, lanes] = jnp.dot(ws, vb16[r0:r0 + n_mix, lanes], preferred_element_type=F32) + bias
    d = u * mixed[...]
    y = _bdot(c, w_out_ref[0:D_HALF, :]) + _bdot(d, w_out_ref[D_HALF:D_MODEL, :])
    y_ref[...] = _layer_norm(ALPHA * x + y, lng_ref[...], lnb_ref[...])


def _mixer_cd(x, row_start, buf_p, length, w_in, pw, ps, vg, vb, ws, bs, w_out, lng, lnb, *,
              n_seq, seg, pos0, emit_v):
    batch = buf_p.shape[0]
    d_in = w_in.shape[1]
    n_mix = min(length, GMLP_CHUNK)
    assert seg % n_mix == 0
    n_l = length // seg
    rows = n_seq * seg
    blk0 = row_start // rows
    ws_n = ws[:, :n_mix, :n_mix]
    bs_t = bs[:, :n_mix].T
    out_specs = [pl.BlockSpec((rows, D_MODEL), lambda b, l: (b * n_l + l, 0)),
                 pl.BlockSpec((n_seq, POOL_BUF, D_HALF), lambda b, l: (b, 0, 0))]
    out_shape = [jax.ShapeDtypeStruct((batch * length, D_MODEL), F32),
                 jax.ShapeDtypeStruct((batch, POOL_BUF, D_HALF), F32)]
    if emit_v:
        out_specs.append(pl.BlockSpec((n_seq, seg, D_HALF), lambda b, l: (b, l, 0)))
        out_shape.append(jax.ShapeDtypeStruct((batch, length, D_HALF), F32))
    return pl.pallas_call(
        functools.partial(_mixer_cd_kernel, n_seq=n_seq, seg=seg, pos0=pos0, n_mix=n_mix, emit_v=emit_v),
        grid=(batch // n_seq, n_l),
        in_specs=[pl.BlockSpec((rows, D_MODEL), lambda b, l: (blk0 + b * n_l + l, 0)),
                  pl.BlockSpec((n_seq, POOL_BUF, D_HALF), lambda b, l: (b, 0, 0)),
                  _full((D_MODEL, d_in)), _full((D_HALF, D_HALF)), _full((1, D_HALF)), _full((1, D_HALF)),
                  _full((1, D_HALF)), _full((N_HEAD_D, n_mix, n_mix)), _full((n_mix, N_HEAD_D)),
                  _full((D_MODEL, D_MODEL)), _full((1, D_MODEL)), _full((1, D_MODEL))],
        out_specs=out_specs,
        out_shape=out_shape,
        scratch_shapes=[pltpu.VMEM((n_seq, HIST_P + seg, D_HALF), F32),
                        pltpu.VMEM((rows, D_HALF), F32),
                        pltpu.VMEM((rows, D_HALF), F32)],
        compiler_params=_params(("arbitrary", "arbitrary")),
        name="mixer_cd",
    )(x, buf_p, w_in, pw, _row2(ps), _row2(vg), _row2(vb), ws_n, bs_t, w_out, _row2(lng), _row2(lnb))


def _attn_router_kernel(xp_ref, xs_ref, kp_ref, vp_ref, ks_ref, vs_ref, wq_ref, wo_ref, lng_ref, lnb_ref, rw_ref, rb_ref,
                        rows_ref, meta_ref, counts_ref,
                        q_scr, o_scr, carry, tri, *, n_prompt_tiles, sample_seg):
    rows = ROW_TILE
    step = pl.program_id(0)
    is_prompt = step < n_prompt_tiles
    x = jnp.where(is_prompt, xp_ref[...], xs_ref[...])
    q_scr[...] = (_bdot(x, wq_ref[0]) * (MEM_HEAD_DIM ** -0.5)).astype(BF16)

    def heads(k_ref, v_ref, n_seq, seg):
        for s in range(n_seq):
            for hd in range(MEM_HEADS):
                cols = slice(hd * MEM_HEAD_DIM, (hd + 1) * MEM_HEAD_DIM)
                sc = lax.dot_general(q_scr[s * seg:(s + 1) * seg, cols], k_ref[s, :, cols], (((1,), (1,)), ((), ())),
                                     preferred_element_type=F32)
                p = jnp.exp(sc - jnp.max(sc, axis=-1, keepdims=True))
                den = jnp.sum(p, axis=-1, keepdims=True)
                o_scr[s * seg:(s + 1) * seg, cols] = _bdot(p, v_ref[s, :, cols]) / den

    @pl.when(is_prompt)
    def _():
        heads(kp_ref, vp_ref, 1, rows)

    @pl.when(jnp.logical_not(is_prompt))
    def _():
        heads(ks_ref, vs_ref, rows // sample_seg, sample_seg)

    x2 = _layer_norm(ALPHA * x + _bdot(o_scr[...], wo_ref[0]), lng_ref[...], lnb_ref[...])
    rows_ref[:, 0:D_MODEL] = x2

    nt = (((1,), (1,)), ((), ()))
    x2_hi, x2_lo = _split(x2)
    both = lax.dot_general(rw_ref[...], x2_hi, nt, preferred_element_type=F32)
    cross = lax.dot_general(rw_ref[0:N_EXPERTS, :], x2_lo, nt, preferred_element_type=F32)
    logits = both[0:N_EXPERTS] + both[N_EXPERTS:2 * N_EXPERTS] + cross + rb_ref[...]
    e = jnp.exp(logits - jnp.max(logits, axis=0, keepdims=True))
    scores = e / jnp.sum(e, axis=0, keepdims=True)
    eid = lax.broadcasted_iota(jnp.int32, (N_EXPERTS, rows), 0)
    egrp = eid >> GROUP_SHIFT
    best = jnp.max(jnp.where(egrp == 0, scores, -1.0), axis=0, keepdims=True)
    g_sel = jnp.zeros((1, rows), jnp.int32)
    for g in range(1, N_EXPERT_GROUPS):
        gs = jnp.max(jnp.where(egrp == g, scores, -1.0), axis=0, keepdims=True)
        upd = gs > best
        g_sel = jnp.where(upd, g, g_sel)
        best = jnp.where(upd, gs, best)
    masked = jnp.where(egrp == g_sel, scores, -1.0)
    m1 = jnp.max(masked, axis=0, keepdims=True)
    i1 = jnp.min(jnp.where(masked == m1, eid, N_EXPERTS), axis=0, keepdims=True)
    masked2 = jnp.where(eid == i1, -2.0, masked)
    m2 = jnp.max(masked2, axis=0, keepdims=True)
    i2 = jnp.min(jnp.where(masked2 == m2, eid, N_EXPERTS), axis=0, keepdims=True)
    tot = m1 + m2
    g1 = m1 / tot
    g2 = m2 / tot
    first_low = i1 < i2
    ea = jnp.where(first_low, i1, i2) & (GROUP_SIZE - 1)
    eb = jnp.where(first_low, i2, i1) & (GROUP_SIZE - 1)
    gate_a = jnp.where(first_low, g1, g2)
    gate_b = jnp.where(first_low, g2, g1)
    pair = eb - 1 + jnp.where(ea == 1, 2, 0) + jnp.where(ea == 2, 3, 0)
    cls = g_sel * PAIRS_PER_GROUP + pair

    lane_row = lax.broadcasted_iota(jnp.int32, (LANES, rows), 0)
    gates_t = jnp.where(lane_row == 0, gate_a, jnp.where(lane_row == 1, gate_b, 0.0))
    rows_ref[:, D_MODEL:ROW_W] = gates_t.T

    @pl.when(step == 0)
    def _():
        carry[...] = jnp.zeros_like(carry)
        upper = (lax.broadcasted_iota(jnp.int32, (rows, rows), 0) <= lax.broadcasted_iota(jnp.int32, (rows, rows), 1))
        tri[...] = upper.astype(F32).astype(BF16)

    onehot = (lax.broadcasted_iota(jnp.int32, (CLASS_ROWS, rows), 0) == cls).astype(F32)
    cum = jnp.dot(onehot.astype(BF16), tri[...], preferred_element_type=F32)
    before = carry[:, 0:1]
    rank = jnp.sum(onehot * (before + cum), axis=0, keepdims=True) - 1.0
    sub = lax.broadcasted_iota(jnp.int32, (SUBLANES, rows), 0)
    meta_ref[...] = jnp.where(sub == 0, cls, jnp.where(sub == 1, rank.astype(jnp.int32), 0))
    carry[...] = carry[...] + cum[:, rows - 1:rows]
    counts_ref[...] = carry[...].astype(jnp.int32)


def _attn_router(x_p, x_s, k_p, v_p, k_s, v_s, wq, wo, lng, lnb, rw_t, rb, *, layer, prompt_len, sample_len):
    rows = ROW_TILE
    n_p = x_p.shape[0] // rows
    n_s = x_s.shape[0] // rows
    tiles_per_seq = prompt_len // rows
    seq_per_tile = rows // sample_len
    total = x_p.shape[0] + x_s.shape[0]
    n_prompt_seq = n_p // tiles_per_seq
    p0 = layer * n_prompt_seq
    s0 = layer * n_s
    p_idx = lambda i: jnp.minimum(i, n_p - 1)
    s_idx = lambda i: jnp.maximum(i - n_p, 0)
    layer_w = pl.BlockSpec((1, D_MODEL, D_MODEL), lambda i: (layer, 0, 0), pipeline_mode=pl.Buffered(1))
    return pl.pallas_call(
        functools.partial(_attn_router_kernel, n_prompt_tiles=n_p, sample_seg=sample_len),
        grid=(n_p + n_s,),
        in_specs=[pl.BlockSpec((rows, D_MODEL), lambda i: (p_idx(i), 0)),
                  pl.BlockSpec((rows, D_MODEL), lambda i: (s_idx(i), 0)),
                  pl.BlockSpec((1, N_MEM, D_MODEL), lambda i: (p0 + p_idx(i) // tiles_per_seq, 0, 0)),
                  pl.BlockSpec((1, N_MEM, D_MODEL), lambda i: (p0 + p_idx(i) // tiles_per_seq, 0, 0)),
                  pl.BlockSpec((seq_per_tile, N_MEM, D_MODEL), lambda i: (s0 + s_idx(i), 0, 0)),
                  pl.BlockSpec((seq_per_tile, N_MEM, D_MODEL), lambda i: (s0 + s_idx(i), 0, 0)),
                  layer_w, layer_w, _full((1, D_MODEL)), _full((1, D_MODEL)),
                  _full((2 * N_EXPERTS, D_MODEL)), _full((N_EXPERTS, 1))],
        out_specs=[pl.BlockSpec((rows, ROW_W), lambda i: (i, 0)),
                   pl.BlockSpec((SUBLANES, rows), lambda i: (0, i)),
                   pl.BlockSpec((CLASS_ROWS, LANES), lambda i: (0, 0))],
        out_shape=[jax.ShapeDtypeStruct((total, ROW_W), F32),
                   jax.ShapeDtypeStruct((SUBLANES, total), jnp.int32),
                   jax.ShapeDtypeStruct((CLASS_ROWS, LANES), jnp.int32)],
        scratch_shapes=[pltpu.VMEM((rows, D_MODEL), BF16),
                        pltpu.VMEM((rows, D_MODEL), F32),
                        pltpu.VMEM((CLASS_ROWS, LANES), F32),
                        pltpu.VMEM((rows, rows), BF16)],
        compiler_params=_params(("arbitrary",)),
        name="attn_router",
    )(x_p, x_s, k_p, v_p, k_s, v_s, wq, wo, _row2(lng), _row2(lnb), rw_t, rb.reshape(N_EXPERTS, 1))


def _dispatch_kernel(pos_ref, offs_ref, cnt_ref, end_ref, used_ref, rows_ref, xs_hbm, dst_ref,
                     zeros, sem, zsem, *, tile, n_tiles, n_rows):
    step = pl.program_id(0)

    def class_padding(c, go):
        start = offs_ref[c] + cnt_ref[c]
        head = (-start) & (SUBLANES - 1)
        bulk_start = pl.multiple_of(start + head, SUBLANES)
        bulk = pl.multiple_of(end_ref[c] - bulk_start, SUBLANES)

        def one(r, carry_):
            go(pltpu.make_async_copy(zeros.at[pl.ds(0, 1)], xs_hbm.at[pl.ds(r, 1)], zsem))
            return carry_
        lax.fori_loop(start, start + head, one, 0)

        @pl.when(bulk > 0)
        def _():
            go(pltpu.make_async_copy(zeros.at[pl.ds(0, bulk)], xs_hbm.at[pl.ds(bulk_start, bulk)], zsem))

    def unused_tiles(go):
        def one(j, carry_):
            go(pltpu.make_async_copy(zeros, xs_hbm.at[pl.ds(pl.multiple_of(j * MOE_TILE, MOE_TILE), MOE_TILE)], zsem))
            return carry_
        lax.fori_loop(used_ref[0], n_tiles, one, 0)

    @pl.when(step == 0)
    def _():
        zeros[...] = jnp.zeros_like(zeros)

        def no_token(p, carry_):
            dst_ref[p] = n_rows + ((p >> MOE_SHIFT) & 1) * MOE_TILE + (p & (MOE_TILE - 1))
            return carry_
        for c in range(N_CLASSES):
            lax.fori_loop(offs_ref[c] + cnt_ref[c], end_ref[c], no_token, 0)
            class_padding(c, lambda copy: copy.start())
        lax.fori_loop(used_ref[0] * MOE_TILE, n_tiles * MOE_TILE, no_token, 0)
        unused_tiles(lambda copy: copy.start())

    for r in range(tile):
        t = step * tile + r
        p = pos_ref[t]
        dst_ref[p] = t
        pltpu.make_async_copy(rows_ref.at[pl.ds(r, 1)], xs_hbm.at[pl.ds(p, 1)], sem).start()
    pltpu.make_async_copy(rows_ref, xs_hbm.at[pl.ds(0, tile)], sem).wait()

    @pl.when(step == 0)
    def _():
        for c in range(N_CLASSES):
            class_padding(c, lambda copy: copy.wait())
        unused_tiles(lambda copy: copy.wait())


def _dispatch(pos, offs, cnt, end, used, rows, n_tiles):
    n_rows = rows.shape[0]
    n_sorted = n_tiles * MOE_TILE
    any_spec = pl.BlockSpec(memory_space=pl.ANY)
    return pl.pallas_call(
        functools.partial(_dispatch_kernel, tile=DISPATCH_TILE, n_tiles=n_tiles, n_rows=n_rows),
        grid_spec=pltpu.PrefetchScalarGridSpec(
            num_scalar_prefetch=5, grid=(n_rows // DISPATCH_TILE,),
            in_specs=[pl.BlockSpec((DISPATCH_TILE, ROW_W), lambda i, *_: (i, 0))],
            out_specs=[any_spec, pl.BlockSpec(memory_space=pltpu.SMEM)],
            scratch_shapes=[pltpu.VMEM((MOE_TILE, ROW_W), F32), pltpu.SemaphoreType.DMA(()),
                            pltpu.SemaphoreType.DMA(())]),
        out_shape=[jax.ShapeDtypeStruct((n_sorted, ROW_W), F32), jax.ShapeDtypeStruct((n_sorted,), jnp.int32)],
        compiler_params=_params(("arbitrary",)),
        name="dispatch",
    )(pos, offs, cnt, end, used, rows)


def _experts_kernel(ea_ref, eb_ref, blk_ref, nvalid_ref, dst_ref, xs_ref, wga_ref, wua_ref, wda_ref, wgb_ref, wub_ref,
                    wdb_ref, lng_ref, lnb_ref, out_hbm, ybuf, sems, *, n_rows):
    del ea_ref, eb_ref, blk_ref
    i = pl.program_id(0)
    n_cur = nvalid_ref[i]
    n_prev = jnp.where(i >= 1, nvalid_ref[jnp.maximum(i - 1, 0)], 0)
    n_prev2 = jnp.where(i >= 2, nvalid_ref[jnp.maximum(i - 2, 0)], 0)
    has_cur = n_cur > 0
    has_prev = n_prev > 0

    def tile_copy(slot):
        return pltpu.make_async_copy(ybuf.at[slot], out_hbm.at[pl.ds(0, MOE_TILE)], sems.at[slot])

    def compute(slot):
        x = xs_ref[:, 0:D_MODEL]
        xb = x.astype(BF16)
        y = None
        for lane, (wg, wu, wd) in enumerate(((wga_ref, wua_ref, wda_ref), (wgb_ref, wub_ref, wdb_ref))):
            gate = xs_ref[:, D_MODEL + lane:D_MODEL + lane + 1]
            hg = jnp.dot(xb, wg[0].astype(BF16), preferred_element_type=F32)
            hu = jnp.dot(xb, wu[0].astype(BF16), preferred_element_type=F32)
            hid = hg * jax.nn.sigmoid(hg) * hu * gate
            part = _bdot(hid, wd[0].astype(BF16))
            y = part if y is None else y + part
        ybuf[slot] = _layer_norm(ALPHA * x + y, lng_ref[...], lnb_ref[...])

    def start_rows(tile, slot):
        base = tile * MOE_TILE
        for r in range(MOE_TILE):
            dst = dst_ref[base + r]
            pltpu.make_async_copy(ybuf.at[slot, pl.ds(r, 1)], out_hbm.at[pl.ds(dst, 1)], sems.at[slot]).start()

    @pl.when(i == 0)
    def _():
        ybuf[2] = jnp.zeros((MOE_TILE, D_MODEL), F32)
        for half in range(2):
            fill = pltpu.make_async_copy(ybuf.at[2], out_hbm.at[pl.ds(n_rows + half * MOE_TILE, MOE_TILE)], sems.at[2])
            fill.start()
            fill.wait()

    slot_cur = i % 3
    slot_prev = (i + 2) % 3

    @pl.when(has_prev & has_cur)
    def _():
        start_rows(i - 1, slot_prev)
        compute(slot_cur)

    @pl.when(has_prev & jnp.logical_not(has_cur))
    def _():
        start_rows(i - 1, slot_prev)

    @pl.when(jnp.logical_not(has_prev) & has_cur)
    def _():
        compute(slot_cur)

    @pl.when(n_prev2 > 0)
    def _():
        tile_copy((i + 1) % 3).wait()


def _experts(tile_a, tile_b, tile_blk, tile_nvalid, tok, xs, w_gate, w_up, w_down, lng, lnb, n_rows, layer):
    n_steps = tile_nvalid.shape[0]
    e0 = layer * N_EXPERTS
    up_a = pl.BlockSpec((1, D_MODEL, D_EXPERT), lambda i, ea, eb, *_: (e0 + ea[i], 0, 0))
    up_b = pl.BlockSpec((1, D_MODEL, D_EXPERT), lambda i, ea, eb, *_: (e0 + eb[i], 0, 0))
    dn_a = pl.BlockSpec((1, D_EXPERT, D_MODEL), lambda i, ea, eb, *_: (e0 + ea[i], 0, 0))
    dn_b = pl.BlockSpec((1, D_EXPERT, D_MODEL), lambda i, ea, eb, *_: (e0 + eb[i], 0, 0))
    vec = pl.BlockSpec((1, D_MODEL), lambda i, *_: (0, 0))
    return pl.pallas_call(
        functools.partial(_experts_kernel, n_rows=n_rows),
        grid_spec=pltpu.PrefetchScalarGridSpec(
            num_scalar_prefetch=5, grid=(n_steps,),
            in_specs=[pl.BlockSpec((MOE_TILE, ROW_W), lambda i, ea, eb, blk, *_: (blk[i], 0)),
                      up_a, up_a, dn_a, up_b, up_b, dn_b, vec, vec],
            out_specs=pl.BlockSpec(memory_space=pl.ANY),
            scratch_shapes=[pltpu.VMEM((3, MOE_TILE, D_MODEL), F32), pltpu.SemaphoreType.DMA((3,))]),
        out_shape=jax.ShapeDtypeStruct((n_rows + 2 * MOE_TILE, D_MODEL), F32),
        compiler_params=_params(("arbitrary",)),
        name="experts",
    )(tile_a, tile_b, tile_blk, tile_nvalid, tok, xs, w_gate, w_up, w_down, w_gate, w_up, w_down,
      _row2(lng), _row2(lnb))


def _lookup(table, idx):
    pick = idx[:, None] == jnp.arange(table.shape[0], dtype=jnp.int32)[None, :]
    return jnp.sum(jnp.where(pick, table[None, :], 0), axis=1)


def _moe(rows, meta, counts, w_gate, w_up, w_down, lng, lnb, layer):
    n_rows = rows.shape[0]
    n_tiles = n_rows // MOE_TILE + N_CLASSES
    cnt = counts[:N_CLASSES, 0]
    tiles_per = (cnt + MOE_TILE - 1) // MOE_TILE
    cls_id = jnp.arange(N_CLASSES, dtype=jnp.int32)
    tile_end = jnp.sum(jnp.where(cls_id[None, :] <= cls_id[:, None], tiles_per[None, :], 0), axis=1)
    tile_start = tile_end - tiles_per
    offs = tile_start * MOE_TILE
    end = tile_end * MOE_TILE
    used = tile_end[N_CLASSES - 1:N_CLASSES]
    tile_id = jnp.arange(n_tiles + 2, dtype=jnp.int32)
    tile_blk = jnp.minimum(tile_id, used - 1)
    tile_cls = jnp.sum((tile_blk[:, None] >= tile_end[None, :]).astype(jnp.int32), axis=1)
    tile_a = _lookup(jnp.asarray(_CLASS_A), tile_cls)
    tile_b = _lookup(jnp.asarray(_CLASS_B), tile_cls)
    tile_nvalid = jnp.clip(_lookup(cnt, tile_cls) - (tile_id - _lookup(tile_start, tile_cls)) * MOE_TILE, 0, MOE_TILE)
    pos = _lookup(offs, meta[0]) + meta[1]
    xs, dst = _dispatch(pos, offs, cnt, end, used, rows, n_tiles)
    return _experts(tile_a, tile_b, tile_blk, tile_nvalid, dst, xs, w_gate, w_up, w_down, lng, lnb, n_rows, layer)


def kernel(x_prompt, x_sample, mem_prompt, cache_mem_k, cache_mem_v, state_conv_a, state_conv_b, state_pool,
           ln_g, ln_b, ab_w_in, ab_conv_a_w, ab_conv_a_b, ab_norm_a_g, ab_norm_a_b, ab_conv_b_w, ab_w_out,
           cd_w_in, cd_pool_w, cd_pool_scale, cd_v_norm_g, cd_v_norm_b, cd_w_s, cd_b_s, cd_w_out,
           ca_wq, ca_wk, ca_wv, ca_wo, router_w, router_b, moe_w_gate, moe_w_up, moe_w_down):
    assert ln_g.shape[0] == DEPTH and ab_w_in.shape[0] == 1 and cd_w_in.shape[0] == 1
    bsz, seq, _ = x_prompt.shape
    dec_b, dec_seq, _ = x_sample.shape
    n_prompt = bsz * seq
    n_sample = dec_b * dec_seq
    sample_seqs = ROW_TILE // dec_seq

    n_pool = len(POOL_WINDOWS)
    pool_c = D_HALF // n_pool
    pool_bd = jnp.zeros((D_HALF, D_HALF), F32)
    for g in range(n_pool):
        pool_bd = pool_bd.at[g * pool_c:(g + 1) * pool_c, g * pool_c:(g + 1) * pool_c].set(cd_pool_w[0, g])
    pool_bd = pool_bd.astype(BF16)
    ab_w_in_hi, ab_w_in_lo = _split_weight(ab_w_in[0])
    ab_w_out_hi, ab_w_out_lo = _split_weight(ab_w_out[0])
    rw_hi, rw_lo = _split_weight(router_w.T)
    rw_t = jnp.concatenate([rw_hi, rw_lo], axis=0)
    cd_w_in_b, cd_w_out_b = cd_w_in[0].astype(BF16), cd_w_out[0].astype(BF16)
    wq_b, wo_b = ca_wq.astype(BF16), ca_wo.astype(BF16)
    wg_b = moe_w_gate.reshape(DEPTH * N_EXPERTS, D_MODEL, D_EXPERT)
    wu_b = moe_w_up.reshape(DEPTH * N_EXPERTS, D_MODEL, D_EXPERT)
    wd_b = moe_w_down.reshape(DEPTH * N_EXPERTS, D_EXPERT, D_MODEL)

    mem_k_prompt, k_p = _mem_projection(mem_prompt, ca_wk)
    mem_v_prompt, v_p = _mem_projection(mem_prompt, ca_wv)
    k_s = _heads_to_columns(cache_mem_k.reshape(DEPTH * dec_b, N_MEM, MEM_HEADS, MEM_HEAD_DIM))
    v_s = _heads_to_columns(cache_mem_v.reshape(DEPTH * dec_b, N_MEM, MEM_HEADS, MEM_HEAD_DIM))

    def mixer_ab(x, buf_a, buf_b, n_seq, seg, **kw):
        return _mixer_ab(x, buf_a, buf_b, ab_w_in_hi, ab_w_in_lo, ab_conv_a_w[0], ab_conv_a_b[0], ab_norm_a_g[0],
                         ab_norm_a_b[0], ab_conv_b_w[0], ab_w_out_hi, ab_w_out_lo, ln_g[0, 0], ln_b[0, 0],
                         n_seq=n_seq, seg=seg, **kw)

    def mixer_cd(x, row_start, buf_p, length, n_seq, seg, pos0, emit_v):
        return _mixer_cd(x, row_start, buf_p, length, cd_w_in_b, pool_bd, cd_pool_scale[0], cd_v_norm_g[0],
                         cd_v_norm_b[0], cd_w_s[0], cd_b_s[0], cd_w_out_b, ln_g[1, 0], ln_b[1, 0],
                         n_seq=n_seq, seg=seg, pos0=pos0, emit_v=emit_v)

    def attn_moe(layer, x_p, x_s):
        rows, meta, counts = _attn_router(x_p, x_s, k_p, v_p, k_s, v_s, wq_b, wo_b, ln_g[layer, 1], ln_b[layer, 1],
                                          rw_t, router_b, layer=layer, prompt_len=seq, sample_len=dec_seq)
        return _moe(rows, meta, counts, wg_b, wu_b, wd_b, ln_g[layer, 2], ln_b[layer, 2], layer)

    xp_flat = x_prompt.reshape(n_prompt, D_MODEL)
    tiles_p = seq // ROW_TILE
    x_p, conv_a_p, conv_b_p = mixer_ab(xp_flat, jnp.zeros((bsz, CONV_A - 1, D_HALF), F32),
                                       jnp.zeros((bsz, CONV_B - 1, D_HALF), F32), 1, ROW_TILE,
                                       precise=False, skip_tail=STATE_TILES)
    x_p, conv_a_p, conv_b_p = mixer_ab(xp_flat, conv_a_p, conv_b_p, 1, ROW_TILE, precise=True,
                                       tile_lo=tiles_p - STATE_TILES, tiles=STATE_TILES, y_prev=x_p)
    x_s, conv_a_s, conv_b_s = mixer_ab(x_sample.reshape(n_sample, D_MODEL), state_conv_a[0], state_conv_b[0],
                                       sample_seqs, dec_seq, precise=False)
    x_all = attn_moe(0, x_p, x_s)

    x_p, pool_p = mixer_cd(x_all, 0, jnp.zeros((bsz, POOL_BUF, D_HALF), F32), seq, 1, ROW_TILE, 0, False)
    x_s, pool_s, v_s_rows = mixer_cd(x_all, n_prompt, state_pool[0], dec_seq, sample_seqs, dec_seq, PAST_LEN, True)
    x_all = attn_moe(1, x_p, x_s)

    y_prompt = x_all[0:n_prompt].reshape(bsz, seq, D_MODEL)
    y_sample = x_all[n_prompt:n_prompt + n_sample].reshape(dec_b, dec_seq, D_MODEL)
    return (y_prompt, y_sample, mem_k_prompt, mem_v_prompt, conv_a_p[None], conv_b_p[None], pool_p[None],
            conv_a_s[None], conv_b_s[None], pool_s[None], v_s_rows[None])
```

```python
import functools

import jax
import jax.numpy as jnp
import numpy as np
from jax import lax
from jax.experimental import pallas as pl
from jax.experimental.pallas import tpu as pltpu

F32 = jnp.float32
BF16 = jnp.bfloat16

D_MODEL = 1024
D_HALF = D_MODEL // 2
DEPTH = 2
PAST_LEN = 4096
CHUNK = 64
CHUNK_SHIFT = CHUNK.bit_length() - 1
CONV_A = 31
CONV_B = 3
POOL_WINDOWS = (2, 4, 8, 16)
POOL_BUF = max(POOL_WINDOWS) - 1
N_HEAD_D = 4
GMLP_CHUNK = 128
N_MEM = 256
MEM_HEADS = 4
MEM_HEAD_DIM = D_MODEL // MEM_HEADS
N_EXPERTS = 16
N_EXPERT_GROUPS = 4
GROUP_SIZE = N_EXPERTS // N_EXPERT_GROUPS
GROUP_SHIFT = GROUP_SIZE.bit_length() - 1
PAIRS_PER_GROUP = GROUP_SIZE * (GROUP_SIZE - 1) // 2
N_CLASSES = N_EXPERT_GROUPS * PAIRS_PER_GROUP
D_EXPERT = D_MODEL // 2
ALPHA = (2 * DEPTH) ** 0.25
LN_EPS = 1e-5

LANES = 128
SUBLANES = 8
ROW_W = D_MODEL + LANES
HIST_A = 32
HIST_B = 8
HIST_P = 16
CONV_ROWS = 64
ROW_TILE = 512
MOE_TILE = 256
MOE_SHIFT = MOE_TILE.bit_length() - 1
CLASS_ROWS = 32
DISPATCH_TILE = 1024
SPLIT_ROWS = 256
CACHE_SEQS = 4
STATE_TILES = 1
VMEM_LIMIT = 56 * 1024 * 1024

_PAIR_AB = [(a, b) for a in range(GROUP_SIZE) for b in range(a + 1, GROUP_SIZE)]
_CLASS_A = np.array([g * GROUP_SIZE + a for g in range(N_EXPERT_GROUPS) for a, _ in _PAIR_AB], np.int32)
_CLASS_B = np.array([g * GROUP_SIZE + b for g in range(N_EXPERT_GROUPS) for _, b in _PAIR_AB], np.int32)


def _layer_norm(x, g, b):
    mu = jnp.mean(x, axis=-1, keepdims=True)
    xc = x - mu
    var = jnp.mean(xc * xc, axis=-1, keepdims=True)
    return xc * lax.rsqrt(var + LN_EPS) * g + b


def _bdot(a, w):
    return jnp.dot(a.astype(BF16), w, preferred_element_type=F32)


def _split(a):
    hi = a.astype(BF16)
    return hi, (a - hi.astype(F32)).astype(BF16)


def _dot3(a, w_hi, w_lo):
    a_hi, a_lo = _split(a)
    return (jnp.dot(a_hi, w_hi, preferred_element_type=F32) + jnp.dot(a_lo, w_hi, preferred_element_type=F32)
            + jnp.dot(a_hi, w_lo, preferred_element_type=F32))


def _params(sem):
    return pltpu.CompilerParams(dimension_semantics=sem, vmem_limit_bytes=VMEM_LIMIT)


def _full(shape):
    return pl.BlockSpec(shape, lambda *_: (0,) * len(shape), pipeline_mode=pl.Buffered(1))


def _row2(v):
    return v.reshape(1, -1)


def _split_weight_kernel(w_ref, hi_ref, lo_ref):
    hi, lo = _split(w_ref[...])
    hi_ref[...] = hi
    lo_ref[...] = lo


def _split_weight(w):
    rows, cols = w.shape
    blk = min(rows, SPLIT_ROWS)
    spec = pl.BlockSpec((blk, cols), lambda i: (i, 0))
    return pl.pallas_call(
        _split_weight_kernel,
        grid=(rows // blk,),
        in_specs=[spec],
        out_specs=[spec, spec],
        out_shape=[jax.ShapeDtypeStruct(w.shape, BF16)] * 2,
        compiler_params=_params(("arbitrary",)),
        name="split_weight",
    )(w)


def _proj_kernel(x_ref, w_ref, o_ref, o16_ref, *, batch):
    res = _bdot(x_ref[...], w_ref[0].astype(BF16))
    o16_ref[...] = res.astype(BF16).reshape(batch, N_MEM, D_MODEL)
    for hd in range(MEM_HEADS):
        cols = slice(hd * MEM_HEAD_DIM, (hd + 1) * MEM_HEAD_DIM)
        o_ref[0, :, :, hd, :] = res[:, cols].reshape(batch, N_MEM, MEM_HEAD_DIM)


def _mem_projection(mem, w):
    batch, n = mem.shape[0], w.shape[0]
    return pl.pallas_call(
        functools.partial(_proj_kernel, batch=batch),
        grid=(n,),
        in_specs=[pl.BlockSpec((batch * N_MEM, D_MODEL), lambda j: (0, 0)),
                  pl.BlockSpec((1, D_MODEL, D_MODEL), lambda j: (j, 0, 0))],
        out_specs=[pl.BlockSpec((1, batch, N_MEM, MEM_HEADS, MEM_HEAD_DIM), lambda j: (j, 0, 0, 0, 0)),
                   pl.BlockSpec((batch, N_MEM, D_MODEL), lambda j: (j, 0, 0))],
        out_shape=[jax.ShapeDtypeStruct((n, batch, N_MEM, MEM_HEADS, MEM_HEAD_DIM), F32),
                   jax.ShapeDtypeStruct((n * batch, N_MEM, D_MODEL), BF16)],
        compiler_params=_params(("arbitrary",)),
        name="mem_projection",
    )(mem.reshape(batch * N_MEM, D_MODEL), w)


def _heads_to_columns_kernel(c_hbm, o_ref, buf, sem):
    step = pl.program_id(0)

    def head_copies(at_step, slot):
        return [pltpu.make_async_copy(c_hbm.at[at_step * CACHE_SEQS + s, :, hd, :], buf.at[slot, s, hd], sem.at[slot])
                for s in range(CACHE_SEQS) for hd in range(MEM_HEADS)]

    @pl.when(step == 0)
    def _():
        for copy in head_copies(0, 0):
            copy.start()

    @pl.when(step + 1 < pl.num_programs(0))
    def _():
        for copy in head_copies(step + 1, (step + 1) % 2):
            copy.start()

    slot = step % 2
    for copy in head_copies(step, slot):
        copy.wait()
    for s in range(CACHE_SEQS):
        for hd in range(MEM_HEADS):
            o_ref[s, :, hd * MEM_HEAD_DIM:(hd + 1) * MEM_HEAD_DIM] = buf[slot, s, hd].astype(BF16)


def _heads_to_columns(cache):
    n = cache.shape[0]
    return pl.pallas_call(
        _heads_to_columns_kernel,
        grid=(n // CACHE_SEQS,),
        in_specs=[pl.BlockSpec(memory_space=pl.ANY)],
        out_specs=pl.BlockSpec((CACHE_SEQS, N_MEM, D_MODEL), lambda i: (i, 0, 0)),
        out_shape=jax.ShapeDtypeStruct((n, N_MEM, D_MODEL), BF16),
        scratch_shapes=[pltpu.VMEM((2, CACHE_SEQS, MEM_HEADS, N_MEM, MEM_HEAD_DIM), F32),
                        pltpu.SemaphoreType.DMA((2,))],
        compiler_params=_params(("arbitrary",)),
        name="heads_to_columns",
    )(cache)


def _load_history(ext_ref, buf_ref, first, hist, keep, seg):
    @pl.when(first)
    def _():
        ext_ref[:, hist - keep:hist, :] = buf_ref[...]

    @pl.when(jnp.logical_not(first))
    def _():
        ext_ref[:, hist - keep:hist, :] = ext_ref[:, seg + hist - keep:seg + hist, :]


def _depthwise_conv(ext_ref, w_ref, out_ref, *, n_seq, seg, taps, hist, shifted_ref=None):
    rc = min(CONV_ROWS, seg)
    off0 = hist - (taps - 1)
    length = hist + seg
    if shifted_ref is not None:
        for r in range(1, SUBLANES):
            shifted_ref[r - 1, :, 0:length - SUBLANES, :] = ext_ref[:, r:r + length - SUBLANES, :]
    for s in range(n_seq):
        for r0 in range(0, seg, rc):
            for lb in range(0, D_HALF, LANES):
                acc = None
                for k in range(taps):
                    lo = off0 + k + r0
                    shift = (off0 + k) % SUBLANES
                    if shifted_ref is None or shift == 0:
                        win = ext_ref[s, lo:lo + rc, lb:lb + LANES]
                    else:
                        win = shifted_ref[shift - 1, s, lo - shift:lo - shift + rc, lb:lb + LANES]
                    term = w_ref[k:k + 1, lb:lb + LANES] * win
                    acc = term if acc is None else acc + term
                out_ref[s * seg + r0:s * seg + r0 + rc, lb:lb + LANES] = acc


def _mixer_ab_kernel(x_ref, bufa_ref, bufb_ref, w_in_ref, w_in_lo_ref, caw_ref, cab_ref, nag_ref, nab_ref, cbw_ref,
                     w_out_ref, w_out_lo_ref, lng_ref, lnb_ref, *refs, n_seq, seg, precise, skip_tail, has_prev):
    if has_prev:
        refs = refs[1:]
    y_ref, nbufa_ref, nbufb_ref, a_ext, a_shift, cb_ext, conv_a, conv_b = refs

    def body():
        first = pl.program_id(1) == 0
        _load_history(a_ext, bufa_ref, first, HIST_A, CONV_A - 1, seg)
        _load_history(cb_ext, bufb_ref, first, HIST_B, CONV_B - 1, seg)
        x = x_ref[...]
        h = _dot3(x, w_in_ref[...], w_in_lo_ref[...]) if precise else _bdot(x, w_in_ref[...])
        a = h[:, 0:D_HALF] * jax.nn.sigmoid(h[:, D_HALF:2 * D_HALF])
        cb = h[:, 3 * D_HALF:4 * D_HALF] * h[:, 4 * D_HALF:5 * D_HALF]
        a_ext[:, HIST_A:HIST_A + seg, :] = a.reshape(n_seq, seg, D_HALF)
        cb_ext[:, HIST_B:HIST_B + seg, :] = cb.reshape(n_seq, seg, D_HALF)
        nbufa_ref[...] = a_ext[:, seg + HIST_A - (CONV_A - 1):seg + HIST_A, :]
        nbufb_ref[...] = cb_ext[:, seg + HIST_B - (CONV_B - 1):seg + HIST_B, :]

        _depthwise_conv(a_ext, caw_ref, conv_a, n_seq=n_seq, seg=seg, taps=CONV_A, hist=HIST_A, shifted_ref=a_shift)
        _depthwise_conv(cb_ext, cbw_ref, conv_b, n_seq=n_seq, seg=seg, taps=CONV_B, hist=HIST_B)

        a2 = _layer_norm(conv_a[...] + cab_ref[...], nag_ref[...], nab_ref[...])
        a2 = a2 * jax.nn.sigmoid(a2)
        b2 = h[:, 2 * D_HALF:3 * D_HALF] * conv_b[...]
        if precise:
            y = (_dot3(a2, w_out_ref[0:D_HALF, :], w_out_lo_ref[0:D_HALF, :])
                 + _dot3(b2, w_out_ref[D_HALF:D_MODEL, :], w_out_lo_ref[D_HALF:D_MODEL, :]))
        else:
            y = _bdot(a2, w_out_ref[0:D_HALF, :]) + _bdot(b2, w_out_ref[D_HALF:D_MODEL, :])
        y_ref[...] = _layer_norm(ALPHA * x + y, lng_ref[...], lnb_ref[...])

    if skip_tail == 0:
        body()
    else:
        live = pl.program_id(1) < pl.num_programs(1) - skip_tail
        pl.when(live)(body)

        @pl.when(jnp.logical_not(live))
        def _():
            y_ref[...] = jnp.zeros_like(y_ref)


def _mixer_ab(x, buf_a, buf_b, w_in, w_in_lo, caw, cab, nag, nab, cbw, w_out, w_out_lo, lng, lnb, *,
              n_seq, seg, precise, tile_lo=0, tiles=None, skip_tail=0, y_prev=None):
    batch = buf_a.shape[0]
    length = x.shape[0] // batch
    d_in = w_in.shape[1]
    n_l = length // seg
    tiles = n_l if tiles is None else tiles
    rows = n_seq * seg
    lo_in = _full((D_MODEL, d_in)) if precise else _full((SUBLANES, LANES))
    lo_out = _full((D_MODEL, D_MODEL)) if precise else _full((SUBLANES, LANES))
    row_blk = pl.BlockSpec((rows, D_MODEL), lambda b, l: (b * n_l + tile_lo + l, 0))
    in_specs = [row_blk,
                pl.BlockSpec((n_seq, CONV_A - 1, D_HALF), lambda b, l: (b, 0, 0)),
                pl.BlockSpec((n_seq, CONV_B - 1, D_HALF), lambda b, l: (b, 0, 0)),
                _full((D_MODEL, d_in)), lo_in, _full((CONV_A, D_HALF)), _full((1, D_HALF)), _full((1, D_HALF)),
                _full((1, D_HALF)), _full((CONV_B, D_HALF)), _full((D_MODEL, D_MODEL)), lo_out,
                _full((1, D_MODEL)), _full((1, D_MODEL))]
    args = [x, buf_a, buf_b, w_in, w_in_lo, caw, _row2(cab), _row2(nag), _row2(nab), cbw, w_out, w_out_lo,
            _row2(lng), _row2(lnb)]
    aliases = {}
    if y_prev is not None:
        aliases = {len(args): 0}
        in_specs.append(pl.BlockSpec(memory_space=pl.ANY))
        args.append(y_prev)
    return pl.pallas_call(
        functools.partial(_mixer_ab_kernel, n_seq=n_seq, seg=seg, precise=precise, skip_tail=skip_tail,
                          has_prev=y_prev is not None),
        grid=(batch // n_seq, tiles),
        in_specs=in_specs,
        out_specs=[row_blk,
                   pl.BlockSpec((n_seq, CONV_A - 1, D_HALF), lambda b, l: (b, 0, 0)),
                   pl.BlockSpec((n_seq, CONV_B - 1, D_HALF), lambda b, l: (b, 0, 0))],
        out_shape=[jax.ShapeDtypeStruct((batch * length, D_MODEL), F32),
                   jax.ShapeDtypeStruct((batch, CONV_A - 1, D_HALF), F32),
                   jax.ShapeDtypeStruct((batch, CONV_B - 1, D_HALF), F32)],
        scratch_shapes=[pltpu.VMEM((n_seq, HIST_A + seg, D_HALF), F32),
                        pltpu.VMEM((SUBLANES - 1, n_seq, HIST_A + seg - SUBLANES, D_HALF), F32),
                        pltpu.VMEM((n_seq, HIST_B + seg, D_HALF), F32),
                        pltpu.VMEM((rows, D_HALF), F32),
                        pltpu.VMEM((rows, D_HALF), F32)],
        input_output_aliases=aliases,
        compiler_params=_params(("arbitrary", "arbitrary")),
        name="mixer_ab",
    )(*args)


def _mixer_cd_kernel(x_ref, bufp_ref, w_in_ref, pw_ref, ps_ref, vg_ref, vb_ref, ws_ref, bs_ref, w_out_ref,
                     lng_ref, lnb_ref, *refs, n_seq, seg, pos0, n_mix, emit_v):
    if emit_v:
        y_ref, nbufp_ref, v_ref, c_ext, pooled, mixed = refs
    else:
        y_ref, nbufp_ref, c_ext, pooled, mixed = refs
    rows = n_seq * seg
    li = pl.program_id(1)
    _load_history(c_ext, bufp_ref, li == 0, HIST_P, POOL_BUF, seg)
    x = x_ref[...]
    h = _bdot(x, w_in_ref[...])
    c_in = h[:, 0:D_HALF]
    c_ext[:, HIST_P:HIST_P + seg, :] = c_in.reshape(n_seq, seg, D_HALF)
    nbufp_ref[...] = c_ext[:, seg + HIST_P - POOL_BUF:seg + HIST_P, :]

    rc = min(CONV_ROWS, seg)
    for g, win in enumerate(POOL_WINDOWS):
        lanes = slice(g * LANES, (g + 1) * LANES)
        for s in range(n_seq):
            for r0 in range(0, seg, rc):
                cur = c_ext[s, HIST_P + r0:HIST_P + r0 + rc, lanes]
                acc = cur
                for j in range(1, win):
                    acc = acc + c_ext[s, HIST_P + r0 - j:HIST_P + r0 - j + rc, lanes]
                pos = pos0 + li * seg + r0 + lax.broadcasted_iota(jnp.int32, (rc, LANES), 0)
                cnt = jnp.minimum(pos + 1, win).astype(F32)
                pooled[s * seg + r0:s * seg + r0 + rc, lanes] = acc / cnt - cur
    c = _bdot(pooled[...], pw_ref[...]) * ps_ref[...]

    z = jax.nn.gelu(h[:, D_HALF:3 * D_HALF], approximate=True)
    u = z[:, 0:D_HALF]
    v = _layer_norm(z[:, D_HALF:2 * D_HALF], vg_ref[...], vb_ref[...])
    if emit_v:
        v_ref[...] = v.reshape(n_seq, seg, D_HALF)
    vb16 = v.astype(BF16)

    ri = lax.broadcasted_iota(jnp.int32, (n_mix, n_mix), 0) >> CHUNK_SHIFT
    ci = lax.broadcasted_iota(jnp.int32, (n_mix, n_mix), 1) >> CHUNK_SHIFT
    for g in range(N_HEAD_D):
        lanes = slice(g * LANES, (g + 1) * LANES)
        ws = jnp.where(ci <= ri, ws_ref[g], 0.0).astype(BF16)
        bias = bs_ref[:, g:g + 1]
        for r0 in range(0, rows, n_mix):
            mixed[r0:r0 + n_mix, lanes] = jnp.dot(ws, vb16[r0:r0 + n_mix, lanes], preferred_element_type=F32) + bias
    d = u * mixed[...]
    y = _bdot(c, w_out_ref[0:D_HALF, :]) + _bdot(d, w_out_ref[D_HALF:D_MODEL, :])
    y_ref[...] = _layer_norm(ALPHA * x + y, lng_ref[...], lnb_ref[...])


def _mixer_cd(x, row_start, buf_p, length, w_in, pw, ps, vg, vb, ws, bs, w_out, lng, lnb, *,
              n_seq, seg, pos0, emit_v):
    batch = buf_p.shape[0]
    d_in = w_in.shape[1]
    n_mix = min(length, GMLP_CHUNK)
    assert seg % n_mix == 0
    n_l = length // seg
    rows = n_seq * seg
    blk0 = row_start // rows
    ws_n = ws[:, :n_mix, :n_mix]
    bs_t = bs[:, :n_mix].T
    out_specs = [pl.BlockSpec((rows, D_MODEL), lambda b, l: (b * n_l + l, 0)),
                 pl.BlockSpec((n_seq, POOL_BUF, D_HALF), lambda b, l: (b, 0, 0))]
    out_shape = [jax.ShapeDtypeStruct((batch * length, D_MODEL), F32),
                 jax.ShapeDtypeStruct((batch, POOL_BUF, D_HALF), F32)]
    if emit_v:
        out_specs.append(pl.BlockSpec((n_seq, seg, D_HALF), lambda b, l: (b, l, 0)))
        out_shape.append(jax.ShapeDtypeStruct((batch, length, D_HALF), F32))
    return pl.pallas_call(
        functools.partial(_mixer_cd_kernel, n_seq=n_seq, seg=seg, pos0=pos0, n_mix=n_mix, emit_v=emit_v),
        grid=(batch // n_seq, n_l),
        in_specs=[pl.BlockSpec((rows, D_MODEL), lambda b, l: (blk0 + b * n_l + l, 0)),
                  pl.BlockSpec((n_seq, POOL_BUF, D_HALF), lambda b, l: (b, 0, 0)),
                  _full((D_MODEL, d_in)), _full((D_HALF, D_HALF)), _full((1, D_HALF)), _full((1, D_HALF)),
                  _full((1, D_HALF)), _full((N_HEAD_D, n_mix, n_mix)), _full((n_mix, N_HEAD_D)),
                  _full((D_MODEL, D_MODEL)), _full((1, D_MODEL)), _full((1, D_MODEL))],
        out_specs=out_specs,
        out_shape=out_shape,
        scratch_shapes=[pltpu.VMEM((n_seq, HIST_P + seg, D_HALF), F32),
                        pltpu.VMEM((rows, D_HALF), F32),
                        pltpu.VMEM((rows, D_HALF), F32)],
        compiler_params=_params(("arbitrary", "arbitrary")),
        name="mixer_cd",
    )(x, buf_p, w_in, pw, _row2(ps), _row2(vg), _row2(vb), ws_n, bs_t, w_out, _row2(lng), _row2(lnb))


def _attn_router_kernel(xp_ref, xs_ref, kp_ref, vp_ref, ks_ref, vs_ref, wq_ref, wo_ref, lng_ref, lnb_ref, rw_ref, rb_ref,
                        rows_ref, meta_ref, counts_ref,
                        q_scr, o_scr, carry, tri, *, n_prompt_tiles, sample_seg):
    rows = ROW_TILE
    step = pl.program_id(0)
    is_prompt = step < n_prompt_tiles

    @pl.when(step == 0)
    def _():
        carry[...] = jnp.zeros_like(carry)
        upper = (lax.broadcasted_iota(jnp.int32, (rows, rows), 0) <= lax.broadcasted_iota(jnp.int32, (rows, rows), 1))
        tri[...] = upper.astype(F32).astype(BF16)

    def run(x_ref, k_ref, v_ref, n_seq, seg):
        x = x_ref[...]
        q_scr[...] = (_bdot(x, wq_ref[0]) * (MEM_HEAD_DIM ** -0.5)).astype(BF16)
        for s in range(n_seq):
            for hd in range(MEM_HEADS):
                cols = slice(hd * MEM_HEAD_DIM, (hd + 1) * MEM_HEAD_DIM)
                sc = lax.dot_general(q_scr[s * seg:(s + 1) * seg, cols], k_ref[s, :, cols], (((1,), (1,)), ((), ())),
                                     preferred_element_type=F32)
                p = jnp.exp(sc - jnp.max(sc, axis=-1, keepdims=True))
                den = jnp.sum(p, axis=-1, keepdims=True)
                o_scr[s * seg:(s + 1) * seg, cols] = _bdot(p, v_ref[s, :, cols]) / den
        x2 = _layer_norm(ALPHA * x + _bdot(o_scr[...], wo_ref[0]), lng_ref[...], lnb_ref[...])
        rows_ref[:, 0:D_MODEL] = x2

        nt = (((1,), (1,)), ((), ()))
        x2_hi, x2_lo = _split(x2)
        both = lax.dot_general(rw_ref[...], x2_hi, nt, preferred_element_type=F32)
        cross = lax.dot_general(rw_ref[0:N_EXPERTS, :], x2_lo, nt, preferred_element_type=F32)
        logits = both[0:N_EXPERTS] + both[N_EXPERTS:2 * N_EXPERTS] + cross + rb_ref[...]
        e = jnp.exp(logits - jnp.max(logits, axis=0, keepdims=True))
        scores = e / jnp.sum(e, axis=0, keepdims=True)
        eid = lax.broadcasted_iota(jnp.int32, (N_EXPERTS, rows), 0)
        egrp = eid >> GROUP_SHIFT
        best = jnp.max(jnp.where(egrp == 0, scores, -1.0), axis=0, keepdims=True)
        g_sel = jnp.zeros((1, rows), jnp.int32)
        for g in range(1, N_EXPERT_GROUPS):
            gs = jnp.max(jnp.where(egrp == g, scores, -1.0), axis=0, keepdims=True)
            upd = gs > best
            g_sel = jnp.where(upd, g, g_sel)
            best = jnp.where(upd, gs, best)
        masked = jnp.where(egrp == g_sel, scores, -1.0)
        m1 = jnp.max(masked, axis=0, keepdims=True)
        i1 = jnp.min(jnp.where(masked == m1, eid, N_EXPERTS), axis=0, keepdims=True)
        masked2 = jnp.where(eid == i1, -2.0, masked)
        m2 = jnp.max(masked2, axis=0, keepdims=True)
        i2 = jnp.min(jnp.where(masked2 == m2, eid, N_EXPERTS), axis=0, keepdims=True)
        tot = m1 + m2
        g1 = m1 / tot
        g2 = m2 / tot
        first_low = i1 < i2
        ea = jnp.where(first_low, i1, i2) & (GROUP_SIZE - 1)
        eb = jnp.where(first_low, i2, i1) & (GROUP_SIZE - 1)
        gate_a = jnp.where(first_low, g1, g2)
        gate_b = jnp.where(first_low, g2, g1)
        pair = eb - 1 + jnp.where(ea == 1, 2, 0) + jnp.where(ea == 2, 3, 0)
        cls = g_sel * PAIRS_PER_GROUP + pair

        lane_row = lax.broadcasted_iota(jnp.int32, (LANES, rows), 0)
        gates_t = jnp.where(lane_row == 0, gate_a, jnp.where(lane_row == 1, gate_b, 0.0))
        rows_ref[:, D_MODEL:ROW_W] = gates_t.T

        onehot = (lax.broadcasted_iota(jnp.int32, (CLASS_ROWS, rows), 0) == cls).astype(F32)
        cum = jnp.dot(onehot.astype(BF16), tri[...], preferred_element_type=F32)
        before = carry[:, 0:1]
        rank = jnp.sum(onehot * (before + cum), axis=0, keepdims=True) - 1.0
        sub = lax.broadcasted_iota(jnp.int32, (SUBLANES, rows), 0)
        meta_ref[...] = jnp.where(sub == 0, cls, jnp.where(sub == 1, rank.astype(jnp.int32), 0))
        carry[...] = carry[...] + cum[:, rows - 1:rows]
        counts_ref[...] = carry[...].astype(jnp.int32)

    pl.when(is_prompt)(lambda: run(xp_ref, kp_ref, vp_ref, 1, rows))
    pl.when(jnp.logical_not(is_prompt))(lambda: run(xs_ref, ks_ref, vs_ref, rows // sample_seg, sample_seg))


def _attn_router(x_p, x_s, k_p, v_p, k_s, v_s, wq, wo, lng, lnb, rw_t, rb, *, layer, prompt_len, sample_len):
    rows = ROW_TILE
    n_p = x_p.shape[0] // rows
    n_s = x_s.shape[0] // rows
    tiles_per_seq = prompt_len // rows
    seq_per_tile = rows // sample_len
    total = x_p.shape[0] + x_s.shape[0]
    n_prompt_seq = n_p // tiles_per_seq
    p0 = layer * n_prompt_seq
    s0 = layer * n_s
    p_idx = lambda i: jnp.minimum(i, n_p - 1)
    s_idx = lambda i: jnp.maximum(i - n_p, 0)
    layer_w = pl.BlockSpec((1, D_MODEL, D_MODEL), lambda i: (layer, 0, 0), pipeline_mode=pl.Buffered(1))
    return pl.pallas_call(
        functools.partial(_attn_router_kernel, n_prompt_tiles=n_p, sample_seg=sample_len),
        grid=(n_p + n_s,),
        in_specs=[pl.BlockSpec((rows, D_MODEL), lambda i: (p_idx(i), 0)),
                  pl.BlockSpec((rows, D_MODEL), lambda i: (s_idx(i), 0)),
                  pl.BlockSpec((1, N_MEM, D_MODEL), lambda i: (p0 + p_idx(i) // tiles_per_seq, 0, 0)),
                  pl.BlockSpec((1, N_MEM, D_MODEL), lambda i: (p0 + p_idx(i) // tiles_per_seq, 0, 0)),
                  pl.BlockSpec((seq_per_tile, N_MEM, D_MODEL), lambda i: (s0 + s_idx(i), 0, 0)),
                  pl.BlockSpec((seq_per_tile, N_MEM, D_MODEL), lambda i: (s0 + s_idx(i), 0, 0)),
                  layer_w, layer_w, _full((1, D_MODEL)), _full((1, D_MODEL)),
                  _full((2 * N_EXPERTS, D_MODEL)), _full((N_EXPERTS, 1))],
        out_specs=[pl.BlockSpec((rows, ROW_W), lambda i: (i, 0)),
                   pl.BlockSpec((SUBLANES, rows), lambda i: (0, i)),
                   pl.BlockSpec((CLASS_ROWS, LANES), lambda i: (0, 0))],
        out_shape=[jax.ShapeDtypeStruct((total, ROW_W), F32),
                   jax.ShapeDtypeStruct((SUBLANES, total), jnp.int32),
                   jax.ShapeDtypeStruct((CLASS_ROWS, LANES), jnp.int32)],
        scratch_shapes=[pltpu.VMEM((rows, D_MODEL), BF16),
                        pltpu.VMEM((rows, D_MODEL), F32),
                        pltpu.VMEM((CLASS_ROWS, LANES), F32),
                        pltpu.VMEM((rows, rows), BF16)],
        compiler_params=_params(("arbitrary",)),
        name="attn_router",
    )(x_p, x_s, k_p, v_p, k_s, v_s, wq, wo, _row2(lng), _row2(lnb), rw_t, rb.reshape(N_EXPERTS, 1))


def _dispatch_kernel(pos_ref, offs_ref, cnt_ref, end_ref, used_ref, rows_ref, xs_hbm, dst_ref,
                     zeros, sem, zsem, *, tile, n_tiles, n_rows):
    step = pl.program_id(0)

    def class_padding(c, go):
        start = offs_ref[c] + cnt_ref[c]
        head = (-start) & (SUBLANES - 1)
        bulk_start = pl.multiple_of(start + head, SUBLANES)
        bulk = pl.multiple_of(end_ref[c] - bulk_start, SUBLANES)

        def one(r, carry_):
            go(pltpu.make_async_copy(zeros.at[pl.ds(0, 1)], xs_hbm.at[pl.ds(r, 1)], zsem))
            return carry_
        lax.fori_loop(start, start + head, one, 0)

        @pl.when(bulk > 0)
        def _():
            go(pltpu.make_async_copy(zeros.at[pl.ds(0, bulk)], xs_hbm.at[pl.ds(bulk_start, bulk)], zsem))

    def unused_tiles(go):
        def one(j, carry_):
            go(pltpu.make_async_copy(zeros, xs_hbm.at[pl.ds(pl.multiple_of(j * MOE_TILE, MOE_TILE), MOE_TILE)], zsem))
            return carry_
        lax.fori_loop(used_ref[0], n_tiles, one, 0)

    @pl.when(step == 0)
    def _():
        zeros[...] = jnp.zeros_like(zeros)

        def no_token(p, carry_):
            dst_ref[p] = n_rows + ((p >> MOE_SHIFT) & 1) * MOE_TILE + (p & (MOE_TILE - 1))
            return carry_
        for c in range(N_CLASSES):
            lax.fori_loop(offs_ref[c] + cnt_ref[c], end_ref[c], no_token, 0)
            class_padding(c, lambda copy: copy.start())
        lax.fori_loop(used_ref[0] * MOE_TILE, n_tiles * MOE_TILE, no_token, 0)
        unused_tiles(lambda copy: copy.start())

    for r in range(tile):
        t = step * tile + r
        p = pos_ref[t]
        dst_ref[p] = t
        pltpu.make_async_copy(rows_ref.at[pl.ds(r, 1)], xs_hbm.at[pl.ds(p, 1)], sem).start(priority=r % 2)
    pltpu.make_async_copy(rows_ref, xs_hbm.at[pl.ds(0, tile)], sem).wait()

    @pl.when(step == 0)
    def _():
        for c in range(N_CLASSES):
            class_padding(c, lambda copy: copy.wait())
        unused_tiles(lambda copy: copy.wait())


def _dispatch(pos, offs, cnt, end, used, rows, n_tiles):
    n_rows = rows.shape[0]
    n_sorted = n_tiles * MOE_TILE
    any_spec = pl.BlockSpec(memory_space=pl.ANY)
    return pl.pallas_call(
        functools.partial(_dispatch_kernel, tile=DISPATCH_TILE, n_tiles=n_tiles, n_rows=n_rows),
        grid_spec=pltpu.PrefetchScalarGridSpec(
            num_scalar_prefetch=5, grid=(n_rows // DISPATCH_TILE,),
            in_specs=[pl.BlockSpec((DISPATCH_TILE, ROW_W), lambda i, *_: (i, 0))],
            out_specs=[any_spec, pl.BlockSpec(memory_space=pltpu.SMEM)],
            scratch_shapes=[pltpu.VMEM((MOE_TILE, ROW_W), F32), pltpu.SemaphoreType.DMA(()),
                            pltpu.SemaphoreType.DMA(())]),
        out_shape=[jax.ShapeDtypeStruct((n_sorted, ROW_W), F32), jax.ShapeDtypeStruct((n_sorted,), jnp.int32)],
        compiler_params=_params(("arbitrary",)),
        name="dispatch",
    )(pos, offs, cnt, end, used, rows)


def _experts_kernel(ea_ref, eb_ref, blk_ref, nvalid_ref, dst_ref, xs_ref, wga_ref, wua_ref, wda_ref, wgb_ref, wub_ref,
                    wdb_ref, lng_ref, lnb_ref, out_hbm, ybuf, sems, *, n_rows):
    del ea_ref, eb_ref, blk_ref
    i = pl.program_id(0)
    n_cur = nvalid_ref[i]
    n_prev = jnp.where(i >= 1, nvalid_ref[jnp.maximum(i - 1, 0)], 0)
    n_prev2 = jnp.where(i >= 2, nvalid_ref[jnp.maximum(i - 2, 0)], 0)
    has_cur = n_cur > 0
    has_prev = n_prev > 0

    def tile_copy(slot):
        return pltpu.make_async_copy(ybuf.at[slot], out_hbm.at[pl.ds(0, MOE_TILE)], sems.at[slot])

    def compute(slot):
        x = xs_ref[:, 0:D_MODEL]
        xb = x.astype(BF16)
        y = None
        for lane, (wg, wu, wd) in enumerate(((wga_ref, wua_ref, wda_ref), (wgb_ref, wub_ref, wdb_ref))):
            gate = xs_ref[:, D_MODEL + lane:D_MODEL + lane + 1]
            hg = jnp.dot(xb, wg[0].astype(BF16), preferred_element_type=F32)
            hu = jnp.dot(xb, wu[0].astype(BF16), preferred_element_type=F32)
            hid = hg * jax.nn.sigmoid(hg) * hu * gate
            part = _bdot(hid, wd[0].astype(BF16))
            y = part if y is None else y + part
        ybuf[slot] = _layer_norm(ALPHA * x + y, lng_ref[...], lnb_ref[...])

    def start_rows(tile, slot):
        base = tile * MOE_TILE
        for r in range(MOE_TILE):
            dst = dst_ref[base + r]
            copy = pltpu.make_async_copy(ybuf.at[slot, pl.ds(r, 1)], out_hbm.at[pl.ds(dst, 1)], sems.at[slot])
            copy.start(priority=r % 2)

    @pl.when(i == 0)
    def _():
        ybuf[2] = jnp.zeros((MOE_TILE, D_MODEL), F32)
        for half in range(2):
            fill = pltpu.make_async_copy(ybuf.at[2], out_hbm.at[pl.ds(n_rows + half * MOE_TILE, MOE_TILE)], sems.at[2])
            fill.start()
            fill.wait()

    slot_cur = i % 3
    slot_prev = (i + 2) % 3

    @pl.when(has_prev & has_cur)
    def _():
        start_rows(i - 1, slot_prev)
        compute(slot_cur)

    @pl.when(has_prev & jnp.logical_not(has_cur))
    def _():
        start_rows(i - 1, slot_prev)

    @pl.when(jnp.logical_not(has_prev) & has_cur)
    def _():
        compute(slot_cur)

    @pl.when(n_prev2 > 0)
    def _():
        tile_copy((i + 1) % 3).wait()


def _experts(tile_a, tile_b, tile_blk, tile_nvalid, tok, xs, w_gate, w_up, w_down, lng, lnb, n_rows, layer):
    n_steps = tile_nvalid.shape[0]
    e0 = layer * N_EXPERTS
    up_a = pl.BlockSpec((1, D_MODEL, D_EXPERT), lambda i, ea, eb, *_: (e0 + ea[i], 0, 0))
    up_b = pl.BlockSpec((1, D_MODEL, D_EXPERT), lambda i, ea, eb, *_: (e0 + eb[i], 0, 0))
    dn_a = pl.BlockSpec((1, D_EXPERT, D_MODEL), lambda i, ea, eb, *_: (e0 + ea[i], 0, 0))
    dn_b = pl.BlockSpec((1, D_EXPERT, D_MODEL), lambda i, ea, eb, *_: (e0 + eb[i], 0, 0))
    vec = pl.BlockSpec((1, D_MODEL), lambda i, *_: (0, 0))
    return pl.pallas_call(
        functools.partial(_experts_kernel, n_rows=n_rows),
        grid_spec=pltpu.PrefetchScalarGridSpec(
            num_scalar_prefetch=5, grid=(n_steps,),
            in_specs=[pl.BlockSpec((MOE_TILE, ROW_W), lambda i, ea, eb, blk, *_: (blk[i], 0)),
                      up_a, up_a, dn_a, up_b, up_b, dn_b, vec, vec],
            out_specs=pl.BlockSpec(memory_space=pl.ANY),
            scratch_shapes=[pltpu.VMEM((3, MOE_TILE, D_MODEL), F32), pltpu.SemaphoreType.DMA((3,))]),
        out_shape=jax.ShapeDtypeStruct((n_rows + 2 * MOE_TILE, D_MODEL), F32),
        compiler_params=_params(("arbitrary",)),
        name="experts",
    )(tile_a, tile_b, tile_blk, tile_nvalid, tok, xs, w_gate, w_up, w_down, w_gate, w_up, w_down,
      _row2(lng), _row2(lnb))


def _lookup(table, idx):
    pick = idx[:, None] == jnp.arange(table.shape[0], dtype=jnp.int32)[None, :]
    return jnp.sum(jnp.where(pick, table[None, :], 0), axis=1)


def _moe(rows, meta, counts, w_gate, w_up, w_down, lng, lnb, layer):
    n_rows = rows.shape[0]
    n_tiles = n_rows // MOE_TILE + N_CLASSES
    cnt = counts[:N_CLASSES, 0]
    tiles_per = (cnt + MOE_TILE - 1) // MOE_TILE
    cls_id = jnp.arange(N_CLASSES, dtype=jnp.int32)
    tile_end = jnp.sum(jnp.where(cls_id[None, :] <= cls_id[:, None], tiles_per[None, :], 0), axis=1)
    tile_start = tile_end - tiles_per
    offs = tile_start * MOE_TILE
    end = tile_end * MOE_TILE
    used = tile_end[N_CLASSES - 1:N_CLASSES]
    tile_id = jnp.arange(n_tiles + 2, dtype=jnp.int32)
    tile_blk = jnp.minimum(tile_id, used - 1)
    tile_cls = jnp.sum((tile_blk[:, None] >= tile_end[None, :]).astype(jnp.int32), axis=1)
    tile_a = _lookup(jnp.asarray(_CLASS_A), tile_cls)
    tile_b = _lookup(jnp.asarray(_CLASS_B), tile_cls)
    tile_nvalid = jnp.clip(_lookup(cnt, tile_cls) - (tile_id - _lookup(tile_start, tile_cls)) * MOE_TILE, 0, MOE_TILE)
    pos = _lookup(offs, meta[0]) + meta[1]
    xs, dst = _dispatch(pos, offs, cnt, end, used, rows, n_tiles)
    return _experts(tile_a, tile_b, tile_blk, tile_nvalid, dst, xs, w_gate, w_up, w_down, lng, lnb, n_rows, layer)


def kernel(x_prompt, x_sample, mem_prompt, cache_mem_k, cache_mem_v, state_conv_a, state_conv_b, state_pool,
           ln_g, ln_b, ab_w_in, ab_conv_a_w, ab_conv_a_b, ab_norm_a_g, ab_norm_a_b, ab_conv_b_w, ab_w_out,
           cd_w_in, cd_pool_w, cd_pool_scale, cd_v_norm_g, cd_v_norm_b, cd_w_s, cd_b_s, cd_w_out,
           ca_wq, ca_wk, ca_wv, ca_wo, router_w, router_b, moe_w_gate, moe_w_up, moe_w_down):
    assert ln_g.shape[0] == DEPTH and ab_w_in.shape[0] == 1 and cd_w_in.shape[0] == 1
    bsz, seq, _ = x_prompt.shape
    dec_b, dec_seq, _ = x_sample.shape
    n_prompt = bsz * seq
    n_sample = dec_b * dec_seq
    sample_seqs = ROW_TILE // dec_seq

    n_pool = len(POOL_WINDOWS)
    pool_c = D_HALF // n_pool
    pool_bd = jnp.zeros((D_HALF, D_HALF), F32)
    for g in range(n_pool):
        pool_bd = pool_bd.at[g * pool_c:(g + 1) * pool_c, g * pool_c:(g + 1) * pool_c].set(cd_pool_w[0, g])
    pool_bd = pool_bd.astype(BF16)
    ab_w_in_hi, ab_w_in_lo = _split_weight(ab_w_in[0])
    ab_w_out_hi, ab_w_out_lo = _split_weight(ab_w_out[0])
    rw_hi, rw_lo = _split_weight(router_w.T)
    rw_t = jnp.concatenate([rw_hi, rw_lo], axis=0)
    cd_w_in_b, cd_w_out_b = cd_w_in[0].astype(BF16), cd_w_out[0].astype(BF16)
    wq_b, wo_b = ca_wq.astype(BF16), ca_wo.astype(BF16)
    wg_b = moe_w_gate.reshape(DEPTH * N_EXPERTS, D_MODEL, D_EXPERT)
    wu_b = moe_w_up.reshape(DEPTH * N_EXPERTS, D_MODEL, D_EXPERT)
    wd_b = moe_w_down.reshape(DEPTH * N_EXPERTS, D_EXPERT, D_MODEL)

    mem_k_prompt, k_p = _mem_projection(mem_prompt, ca_wk)
    mem_v_prompt, v_p = _mem_projection(mem_prompt, ca_wv)
    k_s = _heads_to_columns(cache_mem_k.reshape(DEPTH * dec_b, N_MEM, MEM_HEADS, MEM_HEAD_DIM))
    v_s = _heads_to_columns(cache_mem_v.reshape(DEPTH * dec_b, N_MEM, MEM_HEADS, MEM_HEAD_DIM))

    def mixer_ab(x, buf_a, buf_b, n_seq, seg, **kw):
        return _mixer_ab(x, buf_a, buf_b, ab_w_in_hi, ab_w_in_lo, ab_conv_a_w[0], ab_conv_a_b[0], ab_norm_a_g[0],
                         ab_norm_a_b[0], ab_conv_b_w[0], ab_w_out_hi, ab_w_out_lo, ln_g[0, 0], ln_b[0, 0],
                         n_seq=n_seq, seg=seg, **kw)

    def mixer_cd(x, row_start, buf_p, length, n_seq, seg, pos0, emit_v):
        return _mixer_cd(x, row_start, buf_p, length, cd_w_in_b, pool_bd, cd_pool_scale[0], cd_v_norm_g[0],
                         cd_v_norm_b[0], cd_w_s[0], cd_b_s[0], cd_w_out_b, ln_g[1, 0], ln_b[1, 0],
                         n_seq=n_seq, seg=seg, pos0=pos0, emit_v=emit_v)

    def attn_moe(layer, x_p, x_s):
        rows, meta, counts = _attn_router(x_p, x_s, k_p, v_p, k_s, v_s, wq_b, wo_b, ln_g[layer, 1], ln_b[layer, 1],
                                          rw_t, router_b, layer=layer, prompt_len=seq, sample_len=dec_seq)
        return _moe(rows, meta, counts, wg_b, wu_b, wd_b, ln_g[layer, 2], ln_b[layer, 2], layer)

    xp_flat = x_prompt.reshape(n_prompt, D_MODEL)
    tiles_p = seq // ROW_TILE
    x_p, conv_a_p, conv_b_p = mixer_ab(xp_flat, jnp.zeros((bsz, CONV_A - 1, D_HALF), F32),
                                       jnp.zeros((bsz, CONV_B - 1, D_HALF), F32), 1, ROW_TILE,
                                       precise=False, skip_tail=STATE_TILES)
    x_p, conv_a_p, conv_b_p = mixer_ab(xp_flat, conv_a_p, conv_b_p, 1, ROW_TILE, precise=True,
                                       tile_lo=tiles_p - STATE_TILES, tiles=STATE_TILES, y_prev=x_p)
    x_s, conv_a_s, conv_b_s = mixer_ab(x_sample.reshape(n_sample, D_MODEL), state_conv_a[0], state_conv_b[0],
                                       sample_seqs, dec_seq, precise=False)
    x_all = attn_moe(0, x_p, x_s)

    x_p, pool_p = mixer_cd(x_all, 0, jnp.zeros((bsz, POOL_BUF, D_HALF), F32), seq, 1, ROW_TILE, 0, False)
    x_s, pool_s, v_s_rows = mixer_cd(x_all, n_prompt, state_pool[0], dec_seq, sample_seqs, dec_seq, PAST_LEN, True)
    x_all = attn_moe(1, x_p, x_s)

    y_prompt = x_all[0:n_prompt].reshape(bsz, seq, D_MODEL)
    y_sample = x_all[n_prompt:n_prompt + n_sample].reshape(dec_b, dec_seq, D_MODEL)
    return (y_prompt, y_sample, mem_k_prompt, mem_v_prompt, conv_a_p[None], conv_b_p[None], pool_p[None],
            conv_a_s[None], conv_b_s[None], pool_s[None], v_s_rows[None])
```

```python
import functools

import jax
import jax.numpy as jnp
import numpy as np
from jax import lax
from jax.experimental import pallas as pl
from jax.experimental.pallas import tpu as pltpu

F32 = jnp.float32
BF16 = jnp.bfloat16

D_MODEL = 1024
D_HALF = D_MODEL // 2
DEPTH = 2
PAST_LEN = 4096
CHUNK = 64
CHUNK_SHIFT = CHUNK.bit_length() - 1
CONV_A = 31
CONV_B = 3
POOL_WINDOWS = (2, 4, 8, 16)
POOL_BUF = max(POOL_WINDOWS) - 1
N_HEAD_D = 4
GMLP_CHUNK = 128
N_MEM = 256
MEM_HEADS = 4
MEM_HEAD_DIM = D_MODEL // MEM_HEADS
N_EXPERTS = 16
N_EXPERT_GROUPS = 4
GROUP_SIZE = N_EXPERTS // N_EXPERT_GROUPS
GROUP_SHIFT = GROUP_SIZE.bit_length() - 1
PAIRS_PER_GROUP = GROUP_SIZE * (GROUP_SIZE - 1) // 2
N_CLASSES = N_EXPERT_GROUPS * PAIRS_PER_GROUP
D_EXPERT = D_MODEL // 2
ALPHA = (2 * DEPTH) ** 0.25
LN_EPS = 1e-5

LANES = 128
SUBLANES = 8
ROW_W = D_MODEL + LANES
HIST_A = 32
HIST_B = 8
HIST_P = 16
CONV_ROWS = 64
ROW_TILE = 512
MOE_TILE = 256
MOE_SHIFT = MOE_TILE.bit_length() - 1
CLASS_ROWS = 32
DISPATCH_STEPS = 8
SPLIT_ROWS = 256
CACHE_SEQS = 4
STATE_TILES = 1
VMEM_LIMIT = 56 * 1024 * 1024

_PAIR_AB = [(a, b) for a in range(GROUP_SIZE) for b in range(a + 1, GROUP_SIZE)]
_CLASS_A = np.array([g * GROUP_SIZE + a for g in range(N_EXPERT_GROUPS) for a, _ in _PAIR_AB], np.int32)
_CLASS_B = np.array([g * GROUP_SIZE + b for g in range(N_EXPERT_GROUPS) for _, b in _PAIR_AB], np.int32)


def _layer_norm(x, g, b):
    mu = jnp.mean(x, axis=-1, keepdims=True)
    xc = x - mu
    var = jnp.mean(xc * xc, axis=-1, keepdims=True)
    return xc * lax.rsqrt(var + LN_EPS) * g + b


def _bdot(a, w):
    return jnp.dot(a.astype(BF16), w, preferred_element_type=F32)


def _split(a):
    hi = a.astype(BF16)
    return hi, (a - hi.astype(F32)).astype(BF16)


def _dot3(a, w_hi, w_lo):
    a_hi, a_lo = _split(a)
    return (jnp.dot(a_hi, w_hi, preferred_element_type=F32) + jnp.dot(a_lo, w_hi, preferred_element_type=F32)
            + jnp.dot(a_hi, w_lo, preferred_element_type=F32))


def _params(sem):
    return pltpu.CompilerParams(dimension_semantics=sem, vmem_limit_bytes=VMEM_LIMIT)


def _full(shape):
    return pl.BlockSpec(shape, lambda *_: (0,) * len(shape), pipeline_mode=pl.Buffered(1))


def _row2(v):
    return v.reshape(1, -1)


def _split_weight_kernel(w_ref, hi_ref, lo_ref):
    hi, lo = _split(w_ref[...])
    hi_ref[...] = hi
    lo_ref[...] = lo


def _split_weight(w):
    rows, cols = w.shape
    blk = min(rows, SPLIT_ROWS)
    spec = pl.BlockSpec((blk, cols), lambda i: (i, 0))
    return pl.pallas_call(
        _split_weight_kernel,
        grid=(rows // blk,),
        in_specs=[spec],
        out_specs=[spec, spec],
        out_shape=[jax.ShapeDtypeStruct(w.shape, BF16)] * 2,
        compiler_params=_params(("arbitrary",)),
        name="split_weight",
    )(w)


def _proj_kernel(x_ref, w_ref, o_ref, o16_ref, *, batch):
    res = _bdot(x_ref[...], w_ref[0].astype(BF16))
    o16_ref[...] = res.astype(BF16).reshape(batch, N_MEM, D_MODEL)
    for hd in range(MEM_HEADS):
        cols = slice(hd * MEM_HEAD_DIM, (hd + 1) * MEM_HEAD_DIM)
        o_ref[0, :, :, hd, :] = res[:, cols].reshape(batch, N_MEM, MEM_HEAD_DIM)


def _mem_projection(mem, w):
    batch, n = mem.shape[0], w.shape[0]
    return pl.pallas_call(
        functools.partial(_proj_kernel, batch=batch),
        grid=(n,),
        in_specs=[pl.BlockSpec((batch * N_MEM, D_MODEL), lambda j: (0, 0)),
                  pl.BlockSpec((1, D_MODEL, D_MODEL), lambda j: (j, 0, 0))],
        out_specs=[pl.BlockSpec((1, batch, N_MEM, MEM_HEADS, MEM_HEAD_DIM), lambda j: (j, 0, 0, 0, 0)),
                   pl.BlockSpec((batch, N_MEM, D_MODEL), lambda j: (j, 0, 0))],
        out_shape=[jax.ShapeDtypeStruct((n, batch, N_MEM, MEM_HEADS, MEM_HEAD_DIM), F32),
                   jax.ShapeDtypeStruct((n * batch, N_MEM, D_MODEL), BF16)],
        compiler_params=_params(("arbitrary",)),
        name="mem_projection",
    )(mem.reshape(batch * N_MEM, D_MODEL), w)


def _heads_to_columns_kernel(c_hbm, o_ref, buf, sem):
    step = pl.program_id(0)

    def head_copies(at_step, slot):
        return [pltpu.make_async_copy(c_hbm.at[at_step * CACHE_SEQS + s, :, hd, :], buf.at[slot, s, hd], sem.at[slot])
                for s in range(CACHE_SEQS) for hd in range(MEM_HEADS)]

    @pl.when(step == 0)
    def _():
        for copy in head_copies(0, 0):
            copy.start()

    @pl.when(step + 1 < pl.num_programs(0))
    def _():
        for copy in head_copies(step + 1, (step + 1) % 2):
            copy.start()

    slot = step % 2
    for copy in head_copies(step, slot):
        copy.wait()
    for s in range(CACHE_SEQS):
        for hd in range(MEM_HEADS):
            o_ref[s, :, hd * MEM_HEAD_DIM:(hd + 1) * MEM_HEAD_DIM] = buf[slot, s, hd].astype(BF16)


def _heads_to_columns(cache):
    n = cache.shape[0]
    return pl.pallas_call(
        _heads_to_columns_kernel,
        grid=(n // CACHE_SEQS,),
        in_specs=[pl.BlockSpec(memory_space=pl.ANY)],
        out_specs=pl.BlockSpec((CACHE_SEQS, N_MEM, D_MODEL), lambda i: (i, 0, 0)),
        out_shape=jax.ShapeDtypeStruct((n, N_MEM, D_MODEL), BF16),
        scratch_shapes=[pltpu.VMEM((2, CACHE_SEQS, MEM_HEADS, N_MEM, MEM_HEAD_DIM), F32),
                        pltpu.SemaphoreType.DMA((2,))],
        compiler_params=_params(("arbitrary",)),
        name="heads_to_columns",
    )(cache)


def _load_history(ext_ref, buf_ref, first, hist, keep, seg):
    @pl.when(first)
    def _():
        ext_ref[:, hist - keep:hist, :] = buf_ref[...]

    @pl.when(jnp.logical_not(first))
    def _():
        ext_ref[:, hist - keep:hist, :] = ext_ref[:, seg + hist - keep:seg + hist, :]


def _depthwise_conv(ext_ref, w_ref, out_ref, *, n_seq, seg, taps, hist, shifted_ref=None):
    rc = min(CONV_ROWS, seg)
    off0 = hist - (taps - 1)
    length = hist + seg
    if shifted_ref is not None:
        for r in range(1, SUBLANES):
            shifted_ref[r - 1, :, 0:length - SUBLANES, :] = ext_ref[:, r:r + length - SUBLANES, :]
    for s in range(n_seq):
        for r0 in range(0, seg, rc):
            for lb in range(0, D_HALF, LANES):
                acc = None
                for k in range(taps):
                    lo = off0 + k + r0
                    shift = (off0 + k) % SUBLANES
                    if shifted_ref is None or shift == 0:
                        win = ext_ref[s, lo:lo + rc, lb:lb + LANES]
                    else:
                        win = shifted_ref[shift - 1, s, lo - shift:lo - shift + rc, lb:lb + LANES]
                    term = w_ref[k:k + 1, lb:lb + LANES] * win
                    acc = term if acc is None else acc + term
                out_ref[s * seg + r0:s * seg + r0 + rc, lb:lb + LANES] = acc


def _mixer_ab_kernel(x_ref, bufa_ref, bufb_ref, w_in_ref, w_in_lo_ref, caw_ref, cab_ref, nag_ref, nab_ref, cbw_ref,
                     w_out_ref, w_out_lo_ref, lng_ref, lnb_ref, *refs, n_seq, seg, precise, skip_tail, has_prev):
    if has_prev:
        refs = refs[1:]
    y_ref, nbufa_ref, nbufb_ref, a_ext, a_shift, cb_ext, conv_a, conv_b = refs

    def body():
        first = pl.program_id(1) == 0
        _load_history(a_ext, bufa_ref, first, HIST_A, CONV_A - 1, seg)
        _load_history(cb_ext, bufb_ref, first, HIST_B, CONV_B - 1, seg)
        x = x_ref[...]
        h = _dot3(x, w_in_ref[...], w_in_lo_ref[...]) if precise else _bdot(x, w_in_ref[...])
        a = h[:, 0:D_HALF] * jax.nn.sigmoid(h[:, D_HALF:2 * D_HALF])
        cb = h[:, 3 * D_HALF:4 * D_HALF] * h[:, 4 * D_HALF:5 * D_HALF]
        a_ext[:, HIST_A:HIST_A + seg, :] = a.reshape(n_seq, seg, D_HALF)
        cb_ext[:, HIST_B:HIST_B + seg, :] = cb.reshape(n_seq, seg, D_HALF)
        nbufa_ref[...] = a_ext[:, seg + HIST_A - (CONV_A - 1):seg + HIST_A, :]
        nbufb_ref[...] = cb_ext[:, seg + HIST_B - (CONV_B - 1):seg + HIST_B, :]

        _depthwise_conv(a_ext, caw_ref, conv_a, n_seq=n_seq, seg=seg, taps=CONV_A, hist=HIST_A, shifted_ref=a_shift)
        _depthwise_conv(cb_ext, cbw_ref, conv_b, n_seq=n_seq, seg=seg, taps=CONV_B, hist=HIST_B)

        a2 = _layer_norm(conv_a[...] + cab_ref[...], nag_ref[...], nab_ref[...])
        a2 = a2 * jax.nn.sigmoid(a2)
        b2 = h[:, 2 * D_HALF:3 * D_HALF] * conv_b[...]
        if precise:
            y = (_dot3(a2, w_out_ref[0:D_HALF, :], w_out_lo_ref[0:D_HALF, :])
                 + _dot3(b2, w_out_ref[D_HALF:D_MODEL, :], w_out_lo_ref[D_HALF:D_MODEL, :]))
        else:
            y = _bdot(a2, w_out_ref[0:D_HALF, :]) + _bdot(b2, w_out_ref[D_HALF:D_MODEL, :])
        y_ref[...] = _layer_norm(ALPHA * x + y, lng_ref[...], lnb_ref[...])

    if skip_tail == 0:
        body()
    else:
        live = pl.program_id(1) < pl.num_programs(1) - skip_tail
        pl.when(live)(body)

        @pl.when(jnp.logical_not(live))
        def _():
            y_ref[...] = jnp.zeros_like(y_ref)


def _mixer_ab(x, buf_a, buf_b, w_in, w_in_lo, caw, cab, nag, nab, cbw, w_out, w_out_lo, lng, lnb, *,
              n_seq, seg, precise, tile_lo=0, tiles=None, skip_tail=0, y_prev=None):
    batch = buf_a.shape[0]
    length = x.shape[0] // batch
    d_in = w_in.shape[1]
    n_l = length // seg
    tiles = n_l if tiles is None else tiles
    rows = n_seq * seg
    lo_in = _full((D_MODEL, d_in)) if precise else _full((SUBLANES, LANES))
    lo_out = _full((D_MODEL, D_MODEL)) if precise else _full((SUBLANES, LANES))
    row_blk = pl.BlockSpec((rows, D_MODEL), lambda b, l: (b * n_l + tile_lo + l, 0))
    in_specs = [row_blk,
                pl.BlockSpec((n_seq, CONV_A - 1, D_HALF), lambda b, l: (b, 0, 0)),
                pl.BlockSpec((n_seq, CONV_B - 1, D_HALF), lambda b, l: (b, 0, 0)),
                _full((D_MODEL, d_in)), lo_in, _full((CONV_A, D_HALF)), _full((1, D_HALF)), _full((1, D_HALF)),
                _full((1, D_HALF)), _full((CONV_B, D_HALF)), _full((D_MODEL, D_MODEL)), lo_out,
                _full((1, D_MODEL)), _full((1, D_MODEL))]
    args = [x, buf_a, buf_b, w_in, w_in_lo, caw, _row2(cab), _row2(nag), _row2(nab), cbw, w_out, w_out_lo,
            _row2(lng), _row2(lnb)]
    aliases = {}
    if y_prev is not None:
        aliases = {len(args): 0}
        in_specs.append(pl.BlockSpec(memory_space=pl.ANY))
        args.append(y_prev)
    return pl.pallas_call(
        functools.partial(_mixer_ab_kernel, n_seq=n_seq, seg=seg, precise=precise, skip_tail=skip_tail,
                          has_prev=y_prev is not None),
        grid=(batch // n_seq, tiles),
        in_specs=in_specs,
        out_specs=[row_blk,
                   pl.BlockSpec((n_seq, CONV_A - 1, D_HALF), lambda b, l: (b, 0, 0)),
                   pl.BlockSpec((n_seq, CONV_B - 1, D_HALF), lambda b, l: (b, 0, 0))],
        out_shape=[jax.ShapeDtypeStruct((batch * length, D_MODEL), F32),
                   jax.ShapeDtypeStruct((batch, CONV_A - 1, D_HALF), F32),
                   jax.ShapeDtypeStruct((batch, CONV_B - 1, D_HALF), F32)],
        scratch_shapes=[pltpu.VMEM((n_seq, HIST_A + seg, D_HALF), F32),
                        pltpu.VMEM((SUBLANES - 1, n_seq, HIST_A + seg - SUBLANES, D_HALF), F32),
                        pltpu.VMEM((n_seq, HIST_B + seg, D_HALF), F32),
                        pltpu.VMEM((rows, D_HALF), F32),
                        pltpu.VMEM((rows, D_HALF), F32)],
        input_output_aliases=aliases,
        compiler_params=_params(("arbitrary", "arbitrary")),
        name="mixer_ab",
    )(*args)


def _mixer_cd_kernel(x_ref, bufp_ref, w_in_ref, pw_ref, ps_ref, vg_ref, vb_ref, ws_ref, bs_ref, w_out_ref,
                     lng_ref, lnb_ref, *refs, n_seq, seg, pos0, n_mix, emit_v):
    if emit_v:
        y_ref, nbufp_ref, v_ref, c_ext, pooled, mixed = refs
    else:
        y_ref, nbufp_ref, c_ext, pooled, mixed = refs
    rows = n_seq * seg
    li = pl.program_id(1)
    _load_history(c_ext, bufp_ref, li == 0, HIST_P, POOL_BUF, seg)
    x = x_ref[...]
    h = _bdot(x, w_in_ref[...])
    c_in = h[:, 0:D_HALF]
    c_ext[:, HIST_P:HIST_P + seg, :] = c_in.reshape(n_seq, seg, D_HALF)
    nbufp_ref[...] = c_ext[:, seg + HIST_P - POOL_BUF:seg + HIST_P, :]

    rc = min(CONV_ROWS, seg)
    for g, win in enumerate(POOL_WINDOWS):
        lanes = slice(g * LANES, (g + 1) * LANES)
        for s in range(n_seq):
            for r0 in range(0, seg, rc):
                cur = c_ext[s, HIST_P + r0:HIST_P + r0 + rc, lanes]
                acc = cur
                for j in range(1, win):
                    acc = acc + c_ext[s, HIST_P + r0 - j:HIST_P + r0 - j + rc, lanes]
                pos = pos0 + li * seg + r0 + lax.broadcasted_iota(jnp.int32, (rc, LANES), 0)
                cnt = jnp.minimum(pos + 1, win).astype(F32)
                pooled[s * seg + r0:s * seg + r0 + rc, lanes] = acc / cnt - cur
    c = _bdot(pooled[...], pw_ref[...]) * ps_ref[...]

    z = jax.nn.gelu(h[:, D_HALF:3 * D_HALF], approximate=True)
    u = z[:, 0:D_HALF]
    v = _layer_norm(z[:, D_HALF:2 * D_HALF], vg_ref[...], vb_ref[...])
    if emit_v:
        v_ref[...] = v.reshape(n_seq, seg, D_HALF)
    vb16 = v.astype(BF16)

    ri = lax.broadcasted_iota(jnp.int32, (n_mix, n_mix), 0) >> CHUNK_SHIFT
    ci = lax.broadcasted_iota(jnp.int32, (n_mix, n_mix), 1) >> CHUNK_SHIFT
    for g in range(N_HEAD_D):
        lanes = slice(g * LANES, (g + 1) * LANES)
        ws = jnp.where(ci <= ri, ws_ref[g], 0.0).astype(BF16)
        bias = bs_ref[:, g:g + 1]
        for r0 in range(0, rows, n_mix):
            mixed[r0:r0 + n_mix, lanes] = jnp.dot(ws, vb16[r0:r0 + n_mix, lanes], preferred_element_type=F32) + bias
    d = u * mixed[...]
    y = _bdot(c, w_out_ref[0:D_HALF, :]) + _bdot(d, w_out_ref[D_HALF:D_MODEL, :])
    y_ref[...] = _layer_norm(ALPHA * x + y, lng_ref[...], lnb_ref[...])


def _mixer_cd(x, row_start, buf_p, length, w_in, pw, ps, vg, vb, ws, bs, w_out, lng, lnb, *,
              n_seq, seg, pos0, emit_v):
    batch = buf_p.shape[0]
    d_in = w_in.shape[1]
    n_mix = min(length, GMLP_CHUNK)
    assert seg % n_mix == 0
    n_l = length // seg
    rows = n_seq * seg
    blk0 = row_start // rows
    ws_n = ws[:, :n_mix, :n_mix]
    bs_t = bs[:, :n_mix].T
    out_specs = [pl.BlockSpec((rows, D_MODEL), lambda b, l: (b * n_l + l, 0)),
                 pl.BlockSpec((n_seq, POOL_BUF, D_HALF), lambda b, l: (b, 0, 0))]
    out_shape = [jax.ShapeDtypeStruct((batch * length, D_MODEL), F32),
                 jax.ShapeDtypeStruct((batch, POOL_BUF, D_HALF), F32)]
    if emit_v:
        out_specs.append(pl.BlockSpec((n_seq, seg, D_HALF), lambda b, l: (b, l, 0)))
        out_shape.append(jax.ShapeDtypeStruct((batch, length, D_HALF), F32))
    return pl.pallas_call(
        functools.partial(_mixer_cd_kernel, n_seq=n_seq, seg=seg, pos0=pos0, n_mix=n_mix, emit_v=emit_v),
        grid=(batch // n_seq, n_l),
        in_specs=[pl.BlockSpec((rows, D_MODEL), lambda b, l: (blk0 + b * n_l + l, 0)),
                  pl.BlockSpec((n_seq, POOL_BUF, D_HALF), lambda b, l: (b, 0, 0)),
                  _full((D_MODEL, d_in)), _full((D_HALF, D_HALF)), _full((1, D_HALF)), _full((1, D_HALF)),
                  _full((1, D_HALF)), _full((N_HEAD_D, n_mix, n_mix)), _full((n_mix, N_HEAD_D)),
                  _full((D_MODEL, D_MODEL)), _full((1, D_MODEL)), _full((1, D_MODEL))],
        out_specs=out_specs,
        out_shape=out_shape,
        scratch_shapes=[pltpu.VMEM((n_seq, HIST_P + seg, D_HALF), F32),
                        pltpu.VMEM((rows, D_HALF), F32),
                        pltpu.VMEM((rows, D_HALF), F32)],
        compiler_params=_params(("arbitrary", "arbitrary")),
        name="mixer_cd",
    )(x, buf_p, w_in, pw, _row2(ps), _row2(vg), _row2(vb), ws_n, bs_t, w_out, _row2(lng), _row2(lnb))


def _attn_router_kernel(xp_ref, xs_ref, kp_ref, vp_ref, ks_ref, vs_ref, wq_ref, wo_ref, lng_ref, lnb_ref, rw_ref, rb_ref,
                        rows_ref, meta_ref, counts_ref,
                        q_scr, o_scr, carry, tri, *, n_prompt_tiles, sample_seg):
    rows = ROW_TILE
    step = pl.program_id(0)
    is_prompt = step < n_prompt_tiles

    @pl.when(step == 0)
    def _():
        carry[...] = jnp.zeros_like(carry)
        upper = (lax.broadcasted_iota(jnp.int32, (rows, rows), 0) <= lax.broadcasted_iota(jnp.int32, (rows, rows), 1))
        tri[...] = upper.astype(F32).astype(BF16)

    def run(x_ref, k_ref, v_ref, n_seq, seg):
        x = x_ref[...]
        q_scr[...] = (_bdot(x, wq_ref[0]) * (MEM_HEAD_DIM ** -0.5)).astype(BF16)
        for s in range(n_seq):
            for hd in range(MEM_HEADS):
                cols = slice(hd * MEM_HEAD_DIM, (hd + 1) * MEM_HEAD_DIM)
                sc = lax.dot_general(q_scr[s * seg:(s + 1) * seg, cols], k_ref[s, :, cols], (((1,), (1,)), ((), ())),
                                     preferred_element_type=F32)
                p = jnp.exp(sc - jnp.max(sc, axis=-1, keepdims=True))
                den = jnp.sum(p, axis=-1, keepdims=True)
                o_scr[s * seg:(s + 1) * seg, cols] = _bdot(p, v_ref[s, :, cols]) / den
        x2 = _layer_norm(ALPHA * x + _bdot(o_scr[...], wo_ref[0]), lng_ref[...], lnb_ref[...])
        rows_ref[:, 0:D_MODEL] = x2

        nt = (((1,), (1,)), ((), ()))
        x2_hi, x2_lo = _split(x2)
        both = lax.dot_general(rw_ref[...], x2_hi, nt, preferred_element_type=F32)
        cross = lax.dot_general(rw_ref[0:N_EXPERTS, :], x2_lo, nt, preferred_element_type=F32)
        logits = both[0:N_EXPERTS] + both[N_EXPERTS:2 * N_EXPERTS] + cross + rb_ref[...]
        e = jnp.exp(logits - jnp.max(logits, axis=0, keepdims=True))
        scores = e / jnp.sum(e, axis=0, keepdims=True)
        eid = lax.broadcasted_iota(jnp.int32, (N_EXPERTS, rows), 0)
        egrp = eid >> GROUP_SHIFT
        best = jnp.max(jnp.where(egrp == 0, scores, -1.0), axis=0, keepdims=True)
        g_sel = jnp.zeros((1, rows), jnp.int32)
        for g in range(1, N_EXPERT_GROUPS):
            gs = jnp.max(jnp.where(egrp == g, scores, -1.0), axis=0, keepdims=True)
            upd = gs > best
            g_sel = jnp.where(upd, g, g_sel)
            best = jnp.where(upd, gs, best)
        masked = jnp.where(egrp == g_sel, scores, -1.0)
        m1 = jnp.max(masked, axis=0, keepdims=True)
        i1 = jnp.min(jnp.where(masked == m1, eid, N_EXPERTS), axis=0, keepdims=True)
        masked2 = jnp.where(eid == i1, -2.0, masked)
        m2 = jnp.max(masked2, axis=0, keepdims=True)
        i2 = jnp.min(jnp.where(masked2 == m2, eid, N_EXPERTS), axis=0, keepdims=True)
        tot = m1 + m2
        g1 = m1 / tot
        g2 = m2 / tot
        first_low = i1 < i2
        ea = jnp.where(first_low, i1, i2) & (GROUP_SIZE - 1)
        eb = jnp.where(first_low, i2, i1) & (GROUP_SIZE - 1)
        gate_a = jnp.where(first_low, g1, g2)
        gate_b = jnp.where(first_low, g2, g1)
        pair = eb - 1 + jnp.where(ea == 1, 2, 0) + jnp.where(ea == 2, 3, 0)
        cls = g_sel * PAIRS_PER_GROUP + pair

        lane_row = lax.broadcasted_iota(jnp.int32, (LANES, rows), 0)
        gates_t = jnp.where(lane_row == 0, gate_a, jnp.where(lane_row == 1, gate_b, 0.0))
        rows_ref[:, D_MODEL:ROW_W] = gates_t.T

        onehot = (lax.broadcasted_iota(jnp.int32, (CLASS_ROWS, rows), 0) == cls).astype(F32)
        cum = jnp.dot(onehot.astype(BF16), tri[...], preferred_element_type=F32)
        before = carry[:, 0:1]
        rank = jnp.sum(onehot * (before + cum), axis=0, keepdims=True) - 1.0
        sub = lax.broadcasted_iota(jnp.int32, (SUBLANES, rows), 0)
        meta_ref[...] = jnp.where(sub == 0, cls, jnp.where(sub == 1, rank.astype(jnp.int32), 0))
        carry[...] = carry[...] + cum[:, rows - 1:rows]
        counts_ref[...] = carry[...].astype(jnp.int32)

    pl.when(is_prompt)(lambda: run(xp_ref, kp_ref, vp_ref, 1, rows))
    pl.when(jnp.logical_not(is_prompt))(lambda: run(xs_ref, ks_ref, vs_ref, rows // sample_seg, sample_seg))


def _attn_router(x_p, x_s, k_p, v_p, k_s, v_s, wq, wo, lng, lnb, rw_t, rb, *, layer, prompt_len, sample_len):
    rows = ROW_TILE
    n_p = x_p.shape[0] // rows
    n_s = x_s.shape[0] // rows
    tiles_per_seq = prompt_len // rows
    seq_per_tile = rows // sample_len
    total = x_p.shape[0] + x_s.shape[0]
    n_prompt_seq = n_p // tiles_per_seq
    p0 = layer * n_prompt_seq
    s0 = layer * n_s
    p_idx = lambda i: jnp.minimum(i, n_p - 1)
    s_idx = lambda i: jnp.maximum(i - n_p, 0)
    layer_w = pl.BlockSpec((1, D_MODEL, D_MODEL), lambda i: (layer, 0, 0), pipeline_mode=pl.Buffered(1))
    return pl.pallas_call(
        functools.partial(_attn_router_kernel, n_prompt_tiles=n_p, sample_seg=sample_len),
        grid=(n_p + n_s,),
        in_specs=[pl.BlockSpec((rows, D_MODEL), lambda i: (p_idx(i), 0)),
                  pl.BlockSpec((rows, D_MODEL), lambda i: (s_idx(i), 0)),
                  pl.BlockSpec((1, N_MEM, D_MODEL), lambda i: (p0 + p_idx(i) // tiles_per_seq, 0, 0)),
                  pl.BlockSpec((1, N_MEM, D_MODEL), lambda i: (p0 + p_idx(i) // tiles_per_seq, 0, 0)),
                  pl.BlockSpec((seq_per_tile, N_MEM, D_MODEL), lambda i: (s0 + s_idx(i), 0, 0)),
                  pl.BlockSpec((seq_per_tile, N_MEM, D_MODEL), lambda i: (s0 + s_idx(i), 0, 0)),
                  layer_w, layer_w, _full((1, D_MODEL)), _full((1, D_MODEL)),
                  _full((2 * N_EXPERTS, D_MODEL)), _full((N_EXPERTS, 1))],
        out_specs=[pl.BlockSpec((rows, ROW_W), lambda i: (i, 0)),
                   pl.BlockSpec((SUBLANES, rows), lambda i: (0, i)),
                   pl.BlockSpec((CLASS_ROWS, LANES), lambda i: (0, 0))],
        out_shape=[jax.ShapeDtypeStruct((total, ROW_W), F32),
                   jax.ShapeDtypeStruct((SUBLANES, total), jnp.int32),
                   jax.ShapeDtypeStruct((CLASS_ROWS, LANES), jnp.int32)],
        scratch_shapes=[pltpu.VMEM((rows, D_MODEL), BF16),
                        pltpu.VMEM((rows, D_MODEL), F32),
                        pltpu.VMEM((CLASS_ROWS, LANES), F32),
                        pltpu.VMEM((rows, rows), BF16)],
        compiler_params=_params(("arbitrary",)),
        name="attn_router",
    )(x_p, x_s, k_p, v_p, k_s, v_s, wq, wo, _row2(lng), _row2(lnb), rw_t, rb.reshape(N_EXPERTS, 1))


def _dispatch_kernel(pos_ref, offs_ref, cnt_ref, end_ref, used_ref, rows_ref, xs_hbm, dst_ref,
                     zeros, sem, zsem, *, tile, n_tiles, n_rows):
    step = pl.program_id(0)

    def class_padding(c, go):
        start = offs_ref[c] + cnt_ref[c]
        head = (-start) & (SUBLANES - 1)
        bulk_start = pl.multiple_of(start + head, SUBLANES)
        bulk = pl.multiple_of(end_ref[c] - bulk_start, SUBLANES)

        def one(r, carry_):
            go(pltpu.make_async_copy(zeros.at[pl.ds(0, 1)], xs_hbm.at[pl.ds(r, 1)], zsem))
            return carry_
        lax.fori_loop(start, start + head, one, 0)

        @pl.when(bulk > 0)
        def _():
            go(pltpu.make_async_copy(zeros.at[pl.ds(0, bulk)], xs_hbm.at[pl.ds(bulk_start, bulk)], zsem))

    def unused_tiles(go):
        def one(j, carry_):
            go(pltpu.make_async_copy(zeros, xs_hbm.at[pl.ds(pl.multiple_of(j * MOE_TILE, MOE_TILE), MOE_TILE)], zsem))
            return carry_
        lax.fori_loop(used_ref[0], n_tiles, one, 0)

    @pl.when(step == 0)
    def _():
        zeros[...] = jnp.zeros_like(zeros)

        def no_token(p, carry_):
            dst_ref[p] = n_rows + ((p >> MOE_SHIFT) & 1) * MOE_TILE + (p & (MOE_TILE - 1))
            return carry_
        for c in range(N_CLASSES):
            lax.fori_loop(offs_ref[c] + cnt_ref[c], end_ref[c], no_token, 0)
            class_padding(c, lambda copy: copy.start())
        lax.fori_loop(used_ref[0] * MOE_TILE, n_tiles * MOE_TILE, no_token, 0)
        unused_tiles(lambda copy: copy.start())

    for r in range(tile):
        t = step * tile + r
        p = pos_ref[t]
        dst_ref[p] = t
        pltpu.make_async_copy(rows_ref.at[pl.ds(r, 1)], xs_hbm.at[pl.ds(p, 1)], sem).start(priority=r % 2)
    pltpu.make_async_copy(rows_ref, xs_hbm.at[pl.ds(0, tile)], sem).wait()

    @pl.when(step == 0)
    def _():
        for c in range(N_CLASSES):
            class_padding(c, lambda copy: copy.wait())
        unused_tiles(lambda copy: copy.wait())


def _dispatch(pos, offs, cnt, end, used, rows, n_tiles):
    n_rows = rows.shape[0]
    n_sorted = n_tiles * MOE_TILE
    tile = n_rows // DISPATCH_STEPS
    assert tile * DISPATCH_STEPS == n_rows and tile % SUBLANES == 0
    any_spec = pl.BlockSpec(memory_space=pl.ANY)
    return pl.pallas_call(
        functools.partial(_dispatch_kernel, tile=tile, n_tiles=n_tiles, n_rows=n_rows),
        grid_spec=pltpu.PrefetchScalarGridSpec(
            num_scalar_prefetch=5, grid=(DISPATCH_STEPS,),
            in_specs=[pl.BlockSpec((tile, ROW_W), lambda i, *_: (i, 0))],
            out_specs=[any_spec, pl.BlockSpec(memory_space=pltpu.SMEM)],
            scratch_shapes=[pltpu.VMEM((MOE_TILE, ROW_W), F32), pltpu.SemaphoreType.DMA(()),
                            pltpu.SemaphoreType.DMA(())]),
        out_shape=[jax.ShapeDtypeStruct((n_sorted, ROW_W), F32), jax.ShapeDtypeStruct((n_sorted,), jnp.int32)],
        compiler_params=_params(("arbitrary",)),
        name="dispatch",
    )(pos, offs, cnt, end, used, rows)


def _experts_kernel(ea_ref, eb_ref, blk_ref, nvalid_ref, dst_ref, xs_ref, wga_ref, wua_ref, wda_ref, wgb_ref, wub_ref,
                    wdb_ref, lng_ref, lnb_ref, out_hbm, ybuf, sems, *, n_rows):
    del ea_ref, eb_ref, blk_ref
    i = pl.program_id(0)
    n_cur = nvalid_ref[i]
    n_prev = jnp.where(i >= 1, nvalid_ref[jnp.maximum(i - 1, 0)], 0)
    n_prev2 = jnp.where(i >= 2, nvalid_ref[jnp.maximum(i - 2, 0)], 0)
    has_cur = n_cur > 0
    has_prev = n_prev > 0

    def tile_copy(slot):
        return pltpu.make_async_copy(ybuf.at[slot], out_hbm.at[pl.ds(0, MOE_TILE)], sems.at[slot])

    def compute(slot):
        x = xs_ref[:, 0:D_MODEL]
        xb = x.astype(BF16)
        y = None
        for lane, (wg, wu, wd) in enumerate(((wga_ref, wua_ref, wda_ref), (wgb_ref, wub_ref, wdb_ref))):
            gate = xs_ref[:, D_MODEL + lane:D_MODEL + lane + 1]
            hg = jnp.dot(xb, wg[0].astype(BF16), preferred_element_type=F32)
            hu = jnp.dot(xb, wu[0].astype(BF16), preferred_element_type=F32)
            hid = hg * jax.nn.sigmoid(hg) * hu * gate
            part = _bdot(hid, wd[0].astype(BF16))
            y = part if y is None else y + part
        ybuf[slot] = _layer_norm(ALPHA * x + y, lng_ref[...], lnb_ref[...])

    def start_rows(tile, slot):
        base = tile * MOE_TILE
        for r in range(MOE_TILE):
            dst = dst_ref[base + r]
            copy = pltpu.make_async_copy(ybuf.at[slot, pl.ds(r, 1)], out_hbm.at[pl.ds(dst, 1)], sems.at[slot])
            copy.start(priority=r % 2)

    @pl.when(i == 0)
    def _():
        ybuf[2] = jnp.zeros((MOE_TILE, D_MODEL), F32)
        for half in range(2):
            fill = pltpu.make_async_copy(ybuf.at[2], out_hbm.at[pl.ds(n_rows + half * MOE_TILE, MOE_TILE)], sems.at[2])
            fill.start()
            fill.wait()

    slot_cur = i % 3
    slot_prev = (i + 2) % 3

    @pl.when(has_prev & has_cur)
    def _():
        start_rows(i - 1, slot_prev)
        compute(slot_cur)

    @pl.when(has_prev & jnp.logical_not(has_cur))
    def _():
        start_rows(i - 1, slot_prev)

    @pl.when(jnp.logical_not(has_prev) & has_cur)
    def _():
        compute(slot_cur)

    @pl.when(n_prev2 > 0)
    def _():
        tile_copy((i + 1) % 3).wait()


def _experts(tile_a, tile_b, tile_blk, tile_nvalid, tok, xs, w_gate, w_up, w_down, lng, lnb, n_rows, layer):
    n_steps = tile_nvalid.shape[0]
    e0 = layer * N_EXPERTS
    up_a = pl.BlockSpec((1, D_MODEL, D_EXPERT), lambda i, ea, eb, *_: (e0 + ea[i], 0, 0))
    up_b = pl.BlockSpec((1, D_MODEL, D_EXPERT), lambda i, ea, eb, *_: (e0 + eb[i], 0, 0))
    dn_a = pl.BlockSpec((1, D_EXPERT, D_MODEL), lambda i, ea, eb, *_: (e0 + ea[i], 0, 0))
    dn_b = pl.BlockSpec((1, D_EXPERT, D_MODEL), lambda i, ea, eb, *_: (e0 + eb[i], 0, 0))
    vec = pl.BlockSpec((1, D_MODEL), lambda i, *_: (0, 0))
    return pl.pallas_call(
        functools.partial(_experts_kernel, n_rows=n_rows),
        grid_spec=pltpu.PrefetchScalarGridSpec(
            num_scalar_prefetch=5, grid=(n_steps,),
            in_specs=[pl.BlockSpec((MOE_TILE, ROW_W), lambda i, ea, eb, blk, *_: (blk[i], 0)),
                      up_a, up_a, dn_a, up_b, up_b, dn_b, vec, vec],
            out_specs=pl.BlockSpec(memory_space=pl.ANY),
            scratch_shapes=[pltpu.VMEM((3, MOE_TILE, D_MODEL), F32), pltpu.SemaphoreType.DMA((3,))]),
        out_shape=jax.ShapeDtypeStruct((n_rows + 2 * MOE_TILE, D_MODEL), F32),
        compiler_params=_params(("arbitrary",)),
        name="experts",
    )(tile_a, tile_b, tile_blk, tile_nvalid, tok, xs, w_gate, w_up, w_down, w_gate, w_up, w_down,
      _row2(lng), _row2(lnb))


def _lookup(table, idx):
    pick = idx[:, None] == jnp.arange(table.shape[0], dtype=jnp.int32)[None, :]
    return jnp.sum(jnp.where(pick, table[None, :], 0), axis=1)


def _moe(rows, meta, counts, w_gate, w_up, w_down, lng, lnb, layer):
    n_rows = rows.shape[0]
    n_tiles = n_rows // MOE_TILE + N_CLASSES
    cnt = counts[:N_CLASSES, 0]
    tiles_per = (cnt + MOE_TILE - 1) // MOE_TILE
    cls_id = jnp.arange(N_CLASSES, dtype=jnp.int32)
    tile_end = jnp.sum(jnp.where(cls_id[None, :] <= cls_id[:, None], tiles_per[None, :], 0), axis=1)
    tile_start = tile_end - tiles_per
    offs = tile_start * MOE_TILE
    end = tile_end * MOE_TILE
    used = tile_end[N_CLASSES - 1:N_CLASSES]
    tile_id = jnp.arange(n_tiles + 2, dtype=jnp.int32)
    tile_blk = jnp.minimum(tile_id, used - 1)
    tile_cls = jnp.sum((tile_blk[:, None] >= tile_end[None, :]).astype(jnp.int32), axis=1)
    tile_a = _lookup(jnp.asarray(_CLASS_A), tile_cls)
    tile_b = _lookup(jnp.asarray(_CLASS_B), tile_cls)
    tile_nvalid = jnp.clip(_lookup(cnt, tile_cls) - (tile_id - _lookup(tile_start, tile_cls)) * MOE_TILE, 0, MOE_TILE)
    pos = _lookup(offs, meta[0]) + meta[1]
    xs, dst = _dispatch(pos, offs, cnt, end, used, rows, n_tiles)
    return _experts(tile_a, tile_b, tile_blk, tile_nvalid, dst, xs, w_gate, w_up, w_down, lng, lnb, n_rows, layer)


def kernel(x_prompt, x_sample, mem_prompt, cache_mem_k, cache_mem_v, state_conv_a, state_conv_b, state_pool,
           ln_g, ln_b, ab_w_in, ab_conv_a_w, ab_conv_a_b, ab_norm_a_g, ab_norm_a_b, ab_conv_b_w, ab_w_out,
           cd_w_in, cd_pool_w, cd_pool_scale, cd_v_norm_g, cd_v_norm_b, cd_w_s, cd_b_s, cd_w_out,
           ca_wq, ca_wk, ca_wv, ca_wo, router_w, router_b, moe_w_gate, moe_w_up, moe_w_down):
    assert ln_g.shape[0] == DEPTH and ab_w_in.shape[0] == 1 and cd_w_in.shape[0] == 1
    bsz, seq, _ = x_prompt.shape
    dec_b, dec_seq, _ = x_sample.shape
    n_prompt = bsz * seq
    n_sample = dec_b * dec_seq
    sample_seqs = ROW_TILE // dec_seq

    n_pool = len(POOL_WINDOWS)
    pool_c = D_HALF // n_pool
    pool_bd = jnp.zeros((D_HALF, D_HALF), F32)
    for g in range(n_pool):
        pool_bd = pool_bd.at[g * pool_c:(g + 1) * pool_c, g * pool_c:(g + 1) * pool_c].set(cd_pool_w[0, g])
    pool_bd = pool_bd.astype(BF16)
    ab_w_in_hi, ab_w_in_lo = _split_weight(ab_w_in[0])
    ab_w_out_hi, ab_w_out_lo = _split_weight(ab_w_out[0])
    rw_hi, rw_lo = _split_weight(router_w.T)
    rw_t = jnp.concatenate([rw_hi, rw_lo], axis=0)
    cd_w_in_b, cd_w_out_b = cd_w_in[0].astype(BF16), cd_w_out[0].astype(BF16)
    wq_b, wo_b = ca_wq.astype(BF16), ca_wo.astype(BF16)
    wg_b = moe_w_gate.reshape(DEPTH * N_EXPERTS, D_MODEL, D_EXPERT)
    wu_b = moe_w_up.reshape(DEPTH * N_EXPERTS, D_MODEL, D_EXPERT)
    wd_b = moe_w_down.reshape(DEPTH * N_EXPERTS, D_EXPERT, D_MODEL)

    mem_k_prompt, k_p = _mem_projection(mem_prompt, ca_wk)
    mem_v_prompt, v_p = _mem_projection(mem_prompt, ca_wv)
    k_s = _heads_to_columns(cache_mem_k.reshape(DEPTH * dec_b, N_MEM, MEM_HEADS, MEM_HEAD_DIM))
    v_s = _heads_to_columns(cache_mem_v.reshape(DEPTH * dec_b, N_MEM, MEM_HEADS, MEM_HEAD_DIM))

    def mixer_ab(x, buf_a, buf_b, n_seq, seg, **kw):
        return _mixer_ab(x, buf_a, buf_b, ab_w_in_hi, ab_w_in_lo, ab_conv_a_w[0], ab_conv_a_b[0], ab_norm_a_g[0],
                         ab_norm_a_b[0], ab_conv_b_w[0], ab_w_out_hi, ab_w_out_lo, ln_g[0, 0], ln_b[0, 0],
                         n_seq=n_seq, seg=seg, **kw)

    def mixer_cd(x, row_start, buf_p, length, n_seq, seg, pos0, emit_v):
        return _mixer_cd(x, row_start, buf_p, length, cd_w_in_b, pool_bd, cd_pool_scale[0], cd_v_norm_g[0],
                         cd_v_norm_b[0], cd_w_s[0], cd_b_s[0], cd_w_out_b, ln_g[1, 0], ln_b[1, 0],
                         n_seq=n_seq, seg=seg, pos0=pos0, emit_v=emit_v)

    def attn_moe(layer, x_p, x_s):
        rows, meta, counts = _attn_router(x_p, x_s, k_p, v_p, k_s, v_s, wq_b, wo_b, ln_g[layer, 1], ln_b[layer, 1],
                                          rw_t, router_b, layer=layer, prompt_len=seq, sample_len=dec_seq)
        return _moe(rows, meta, counts, wg_b, wu_b, wd_b, ln_g[layer, 2], ln_b[layer, 2], layer)

    xp_flat = x_prompt.reshape(n_prompt, D_MODEL)
    tiles_p = seq // ROW_TILE
    x_p, conv_a_p, conv_b_p = mixer_ab(xp_flat, jnp.zeros((bsz, CONV_A - 1, D_HALF), F32),
                                       jnp.zeros((bsz, CONV_B - 1, D_HALF), F32), 1, ROW_TILE,
                                       precise=False, skip_tail=STATE_TILES)
    x_p, conv_a_p, conv_b_p = mixer_ab(xp_flat, conv_a_p, conv_b_p, 1, ROW_TILE, precise=True,
                                       tile_lo=tiles_p - STATE_TILES, tiles=STATE_TILES, y_prev=x_p)
    x_s, conv_a_s, conv_b_s = mixer_ab(x_sample.reshape(n_sample, D_MODEL), state_conv_a[0], state_conv_b[0],
                                       sample_seqs, dec_seq, precise=False)
    x_all = attn_moe(0, x_p, x_s)

    x_p, pool_p = mixer_cd(x_all, 0, jnp.zeros((bsz, POOL_BUF, D_HALF), F32), seq, 1, ROW_TILE, 0, False)
    x_s, pool_s, v_s_rows = mixer_cd(x_all, n_prompt, state_pool[0], dec_seq, sample_seqs, dec_seq, PAST_LEN, True)
    x_all = attn_moe(1, x_p, x_s)

    y_prompt = x_all[0:n_prompt].reshape(bsz, seq, D_MODEL)
    y_sample = x_all[n_prompt:n_prompt + n_sample].reshape(dec_b, dec_seq, D_MODEL)
    return (y_prompt, y_sample, mem_k_prompt, mem_v_prompt, conv_a_p[None], conv_b_p[None], pool_p[None],
            conv_a_s[None], conv_b_s[None], pool_s[None], v_s_rows[None])
```

```python
import functools

import jax
import jax.numpy as jnp
import numpy as np
from jax import lax
from jax.experimental import pallas as pl
from jax.experimental.pallas import tpu as pltpu

F32 = jnp.float32
BF16 = jnp.bfloat16

D_MODEL = 1024
D_HALF = D_MODEL // 2
DEPTH = 2
PAST_LEN = 4096
CHUNK = 64
CHUNK_SHIFT = CHUNK.bit_length() - 1
CONV_A = 31
CONV_B = 3
POOL_WINDOWS = (2, 4, 8, 16)
POOL_BUF = max(POOL_WINDOWS) - 1
N_HEAD_D = 4
GMLP_CHUNK = 128
N_MEM = 256
MEM_HEADS = 4
MEM_HEAD_DIM = D_MODEL // MEM_HEADS
N_EXPERTS = 16
N_EXPERT_GROUPS = 4
GROUP_SIZE = N_EXPERTS // N_EXPERT_GROUPS
GROUP_SHIFT = GROUP_SIZE.bit_length() - 1
PAIRS_PER_GROUP = GROUP_SIZE * (GROUP_SIZE - 1) // 2
N_CLASSES = N_EXPERT_GROUPS * PAIRS_PER_GROUP
D_EXPERT = D_MODEL // 2
ALPHA = (2 * DEPTH) ** 0.25
LN_EPS = 1e-5

LANES = 128
SUBLANES = 8
ROW_W = D_MODEL + LANES
HIST_A = 32
HIST_B = 8
HIST_P = 16
CONV_ROWS = 64
ROW_TILE = 512
MOE_TILE = 256
MOE_SHIFT = MOE_TILE.bit_length() - 1
CLASS_ROWS = 32
DISPATCH_STEPS = 8
SPLIT_ROWS = 256
CACHE_SEQS = 4
STATE_TILES = 1
V7X_VMEM_BYTES = 64 * 1024 * 1024
VMEM_LIMIT = V7X_VMEM_BYTES * 7 // 8

_PAIR_AB = [(a, b) for a in range(GROUP_SIZE) for b in range(a + 1, GROUP_SIZE)]
_CLASS_A = np.array([g * GROUP_SIZE + a for g in range(N_EXPERT_GROUPS) for a, _ in _PAIR_AB], np.int32)
_CLASS_B = np.array([g * GROUP_SIZE + b for g in range(N_EXPERT_GROUPS) for _, b in _PAIR_AB], np.int32)


def _layer_norm(x, g, b):
    mu = jnp.mean(x, axis=-1, keepdims=True)
    xc = x - mu
    var = jnp.mean(xc * xc, axis=-1, keepdims=True)
    return xc * lax.rsqrt(var + LN_EPS) * g + b


def _bdot(a, w):
    return jnp.dot(a.astype(BF16), w, preferred_element_type=F32)


def _split(a):
    hi = a.astype(BF16)
    return hi, (a - hi.astype(F32)).astype(BF16)


def _dot3(a, w_hi, w_lo):
    a_hi, a_lo = _split(a)
    return (jnp.dot(a_hi, w_hi, preferred_element_type=F32) + jnp.dot(a_lo, w_hi, preferred_element_type=F32)
            + jnp.dot(a_hi, w_lo, preferred_element_type=F32))


def _params(sem):
    return pltpu.CompilerParams(dimension_semantics=sem, vmem_limit_bytes=VMEM_LIMIT)


def _full(shape):
    return pl.BlockSpec(shape, lambda *_: (0,) * len(shape), pipeline_mode=pl.Buffered(1))


def _row2(v):
    return v.reshape(1, -1)


def _split_weight_kernel(w_ref, hi_ref, lo_ref):
    hi, lo = _split(w_ref[...])
    hi_ref[...] = hi
    lo_ref[...] = lo


def _split_weight(w):
    rows, cols = w.shape
    blk = min(rows, SPLIT_ROWS)
    spec = pl.BlockSpec((blk, cols), lambda i: (i, 0))
    return pl.pallas_call(
        _split_weight_kernel,
        grid=(rows // blk,),
        in_specs=[spec],
        out_specs=[spec, spec],
        out_shape=[jax.ShapeDtypeStruct(w.shape, BF16)] * 2,
        compiler_params=_params(("arbitrary",)),
        name="split_weight",
    )(w)


def _proj_kernel(x_ref, w_ref, o_ref, o16_ref, *, batch):
    res = _bdot(x_ref[...], w_ref[0].astype(BF16))
    o16_ref[...] = res.astype(BF16).reshape(batch, N_MEM, D_MODEL)
    for hd in range(MEM_HEADS):
        cols = slice(hd * MEM_HEAD_DIM, (hd + 1) * MEM_HEAD_DIM)
        o_ref[0, :, :, hd, :] = res[:, cols].reshape(batch, N_MEM, MEM_HEAD_DIM)


def _mem_projection(mem, w):
    batch, n = mem.shape[0], w.shape[0]
    return pl.pallas_call(
        functools.partial(_proj_kernel, batch=batch),
        grid=(n,),
        in_specs=[pl.BlockSpec((batch * N_MEM, D_MODEL), lambda j: (0, 0)),
                  pl.BlockSpec((1, D_MODEL, D_MODEL), lambda j: (j, 0, 0))],
        out_specs=[pl.BlockSpec((1, batch, N_MEM, MEM_HEADS, MEM_HEAD_DIM), lambda j: (j, 0, 0, 0, 0)),
                   pl.BlockSpec((batch, N_MEM, D_MODEL), lambda j: (j, 0, 0))],
        out_shape=[jax.ShapeDtypeStruct((n, batch, N_MEM, MEM_HEADS, MEM_HEAD_DIM), F32),
                   jax.ShapeDtypeStruct((n * batch, N_MEM, D_MODEL), BF16)],
        compiler_params=_params(("arbitrary",)),
        name="mem_projection",
    )(mem.reshape(batch * N_MEM, D_MODEL), w)


def _heads_to_columns_kernel(c_hbm, o_ref, buf, sem):
    step = pl.program_id(0)

    def head_copies(at_step, slot):
        return [pltpu.make_async_copy(c_hbm.at[at_step * CACHE_SEQS + s, :, hd, :], buf.at[slot, s, hd], sem.at[slot])
                for s in range(CACHE_SEQS) for hd in range(MEM_HEADS)]

    @pl.when(step == 0)
    def _():
        for copy in head_copies(0, 0):
            copy.start()

    @pl.when(step + 1 < pl.num_programs(0))
    def _():
        for copy in head_copies(step + 1, (step + 1) % 2):
            copy.start()

    slot = step % 2
    for copy in head_copies(step, slot):
        copy.wait()
    for s in range(CACHE_SEQS):
        for hd in range(MEM_HEADS):
            o_ref[s, :, hd * MEM_HEAD_DIM:(hd + 1) * MEM_HEAD_DIM] = buf[slot, s, hd].astype(BF16)


def _heads_to_columns(cache):
    n = cache.shape[0]
    return pl.pallas_call(
        _heads_to_columns_kernel,
        grid=(n // CACHE_SEQS,),
        in_specs=[pl.BlockSpec(memory_space=pl.ANY)],
        out_specs=pl.BlockSpec((CACHE_SEQS, N_MEM, D_MODEL), lambda i: (i, 0, 0)),
        out_shape=jax.ShapeDtypeStruct((n, N_MEM, D_MODEL), BF16),
        scratch_shapes=[pltpu.VMEM((2, CACHE_SEQS, MEM_HEADS, N_MEM, MEM_HEAD_DIM), F32),
                        pltpu.SemaphoreType.DMA((2,))],
        compiler_params=_params(("arbitrary",)),
        name="heads_to_columns",
    )(cache)


def _load_history(ext_ref, buf_ref, first, hist, keep, seg):
    @pl.when(first)
    def _():
        ext_ref[:, hist - keep:hist, :] = buf_ref[...]

    @pl.when(jnp.logical_not(first))
    def _():
        ext_ref[:, hist - keep:hist, :] = ext_ref[:, seg + hist - keep:seg + hist, :]


def _depthwise_conv(ext_ref, w_ref, out_ref, *, n_seq, seg, taps, hist, shifted_ref=None):
    rc = min(CONV_ROWS, seg)
    off0 = hist - (taps - 1)
    length = hist + seg
    if shifted_ref is not None:
        for r in range(1, SUBLANES):
            shifted_ref[r - 1, :, 0:length - SUBLANES, :] = ext_ref[:, r:r + length - SUBLANES, :]
    for s in range(n_seq):
        for r0 in range(0, seg, rc):
            for lb in range(0, D_HALF, LANES):
                acc = None
                for k in range(taps):
                    lo = off0 + k + r0
                    shift = (off0 + k) % SUBLANES
                    if shifted_ref is None or shift == 0:
                        win = ext_ref[s, lo:lo + rc, lb:lb + LANES]
                    else:
                        win = shifted_ref[shift - 1, s, lo - shift:lo - shift + rc, lb:lb + LANES]
                    term = w_ref[k:k + 1, lb:lb + LANES] * win
                    acc = term if acc is None else acc + term
                out_ref[s * seg + r0:s * seg + r0 + rc, lb:lb + LANES] = acc


def _mixer_ab_kernel(x_ref, bufa_ref, bufb_ref, w_in_ref, w_in_lo_ref, caw_ref, cab_ref, nag_ref, nab_ref, cbw_ref,
                     w_out_ref, w_out_lo_ref, lng_ref, lnb_ref, *refs, n_seq, seg, precise, skip_tail, has_prev):
    if has_prev:
        refs = refs[1:]
    y_ref, nbufa_ref, nbufb_ref, a_ext, a_shift, cb_ext, conv_a, conv_b = refs

    def body():
        first = pl.program_id(1) == 0
        _load_history(a_ext, bufa_ref, first, HIST_A, CONV_A - 1, seg)
        _load_history(cb_ext, bufb_ref, first, HIST_B, CONV_B - 1, seg)
        x = x_ref[...]
        h = _dot3(x, w_in_ref[...], w_in_lo_ref[...]) if precise else _bdot(x, w_in_ref[...])
        a = h[:, 0:D_HALF] * jax.nn.sigmoid(h[:, D_HALF:2 * D_HALF])
        cb = h[:, 3 * D_HALF:4 * D_HALF] * h[:, 4 * D_HALF:5 * D_HALF]
        a_ext[:, HIST_A:HIST_A + seg, :] = a.reshape(n_seq, seg, D_HALF)
        cb_ext[:, HIST_B:HIST_B + seg, :] = cb.reshape(n_seq, seg, D_HALF)
        nbufa_ref[...] = a_ext[:, seg + HIST_A - (CONV_A - 1):seg + HIST_A, :]
        nbufb_ref[...] = cb_ext[:, seg + HIST_B - (CONV_B - 1):seg + HIST_B, :]

        _depthwise_conv(a_ext, caw_ref, conv_a, n_seq=n_seq, seg=seg, taps=CONV_A, hist=HIST_A, shifted_ref=a_shift)
        _depthwise_conv(cb_ext, cbw_ref, conv_b, n_seq=n_seq, seg=seg, taps=CONV_B, hist=HIST_B)

        a2 = _layer_norm(conv_a[...] + cab_ref[...], nag_ref[...], nab_ref[...])
        a2 = a2 * jax.nn.sigmoid(a2)
        b2 = h[:, 2 * D_HALF:3 * D_HALF] * conv_b[...]
        if precise:
            y = (_dot3(a2, w_out_ref[0:D_HALF, :], w_out_lo_ref[0:D_HALF, :])
                 + _dot3(b2, w_out_ref[D_HALF:D_MODEL, :], w_out_lo_ref[D_HALF:D_MODEL, :]))
        else:
            y = _bdot(a2, w_out_ref[0:D_HALF, :]) + _bdot(b2, w_out_ref[D_HALF:D_MODEL, :])
        y_ref[...] = _layer_norm(ALPHA * x + y, lng_ref[...], lnb_ref[...])

    if skip_tail == 0:
        body()
    else:
        live = pl.program_id(1) < pl.num_programs(1) - skip_tail
        pl.when(live)(body)

        @pl.when(jnp.logical_not(live))
        def _():
            y_ref[...] = jnp.zeros_like(y_ref)


def _mixer_ab(x, buf_a, buf_b, w_in, w_in_lo, caw, cab, nag, nab, cbw, w_out, w_out_lo, lng, lnb, *,
              n_seq, seg, precise, tile_lo=0, tiles=None, skip_tail=0, y_prev=None):
    batch = buf_a.shape[0]
    length = x.shape[0] // batch
    d_in = w_in.shape[1]
    n_l = length // seg
    tiles = n_l if tiles is None else tiles
    rows = n_seq * seg
    lo_in = _full((D_MODEL, d_in)) if precise else _full((SUBLANES, LANES))
    lo_out = _full((D_MODEL, D_MODEL)) if precise else _full((SUBLANES, LANES))
    row_blk = pl.BlockSpec((rows, D_MODEL), lambda b, l: (b * n_l + tile_lo + l, 0))
    in_specs = [row_blk,
                pl.BlockSpec((n_seq, CONV_A - 1, D_HALF), lambda b, l: (b, 0, 0)),
                pl.BlockSpec((n_seq, CONV_B - 1, D_HALF), lambda b, l: (b, 0, 0)),
                _full((D_MODEL, d_in)), lo_in, _full((CONV_A, D_HALF)), _full((1, D_HALF)), _full((1, D_HALF)),
                _full((1, D_HALF)), _full((CONV_B, D_HALF)), _full((D_MODEL, D_MODEL)), lo_out,
                _full((1, D_MODEL)), _full((1, D_MODEL))]
    args = [x, buf_a, buf_b, w_in, w_in_lo, caw, _row2(cab), _row2(nag), _row2(nab), cbw, w_out, w_out_lo,
            _row2(lng), _row2(lnb)]
    aliases = {}
    if y_prev is not None:
        aliases = {len(args): 0}
        in_specs.append(pl.BlockSpec(memory_space=pl.ANY))
        args.append(y_prev)
    return pl.pallas_call(
        functools.partial(_mixer_ab_kernel, n_seq=n_seq, seg=seg, precise=precise, skip_tail=skip_tail,
                          has_prev=y_prev is not None),
        grid=(batch // n_seq, tiles),
        in_specs=in_specs,
        out_specs=[row_blk,
                   pl.BlockSpec((n_seq, CONV_A - 1, D_HALF), lambda b, l: (b, 0, 0)),
                   pl.BlockSpec((n_seq, CONV_B - 1, D_HALF), lambda b, l: (b, 0, 0))],
        out_shape=[jax.ShapeDtypeStruct((batch * length, D_MODEL), F32),
                   jax.ShapeDtypeStruct((batch, CONV_A - 1, D_HALF), F32),
                   jax.ShapeDtypeStruct((batch, CONV_B - 1, D_HALF), F32)],
        scratch_shapes=[pltpu.VMEM((n_seq, HIST_A + seg, D_HALF), F32),
                        pltpu.VMEM((SUBLANES - 1, n_seq, HIST_A + seg - SUBLANES, D_HALF), F32),
                        pltpu.VMEM((n_seq, HIST_B + seg, D_HALF), F32),
                        pltpu.VMEM((rows, D_HALF), F32),
                        pltpu.VMEM((rows, D_HALF), F32)],
        input_output_aliases=aliases,
        compiler_params=_params(("arbitrary", "arbitrary")),
        name="mixer_ab",
    )(*args)


def _mixer_cd_kernel(x_ref, bufp_ref, w_in_ref, pw_ref, ps_ref, vg_ref, vb_ref, ws_ref, bs_ref, w_out_ref,
                     lng_ref, lnb_ref, *refs, n_seq, seg, pos0, n_mix, emit_v):
    if emit_v:
        y_ref, nbufp_ref, v_ref, c_ext, pooled, mixed = refs
    else:
        y_ref, nbufp_ref, c_ext, pooled, mixed = refs
    rows = n_seq * seg
    li = pl.program_id(1)
    _load_history(c_ext, bufp_ref, li == 0, HIST_P, POOL_BUF, seg)
    x = x_ref[...]
    h = _bdot(x, w_in_ref[...])
    c_in = h[:, 0:D_HALF]
    c_ext[:, HIST_P:HIST_P + seg, :] = c_in.reshape(n_seq, seg, D_HALF)
    nbufp_ref[...] = c_ext[:, seg + HIST_P - POOL_BUF:seg + HIST_P, :]

    rc = min(CONV_ROWS, seg)
    for g, win in enumerate(POOL_WINDOWS):
        lanes = slice(g * LANES, (g + 1) * LANES)
        for s in range(n_seq):
            for r0 in range(0, seg, rc):
                cur = c_ext[s, HIST_P + r0:HIST_P + r0 + rc, lanes]
                acc = cur
                for j in range(1, win):
                    acc = acc + c_ext[s, HIST_P + r0 - j:HIST_P + r0 - j + rc, lanes]
                pos = pos0 + li * seg + r0 + lax.broadcasted_iota(jnp.int32, (rc, LANES), 0)
                cnt = jnp.minimum(pos + 1, win).astype(F32)
                pooled[s * seg + r0:s * seg + r0 + rc, lanes] = acc / cnt - cur
    c = _bdot(pooled[...], pw_ref[...]) * ps_ref[...]

    z = jax.nn.gelu(h[:, D_HALF:3 * D_HALF], approximate=True)
    u = z[:, 0:D_HALF]
    v = _layer_norm(z[:, D_HALF:2 * D_HALF], vg_ref[...], vb_ref[...])
    if emit_v:
        v_ref[...] = v.reshape(n_seq, seg, D_HALF)
    vb16 = v.astype(BF16)

    ri = lax.broadcasted_iota(jnp.int32, (n_mix, n_mix), 0) >> CHUNK_SHIFT
    ci = lax.broadcasted_iota(jnp.int32, (n_mix, n_mix), 1) >> CHUNK_SHIFT
    for g in range(N_HEAD_D):
        lanes = slice(g * LANES, (g + 1) * LANES)
        ws = jnp.where(ci <= ri, ws_ref[g], 0.0).astype(BF16)
        bias = bs_ref[:, g:g + 1]
        for r0 in range(0, rows, n_mix):
            mixed[r0:r0 + n_mix, lanes] = jnp.dot(ws, vb16[r0:r0 + n_mix, lanes], preferred_element_type=F32) + bias
    d = u * mixed[...]
    y = _bdot(c, w_out_ref[0:D_HALF, :]) + _bdot(d, w_out_ref[D_HALF:D_MODEL, :])
    y_ref[...] = _layer_norm(ALPHA * x + y, lng_ref[...], lnb_ref[...])


def _mixer_cd(x, row_start, buf_p, length, w_in, pw, ps, vg, vb, ws, bs, w_out, lng, lnb, *,
              n_seq, seg, pos0, emit_v):
    batch = buf_p.shape[0]
    d_in = w_in.shape[1]
    n_mix = min(length, GMLP_CHUNK)
    assert seg % n_mix == 0
    n_l = length // seg
    rows = n_seq * seg
    blk0 = row_start // rows
    ws_n = ws[:, :n_mix, :n_mix]
    bs_t = bs[:, :n_mix].T
    out_specs = [pl.BlockSpec((rows, D_MODEL), lambda b, l: (b * n_l + l, 0)),
                 pl.BlockSpec((n_seq, POOL_BUF, D_HALF), lambda b, l: (b, 0, 0))]
    out_shape = [jax.ShapeDtypeStruct((batch * length, D_MODEL), F32),
                 jax.ShapeDtypeStruct((batch, POOL_BUF, D_HALF), F32)]
    if emit_v:
        out_specs.append(pl.BlockSpec((n_seq, seg, D_HALF), lambda b, l: (b, l, 0)))
        out_shape.append(jax.ShapeDtypeStruct((batch, length, D_HALF), F32))
    return pl.pallas_call(
        functools.partial(_mixer_cd_kernel, n_seq=n_seq, seg=seg, pos0=pos0, n_mix=n_mix, emit_v=emit_v),
        grid=(batch // n_seq, n_l),
        in_specs=[pl.BlockSpec((rows, D_MODEL), lambda b, l: (blk0 + b * n_l + l, 0)),
                  pl.BlockSpec((n_seq, POOL_BUF, D_HALF), lambda b, l: (b, 0, 0)),
                  _full((D_MODEL, d_in)), _full((D_HALF, D_HALF)), _full((1, D_HALF)), _full((1, D_HALF)),
                  _full((1, D_HALF)), _full((N_HEAD_D, n_mix, n_mix)), _full((n_mix, N_HEAD_D)),
                  _full((D_MODEL, D_MODEL)), _full((1, D_MODEL)), _full((1, D_MODEL))],
        out_specs=out_specs,
        out_shape=out_shape,
        scratch_shapes=[pltpu.VMEM((n_seq, HIST_P + seg, D_HALF), F32),
                        pltpu.VMEM((rows, D_HALF), F32),
                        pltpu.VMEM((rows, D_HALF), F32)],
        compiler_params=_params(("arbitrary", "arbitrary")),
        name="mixer_cd",
    )(x, buf_p, w_in, pw, _row2(ps), _row2(vg), _row2(vb), ws_n, bs_t, w_out, _row2(lng), _row2(lnb))


def _attn_router_kernel(xp_ref, xs_ref, kp_ref, vp_ref, ks_ref, vs_ref, wq_ref, wo_ref, lng_ref, lnb_ref, rw_ref, rb_ref,
                        rows_ref, meta_ref, counts_ref,
                        q_scr, o_scr, carry, tri, *, n_prompt_tiles, sample_seg):
    rows = ROW_TILE
    step = pl.program_id(0)
    is_prompt = step < n_prompt_tiles

    @pl.when(step == 0)
    def _():
        carry[...] = jnp.zeros_like(carry)
        upper = (lax.broadcasted_iota(jnp.int32, (rows, rows), 0) <= lax.broadcasted_iota(jnp.int32, (rows, rows), 1))
        tri[...] = upper.astype(F32).astype(BF16)

    def run(x_ref, k_ref, v_ref, n_seq, seg):
        x = x_ref[...]
        q_scr[...] = (_bdot(x, wq_ref[0]) * (MEM_HEAD_DIM ** -0.5)).astype(BF16)
        for s in range(n_seq):
            for hd in range(MEM_HEADS):
                cols = slice(hd * MEM_HEAD_DIM, (hd + 1) * MEM_HEAD_DIM)
                sc = lax.dot_general(q_scr[s * seg:(s + 1) * seg, cols], k_ref[s, :, cols], (((1,), (1,)), ((), ())),
                                     preferred_element_type=F32)
                p = jnp.exp(sc - jnp.max(sc, axis=-1, keepdims=True))
                den = jnp.sum(p, axis=-1, keepdims=True)
                o_scr[s * seg:(s + 1) * seg, cols] = _bdot(p, v_ref[s, :, cols]) / den
        x2 = _layer_norm(ALPHA * x + _bdot(o_scr[...], wo_ref[0]), lng_ref[...], lnb_ref[...])
        rows_ref[:, 0:D_MODEL] = x2

        nt = (((1,), (1,)), ((), ()))
        x2_hi, x2_lo = _split(x2)
        both = lax.dot_general(rw_ref[...], x2_hi, nt, preferred_element_type=F32)
        cross = lax.dot_general(rw_ref[0:N_EXPERTS, :], x2_lo, nt, preferred_element_type=F32)
        logits = both[0:N_EXPERTS] + both[N_EXPERTS:2 * N_EXPERTS] + cross + rb_ref[...]
        e = jnp.exp(logits - jnp.max(logits, axis=0, keepdims=True))
        scores = e / jnp.sum(e, axis=0, keepdims=True)
        eid = lax.broadcasted_iota(jnp.int32, (N_EXPERTS, rows), 0)
        egrp = eid >> GROUP_SHIFT
        best = jnp.max(jnp.where(egrp == 0, scores, -1.0), axis=0, keepdims=True)
        g_sel = jnp.zeros((1, rows), jnp.int32)
        for g in range(1, N_EXPERT_GROUPS):
            gs = jnp.max(jnp.where(egrp == g, scores, -1.0), axis=0, keepdims=True)
            upd = gs > best
            g_sel = jnp.where(upd, g, g_sel)
            best = jnp.where(upd, gs, best)
        masked = jnp.where(egrp == g_sel, scores, -1.0)
        m1 = jnp.max(masked, axis=0, keepdims=True)
        i1 = jnp.min(jnp.where(masked == m1, eid, N_EXPERTS), axis=0, keepdims=True)
        masked2 = jnp.where(eid == i1, -2.0, masked)
        m2 = jnp.max(masked2, axis=0, keepdims=True)
        i2 = jnp.min(jnp.where(masked2 == m2, eid, N_EXPERTS), axis=0, keepdims=True)
        tot = m1 + m2
        g1 = m1 / tot
        g2 = m2 / tot
        first_low = i1 < i2
        ea = jnp.where(first_low, i1, i2) & (GROUP_SIZE - 1)
        eb = jnp.where(first_low, i2, i1) & (GROUP_SIZE - 1)
        gate_a = jnp.where(first_low, g1, g2)
        gate_b = jnp.where(first_low, g2, g1)
        pair = eb - 1 + jnp.where(ea == 1, 2, 0) + jnp.where(ea == 2, 3, 0)
        cls = g_sel * PAIRS_PER_GROUP + pair

        lane_row = lax.broadcasted_iota(jnp.int32, (LANES, rows), 0)
        gates_t = jnp.where(lane_row == 0, gate_a, jnp.where(lane_row == 1, gate_b, 0.0))
        rows_ref[:, D_MODEL:ROW_W] = gates_t.T

        onehot = (lax.broadcasted_iota(jnp.int32, (CLASS_ROWS, rows), 0) == cls).astype(F32)
        cum = jnp.dot(onehot.astype(BF16), tri[...], preferred_element_type=F32)
        before = carry[:, 0:1]
        rank = jnp.sum(onehot * (before + cum), axis=0, keepdims=True) - 1.0
        sub = lax.broadcasted_iota(jnp.int32, (SUBLANES, rows), 0)
        meta_ref[...] = jnp.where(sub == 0, cls, jnp.where(sub == 1, rank.astype(jnp.int32), 0))
        carry[...] = carry[...] + cum[:, rows - 1:rows]
        counts_ref[...] = carry[...].astype(jnp.int32)

    pl.when(is_prompt)(lambda: run(xp_ref, kp_ref, vp_ref, 1, rows))
    pl.when(jnp.logical_not(is_prompt))(lambda: run(xs_ref, ks_ref, vs_ref, rows // sample_seg, sample_seg))


def _attn_router(x_p, x_s, k_p, v_p, k_s, v_s, wq, wo, lng, lnb, rw_t, rb, *, layer, prompt_len, sample_len):
    rows = ROW_TILE
    n_p = x_p.shape[0] // rows
    n_s = x_s.shape[0] // rows
    tiles_per_seq = prompt_len // rows
    seq_per_tile = rows // sample_len
    total = x_p.shape[0] + x_s.shape[0]
    n_prompt_seq = n_p // tiles_per_seq
    p0 = layer * n_prompt_seq
    s0 = layer * n_s
    p_idx = lambda i: jnp.minimum(i, n_p - 1)
    s_idx = lambda i: jnp.maximum(i - n_p, 0)
    layer_w = pl.BlockSpec((1, D_MODEL, D_MODEL), lambda i: (layer, 0, 0), pipeline_mode=pl.Buffered(1))
    return pl.pallas_call(
        functools.partial(_attn_router_kernel, n_prompt_tiles=n_p, sample_seg=sample_len),
        grid=(n_p + n_s,),
        in_specs=[pl.BlockSpec((rows, D_MODEL), lambda i: (p_idx(i), 0)),
                  pl.BlockSpec((rows, D_MODEL), lambda i: (s_idx(i), 0)),
                  pl.BlockSpec((1, N_MEM, D_MODEL), lambda i: (p0 + p_idx(i) // tiles_per_seq, 0, 0)),
                  pl.BlockSpec((1, N_MEM, D_MODEL), lambda i: (p0 + p_idx(i) // tiles_per_seq, 0, 0)),
                  pl.BlockSpec((seq_per_tile, N_MEM, D_MODEL), lambda i: (s0 + s_idx(i), 0, 0)),
                  pl.BlockSpec((seq_per_tile, N_MEM, D_MODEL), lambda i: (s0 + s_idx(i), 0, 0)),
                  layer_w, layer_w, _full((1, D_MODEL)), _full((1, D_MODEL)),
                  _full((2 * N_EXPERTS, D_MODEL)), _full((N_EXPERTS, 1))],
        out_specs=[pl.BlockSpec((rows, ROW_W), lambda i: (i, 0)),
                   pl.BlockSpec((SUBLANES, rows), lambda i: (0, i)),
                   pl.BlockSpec((CLASS_ROWS, LANES), lambda i: (0, 0))],
        out_shape=[jax.ShapeDtypeStruct((total, ROW_W), F32),
                   jax.ShapeDtypeStruct((SUBLANES, total), jnp.int32),
                   jax.ShapeDtypeStruct((CLASS_ROWS, LANES), jnp.int32)],
        scratch_shapes=[pltpu.VMEM((rows, D_MODEL), BF16),
                        pltpu.VMEM((rows, D_MODEL), F32),
                        pltpu.VMEM((CLASS_ROWS, LANES), F32),
                        pltpu.VMEM((rows, rows), BF16)],
        compiler_params=_params(("arbitrary",)),
        name="attn_router",
    )(x_p, x_s, k_p, v_p, k_s, v_s, wq, wo, _row2(lng), _row2(lnb), rw_t, rb.reshape(N_EXPERTS, 1))


def _dispatch_kernel(pos_ref, offs_ref, cnt_ref, end_ref, used_ref, rows_ref, xs_hbm, dst_ref,
                     zeros, sem, zsem, *, tile, n_tiles, n_rows):
    step = pl.program_id(0)

    def class_padding(c, go):
        start = offs_ref[c] + cnt_ref[c]
        head = (-start) & (SUBLANES - 1)
        bulk_start = pl.multiple_of(start + head, SUBLANES)
        bulk = pl.multiple_of(end_ref[c] - bulk_start, SUBLANES)

        def one(r, carry_):
            go(pltpu.make_async_copy(zeros.at[pl.ds(0, 1)], xs_hbm.at[pl.ds(r, 1)], zsem))
            return carry_
        lax.fori_loop(start, start + head, one, 0)

        @pl.when(bulk > 0)
        def _():
            go(pltpu.make_async_copy(zeros.at[pl.ds(0, bulk)], xs_hbm.at[pl.ds(bulk_start, bulk)], zsem))

    def unused_tiles(go):
        def one(j, carry_):
            go(pltpu.make_async_copy(zeros, xs_hbm.at[pl.ds(pl.multiple_of(j * MOE_TILE, MOE_TILE), MOE_TILE)], zsem))
            return carry_
        lax.fori_loop(used_ref[0], n_tiles, one, 0)

    @pl.when(step == 0)
    def _():
        zeros[...] = jnp.zeros_like(zeros)

        def no_token(p, carry_):
            dst_ref[p] = n_rows + ((p >> MOE_SHIFT) & 1) * MOE_TILE + (p & (MOE_TILE - 1))
            return carry_
        for c in range(N_CLASSES):
            lax.fori_loop(offs_ref[c] + cnt_ref[c], end_ref[c], no_token, 0)
            class_padding(c, lambda copy: copy.start())
        lax.fori_loop(used_ref[0] * MOE_TILE, n_tiles * MOE_TILE, no_token, 0)
        unused_tiles(lambda copy: copy.start())

    for r in range(tile):
        t = step * tile + r
        p = pos_ref[t]
        dst_ref[p] = t
        pltpu.make_async_copy(rows_ref.at[pl.ds(r, 1)], xs_hbm.at[pl.ds(p, 1)], sem).start(priority=r % 2)
    pltpu.make_async_copy(rows_ref, xs_hbm.at[pl.ds(0, tile)], sem).wait()

    @pl.when(step == 0)
    def _():
        for c in range(N_CLASSES):
            class_padding(c, lambda copy: copy.wait())
        unused_tiles(lambda copy: copy.wait())


def _dispatch(pos, offs, cnt, end, used, rows, n_tiles):
    n_rows = rows.shape[0]
    n_sorted = n_tiles * MOE_TILE
    tile = n_rows // DISPATCH_STEPS
    assert tile * DISPATCH_STEPS == n_rows and tile % SUBLANES == 0
    any_spec = pl.BlockSpec(memory_space=pl.ANY)
    return pl.pallas_call(
        functools.partial(_dispatch_kernel, tile=tile, n_tiles=n_tiles, n_rows=n_rows),
        grid_spec=pltpu.PrefetchScalarGridSpec(
            num_scalar_prefetch=5, grid=(DISPATCH_STEPS,),
            in_specs=[pl.BlockSpec((tile, ROW_W), lambda i, *_: (i, 0))],
            out_specs=[any_spec, pl.BlockSpec(memory_space=pltpu.SMEM)],
            scratch_shapes=[pltpu.VMEM((MOE_TILE, ROW_W), F32), pltpu.SemaphoreType.DMA(()),
                            pltpu.SemaphoreType.DMA(())]),
        out_shape=[jax.ShapeDtypeStruct((n_sorted, ROW_W), F32), jax.ShapeDtypeStruct((n_sorted,), jnp.int32)],
        compiler_params=_params(("arbitrary",)),
        name="dispatch",
    )(pos, offs, cnt, end, used, rows)


def _experts_kernel(ea_ref, eb_ref, blk_ref, nvalid_ref, dst_ref, xs_ref, wga_ref, wua_ref, wda_ref, wgb_ref, wub_ref,
                    wdb_ref, lng_ref, lnb_ref, out_hbm, ybuf, sems, *, n_rows):
    del ea_ref, eb_ref, blk_ref
    i = pl.program_id(0)
    n_cur = nvalid_ref[i]
    n_prev = jnp.where(i >= 1, nvalid_ref[jnp.maximum(i - 1, 0)], 0)
    n_prev2 = jnp.where(i >= 2, nvalid_ref[jnp.maximum(i - 2, 0)], 0)
    has_cur = n_cur > 0
    has_prev = n_prev > 0

    def tile_copy(slot):
        return pltpu.make_async_copy(ybuf.at[slot], out_hbm.at[pl.ds(0, MOE_TILE)], sems.at[slot])

    def compute(slot):
        x = xs_ref[:, 0:D_MODEL]
        xb = x.astype(BF16)
        y = None
        for lane, (wg, wu, wd) in enumerate(((wga_ref, wua_ref, wda_ref), (wgb_ref, wub_ref, wdb_ref))):
            gate = xs_ref[:, D_MODEL + lane:D_MODEL + lane + 1]
            hg = jnp.dot(xb, wg[0].astype(BF16), preferred_element_type=F32)
            hu = jnp.dot(xb, wu[0].astype(BF16), preferred_element_type=F32)
            hid = hg * jax.nn.sigmoid(hg) * hu * gate
            part = _bdot(hid, wd[0].astype(BF16))
            y = part if y is None else y + part
        ybuf[slot] = _layer_norm(ALPHA * x + y, lng_ref[...], lnb_ref[...])

    def start_rows(tile, slot):
        base = tile * MOE_TILE
        for r in range(MOE_TILE):
            dst = dst_ref[base + r]
            copy = pltpu.make_async_copy(ybuf.at[slot, pl.ds(r, 1)], out_hbm.at[pl.ds(dst, 1)], sems.at[slot])
            copy.start(priority=r % 2)

    @pl.when(i == 0)
    def _():
        ybuf[2] = jnp.zeros((MOE_TILE, D_MODEL), F32)
        for half in range(2):
            fill = pltpu.make_async_copy(ybuf.at[2], out_hbm.at[pl.ds(n_rows + half * MOE_TILE, MOE_TILE)], sems.at[2])
            fill.start()
            fill.wait()

    slot_cur = i % 3
    slot_prev = (i + 2) % 3

    @pl.when(has_prev & has_cur)
    def _():
        start_rows(i - 1, slot_prev)
        compute(slot_cur)

    @pl.when(has_prev & jnp.logical_not(has_cur))
    def _():
        start_rows(i - 1, slot_prev)

    @pl.when(jnp.logical_not(has_prev) & has_cur)
    def _():
        compute(slot_cur)

    @pl.when(n_prev2 > 0)
    def _():
        tile_copy((i + 1) % 3).wait()


def _experts(tile_a, tile_b, tile_blk, tile_nvalid, dst, xs, w_gate, w_up, w_down, lng, lnb, n_rows, layer):
    n_steps = tile_nvalid.shape[0]
    e0 = layer * N_EXPERTS
    up_a = pl.BlockSpec((1, D_MODEL, D_EXPERT), lambda i, ea, eb, *_: (e0 + ea[i], 0, 0))
    up_b = pl.BlockSpec((1, D_MODEL, D_EXPERT), lambda i, ea, eb, *_: (e0 + eb[i], 0, 0))
    dn_a = pl.BlockSpec((1, D_EXPERT, D_MODEL), lambda i, ea, eb, *_: (e0 + ea[i], 0, 0))
    dn_b = pl.BlockSpec((1, D_EXPERT, D_MODEL), lambda i, ea, eb, *_: (e0 + eb[i], 0, 0))
    vec = pl.BlockSpec((1, D_MODEL), lambda i, *_: (0, 0))
    return pl.pallas_call(
        functools.partial(_experts_kernel, n_rows=n_rows),
        grid_spec=pltpu.PrefetchScalarGridSpec(
            num_scalar_prefetch=5, grid=(n_steps,),
            in_specs=[pl.BlockSpec((MOE_TILE, ROW_W), lambda i, ea, eb, blk, *_: (blk[i], 0)),
                      up_a, up_a, dn_a, up_b, up_b, dn_b, vec, vec],
            out_specs=pl.BlockSpec(memory_space=pl.ANY),
            scratch_shapes=[pltpu.VMEM((3, MOE_TILE, D_MODEL), F32), pltpu.SemaphoreType.DMA((3,))]),
        out_shape=jax.ShapeDtypeStruct((n_rows + 2 * MOE_TILE, D_MODEL), F32),
        compiler_params=_params(("arbitrary",)),
        name="experts",
    )(tile_a, tile_b, tile_blk, tile_nvalid, dst, xs, w_gate, w_up, w_down, w_gate, w_up, w_down,
      _row2(lng), _row2(lnb))


def _lookup(table, idx):
    pick = idx[:, None] == jnp.arange(table.shape[0], dtype=jnp.int32)[None, :]
    return jnp.sum(jnp.where(pick, table[None, :], 0), axis=1)


def _moe(rows, meta, counts, w_gate, w_up, w_down, lng, lnb, layer):
    n_rows = rows.shape[0]
    n_tiles = n_rows // MOE_TILE + N_CLASSES
    cnt = counts[:N_CLASSES, 0]
    tiles_per = (cnt + MOE_TILE - 1) // MOE_TILE
    cls_id = jnp.arange(N_CLASSES, dtype=jnp.int32)
    tile_end = jnp.sum(jnp.where(cls_id[None, :] <= cls_id[:, None], tiles_per[None, :], 0), axis=1)
    tile_start = tile_end - tiles_per
    offs = tile_start * MOE_TILE
    end = tile_end * MOE_TILE
    used = tile_end[N_CLASSES - 1:N_CLASSES]
    tile_id = jnp.arange(n_tiles + 2, dtype=jnp.int32)
    tile_blk = jnp.minimum(tile_id, used - 1)
    tile_cls = jnp.sum((tile_blk[:, None] >= tile_end[None, :]).astype(jnp.int32), axis=1)
    tile_a = _lookup(jnp.asarray(_CLASS_A), tile_cls)
    tile_b = _lookup(jnp.asarray(_CLASS_B), tile_cls)
    tile_nvalid = jnp.clip(_lookup(cnt, tile_cls) - (tile_id - _lookup(tile_start, tile_cls)) * MOE_TILE, 0, MOE_TILE)
    pos = _lookup(offs, meta[0]) + meta[1]
    xs, dst = _dispatch(pos, offs, cnt, end, used, rows, n_tiles)
    return _experts(tile_a, tile_b, tile_blk, tile_nvalid, dst, xs, w_gate, w_up, w_down, lng, lnb, n_rows, layer)


def kernel(x_prompt, x_sample, mem_prompt, cache_mem_k, cache_mem_v, state_conv_a, state_conv_b, state_pool,
           ln_g, ln_b, ab_w_in, ab_conv_a_w, ab_conv_a_b, ab_norm_a_g, ab_norm_a_b, ab_conv_b_w, ab_w_out,
           cd_w_in, cd_pool_w, cd_pool_scale, cd_v_norm_g, cd_v_norm_b, cd_w_s, cd_b_s, cd_w_out,
           ca_wq, ca_wk, ca_wv, ca_wo, router_w, router_b, moe_w_gate, moe_w_up, moe_w_down):
    assert ln_g.shape[0] == DEPTH and ab_w_in.shape[0] == 1 and cd_w_in.shape[0] == 1
    bsz, seq, _ = x_prompt.shape
    dec_b, dec_seq, _ = x_sample.shape
    n_prompt = bsz * seq
    n_sample = dec_b * dec_seq
    sample_seqs = ROW_TILE // dec_seq

    n_pool = len(POOL_WINDOWS)
    pool_c = D_HALF // n_pool
    pool_bd = jnp.zeros((D_HALF, D_HALF), F32)
    for g in range(n_pool):
        pool_bd = pool_bd.at[g * pool_c:(g + 1) * pool_c, g * pool_c:(g + 1) * pool_c].set(cd_pool_w[0, g])
    pool_bd = pool_bd.astype(BF16)
    ab_w_in_hi, ab_w_in_lo = _split_weight(ab_w_in[0])
    ab_w_out_hi, ab_w_out_lo = _split_weight(ab_w_out[0])
    rw_hi, rw_lo = _split_weight(router_w.T)
    rw_t = jnp.concatenate([rw_hi, rw_lo], axis=0)
    cd_w_in_b, cd_w_out_b = cd_w_in[0].astype(BF16), cd_w_out[0].astype(BF16)
    wq_b, wo_b = ca_wq.astype(BF16), ca_wo.astype(BF16)
    wg_b = moe_w_gate.reshape(DEPTH * N_EXPERTS, D_MODEL, D_EXPERT)
    wu_b = moe_w_up.reshape(DEPTH * N_EXPERTS, D_MODEL, D_EXPERT)
    wd_b = moe_w_down.reshape(DEPTH * N_EXPERTS, D_EXPERT, D_MODEL)

    mem_k_prompt, k_p = _mem_projection(mem_prompt, ca_wk)
    mem_v_prompt, v_p = _mem_projection(mem_prompt, ca_wv)
    k_s = _heads_to_columns(cache_mem_k.reshape(DEPTH * dec_b, N_MEM, MEM_HEADS, MEM_HEAD_DIM))
    v_s = _heads_to_columns(cache_mem_v.reshape(DEPTH * dec_b, N_MEM, MEM_HEADS, MEM_HEAD_DIM))

    def mixer_ab(x, buf_a, buf_b, n_seq, seg, **kw):
        return _mixer_ab(x, buf_a, buf_b, ab_w_in_hi, ab_w_in_lo, ab_conv_a_w[0], ab_conv_a_b[0], ab_norm_a_g[0],
                         ab_norm_a_b[0], ab_conv_b_w[0], ab_w_out_hi, ab_w_out_lo, ln_g[0, 0], ln_b[0, 0],
                         n_seq=n_seq, seg=seg, **kw)

    def mixer_cd(x, row_start, buf_p, length, n_seq, seg, pos0, emit_v):
        return _mixer_cd(x, row_start, buf_p, length, cd_w_in_b, pool_bd, cd_pool_scale[0], cd_v_norm_g[0],
                         cd_v_norm_b[0], cd_w_s[0], cd_b_s[0], cd_w_out_b, ln_g[1, 0], ln_b[1, 0],
                         n_seq=n_seq, seg=seg, pos0=pos0, emit_v=emit_v)

    def attn_moe(layer, x_p, x_s):
        rows, meta, counts = _attn_router(x_p, x_s, k_p, v_p, k_s, v_s, wq_b, wo_b, ln_g[layer, 1], ln_b[layer, 1],
                                          rw_t, router_b, layer=layer, prompt_len=seq, sample_len=dec_seq)
        return _moe(rows, meta, counts, wg_b, wu_b, wd_b, ln_g[layer, 2], ln_b[layer, 2], layer)

    xp_flat = x_prompt.reshape(n_prompt, D_MODEL)
    tiles_p = seq // ROW_TILE
    x_p, conv_a_p, conv_b_p = mixer_ab(xp_flat, jnp.zeros((bsz, CONV_A - 1, D_HALF), F32),
                                       jnp.zeros((bsz, CONV_B - 1, D_HALF), F32), 1, ROW_TILE,
                                       precise=False, skip_tail=STATE_TILES)
    x_p, conv_a_p, conv_b_p = mixer_ab(xp_flat, conv_a_p, conv_b_p, 1, ROW_TILE, precise=True,
                                       tile_lo=tiles_p - STATE_TILES, tiles=STATE_TILES, y_prev=x_p)
    x_s, conv_a_s, conv_b_s = mixer_ab(x_sample.reshape(n_sample, D_MODEL), state_conv_a[0], state_conv_b[0],
                                       sample_seqs, dec_seq, precise=False)
    x_all = attn_moe(0, x_p, x_s)

    x_p, pool_p = mixer_cd(x_all, 0, jnp.zeros((bsz, POOL_BUF, D_HALF), F32), seq, 1, ROW_TILE, 0, False)
    x_s, pool_s, v_s_rows = mixer_cd(x_all, n_prompt, state_pool[0], dec_seq, sample_seqs, dec_seq, PAST_LEN, True)
    x_all = attn_moe(1, x_p, x_s)

    y_prompt = x_all[0:n_prompt].reshape(bsz, seq, D_MODEL)
    y_sample = x_all[n_prompt:n_prompt + n_sample].reshape(dec_b, dec_seq, D_MODEL)
    return (y_prompt, y_sample, mem_k_prompt, mem_v_prompt, conv_a_p[None], conv_b_p[None], pool_p[None],
            conv_a_s[None], conv_b_s[None], pool_s[None], v_s_rows[None])
```

```python
import functools

import jax
import jax.numpy as jnp
import numpy as np
from jax import lax
from jax.experimental import pallas as pl
from jax.experimental.pallas import tpu as pltpu

F32 = jnp.float32
BF16 = jnp.bfloat16

D_MODEL = 1024
D_HALF = D_MODEL // 2
DEPTH = 2
PAST_LEN = 4096
CHUNK = 64
CHUNK_SHIFT = CHUNK.bit_length() - 1
CONV_A = 31
CONV_B = 3
POOL_WINDOWS = (2, 4, 8, 16)
POOL_BUF = max(POOL_WINDOWS) - 1
N_HEAD_D = 4
GMLP_CHUNK = 128
N_MEM = 256
MEM_HEADS = 4
MEM_HEAD_DIM = D_MODEL // MEM_HEADS
N_EXPERTS = 16
N_EXPERT_GROUPS = 4
GROUP_SIZE = N_EXPERTS // N_EXPERT_GROUPS
GROUP_SHIFT = GROUP_SIZE.bit_length() - 1
PAIRS_PER_GROUP = GROUP_SIZE * (GROUP_SIZE - 1) // 2
N_CLASSES = N_EXPERT_GROUPS * PAIRS_PER_GROUP
D_EXPERT = D_MODEL // 2
ALPHA = (2 * DEPTH) ** 0.25
LN_EPS = 1e-5

LANES = 128
SUBLANES = 8
ROW_W = D_MODEL + LANES
HIST_A = 32
HIST_B = 8
HIST_P = 16
CONV_ROWS = 64
ROW_TILE = 512
MOE_TILE = 256
MOE_SHIFT = MOE_TILE.bit_length() - 1
CLASS_ROWS = 32
DISPATCH_STEPS = 8
SPLIT_ROWS = 256
CACHE_SEQS = 4
STATE_TILES = 1
V7X_VMEM_BYTES = 64 * 1024 * 1024
VMEM_LIMIT = V7X_VMEM_BYTES * 7 // 8

_PAIR_AB = [(a, b) for a in range(GROUP_SIZE) for b in range(a + 1, GROUP_SIZE)]
_CLASS_A = np.array([g * GROUP_SIZE + a for g in range(N_EXPERT_GROUPS) for a, _ in _PAIR_AB], np.int32)
_CLASS_B = np.array([g * GROUP_SIZE + b for g in range(N_EXPERT_GROUPS) for _, b in _PAIR_AB], np.int32)


def _layer_norm(x, g, b):
    mu = jnp.mean(x, axis=-1, keepdims=True)
    xc = x - mu
    var = jnp.mean(xc * xc, axis=-1, keepdims=True)
    return xc * lax.rsqrt(var + LN_EPS) * g + b


def _bdot(a, w):
    return jnp.dot(a.astype(BF16), w, preferred_element_type=F32)


def _split(a):
    hi = a.astype(BF16)
    return hi, (a - hi.astype(F32)).astype(BF16)


def _dot3(a, w_hi, w_lo):
    a_hi, a_lo = _split(a)
    return (jnp.dot(a_hi, w_hi, preferred_element_type=F32) + jnp.dot(a_lo, w_hi, preferred_element_type=F32)
            + jnp.dot(a_hi, w_lo, preferred_element_type=F32))


def _params(sem):
    return pltpu.CompilerParams(dimension_semantics=sem, vmem_limit_bytes=VMEM_LIMIT)


def _full(shape):
    return pl.BlockSpec(shape, lambda *_: (0,) * len(shape), pipeline_mode=pl.Buffered(1))


def _row2(v):
    return v.reshape(1, -1)


def _split_weight_kernel(w_ref, hi_ref, lo_ref):
    hi, lo = _split(w_ref[...])
    hi_ref[...] = hi
    lo_ref[...] = lo


def _split_weight(w):
    rows, cols = w.shape
    blk = min(rows, SPLIT_ROWS)
    spec = pl.BlockSpec((blk, cols), lambda i: (i, 0))
    return pl.pallas_call(
        _split_weight_kernel,
        grid=(rows // blk,),
        in_specs=[spec],
        out_specs=[spec, spec],
        out_shape=[jax.ShapeDtypeStruct(w.shape, BF16)] * 2,
        compiler_params=_params(("arbitrary",)),
        name="split_weight",
    )(w)


def _proj_kernel(x_ref, w_ref, o_ref, o16_ref, *, batch):
    res = _bdot(x_ref[...], w_ref[0].astype(BF16))
    o16_ref[...] = res.astype(BF16).reshape(batch, N_MEM, D_MODEL)
    for hd in range(MEM_HEADS):
        cols = slice(hd * MEM_HEAD_DIM, (hd + 1) * MEM_HEAD_DIM)
        o_ref[0, :, :, hd, :] = res[:, cols].reshape(batch, N_MEM, MEM_HEAD_DIM)


def _mem_projection(mem, w):
    batch, n = mem.shape[0], w.shape[0]
    return pl.pallas_call(
        functools.partial(_proj_kernel, batch=batch),
        grid=(n,),
        in_specs=[pl.BlockSpec((batch * N_MEM, D_MODEL), lambda j: (0, 0)),
                  pl.BlockSpec((1, D_MODEL, D_MODEL), lambda j: (j, 0, 0))],
        out_specs=[pl.BlockSpec((1, batch, N_MEM, MEM_HEADS, MEM_HEAD_DIM), lambda j: (j, 0, 0, 0, 0)),
                   pl.BlockSpec((batch, N_MEM, D_MODEL), lambda j: (j, 0, 0))],
        out_shape=[jax.ShapeDtypeStruct((n, batch, N_MEM, MEM_HEADS, MEM_HEAD_DIM), F32),
                   jax.ShapeDtypeStruct((n * batch, N_MEM, D_MODEL), BF16)],
        compiler_params=_params(("arbitrary",)),
        name="mem_projection",
    )(mem.reshape(batch * N_MEM, D_MODEL), w)


def _heads_to_columns_kernel(c_hbm, o_ref, buf, sem):
    step = pl.program_id(0)

    def head_copies(at_step, slot):
        return [pltpu.make_async_copy(c_hbm.at[at_step * CACHE_SEQS + s, :, hd, :], buf.at[slot, s, hd], sem.at[slot])
                for s in range(CACHE_SEQS) for hd in range(MEM_HEADS)]

    @pl.when(step == 0)
    def _():
        for copy in head_copies(0, 0):
            copy.start()

    @pl.when(step + 1 < pl.num_programs(0))
    def _():
        for copy in head_copies(step + 1, (step + 1) % 2):
            copy.start()

    slot = step % 2
    for copy in head_copies(step, slot):
        copy.wait()
    for s in range(CACHE_SEQS):
        for hd in range(MEM_HEADS):
            o_ref[s, :, hd * MEM_HEAD_DIM:(hd + 1) * MEM_HEAD_DIM] = buf[slot, s, hd].astype(BF16)


def _heads_to_columns(cache):
    n = cache.shape[0]
    return pl.pallas_call(
        _heads_to_columns_kernel,
        grid=(n // CACHE_SEQS,),
        in_specs=[pl.BlockSpec(memory_space=pl.ANY)],
        out_specs=pl.BlockSpec((CACHE_SEQS, N_MEM, D_MODEL), lambda i: (i, 0, 0)),
        out_shape=jax.ShapeDtypeStruct((n, N_MEM, D_MODEL), BF16),
        scratch_shapes=[pltpu.VMEM((2, CACHE_SEQS, MEM_HEADS, N_MEM, MEM_HEAD_DIM), F32),
                        pltpu.SemaphoreType.DMA((2,))],
        compiler_params=_params(("arbitrary",)),
        name="heads_to_columns",
    )(cache)


def _load_history(ext_ref, buf_ref, first, hist, keep, seg):
    @pl.when(first)
    def _():
        ext_ref[:, hist - keep:hist, :] = buf_ref[...]

    @pl.when(jnp.logical_not(first))
    def _():
        ext_ref[:, hist - keep:hist, :] = ext_ref[:, seg + hist - keep:seg + hist, :]


def _depthwise_conv(ext_ref, w_ref, out_ref, *, n_seq, seg, taps, hist, shifted_ref=None):
    rc = min(CONV_ROWS, seg)
    off0 = hist - (taps - 1)
    length = hist + seg
    if shifted_ref is not None:
        for r in range(1, SUBLANES):
            shifted_ref[r - 1, :, 0:length - SUBLANES, :] = ext_ref[:, r:r + length - SUBLANES, :]
    for s in range(n_seq):
        for r0 in range(0, seg, rc):
            for lb in range(0, D_HALF, LANES):
                acc = None
                for k in range(taps):
                    lo = off0 + k + r0
                    shift = (off0 + k) % SUBLANES
                    if shifted_ref is None or shift == 0:
                        win = ext_ref[s, lo:lo + rc, lb:lb + LANES]
                    else:
                        win = shifted_ref[shift - 1, s, lo - shift:lo - shift + rc, lb:lb + LANES]
                    term = w_ref[k:k + 1, lb:lb + LANES] * win
                    acc = term if acc is None else acc + term
                out_ref[s * seg + r0:s * seg + r0 + rc, lb:lb + LANES] = acc


def _mixer_ab_kernel(x_ref, bufa_ref, bufb_ref, w_in_ref, w_in_lo_ref, caw_ref, cab_ref, nag_ref, nab_ref, cbw_ref,
                     w_out_ref, w_out_lo_ref, lng_ref, lnb_ref, *refs, n_seq, seg, precise, skip_tail, has_prev):
    if has_prev:
        refs = refs[1:]
    y_ref, nbufa_ref, nbufb_ref, a_ext, a_shift, cb_ext, conv_a, conv_b = refs

    def body():
        first = pl.program_id(1) == 0
        _load_history(a_ext, bufa_ref, first, HIST_A, CONV_A - 1, seg)
        _load_history(cb_ext, bufb_ref, first, HIST_B, CONV_B - 1, seg)
        x = x_ref[...]
        h = _dot3(x, w_in_ref[...], w_in_lo_ref[...]) if precise else _bdot(x, w_in_ref[...])
        a = h[:, 0:D_HALF] * jax.nn.sigmoid(h[:, D_HALF:2 * D_HALF])
        cb = h[:, 3 * D_HALF:4 * D_HALF] * h[:, 4 * D_HALF:5 * D_HALF]
        a_ext[:, HIST_A:HIST_A + seg, :] = a.reshape(n_seq, seg, D_HALF)
        cb_ext[:, HIST_B:HIST_B + seg, :] = cb.reshape(n_seq, seg, D_HALF)
        nbufa_ref[...] = a_ext[:, seg + HIST_A - (CONV_A - 1):seg + HIST_A, :]
        nbufb_ref[...] = cb_ext[:, seg + HIST_B - (CONV_B - 1):seg + HIST_B, :]

        _depthwise_conv(a_ext, caw_ref, conv_a, n_seq=n_seq, seg=seg, taps=CONV_A, hist=HIST_A, shifted_ref=a_shift)
        _depthwise_conv(cb_ext, cbw_ref, conv_b, n_seq=n_seq, seg=seg, taps=CONV_B, hist=HIST_B)

        a2 = _layer_norm(conv_a[...] + cab_ref[...], nag_ref[...], nab_ref[...])
        a2 = a2 * jax.nn.sigmoid(a2)
        b2 = h[:, 2 * D_HALF:3 * D_HALF] * conv_b[...]
        if precise:
            y = (_dot3(a2, w_out_ref[0:D_HALF, :], w_out_lo_ref[0:D_HALF, :])
                 + _dot3(b2, w_out_ref[D_HALF:D_MODEL, :], w_out_lo_ref[D_HALF:D_MODEL, :]))
        else:
            y = _bdot(a2, w_out_ref[0:D_HALF, :]) + _bdot(b2, w_out_ref[D_HALF:D_MODEL, :])
        y_ref[...] = _layer_norm(ALPHA * x + y, lng_ref[...], lnb_ref[...])

    if skip_tail == 0:
        body()
    else:
        live = pl.program_id(1) < pl.num_programs(1) - skip_tail
        pl.when(live)(body)

        @pl.when(jnp.logical_not(live))
        def _():
            y_ref[...] = jnp.zeros_like(y_ref)


def _mixer_ab(x, buf_a, buf_b, w_in, w_in_lo, caw, cab, nag, nab, cbw, w_out, w_out_lo, lng, lnb, *,
              n_seq, seg, precise, tile_lo=0, tiles=None, skip_tail=0, y_prev=None):
    batch = buf_a.shape[0]
    length = x.shape[0] // batch
    d_in = w_in.shape[1]
    n_l = length // seg
    tiles = n_l if tiles is None else tiles
    rows = n_seq * seg
    lo_in = _full((D_MODEL, d_in)) if precise else _full((SUBLANES, LANES))
    lo_out = _full((D_MODEL, D_MODEL)) if precise else _full((SUBLANES, LANES))
    row_blk = pl.BlockSpec((rows, D_MODEL), lambda b, l: (b * n_l + tile_lo + l, 0))
    in_specs = [row_blk,
                pl.BlockSpec((n_seq, CONV_A - 1, D_HALF), lambda b, l: (b, 0, 0)),
                pl.BlockSpec((n_seq, CONV_B - 1, D_HALF), lambda b, l: (b, 0, 0)),
                _full((D_MODEL, d_in)), lo_in, _full((CONV_A, D_HALF)), _full((1, D_HALF)), _full((1, D_HALF)),
                _full((1, D_HALF)), _full((CONV_B, D_HALF)), _full((D_MODEL, D_MODEL)), lo_out,
                _full((1, D_MODEL)), _full((1, D_MODEL))]
    args = [x, buf_a, buf_b, w_in, w_in_lo, caw, _row2(cab), _row2(nag), _row2(nab), cbw, w_out, w_out_lo,
            _row2(lng), _row2(lnb)]
    aliases = {}
    if y_prev is not None:
        aliases = {len(args): 0}
        in_specs.append(pl.BlockSpec(memory_space=pl.ANY))
        args.append(y_prev)
    return pl.pallas_call(
        functools.partial(_mixer_ab_kernel, n_seq=n_seq, seg=seg, precise=precise, skip_tail=skip_tail,
                          has_prev=y_prev is not None),
        grid=(batch // n_seq, tiles),
        in_specs=in_specs,
        out_specs=[row_blk,
                   pl.BlockSpec((n_seq, CONV_A - 1, D_HALF), lambda b, l: (b, 0, 0)),
                   pl.BlockSpec((n_seq, CONV_B - 1, D_HALF), lambda b, l: (b, 0, 0))],
        out_shape=[jax.ShapeDtypeStruct((batch * length, D_MODEL), F32),
                   jax.ShapeDtypeStruct((batch, CONV_A - 1, D_HALF), F32),
                   jax.ShapeDtypeStruct((batch, CONV_B - 1, D_HALF), F32)],
        scratch_shapes=[pltpu.VMEM((n_seq, HIST_A + seg, D_HALF), F32),
                        pltpu.VMEM((SUBLANES - 1, n_seq, HIST_A + seg - SUBLANES, D_HALF), F32),
                        pltpu.VMEM((n_seq, HIST_B + seg, D_HALF), F32),
                        pltpu.VMEM((rows, D_HALF), F32),
                        pltpu.VMEM((rows, D_HALF), F32)],
        input_output_aliases=aliases,
        compiler_params=_params(("arbitrary", "arbitrary")),
        name="mixer_ab",
    )(*args)


def _mixer_cd_kernel(x_ref, bufp_ref, w_in_ref, pw_ref, ps_ref, vg_ref, vb_ref, ws_ref, bs_ref, w_out_ref,
                     lng_ref, lnb_ref, *refs, n_seq, seg, pos0, n_mix, emit_v):
    if emit_v:
        y_ref, nbufp_ref, v_ref, c_ext, pooled, mixed = refs
    else:
        y_ref, nbufp_ref, c_ext, pooled, mixed = refs
    rows = n_seq * seg
    li = pl.program_id(1)
    _load_history(c_ext, bufp_ref, li == 0, HIST_P, POOL_BUF, seg)
    x = x_ref[...]
    h = _bdot(x, w_in_ref[...])
    c_in = h[:, 0:D_HALF]
    c_ext[:, HIST_P:HIST_P + seg, :] = c_in.reshape(n_seq, seg, D_HALF)
    nbufp_ref[...] = c_ext[:, seg + HIST_P - POOL_BUF:seg + HIST_P, :]

    rc = min(CONV_ROWS, seg)
    for g, win in enumerate(POOL_WINDOWS):
        lanes = slice(g * LANES, (g + 1) * LANES)
        for s in range(n_seq):
            for r0 in range(0, seg, rc):
                cur = c_ext[s, HIST_P + r0:HIST_P + r0 + rc, lanes]
                acc = cur
                for j in range(1, win):
                    acc = acc + c_ext[s, HIST_P + r0 - j:HIST_P + r0 - j + rc, lanes]
                pos = pos0 + li * seg + r0 + lax.broadcasted_iota(jnp.int32, (rc, LANES), 0)
                cnt = jnp.minimum(pos + 1, win).astype(F32)
                pooled[s * seg + r0:s * seg + r0 + rc, lanes] = acc / cnt - cur
    c = _bdot(pooled[...], pw_ref[...]) * ps_ref[...]

    z = jax.nn.gelu(h[:, D_HALF:3 * D_HALF], approximate=True)
    u = z[:, 0:D_HALF]
    v = _layer_norm(z[:, D_HALF:2 * D_HALF], vg_ref[...], vb_ref[...])
    if emit_v:
        v_ref[...] = v.reshape(n_seq, seg, D_HALF)
    vb16 = v.astype(BF16)

    ri = lax.broadcasted_iota(jnp.int32, (n_mix, n_mix), 0) >> CHUNK_SHIFT
    ci = lax.broadcasted_iota(jnp.int32, (n_mix, n_mix), 1) >> CHUNK_SHIFT
    for g in range(N_HEAD_D):
        lanes = slice(g * LANES, (g + 1) * LANES)
        ws = jnp.where(ci <= ri, ws_ref[g], 0.0).astype(BF16)
        bias = bs_ref[:, g:g + 1]
        for r0 in range(0, rows, n_mix):
            mixed[r0:r0 + n_mix, lanes] = jnp.dot(ws, vb16[r0:r0 + n_mix, lanes], preferred_element_type=F32) + bias
    d = u * mixed[...]
    y = _bdot(c, w_out_ref[0:D_HALF, :]) + _bdot(d, w_out_ref[D_HALF:D_MODEL, :])
    y_ref[...] = _layer_norm(ALPHA * x + y, lng_ref[...], lnb_ref[...])


def _mixer_cd(x, row_start, buf_p, length, w_in, pw, ps, vg, vb, ws, bs, w_out, lng, lnb, *,
              n_seq, seg, pos0, emit_v):
    batch = buf_p.shape[0]
    d_in = w_in.shape[1]
    n_mix = min(length, GMLP_CHUNK)
    assert seg % n_mix == 0
    n_l = length // seg
    rows = n_seq * seg
    blk0 = row_start // rows
    ws_n = ws[:, :n_mix, :n_mix]
    bs_t = bs[:, :n_mix].T
    out_specs = [pl.BlockSpec((rows, D_MODEL), lambda b, l: (b * n_l + l, 0)),
                 pl.BlockSpec((n_seq, POOL_BUF, D_HALF), lambda b, l: (b, 0, 0))]
    out_shape = [jax.ShapeDtypeStruct((batch * length, D_MODEL), F32),
                 jax.ShapeDtypeStruct((batch, POOL_BUF, D_HALF), F32)]
    if emit_v:
        out_specs.append(pl.BlockSpec((n_seq, seg, D_HALF), lambda b, l: (b, l, 0)))
        out_shape.append(jax.ShapeDtypeStruct((batch, length, D_HALF), F32))
    return pl.pallas_call(
        functools.partial(_mixer_cd_kernel, n_seq=n_seq, seg=seg, pos0=pos0, n_mix=n_mix, emit_v=emit_v),
        grid=(batch // n_seq, n_l),
        in_specs=[pl.BlockSpec((rows, D_MODEL), lambda b, l: (blk0 + b * n_l + l, 0)),
                  pl.BlockSpec((n_seq, POOL_BUF, D_HALF), lambda b, l: (b, 0, 0)),
                  _full((D_MODEL, d_in)), _full((D_HALF, D_HALF)), _full((1, D_HALF)), _full((1, D_HALF)),
                  _full((1, D_HALF)), _full((N_HEAD_D, n_mix, n_mix)), _full((n_mix, N_HEAD_D)),
                  _full((D_MODEL, D_MODEL)), _full((1, D_MODEL)), _full((1, D_MODEL))],
        out_specs=out_specs,
        out_shape=out_shape,
        scratch_shapes=[pltpu.VMEM((n_seq, HIST_P + seg, D_HALF), F32),
                        pltpu.VMEM((rows, D_HALF), F32),
                        pltpu.VMEM((rows, D_HALF), F32)],
        compiler_params=_params(("arbitrary", "arbitrary")),
        name="mixer_cd",
    )(x, buf_p, w_in, pw, _row2(ps), _row2(vg), _row2(vb), ws_n, bs_t, w_out, _row2(lng), _row2(lnb))


def _attn_router_kernel(xp_ref, xs_ref, kp_ref, vp_ref, ks_ref, vs_ref, wq_ref, wo_ref, lng_ref, lnb_ref, rw_ref, rb_ref,
                        rows_ref, meta_ref, counts_ref,
                        q_scr, o_scr, carry, tri, *, n_prompt_tiles, sample_seg):
    rows = ROW_TILE
    step = pl.program_id(0)
    is_prompt = step < n_prompt_tiles

    @pl.when(step == 0)
    def _():
        carry[...] = jnp.zeros_like(carry)
        upper = (lax.broadcasted_iota(jnp.int32, (rows, rows), 0) <= lax.broadcasted_iota(jnp.int32, (rows, rows), 1))
        tri[...] = upper.astype(F32).astype(BF16)

    def run(x_ref, k_ref, v_ref, n_seq, seg):
        x = x_ref[...]
        q_scr[...] = (_bdot(x, wq_ref[0]) * (MEM_HEAD_DIM ** -0.5)).astype(BF16)
        for s in range(n_seq):
            for hd in range(MEM_HEADS):
                cols = slice(hd * MEM_HEAD_DIM, (hd + 1) * MEM_HEAD_DIM)
                sc = lax.dot_general(q_scr[s * seg:(s + 1) * seg, cols], k_ref[s, :, cols], (((1,), (1,)), ((), ())),
                                     preferred_element_type=F32)
                p = jnp.exp(sc - jnp.max(sc, axis=-1, keepdims=True))
                den = jnp.sum(p, axis=-1, keepdims=True)
                o_scr[s * seg:(s + 1) * seg, cols] = _bdot(p, v_ref[s, :, cols]) / den
        x2 = _layer_norm(ALPHA * x + _bdot(o_scr[...], wo_ref[0]), lng_ref[...], lnb_ref[...])
        rows_ref[:, 0:D_MODEL] = x2

        nt = (((1,), (1,)), ((), ()))
        x2_hi, x2_lo = _split(x2)
        both = lax.dot_general(rw_ref[...], x2_hi, nt, preferred_element_type=F32)
        cross = lax.dot_general(rw_ref[0:N_EXPERTS, :], x2_lo, nt, preferred_element_type=F32)
        logits = both[0:N_EXPERTS] + both[N_EXPERTS:2 * N_EXPERTS] + cross + rb_ref[...]
        e = jnp.exp(logits - jnp.max(logits, axis=0, keepdims=True))
        scores = e / jnp.sum(e, axis=0, keepdims=True)
        eid = lax.broadcasted_iota(jnp.int32, (N_EXPERTS, rows), 0)
        egrp = eid >> GROUP_SHIFT
        best = jnp.max(jnp.where(egrp == 0, scores, -1.0), axis=0, keepdims=True)
        g_sel = jnp.zeros((1, rows), jnp.int32)
        for g in range(1, N_EXPERT_GROUPS):
            gs = jnp.max(jnp.where(egrp == g, scores, -1.0), axis=0, keepdims=True)
            upd = gs > best
            g_sel = jnp.where(upd, g, g_sel)
            best = jnp.where(upd, gs, best)
        masked = jnp.where(egrp == g_sel, scores, -1.0)
        m1 = jnp.max(masked, axis=0, keepdims=True)
        i1 = jnp.min(jnp.where(masked == m1, eid, N_EXPERTS), axis=0, keepdims=True)
        masked2 = jnp.where(eid == i1, -2.0, masked)
        m2 = jnp.max(masked2, axis=0, keepdims=True)
        i2 = jnp.min(jnp.where(masked2 == m2, eid, N_EXPERTS), axis=0, keepdims=True)
        tot = m1 + m2
        g1 = m1 / tot
        g2 = m2 / tot
        first_low = i1 < i2
        ea = jnp.where(first_low, i1, i2) & (GROUP_SIZE - 1)
        eb = jnp.where(first_low, i2, i1) & (GROUP_SIZE - 1)
        gate_a = jnp.where(first_low, g1, g2)
        gate_b = jnp.where(first_low, g2, g1)
        pair = eb - 1 + jnp.where(ea == 1, 2, 0) + jnp.where(ea == 2, 3, 0)
        cls = g_sel * PAIRS_PER_GROUP + pair

        lane_row = lax.broadcasted_iota(jnp.int32, (LANES, rows), 0)
        gates_t = jnp.where(lane_row == 0, gate_a, jnp.where(lane_row == 1, gate_b, 0.0))
        rows_ref[:, D_MODEL:ROW_W] = gates_t.T

        onehot = (lax.broadcasted_iota(jnp.int32, (CLASS_ROWS, rows), 0) == cls).astype(F32)
        cum = jnp.dot(onehot.astype(BF16), tri[...], preferred_element_type=F32)
        before = carry[:, 0:1]
        rank = jnp.sum(onehot * (before + cum), axis=0, keepdims=True) - 1.0
        sub = lax.broadcasted_iota(jnp.int32, (SUBLANES, rows), 0)
        meta_ref[...] = jnp.where(sub == 0, cls, jnp.where(sub == 1, rank.astype(jnp.int32), 0))
        carry[...] = carry[...] + cum[:, rows - 1:rows]
        counts_ref[...] = carry[...].astype(jnp.int32)

    pl.when(is_prompt)(lambda: run(xp_ref, kp_ref, vp_ref, 1, rows))
    pl.when(jnp.logical_not(is_prompt))(lambda: run(xs_ref, ks_ref, vs_ref, rows // sample_seg, sample_seg))


def _attn_router(x_p, x_s, k_p, v_p, k_s, v_s, wq, wo, lng, lnb, rw_t, rb, *, layer, prompt_len, sample_len):
    rows = ROW_TILE
    n_p = x_p.shape[0] // rows
    n_s = x_s.shape[0] // rows
    tiles_per_seq = prompt_len // rows
    seq_per_tile = rows // sample_len
    total = x_p.shape[0] + x_s.shape[0]
    n_prompt_seq = n_p // tiles_per_seq
    p0 = layer * n_prompt_seq
    s0 = layer * n_s
    p_idx = lambda i: jnp.minimum(i, n_p - 1)
    s_idx = lambda i: jnp.maximum(i - n_p, 0)
    layer_w = pl.BlockSpec((1, D_MODEL, D_MODEL), lambda i: (layer, 0, 0), pipeline_mode=pl.Buffered(1))
    return pl.pallas_call(
        functools.partial(_attn_router_kernel, n_prompt_tiles=n_p, sample_seg=sample_len),
        grid=(n_p + n_s,),
        in_specs=[pl.BlockSpec((rows, D_MODEL), lambda i: (p_idx(i), 0)),
                  pl.BlockSpec((rows, D_MODEL), lambda i: (s_idx(i), 0)),
                  pl.BlockSpec((1, N_MEM, D_MODEL), lambda i: (p0 + p_idx(i) // tiles_per_seq, 0, 0)),
                  pl.BlockSpec((1, N_MEM, D_MODEL), lambda i: (p0 + p_idx(i) // tiles_per_seq, 0, 0)),
                  pl.BlockSpec((seq_per_tile, N_MEM, D_MODEL), lambda i: (s0 + s_idx(i), 0, 0)),
                  pl.BlockSpec((seq_per_tile, N_MEM, D_MODEL), lambda i: (s0 + s_idx(i), 0, 0)),
                  layer_w, layer_w, _full((1, D_MODEL)), _full((1, D_MODEL)),
                  _full((2 * N_EXPERTS, D_MODEL)), _full((N_EXPERTS, 1))],
        out_specs=[pl.BlockSpec((rows, ROW_W), lambda i: (i, 0)),
                   pl.BlockSpec((SUBLANES, rows), lambda i: (0, i)),
                   pl.BlockSpec((CLASS_ROWS, LANES), lambda i: (0, 0))],
        out_shape=[jax.ShapeDtypeStruct((total, ROW_W), F32),
                   jax.ShapeDtypeStruct((SUBLANES, total), jnp.int32),
                   jax.ShapeDtypeStruct((CLASS_ROWS, LANES), jnp.int32)],
        scratch_shapes=[pltpu.VMEM((rows, D_MODEL), BF16),
                        pltpu.VMEM((rows, D_MODEL), F32),
                        pltpu.VMEM((CLASS_ROWS, LANES), F32),
                        pltpu.VMEM((rows, rows), BF16)],
        compiler_params=_params(("arbitrary",)),
        name="attn_router",
    )(x_p, x_s, k_p, v_p, k_s, v_s, wq, wo, _row2(lng), _row2(lnb), rw_t, rb.reshape(N_EXPERTS, 1))


def _dispatch_kernel(pos_ref, offs_ref, cnt_ref, end_ref, used_ref, rows_ref, xs_hbm, dst_ref,
                     zeros, sem, zsem, *, tile, n_tiles, n_rows):
    step = pl.program_id(0)

    def class_padding(c, go):
        start = offs_ref[c] + cnt_ref[c]
        head = (-start) & (SUBLANES - 1)
        bulk_start = pl.multiple_of(start + head, SUBLANES)
        bulk = pl.multiple_of(end_ref[c] - bulk_start, SUBLANES)

        def one(r, carry_):
            go(pltpu.make_async_copy(zeros.at[pl.ds(0, 1)], xs_hbm.at[pl.ds(r, 1)], zsem))
            return carry_
        lax.fori_loop(start, start + head, one, 0)

        @pl.when(bulk > 0)
        def _():
            go(pltpu.make_async_copy(zeros.at[pl.ds(0, bulk)], xs_hbm.at[pl.ds(bulk_start, bulk)], zsem))

    def unused_tiles(go):
        def one(j, carry_):
            go(pltpu.make_async_copy(zeros, xs_hbm.at[pl.ds(pl.multiple_of(j * MOE_TILE, MOE_TILE), MOE_TILE)], zsem))
            return carry_
        lax.fori_loop(used_ref[0], n_tiles, one, 0)

    @pl.when(step == 0)
    def _():
        zeros[...] = jnp.zeros_like(zeros)

        def no_token(p, carry_):
            dst_ref[p] = n_rows + ((p >> MOE_SHIFT) & 1) * MOE_TILE + (p & (MOE_TILE - 1))
            return carry_
        for c in range(N_CLASSES):
            lax.fori_loop(offs_ref[c] + cnt_ref[c], end_ref[c], no_token, 0)
            class_padding(c, lambda copy: copy.start())
        lax.fori_loop(used_ref[0] * MOE_TILE, n_tiles * MOE_TILE, no_token, 0)
        unused_tiles(lambda copy: copy.start())

    for r in range(tile):
        t = step * tile + r
        p = pos_ref[t]
        dst_ref[p] = t
        pltpu.make_async_copy(rows_ref.at[pl.ds(r, 1)], xs_hbm.at[pl.ds(p, 1)], sem).start(priority=r % 2)
    pltpu.make_async_copy(rows_ref, xs_hbm.at[pl.ds(0, tile)], sem).wait()

    @pl.when(step == 0)
    def _():
        for c in range(N_CLASSES):
            class_padding(c, lambda copy: copy.wait())
        unused_tiles(lambda copy: copy.wait())


def _dispatch(pos, offs, cnt, end, used, rows, n_tiles):
    n_rows = rows.shape[0]
    n_sorted = n_tiles * MOE_TILE
    tile = n_rows // DISPATCH_STEPS
    assert tile * DISPATCH_STEPS == n_rows and tile % SUBLANES == 0
    any_spec = pl.BlockSpec(memory_space=pl.ANY)
    return pl.pallas_call(
        functools.partial(_dispatch_kernel, tile=tile, n_tiles=n_tiles, n_rows=n_rows),
        grid_spec=pltpu.PrefetchScalarGridSpec(
            num_scalar_prefetch=5, grid=(DISPATCH_STEPS,),
            in_specs=[pl.BlockSpec((tile, ROW_W), lambda i, *_: (i, 0))],
            out_specs=[any_spec, pl.BlockSpec(memory_space=pltpu.SMEM)],
            scratch_shapes=[pltpu.VMEM((MOE_TILE, ROW_W), F32), pltpu.SemaphoreType.DMA(()),
                            pltpu.SemaphoreType.DMA(())]),
        out_shape=[jax.ShapeDtypeStruct((n_sorted, ROW_W), F32), jax.ShapeDtypeStruct((n_sorted,), jnp.int32)],
        compiler_params=_params(("arbitrary",)),
        name="dispatch",
    )(pos, offs, cnt, end, used, rows)


def _experts_kernel(ea_ref, eb_ref, blk_ref, nvalid_ref, dst_ref, xs_ref, wga_ref, wua_ref, wda_ref, wgb_ref, wub_ref,
                    wdb_ref, lng_ref, lnb_ref, out_hbm, ybuf, sems, *, n_rows):
    del ea_ref, eb_ref, blk_ref
    i = pl.program_id(0)
    n_cur = nvalid_ref[i]
    n_prev = jnp.where(i >= 1, nvalid_ref[jnp.maximum(i - 1, 0)], 0)
    n_prev2 = jnp.where(i >= 2, nvalid_ref[jnp.maximum(i - 2, 0)], 0)
    has_cur = n_cur > 0
    has_prev = n_prev > 0

    def tile_copy(slot):
        return pltpu.make_async_copy(ybuf.at[slot], out_hbm.at[pl.ds(0, MOE_TILE)], sems.at[slot])

    def compute(slot, used):
        x = xs_ref[0:used, 0:D_MODEL]
        xb = x.astype(BF16)
        y = None
        for lane, (wg, wu, wd) in enumerate(((wga_ref, wua_ref, wda_ref), (wgb_ref, wub_ref, wdb_ref))):
            gate = xs_ref[0:used, D_MODEL + lane:D_MODEL + lane + 1]
            hg = jnp.dot(xb, wg[0].astype(BF16), preferred_element_type=F32)
            hu = jnp.dot(xb, wu[0].astype(BF16), preferred_element_type=F32)
            hid = hg * jax.nn.sigmoid(hg) * hu * gate
            part = _bdot(hid, wd[0].astype(BF16))
            y = part if y is None else y + part
        ybuf[slot, 0:used, :] = _layer_norm(ALPHA * x + y, lng_ref[...], lnb_ref[...])
        if used < MOE_TILE:
            ybuf[slot, used:MOE_TILE, :] = jnp.zeros((MOE_TILE - used, D_MODEL), F32)

    def start_rows(tile, slot):
        base = tile * MOE_TILE
        for r in range(MOE_TILE):
            dst = dst_ref[base + r]
            copy = pltpu.make_async_copy(ybuf.at[slot, pl.ds(r, 1)], out_hbm.at[pl.ds(dst, 1)], sems.at[slot])
            copy.start(priority=r % 2)

    @pl.when(i == 0)
    def _():
        ybuf[2] = jnp.zeros((MOE_TILE, D_MODEL), F32)
        for half in range(2):
            fill = pltpu.make_async_copy(ybuf.at[2], out_hbm.at[pl.ds(n_rows + half * MOE_TILE, MOE_TILE)], sems.at[2])
            fill.start()
            fill.wait()

    slot_cur = i % 3
    slot_prev = (i + 2) % 3

    half = n_cur <= MOE_TILE // 2

    @pl.when(has_prev & has_cur & jnp.logical_not(half))
    def _():
        start_rows(i - 1, slot_prev)
        compute(slot_cur, MOE_TILE)

    @pl.when(has_prev & has_cur & half)
    def _():
        start_rows(i - 1, slot_prev)
        compute(slot_cur, MOE_TILE // 2)

    @pl.when(has_prev & jnp.logical_not(has_cur))
    def _():
        start_rows(i - 1, slot_prev)

    @pl.when(jnp.logical_not(has_prev) & has_cur)
    def _():
        compute(slot_cur, MOE_TILE)

    @pl.when(n_prev2 > 0)
    def _():
        tile_copy((i + 1) % 3).wait()


def _experts(tile_a, tile_b, tile_blk, tile_nvalid, dst, xs, w_gate, w_up, w_down, lng, lnb, n_rows, layer):
    n_steps = tile_nvalid.shape[0]
    e0 = layer * N_EXPERTS
    up_a = pl.BlockSpec((1, D_MODEL, D_EXPERT), lambda i, ea, eb, *_: (e0 + ea[i], 0, 0))
    up_b = pl.BlockSpec((1, D_MODEL, D_EXPERT), lambda i, ea, eb, *_: (e0 + eb[i], 0, 0))
    dn_a = pl.BlockSpec((1, D_EXPERT, D_MODEL), lambda i, ea, eb, *_: (e0 + ea[i], 0, 0))
    dn_b = pl.BlockSpec((1, D_EXPERT, D_MODEL), lambda i, ea, eb, *_: (e0 + eb[i], 0, 0))
    vec = pl.BlockSpec((1, D_MODEL), lambda i, *_: (0, 0))
    return pl.pallas_call(
        functools.partial(_experts_kernel, n_rows=n_rows),
        grid_spec=pltpu.PrefetchScalarGridSpec(
            num_scalar_prefetch=5, grid=(n_steps,),
            in_specs=[pl.BlockSpec((MOE_TILE, ROW_W), lambda i, ea, eb, blk, *_: (blk[i], 0)),
                      up_a, up_a, dn_a, up_b, up_b, dn_b, vec, vec],
            out_specs=pl.BlockSpec(memory_space=pl.ANY),
            scratch_shapes=[pltpu.VMEM((3, MOE_TILE, D_MODEL), F32), pltpu.SemaphoreType.DMA((3,))]),
        out_shape=jax.ShapeDtypeStruct((n_rows + 2 * MOE_TILE, D_MODEL), F32),
        compiler_params=_params(("arbitrary",)),
        name="experts",
    )(tile_a, tile_b, tile_blk, tile_nvalid, dst, xs, w_gate, w_up, w_down, w_gate, w_up, w_down,
      _row2(lng), _row2(lnb))


def _lookup(table, idx):
    pick = idx[:, None] == jnp.arange(table.shape[0], dtype=jnp.int32)[None, :]
    return jnp.sum(jnp.where(pick, table[None, :], 0), axis=1)


def _moe(rows, meta, counts, w_gate, w_up, w_down, lng, lnb, layer):
    n_rows = rows.shape[0]
    n_tiles = n_rows // MOE_TILE + N_CLASSES
    cnt = counts[:N_CLASSES, 0]
    tiles_per = (cnt + MOE_TILE - 1) // MOE_TILE
    cls_id = jnp.arange(N_CLASSES, dtype=jnp.int32)
    tile_end = jnp.sum(jnp.where(cls_id[None, :] <= cls_id[:, None], tiles_per[None, :], 0), axis=1)
    tile_start = tile_end - tiles_per
    offs = tile_start * MOE_TILE
    end = tile_end * MOE_TILE
    used = tile_end[N_CLASSES - 1:N_CLASSES]
    tile_id = jnp.arange(n_tiles + 2, dtype=jnp.int32)
    tile_blk = jnp.minimum(tile_id, used - 1)
    tile_cls = jnp.sum((tile_blk[:, None] >= tile_end[None, :]).astype(jnp.int32), axis=1)
    tile_a = _lookup(jnp.asarray(_CLASS_A), tile_cls)
    tile_b = _lookup(jnp.asarray(_CLASS_B), tile_cls)
    tile_nvalid = jnp.clip(_lookup(cnt, tile_cls) - (tile_id - _lookup(tile_start, tile_cls)) * MOE_TILE, 0, MOE_TILE)
    pos = _lookup(offs, meta[0]) + meta[1]
    xs, dst = _dispatch(pos, offs, cnt, end, used, rows, n_tiles)
    return _experts(tile_a, tile_b, tile_blk, tile_nvalid, dst, xs, w_gate, w_up, w_down, lng, lnb, n_rows, layer)


def kernel(x_prompt, x_sample, mem_prompt, cache_mem_k, cache_mem_v, state_conv_a, state_conv_b, state_pool,
           ln_g, ln_b, ab_w_in, ab_conv_a_w, ab_conv_a_b, ab_norm_a_g, ab_norm_a_b, ab_conv_b_w, ab_w_out,
           cd_w_in, cd_pool_w, cd_pool_scale, cd_v_norm_g, cd_v_norm_b, cd_w_s, cd_b_s, cd_w_out,
           ca_wq, ca_wk, ca_wv, ca_wo, router_w, router_b, moe_w_gate, moe_w_up, moe_w_down):
    assert ln_g.shape[0] == DEPTH and ab_w_in.shape[0] == 1 and cd_w_in.shape[0] == 1
    bsz, seq, _ = x_prompt.shape
    dec_b, dec_seq, _ = x_sample.shape
    n_prompt = bsz * seq
    n_sample = dec_b * dec_seq
    sample_seqs = ROW_TILE // dec_seq

    n_pool = len(POOL_WINDOWS)
    pool_c = D_HALF // n_pool
    pool_bd = jnp.zeros((D_HALF, D_HALF), F32)
    for g in range(n_pool):
        pool_bd = pool_bd.at[g * pool_c:(g + 1) * pool_c, g * pool_c:(g + 1) * pool_c].set(cd_pool_w[0, g])
    pool_bd = pool_bd.astype(BF16)
    ab_w_in_hi, ab_w_in_lo = _split_weight(ab_w_in[0])
    ab_w_out_hi, ab_w_out_lo = _split_weight(ab_w_out[0])
    rw_hi, rw_lo = _split_weight(router_w.T)
    rw_t = jnp.concatenate([rw_hi, rw_lo], axis=0)
    cd_w_in_b, cd_w_out_b = cd_w_in[0].astype(BF16), cd_w_out[0].astype(BF16)
    wq_b, wo_b = ca_wq.astype(BF16), ca_wo.astype(BF16)
    wg_b = moe_w_gate.reshape(DEPTH * N_EXPERTS, D_MODEL, D_EXPERT)
    wu_b = moe_w_up.reshape(DEPTH * N_EXPERTS, D_MODEL, D_EXPERT)
    wd_b = moe_w_down.reshape(DEPTH * N_EXPERTS, D_EXPERT, D_MODEL)

    mem_k_prompt, k_p = _mem_projection(mem_prompt, ca_wk)
    mem_v_prompt, v_p = _mem_projection(mem_prompt, ca_wv)
    k_s = _heads_to_columns(cache_mem_k.reshape(DEPTH * dec_b, N_MEM, MEM_HEADS, MEM_HEAD_DIM))
    v_s = _heads_to_columns(cache_mem_v.reshape(DEPTH * dec_b, N_MEM, MEM_HEADS, MEM_HEAD_DIM))

    def mixer_ab(x, buf_a, buf_b, n_seq, seg, **kw):
        return _mixer_ab(x, buf_a, buf_b, ab_w_in_hi, ab_w_in_lo, ab_conv_a_w[0], ab_conv_a_b[0], ab_norm_a_g[0],
                         ab_norm_a_b[0], ab_conv_b_w[0], ab_w_out_hi, ab_w_out_lo, ln_g[0, 0], ln_b[0, 0],
                         n_seq=n_seq, seg=seg, **kw)

    def mixer_cd(x, row_start, buf_p, length, n_seq, seg, pos0, emit_v):
        return _mixer_cd(x, row_start, buf_p, length, cd_w_in_b, pool_bd, cd_pool_scale[0], cd_v_norm_g[0],
                         cd_v_norm_b[0], cd_w_s[0], cd_b_s[0], cd_w_out_b, ln_g[1, 0], ln_b[1, 0],
                         n_seq=n_seq, seg=seg, pos0=pos0, emit_v=emit_v)

    def attn_moe(layer, x_p, x_s):
        rows, meta, counts = _attn_router(x_p, x_s, k_p, v_p, k_s, v_s, wq_b, wo_b, ln_g[layer, 1], ln_b[layer, 1],
                                          rw_t, router_b, layer=layer, prompt_len=seq, sample_len=dec_seq)
        return _moe(rows, meta, counts, wg_b, wu_b, wd_b, ln_g[layer, 2], ln_b[layer, 2], layer)

    xp_flat = x_prompt.reshape(n_prompt, D_MODEL)
    tiles_p = seq // ROW_TILE
    x_p, conv_a_p, conv_b_p = mixer_ab(xp_flat, jnp.zeros((bsz, CONV_A - 1, D_HALF), F32),
                                       jnp.zeros((bsz, CONV_B - 1, D_HALF), F32), 1, ROW_TILE,
                                       precise=False, skip_tail=STATE_TILES)
    x_p, conv_a_p, conv_b_p = mixer_ab(xp_flat, conv_a_p, conv_b_p, 1, ROW_TILE, precise=True,
                                       tile_lo=tiles_p - STATE_TILES, tiles=STATE_TILES, y_prev=x_p)
    x_s, conv_a_s, conv_b_s = mixer_ab(x_sample.reshape(n_sample, D_MODEL), state_conv_a[0], state_conv_b[0],
                                       sample_seqs, dec_seq, precise=False)
    x_all = attn_moe(0, x_p, x_s)

    x_p, pool_p = mixer_cd(x_all, 0, jnp.zeros((bsz, POOL_BUF, D_HALF), F32), seq, 1, ROW_TILE, 0, False)
    x_s, pool_s, v_s_rows = mixer_cd(x_all, n_prompt, state_pool[0], dec_seq, sample_seqs, dec_seq, PAST_LEN, True)
    x_all = attn_moe(1, x_p, x_s)

    y_prompt = x_all[0:n_prompt].reshape(bsz, seq, D_MODEL)
    y_sample = x_all[n_prompt:n_prompt + n_sample].reshape(dec_b, dec_seq, D_MODEL)
    return (y_prompt, y_sample, mem_k_prompt, mem_v_prompt, conv_a_p[None], conv_b_p[None], pool_p[None],
            conv_a_s[None], conv_b_s[None], pool_s[None], v_s_rows[None])
```

```python
import functools

import jax
import jax.numpy as jnp
import numpy as np
from jax import lax
from jax.experimental import pallas as pl
from jax.experimental.pallas import tpu as pltpu

F32 = jnp.float32
BF16 = jnp.bfloat16

D_MODEL = 1024
D_HALF = D_MODEL // 2
DEPTH = 2
PAST_LEN = 4096
CHUNK = 64
CHUNK_SHIFT = CHUNK.bit_length() - 1
CONV_A = 31
CONV_B = 3
POOL_WINDOWS = (2, 4, 8, 16)
POOL_BUF = max(POOL_WINDOWS) - 1
N_HEAD_D = 4
GMLP_CHUNK = 128
N_MEM = 256
MEM_HEADS = 4
MEM_HEAD_DIM = D_MODEL // MEM_HEADS
N_EXPERTS = 16
N_EXPERT_GROUPS = 4
GROUP_SIZE = N_EXPERTS // N_EXPERT_GROUPS
GROUP_SHIFT = GROUP_SIZE.bit_length() - 1
PAIRS_PER_GROUP = GROUP_SIZE * (GROUP_SIZE - 1) // 2
N_CLASSES = N_EXPERT_GROUPS * PAIRS_PER_GROUP
D_EXPERT = D_MODEL // 2
ALPHA = (2 * DEPTH) ** 0.25
LN_EPS = 1e-5

LANES = 128
SUBLANES = 8
ROW_W = D_MODEL + LANES
HIST_A = 32
HIST_B = 8
HIST_P = 16
CONV_ROWS = 64
ROW_TILE = 512
MOE_TILE = 256
MOE_SHIFT = MOE_TILE.bit_length() - 1
CLASS_ROWS = 32
DISPATCH_STEPS = 8
SPLIT_ROWS = 256
CACHE_SEQS = 4
STATE_TILES = 1
V7X_VMEM_BYTES = 64 * 1024 * 1024
VMEM_LIMIT = V7X_VMEM_BYTES * 7 // 8

_PAIR_AB = [(a, b) for a in range(GROUP_SIZE) for b in range(a + 1, GROUP_SIZE)]
_CLASS_A = np.array([g * GROUP_SIZE + a for g in range(N_EXPERT_GROUPS) for a, _ in _PAIR_AB], np.int32)
_CLASS_B = np.array([g * GROUP_SIZE + b for g in range(N_EXPERT_GROUPS) for _, b in _PAIR_AB], np.int32)


def _layer_norm(x, g, b):
    mu = jnp.mean(x, axis=-1, keepdims=True)
    xc = x - mu
    var = jnp.mean(xc * xc, axis=-1, keepdims=True)
    return xc * lax.rsqrt(var + LN_EPS) * g + b


def _bdot(a, w):
    return jnp.dot(a.astype(BF16), w, preferred_element_type=F32)


def _split(a):
    hi = a.astype(BF16)
    return hi, (a - hi.astype(F32)).astype(BF16)


def _dot3(a, w_hi, w_lo):
    a_hi, a_lo = _split(a)
    return (jnp.dot(a_hi, w_hi, preferred_element_type=F32) + jnp.dot(a_lo, w_hi, preferred_element_type=F32)
            + jnp.dot(a_hi, w_lo, preferred_element_type=F32))


def _params(sem):
    return pltpu.CompilerParams(dimension_semantics=sem, vmem_limit_bytes=VMEM_LIMIT)


def _full(shape):
    return pl.BlockSpec(shape, lambda *_: (0,) * len(shape), pipeline_mode=pl.Buffered(1))


def _row2(v):
    return v.reshape(1, -1)


def _split_weight_kernel(w_ref, hi_ref, lo_ref):
    hi, lo = _split(w_ref[...])
    hi_ref[...] = hi
    lo_ref[...] = lo


def _split_weight(w):
    rows, cols = w.shape
    blk = min(rows, SPLIT_ROWS)
    spec = pl.BlockSpec((blk, cols), lambda i: (i, 0))
    return pl.pallas_call(
        _split_weight_kernel,
        grid=(rows // blk,),
        in_specs=[spec],
        out_specs=[spec, spec],
        out_shape=[jax.ShapeDtypeStruct(w.shape, BF16)] * 2,
        compiler_params=_params(("arbitrary",)),
        name="split_weight",
    )(w)


def _proj_kernel(x_ref, w_ref, o_ref, o16_ref, *, batch):
    res = _bdot(x_ref[...], w_ref[0].astype(BF16))
    o16_ref[...] = res.astype(BF16).reshape(batch, N_MEM, D_MODEL)
    for hd in range(MEM_HEADS):
        cols = slice(hd * MEM_HEAD_DIM, (hd + 1) * MEM_HEAD_DIM)
        o_ref[0, :, :, hd, :] = res[:, cols].reshape(batch, N_MEM, MEM_HEAD_DIM)


def _mem_projection(mem, w):
    batch, n = mem.shape[0], w.shape[0]
    return pl.pallas_call(
        functools.partial(_proj_kernel, batch=batch),
        grid=(n,),
        in_specs=[pl.BlockSpec((batch * N_MEM, D_MODEL), lambda j: (0, 0)),
                  pl.BlockSpec((1, D_MODEL, D_MODEL), lambda j: (j, 0, 0))],
        out_specs=[pl.BlockSpec((1, batch, N_MEM, MEM_HEADS, MEM_HEAD_DIM), lambda j: (j, 0, 0, 0, 0)),
                   pl.BlockSpec((batch, N_MEM, D_MODEL), lambda j: (j, 0, 0))],
        out_shape=[jax.ShapeDtypeStruct((n, batch, N_MEM, MEM_HEADS, MEM_HEAD_DIM), F32),
                   jax.ShapeDtypeStruct((n * batch, N_MEM, D_MODEL), BF16)],
        compiler_params=_params(("arbitrary",)),
        name="mem_projection",
    )(mem.reshape(batch * N_MEM, D_MODEL), w)


def _heads_to_columns_kernel(c_hbm, o_ref, buf, sem):
    step = pl.program_id(0)

    def head_copies(at_step, slot):
        return [pltpu.make_async_copy(c_hbm.at[at_step * CACHE_SEQS + s, :, hd, :], buf.at[slot, s, hd], sem.at[slot])
                for s in range(CACHE_SEQS) for hd in range(MEM_HEADS)]

    @pl.when(step == 0)
    def _():
        for copy in head_copies(0, 0):
            copy.start()

    @pl.when(step + 1 < pl.num_programs(0))
    def _():
        for copy in head_copies(step + 1, (step + 1) % 2):
            copy.start()

    slot = step % 2
    for copy in head_copies(step, slot):
        copy.wait()
    for s in range(CACHE_SEQS):
        for hd in range(MEM_HEADS):
            o_ref[s, :, hd * MEM_HEAD_DIM:(hd + 1) * MEM_HEAD_DIM] = buf[slot, s, hd].astype(BF16)


def _heads_to_columns(cache):
    n = cache.shape[0]
    return pl.pallas_call(
        _heads_to_columns_kernel,
        grid=(n // CACHE_SEQS,),
        in_specs=[pl.BlockSpec(memory_space=pl.ANY)],
        out_specs=pl.BlockSpec((CACHE_SEQS, N_MEM, D_MODEL), lambda i: (i, 0, 0)),
        out_shape=jax.ShapeDtypeStruct((n, N_MEM, D_MODEL), BF16),
        scratch_shapes=[pltpu.VMEM((2, CACHE_SEQS, MEM_HEADS, N_MEM, MEM_HEAD_DIM), F32),
                        pltpu.SemaphoreType.DMA((2,))],
        compiler_params=_params(("arbitrary",)),
        name="heads_to_columns",
    )(cache)


def _load_history(ext_ref, buf_ref, first, hist, keep, seg):
    @pl.when(first)
    def _():
        ext_ref[:, hist - keep:hist, :] = buf_ref[...]

    @pl.when(jnp.logical_not(first))
    def _():
        ext_ref[:, hist - keep:hist, :] = ext_ref[:, seg + hist - keep:seg + hist, :]


def _depthwise_conv(ext_ref, w_ref, out_ref, *, n_seq, seg, taps, hist, shifted_ref=None):
    rc = min(CONV_ROWS, seg)
    off0 = hist - (taps - 1)
    length = hist + seg
    if shifted_ref is not None:
        for r in range(1, SUBLANES):
            shifted_ref[r - 1, :, 0:length - SUBLANES, :] = ext_ref[:, r:r + length - SUBLANES, :]
    for s in range(n_seq):
        for r0 in range(0, seg, rc):
            for lb in range(0, D_HALF, LANES):
                acc = None
                for k in range(taps):
                    lo = off0 + k + r0
                    shift = (off0 + k) % SUBLANES
                    if shifted_ref is None or shift == 0:
                        win = ext_ref[s, lo:lo + rc, lb:lb + LANES]
                    else:
                        win = shifted_ref[shift - 1, s, lo - shift:lo - shift + rc, lb:lb + LANES]
                    term = w_ref[k:k + 1, lb:lb + LANES] * win
                    acc = term if acc is None else acc + term
                out_ref[s * seg + r0:s * seg + r0 + rc, lb:lb + LANES] = acc


def _mixer_ab_kernel(x_ref, bufa_ref, bufb_ref, w_in_ref, w_in_lo_ref, caw_ref, cab_ref, nag_ref, nab_ref, cbw_ref,
                     w_out_ref, w_out_lo_ref, lng_ref, lnb_ref, *refs, n_seq, seg, precise, skip_tail, has_prev):
    if has_prev:
        refs = refs[1:]
    y_ref, nbufa_ref, nbufb_ref, a_ext, a_shift, cb_ext, conv_a, conv_b = refs

    def body():
        first = pl.program_id(1) == 0
        _load_history(a_ext, bufa_ref, first, HIST_A, CONV_A - 1, seg)
        _load_history(cb_ext, bufb_ref, first, HIST_B, CONV_B - 1, seg)
        x = x_ref[...]
        h = _dot3(x, w_in_ref[...], w_in_lo_ref[...]) if precise else _bdot(x, w_in_ref[...])
        a = h[:, 0:D_HALF] * jax.nn.sigmoid(h[:, D_HALF:2 * D_HALF])
        cb = h[:, 3 * D_HALF:4 * D_HALF] * h[:, 4 * D_HALF:5 * D_HALF]
        a_ext[:, HIST_A:HIST_A + seg, :] = a.reshape(n_seq, seg, D_HALF)
        cb_ext[:, HIST_B:HIST_B + seg, :] = cb.reshape(n_seq, seg, D_HALF)
        nbufa_ref[...] = a_ext[:, seg + HIST_A - (CONV_A - 1):seg + HIST_A, :]
        nbufb_ref[...] = cb_ext[:, seg + HIST_B - (CONV_B - 1):seg + HIST_B, :]

        _depthwise_conv(a_ext, caw_ref, conv_a, n_seq=n_seq, seg=seg, taps=CONV_A, hist=HIST_A, shifted_ref=a_shift)
        _depthwise_conv(cb_ext, cbw_ref, conv_b, n_seq=n_seq, seg=seg, taps=CONV_B, hist=HIST_B)

        a2 = _layer_norm(conv_a[...] + cab_ref[...], nag_ref[...], nab_ref[...])
        a2 = a2 * jax.nn.sigmoid(a2)
        b2 = h[:, 2 * D_HALF:3 * D_HALF] * conv_b[...]
        if precise:
            y = (_dot3(a2, w_out_ref[0:D_HALF, :], w_out_lo_ref[0:D_HALF, :])
                 + _dot3(b2, w_out_ref[D_HALF:D_MODEL, :], w_out_lo_ref[D_HALF:D_MODEL, :]))
        else:
            y = _bdot(a2, w_out_ref[0:D_HALF, :]) + _bdot(b2, w_out_ref[D_HALF:D_MODEL, :])
        y_ref[...] = _layer_norm(ALPHA * x + y, lng_ref[...], lnb_ref[...])

    if skip_tail == 0:
        body()
    else:
        live = pl.program_id(1) < pl.num_programs(1) - skip_tail
        pl.when(live)(body)

        @pl.when(jnp.logical_not(live))
        def _():
            y_ref[...] = jnp.zeros_like(y_ref)


def _mixer_ab(x, buf_a, buf_b, w_in, w_in_lo, caw, cab, nag, nab, cbw, w_out, w_out_lo, lng, lnb, *,
              n_seq, seg, precise, tile_lo=0, tiles=None, skip_tail=0, y_prev=None):
    batch = buf_a.shape[0]
    length = x.shape[0] // batch
    d_in = w_in.shape[1]
    n_l = length // seg
    tiles = n_l if tiles is None else tiles
    rows = n_seq * seg
    lo_in = _full((D_MODEL, d_in)) if precise else _full((SUBLANES, LANES))
    lo_out = _full((D_MODEL, D_MODEL)) if precise else _full((SUBLANES, LANES))
    row_blk = pl.BlockSpec((rows, D_MODEL), lambda b, l: (b * n_l + tile_lo + l, 0))
    in_specs = [row_blk,
                pl.BlockSpec((n_seq, CONV_A - 1, D_HALF), lambda b, l: (b, 0, 0)),
                pl.BlockSpec((n_seq, CONV_B - 1, D_HALF), lambda b, l: (b, 0, 0)),
                _full((D_MODEL, d_in)), lo_in, _full((CONV_A, D_HALF)), _full((1, D_HALF)), _full((1, D_HALF)),
                _full((1, D_HALF)), _full((CONV_B, D_HALF)), _full((D_MODEL, D_MODEL)), lo_out,
                _full((1, D_MODEL)), _full((1, D_MODEL))]
    args = [x, buf_a, buf_b, w_in, w_in_lo, caw, _row2(cab), _row2(nag), _row2(nab), cbw, w_out, w_out_lo,
            _row2(lng), _row2(lnb)]
    aliases = {}
    if y_prev is not None:
        aliases = {len(args): 0}
        in_specs.append(pl.BlockSpec(memory_space=pl.ANY))
        args.append(y_prev)
    return pl.pallas_call(
        functools.partial(_mixer_ab_kernel, n_seq=n_seq, seg=seg, precise=precise, skip_tail=skip_tail,
                          has_prev=y_prev is not None),
        grid=(batch // n_seq, tiles),
        in_specs=in_specs,
        out_specs=[row_blk,
                   pl.BlockSpec((n_seq, CONV_A - 1, D_HALF), lambda b, l: (b, 0, 0)),
                   pl.BlockSpec((n_seq, CONV_B - 1, D_HALF), lambda b, l: (b, 0, 0))],
        out_shape=[jax.ShapeDtypeStruct((batch * length, D_MODEL), F32),
                   jax.ShapeDtypeStruct((batch, CONV_A - 1, D_HALF), F32),
                   jax.ShapeDtypeStruct((batch, CONV_B - 1, D_HALF), F32)],
        scratch_shapes=[pltpu.VMEM((n_seq, HIST_A + seg, D_HALF), F32),
                        pltpu.VMEM((SUBLANES - 1, n_seq, HIST_A + seg - SUBLANES, D_HALF), F32),
                        pltpu.VMEM((n_seq, HIST_B + seg, D_HALF), F32),
                        pltpu.VMEM((rows, D_HALF), F32),
                        pltpu.VMEM((rows, D_HALF), F32)],
        input_output_aliases=aliases,
        compiler_params=_params(("arbitrary", "arbitrary")),
        name="mixer_ab",
    )(*args)


def _mixer_cd_kernel(x_ref, bufp_ref, w_in_ref, pw_ref, ps_ref, vg_ref, vb_ref, ws_ref, bs_ref, w_out_ref,
                     lng_ref, lnb_ref, *refs, n_seq, seg, pos0, n_mix, emit_v):
    if emit_v:
        y_ref, nbufp_ref, v_ref, c_ext, pooled, mixed = refs
    else:
        y_ref, nbufp_ref, c_ext, pooled, mixed = refs
    rows = n_seq * seg
    li = pl.program_id(1)
    _load_history(c_ext, bufp_ref, li == 0, HIST_P, POOL_BUF, seg)
    x = x_ref[...]
    h = _bdot(x, w_in_ref[...])
    c_in = h[:, 0:D_HALF]
    c_ext[:, HIST_P:HIST_P + seg, :] = c_in.reshape(n_seq, seg, D_HALF)
    nbufp_ref[...] = c_ext[:, seg + HIST_P - POOL_BUF:seg + HIST_P, :]

    rc = min(CONV_ROWS, seg)
    for g, win in enumerate(POOL_WINDOWS):
        lanes = slice(g * LANES, (g + 1) * LANES)
        for s in range(n_seq):
            for r0 in range(0, seg, rc):
                cur = c_ext[s, HIST_P + r0:HIST_P + r0 + rc, lanes]
                acc = cur
                for j in range(1, win):
                    acc = acc + c_ext[s, HIST_P + r0 - j:HIST_P + r0 - j + rc, lanes]
                pos = pos0 + li * seg + r0 + lax.broadcasted_iota(jnp.int32, (rc, LANES), 0)
                cnt = jnp.minimum(pos + 1, win).astype(F32)
                pooled[s * seg + r0:s * seg + r0 + rc, lanes] = acc / cnt - cur
    c = _bdot(pooled[...], pw_ref[...]) * ps_ref[...]

    z = jax.nn.gelu(h[:, D_HALF:3 * D_HALF], approximate=True)
    u = z[:, 0:D_HALF]
    v = _layer_norm(z[:, D_HALF:2 * D_HALF], vg_ref[...], vb_ref[...])
    if emit_v:
        v_ref[...] = v.reshape(n_seq, seg, D_HALF)
    vb16 = v.astype(BF16)

    ri = lax.broadcasted_iota(jnp.int32, (n_mix, n_mix), 0) >> CHUNK_SHIFT
    ci = lax.broadcasted_iota(jnp.int32, (n_mix, n_mix), 1) >> CHUNK_SHIFT
    for g in range(N_HEAD_D):
        lanes = slice(g * LANES, (g + 1) * LANES)
        ws = jnp.where(ci <= ri, ws_ref[g], 0.0).astype(BF16)
        bias = bs_ref[:, g:g + 1]
        for r0 in range(0, rows, n_mix):
            mixed[r0:r0 + n_mix, lanes] = jnp.dot(ws, vb16[r0:r0 + n_mix, lanes], preferred_element_type=F32) + bias
    d = u * mixed[...]
    y = _bdot(c, w_out_ref[0:D_HALF, :]) + _bdot(d, w_out_ref[D_HALF:D_MODEL, :])
    y_ref[...] = _layer_norm(ALPHA * x + y, lng_ref[...], lnb_ref[...])


def _mixer_cd(x, row_start, buf_p, length, w_in, pw, ps, vg, vb, ws, bs, w_out, lng, lnb, *,
              n_seq, seg, pos0, emit_v):
    batch = buf_p.shape[0]
    d_in = w_in.shape[1]
    n_mix = min(length, GMLP_CHUNK)
    assert seg % n_mix == 0
    n_l = length // seg
    rows = n_seq * seg
    blk0 = row_start // rows
    ws_n = ws[:, :n_mix, :n_mix]
    bs_t = bs[:, :n_mix].T
    out_specs = [pl.BlockSpec((rows, D_MODEL), lambda b, l: (b * n_l + l, 0)),
                 pl.BlockSpec((n_seq, POOL_BUF, D_HALF), lambda b, l: (b, 0, 0))]
    out_shape = [jax.ShapeDtypeStruct((batch * length, D_MODEL), F32),
                 jax.ShapeDtypeStruct((batch, POOL_BUF, D_HALF), F32)]
    if emit_v:
        out_specs.append(pl.BlockSpec((n_seq, seg, D_HALF), lambda b, l: (b, l, 0)))
        out_shape.append(jax.ShapeDtypeStruct((batch, length, D_HALF), F32))
    return pl.pallas_call(
        functools.partial(_mixer_cd_kernel, n_seq=n_seq, seg=seg, pos0=pos0, n_mix=n_mix, emit_v=emit_v),
        grid=(batch // n_seq, n_l),
        in_specs=[pl.BlockSpec((rows, D_MODEL), lambda b, l: (blk0 + b * n_l + l, 0)),
                  pl.BlockSpec((n_seq, POOL_BUF, D_HALF), lambda b, l: (b, 0, 0)),
                  _full((D_MODEL, d_in)), _full((D_HALF, D_HALF)), _full((1, D_HALF)), _full((1, D_HALF)),
                  _full((1, D_HALF)), _full((N_HEAD_D, n_mix, n_mix)), _full((n_mix, N_HEAD_D)),
                  _full((D_MODEL, D_MODEL)), _full((1, D_MODEL)), _full((1, D_MODEL))],
        out_specs=out_specs,
        out_shape=out_shape,
        scratch_shapes=[pltpu.VMEM((n_seq, HIST_P + seg, D_HALF), F32),
                        pltpu.VMEM((rows, D_HALF), F32),
                        pltpu.VMEM((rows, D_HALF), F32)],
        compiler_params=_params(("arbitrary", "arbitrary")),
        name="mixer_cd",
    )(x, buf_p, w_in, pw, _row2(ps), _row2(vg), _row2(vb), ws_n, bs_t, w_out, _row2(lng), _row2(lnb))


def _attn_router_kernel(xp_ref, xs_ref, kp_ref, vp_ref, ks_ref, vs_ref, wq_ref, wo_ref, lng_ref, lnb_ref, rw_ref, rb_ref,
                        rows_ref, meta_ref, counts_ref,
                        q_scr, o_scr, carry, tri, sc_scr, *, n_prompt_tiles, sample_seg):
    rows = ROW_TILE
    step = pl.program_id(0)
    is_prompt = step < n_prompt_tiles

    @pl.when(step == 0)
    def _():
        carry[...] = jnp.zeros_like(carry)
        upper = (lax.broadcasted_iota(jnp.int32, (rows, rows), 0) <= lax.broadcasted_iota(jnp.int32, (rows, rows), 1))
        tri[...] = upper.astype(F32).astype(BF16)

    def run(x_ref, k_ref, v_ref, n_seq, seg):
        x = x_ref[...]
        q_scr[...] = (_bdot(x, wq_ref[0]) * (MEM_HEAD_DIM ** -0.5)).astype(BF16)
        blocks = [(s, hd) for s in range(n_seq) for hd in range(MEM_HEADS)]
        for j, (s, hd) in enumerate(blocks):
            cols = slice(hd * MEM_HEAD_DIM, (hd + 1) * MEM_HEAD_DIM)
            sc_scr[j * seg:(j + 1) * seg, :] = lax.dot_general(
                q_scr[s * seg:(s + 1) * seg, cols], k_ref[s, :, cols], (((1,), (1,)), ((), ())),
                preferred_element_type=F32)
        sc = sc_scr[...]
        p = jnp.exp(sc - jnp.max(sc, axis=-1, keepdims=True))
        inv = 1.0 / jnp.sum(p, axis=-1, keepdims=True)
        pb = p.astype(BF16)
        for j, (s, hd) in enumerate(blocks):
            cols = slice(hd * MEM_HEAD_DIM, (hd + 1) * MEM_HEAD_DIM)
            o_scr[s * seg:(s + 1) * seg, cols] = (
                jnp.dot(pb[j * seg:(j + 1) * seg, :], v_ref[s, :, cols], preferred_element_type=F32)
                * inv[j * seg:(j + 1) * seg, :])
        x2 = _layer_norm(ALPHA * x + _bdot(o_scr[...], wo_ref[0]), lng_ref[...], lnb_ref[...])
        rows_ref[:, 0:D_MODEL] = x2

        nt = (((1,), (1,)), ((), ()))
        x2_hi, x2_lo = _split(x2)
        both = lax.dot_general(rw_ref[...], x2_hi, nt, preferred_element_type=F32)
        cross = lax.dot_general(rw_ref[0:N_EXPERTS, :], x2_lo, nt, preferred_element_type=F32)
        logits = both[0:N_EXPERTS] + both[N_EXPERTS:2 * N_EXPERTS] + cross + rb_ref[...]
        e = jnp.exp(logits - jnp.max(logits, axis=0, keepdims=True))
        scores = e / jnp.sum(e, axis=0, keepdims=True)
        eid = lax.broadcasted_iota(jnp.int32, (N_EXPERTS, rows), 0)
        egrp = eid >> GROUP_SHIFT
        best = jnp.max(jnp.where(egrp == 0, scores, -1.0), axis=0, keepdims=True)
        g_sel = jnp.zeros((1, rows), jnp.int32)
        for g in range(1, N_EXPERT_GROUPS):
            gs = jnp.max(jnp.where(egrp == g, scores, -1.0), axis=0, keepdims=True)
            upd = gs > best
            g_sel = jnp.where(upd, g, g_sel)
            best = jnp.where(upd, gs, best)
        masked = jnp.where(egrp == g_sel, scores, -1.0)
        m1 = jnp.max(masked, axis=0, keepdims=True)
        i1 = jnp.min(jnp.where(masked == m1, eid, N_EXPERTS), axis=0, keepdims=True)
        masked2 = jnp.where(eid == i1, -2.0, masked)
        m2 = jnp.max(masked2, axis=0, keepdims=True)
        i2 = jnp.min(jnp.where(masked2 == m2, eid, N_EXPERTS), axis=0, keepdims=True)
        tot = m1 + m2
        g1 = m1 / tot
        g2 = m2 / tot
        first_low = i1 < i2
        ea = jnp.where(first_low, i1, i2) & (GROUP_SIZE - 1)
        eb = jnp.where(first_low, i2, i1) & (GROUP_SIZE - 1)
        gate_a = jnp.where(first_low, g1, g2)
        gate_b = jnp.where(first_low, g2, g1)
        pair = eb - 1 + jnp.where(ea == 1, 2, 0) + jnp.where(ea == 2, 3, 0)
        cls = g_sel * PAIRS_PER_GROUP + pair

        lane_row = lax.broadcasted_iota(jnp.int32, (LANES, rows), 0)
        gates_t = jnp.where(lane_row == 0, gate_a, jnp.where(lane_row == 1, gate_b, 0.0))
        rows_ref[:, D_MODEL:ROW_W] = gates_t.T

        onehot = (lax.broadcasted_iota(jnp.int32, (CLASS_ROWS, rows), 0) == cls).astype(F32)
        cum = jnp.dot(onehot.astype(BF16), tri[...], preferred_element_type=F32)
        before = carry[:, 0:1]
        rank = jnp.sum(onehot * (before + cum), axis=0, keepdims=True) - 1.0
        sub = lax.broadcasted_iota(jnp.int32, (SUBLANES, rows), 0)
        meta_ref[...] = jnp.where(sub == 0, cls, jnp.where(sub == 1, rank.astype(jnp.int32), 0))
        carry[...] = carry[...] + cum[:, rows - 1:rows]
        counts_ref[...] = carry[...].astype(jnp.int32)

    pl.when(is_prompt)(lambda: run(xp_ref, kp_ref, vp_ref, 1, rows))
    pl.when(jnp.logical_not(is_prompt))(lambda: run(xs_ref, ks_ref, vs_ref, rows // sample_seg, sample_seg))


def _attn_router(x_p, x_s, k_p, v_p, k_s, v_s, wq, wo, lng, lnb, rw_t, rb, *, layer, prompt_len, sample_len):
    rows = ROW_TILE
    n_p = x_p.shape[0] // rows
    n_s = x_s.shape[0] // rows
    tiles_per_seq = prompt_len // rows
    seq_per_tile = rows // sample_len
    total = x_p.shape[0] + x_s.shape[0]
    n_prompt_seq = n_p // tiles_per_seq
    p0 = layer * n_prompt_seq
    s0 = layer * n_s
    p_idx = lambda i: jnp.minimum(i, n_p - 1)
    s_idx = lambda i: jnp.maximum(i - n_p, 0)
    layer_w = pl.BlockSpec((1, D_MODEL, D_MODEL), lambda i: (layer, 0, 0), pipeline_mode=pl.Buffered(1))
    return pl.pallas_call(
        functools.partial(_attn_router_kernel, n_prompt_tiles=n_p, sample_seg=sample_len),
        grid=(n_p + n_s,),
        in_specs=[pl.BlockSpec((rows, D_MODEL), lambda i: (p_idx(i), 0)),
                  pl.BlockSpec((rows, D_MODEL), lambda i: (s_idx(i), 0)),
                  pl.BlockSpec((1, N_MEM, D_MODEL), lambda i: (p0 + p_idx(i) // tiles_per_seq, 0, 0)),
                  pl.BlockSpec((1, N_MEM, D_MODEL), lambda i: (p0 + p_idx(i) // tiles_per_seq, 0, 0)),
                  pl.BlockSpec((seq_per_tile, N_MEM, D_MODEL), lambda i: (s0 + s_idx(i), 0, 0)),
                  pl.BlockSpec((seq_per_tile, N_MEM, D_MODEL), lambda i: (s0 + s_idx(i), 0, 0)),
                  layer_w, layer_w, _full((1, D_MODEL)), _full((1, D_MODEL)),
                  _full((2 * N_EXPERTS, D_MODEL)), _full((N_EXPERTS, 1))],
        out_specs=[pl.BlockSpec((rows, ROW_W), lambda i: (i, 0)),
                   pl.BlockSpec((SUBLANES, rows), lambda i: (0, i)),
                   pl.BlockSpec((CLASS_ROWS, LANES), lambda i: (0, 0))],
        out_shape=[jax.ShapeDtypeStruct((total, ROW_W), F32),
                   jax.ShapeDtypeStruct((SUBLANES, total), jnp.int32),
                   jax.ShapeDtypeStruct((CLASS_ROWS, LANES), jnp.int32)],
        scratch_shapes=[pltpu.VMEM((rows, D_MODEL), BF16),
                        pltpu.VMEM((rows, D_MODEL), F32),
                        pltpu.VMEM((CLASS_ROWS, LANES), F32),
                        pltpu.VMEM((rows, rows), BF16),
                        pltpu.VMEM((MEM_HEADS * rows, N_MEM), F32)],
        compiler_params=_params(("arbitrary",)),
        name="attn_router",
    )(x_p, x_s, k_p, v_p, k_s, v_s, wq, wo, _row2(lng), _row2(lnb), rw_t, rb.reshape(N_EXPERTS, 1))


def _dispatch_kernel(pos_ref, offs_ref, cnt_ref, end_ref, used_ref, rows_ref, xs_hbm, dst_ref,
                     zeros, sem, zsem, *, tile, n_tiles, n_rows):
    step = pl.program_id(0)

    def class_padding(c, go):
        start = offs_ref[c] + cnt_ref[c]
        head = (-start) & (SUBLANES - 1)
        bulk_start = pl.multiple_of(start + head, SUBLANES)
        bulk = pl.multiple_of(end_ref[c] - bulk_start, SUBLANES)

        def one(r, carry_):
            go(pltpu.make_async_copy(zeros.at[pl.ds(0, 1)], xs_hbm.at[pl.ds(r, 1)], zsem))
            return carry_
        lax.fori_loop(start, start + head, one, 0)

        @pl.when(bulk > 0)
        def _():
            go(pltpu.make_async_copy(zeros.at[pl.ds(0, bulk)], xs_hbm.at[pl.ds(bulk_start, bulk)], zsem))

    def unused_tiles(go):
        def one(j, carry_):
            go(pltpu.make_async_copy(zeros, xs_hbm.at[pl.ds(pl.multiple_of(j * MOE_TILE, MOE_TILE), MOE_TILE)], zsem))
            return carry_
        lax.fori_loop(used_ref[0], n_tiles, one, 0)

    @pl.when(step == 0)
    def _():
        zeros[...] = jnp.zeros_like(zeros)

        def no_token(p, carry_):
            dst_ref[p] = n_rows + ((p >> MOE_SHIFT) & 1) * MOE_TILE + (p & (MOE_TILE - 1))
            return carry_
        for c in range(N_CLASSES):
            lax.fori_loop(offs_ref[c] + cnt_ref[c], end_ref[c], no_token, 0)
            class_padding(c, lambda copy: copy.start())
        lax.fori_loop(used_ref[0] * MOE_TILE, n_tiles * MOE_TILE, no_token, 0)
        unused_tiles(lambda copy: copy.start())

    for r in range(tile):
        t = step * tile + r
        p = pos_ref[t]
        dst_ref[p] = t
        pltpu.make_async_copy(rows_ref.at[pl.ds(r, 1)], xs_hbm.at[pl.ds(p, 1)], sem).start(priority=r % 2)
    pltpu.make_async_copy(rows_ref, xs_hbm.at[pl.ds(0, tile)], sem).wait()

    @pl.when(step == 0)
    def _():
        for c in range(N_CLASSES):
            class_padding(c, lambda copy: copy.wait())
        unused_tiles(lambda copy: copy.wait())


def _dispatch(pos, offs, cnt, end, used, rows, n_tiles):
    n_rows = rows.shape[0]
    n_sorted = n_tiles * MOE_TILE
    tile = n_rows // DISPATCH_STEPS
    assert tile * DISPATCH_STEPS == n_rows and tile % SUBLANES == 0
    any_spec = pl.BlockSpec(memory_space=pl.ANY)
    return pl.pallas_call(
        functools.partial(_dispatch_kernel, tile=tile, n_tiles=n_tiles, n_rows=n_rows),
        grid_spec=pltpu.PrefetchScalarGridSpec(
            num_scalar_prefetch=5, grid=(DISPATCH_STEPS,),
            in_specs=[pl.BlockSpec((tile, ROW_W), lambda i, *_: (i, 0))],
            out_specs=[any_spec, pl.BlockSpec(memory_space=pltpu.SMEM)],
            scratch_shapes=[pltpu.VMEM((MOE_TILE, ROW_W), F32), pltpu.SemaphoreType.DMA(()),
                            pltpu.SemaphoreType.DMA(())]),
        out_shape=[jax.ShapeDtypeStruct((n_sorted, ROW_W), F32), jax.ShapeDtypeStruct((n_sorted,), jnp.int32)],
        compiler_params=_params(("arbitrary",)),
        name="dispatch",
    )(pos, offs, cnt, end, used, rows)


def _experts_kernel(ea_ref, eb_ref, blk_ref, nvalid_ref, dst_ref, xs_ref, wga_ref, wua_ref, wda_ref, wgb_ref, wub_ref,
                    wdb_ref, lng_ref, lnb_ref, out_hbm, ybuf, sems, *, n_rows):
    del ea_ref, eb_ref, blk_ref
    i = pl.program_id(0)
    n_cur = nvalid_ref[i]
    n_prev = jnp.where(i >= 1, nvalid_ref[jnp.maximum(i - 1, 0)], 0)
    n_prev2 = jnp.where(i >= 2, nvalid_ref[jnp.maximum(i - 2, 0)], 0)
    has_cur = n_cur > 0
    has_prev = n_prev > 0

    def tile_copy(slot):
        return pltpu.make_async_copy(ybuf.at[slot], out_hbm.at[pl.ds(0, MOE_TILE)], sems.at[slot])

    def compute(slot, used):
        x = xs_ref[0:used, 0:D_MODEL]
        xb = x.astype(BF16)
        y = None
        for lane, (wg, wu, wd) in enumerate(((wga_ref, wua_ref, wda_ref), (wgb_ref, wub_ref, wdb_ref))):
            gate = xs_ref[0:used, D_MODEL + lane:D_MODEL + lane + 1]
            hg = jnp.dot(xb, wg[0].astype(BF16), preferred_element_type=F32)
            hu = jnp.dot(xb, wu[0].astype(BF16), preferred_element_type=F32)
            hid = hg * jax.nn.sigmoid(hg) * hu * gate
            part = _bdot(hid, wd[0].astype(BF16))
            y = part if y is None else y + part
        ybuf[slot, 0:used, :] = _layer_norm(ALPHA * x + y, lng_ref[...], lnb_ref[...])
        if used < MOE_TILE:
            ybuf[slot, used:MOE_TILE, :] = jnp.zeros((MOE_TILE - used, D_MODEL), F32)

    def start_rows(tile, slot):
        base = tile * MOE_TILE
        for r in range(MOE_TILE):
            dst = dst_ref[base + r]
            copy = pltpu.make_async_copy(ybuf.at[slot, pl.ds(r, 1)], out_hbm.at[pl.ds(dst, 1)], sems.at[slot])
            copy.start(priority=r % 2)

    @pl.when(i == 0)
    def _():
        ybuf[2] = jnp.zeros((MOE_TILE, D_MODEL), F32)
        for half in range(2):
            fill = pltpu.make_async_copy(ybuf.at[2], out_hbm.at[pl.ds(n_rows + half * MOE_TILE, MOE_TILE)], sems.at[2])
            fill.start()
            fill.wait()

    slot_cur = i % 3
    slot_prev = (i + 2) % 3

    half = n_cur <= MOE_TILE // 2

    @pl.when(has_prev & has_cur & jnp.logical_not(half))
    def _():
        start_rows(i - 1, slot_prev)
        compute(slot_cur, MOE_TILE)

    @pl.when(has_prev & has_cur & half)
    def _():
        start_rows(i - 1, slot_prev)
        compute(slot_cur, MOE_TILE // 2)

    @pl.when(has_prev & jnp.logical_not(has_cur))
    def _():
        start_rows(i - 1, slot_prev)

    @pl.when(jnp.logical_not(has_prev) & has_cur)
    def _():
        compute(slot_cur, MOE_TILE)

    @pl.when(n_prev2 > 0)
    def _():
        tile_copy((i + 1) % 3).wait()


def _experts(tile_a, tile_b, tile_blk, tile_nvalid, dst, xs, w_gate, w_up, w_down, lng, lnb, n_rows, layer):
    n_steps = tile_nvalid.shape[0]
    e0 = layer * N_EXPERTS
    up_a = pl.BlockSpec((1, D_MODEL, D_EXPERT), lambda i, ea, eb, *_: (e0 + ea[i], 0, 0))
    up_b = pl.BlockSpec((1, D_MODEL, D_EXPERT), lambda i, ea, eb, *_: (e0 + eb[i], 0, 0))
    dn_a = pl.BlockSpec((1, D_EXPERT, D_MODEL), lambda i, ea, eb, *_: (e0 + ea[i], 0, 0))
    dn_b = pl.BlockSpec((1, D_EXPERT, D_MODEL), lambda i, ea, eb, *_: (e0 + eb[i], 0, 0))
    vec = pl.BlockSpec((1, D_MODEL), lambda i, *_: (0, 0))
    return pl.pallas_call(
        functools.partial(_experts_kernel, n_rows=n_rows),
        grid_spec=pltpu.PrefetchScalarGridSpec(
            num_scalar_prefetch=5, grid=(n_steps,),
            in_specs=[pl.BlockSpec((MOE_TILE, ROW_W), lambda i, ea, eb, blk, *_: (blk[i], 0)),
                      up_a, up_a, dn_a, up_b, up_b, dn_b, vec, vec],
            out_specs=pl.BlockSpec(memory_space=pl.ANY),
            scratch_shapes=[pltpu.VMEM((3, MOE_TILE, D_MODEL), F32), pltpu.SemaphoreType.DMA((3,))]),
        out_shape=jax.ShapeDtypeStruct((n_rows + 2 * MOE_TILE, D_MODEL), F32),
        compiler_params=_params(("arbitrary",)),
        name="experts",
    )(tile_a, tile_b, tile_blk, tile_nvalid, dst, xs, w_gate, w_up, w_down, w_gate, w_up, w_down,
      _row2(lng), _row2(lnb))


def _lookup(table, idx):
    pick = idx[:, None] == jnp.arange(table.shape[0], dtype=jnp.int32)[None, :]
    return jnp.sum(jnp.where(pick, table[None, :], 0), axis=1)


def _moe(rows, meta, counts, w_gate, w_up, w_down, lng, lnb, layer):
    n_rows = rows.shape[0]
    n_tiles = n_rows // MOE_TILE + N_CLASSES
    cnt = counts[:N_CLASSES, 0]
    tiles_per = (cnt + MOE_TILE - 1) // MOE_TILE
    cls_id = jnp.arange(N_CLASSES, dtype=jnp.int32)
    tile_end = jnp.sum(jnp.where(cls_id[None, :] <= cls_id[:, None], tiles_per[None, :], 0), axis=1)
    tile_start = tile_end - tiles_per
    offs = tile_start * MOE_TILE
    end = tile_end * MOE_TILE
    used = tile_end[N_CLASSES - 1:N_CLASSES]
    tile_id = jnp.arange(n_tiles + 2, dtype=jnp.int32)
    tile_blk = jnp.minimum(tile_id, used - 1)
    tile_cls = jnp.sum((tile_blk[:, None] >= tile_end[None, :]).astype(jnp.int32), axis=1)
    tile_a = _lookup(jnp.asarray(_CLASS_A), tile_cls)
    tile_b = _lookup(jnp.asarray(_CLASS_B), tile_cls)
    tile_nvalid = jnp.clip(_lookup(cnt, tile_cls) - (tile_id - _lookup(tile_start, tile_cls)) * MOE_TILE, 0, MOE_TILE)
    pos = _lookup(offs, meta[0]) + meta[1]
    xs, dst = _dispatch(pos, offs, cnt, end, used, rows, n_tiles)
    return _experts(tile_a, tile_b, tile_blk, tile_nvalid, dst, xs, w_gate, w_up, w_down, lng, lnb, n_rows, layer)


def kernel(x_prompt, x_sample, mem_prompt, cache_mem_k, cache_mem_v, state_conv_a, state_conv_b, state_pool,
           ln_g, ln_b, ab_w_in, ab_conv_a_w, ab_conv_a_b, ab_norm_a_g, ab_norm_a_b, ab_conv_b_w, ab_w_out,
           cd_w_in, cd_pool_w, cd_pool_scale, cd_v_norm_g, cd_v_norm_b, cd_w_s, cd_b_s, cd_w_out,
           ca_wq, ca_wk, ca_wv, ca_wo, router_w, router_b, moe_w_gate, moe_w_up, moe_w_down):
    assert ln_g.shape[0] == DEPTH and ab_w_in.shape[0] == 1 and cd_w_in.shape[0] == 1
    bsz, seq, _ = x_prompt.shape
    dec_b, dec_seq, _ = x_sample.shape
    n_prompt = bsz * seq
    n_sample = dec_b * dec_seq
    sample_seqs = ROW_TILE // dec_seq

    n_pool = len(POOL_WINDOWS)
    pool_c = D_HALF // n_pool
    pool_bd = jnp.zeros((D_HALF, D_HALF), F32)
    for g in range(n_pool):
        pool_bd = pool_bd.at[g * pool_c:(g + 1) * pool_c, g * pool_c:(g + 1) * pool_c].set(cd_pool_w[0, g])
    pool_bd = pool_bd.astype(BF16)
    ab_w_in_hi, ab_w_in_lo = _split_weight(ab_w_in[0])
    ab_w_out_hi, ab_w_out_lo = _split_weight(ab_w_out[0])
    rw_hi, rw_lo = _split_weight(router_w.T)
    rw_t = jnp.concatenate([rw_hi, rw_lo], axis=0)
    cd_w_in_b, cd_w_out_b = cd_w_in[0].astype(BF16), cd_w_out[0].astype(BF16)
    wq_b, wo_b = ca_wq.astype(BF16), ca_wo.astype(BF16)
    wg_b = moe_w_gate.reshape(DEPTH * N_EXPERTS, D_MODEL, D_EXPERT)
    wu_b = moe_w_up.reshape(DEPTH * N_EXPERTS, D_MODEL, D_EXPERT)
    wd_b = moe_w_down.reshape(DEPTH * N_EXPERTS, D_EXPERT, D_MODEL)

    mem_k_prompt, k_p = _mem_projection(mem_prompt, ca_wk)
    mem_v_prompt, v_p = _mem_projection(mem_prompt, ca_wv)
    k_s = _heads_to_columns(cache_mem_k.reshape(DEPTH * dec_b, N_MEM, MEM_HEADS, MEM_HEAD_DIM))
    v_s = _heads_to_columns(cache_mem_v.reshape(DEPTH * dec_b, N_MEM, MEM_HEADS, MEM_HEAD_DIM))

    def mixer_ab(x, buf_a, buf_b, n_seq, seg, **kw):
        return _mixer_ab(x, buf_a, buf_b, ab_w_in_hi, ab_w_in_lo, ab_conv_a_w[0], ab_conv_a_b[0], ab_norm_a_g[0],
                         ab_norm_a_b[0], ab_conv_b_w[0], ab_w_out_hi, ab_w_out_lo, ln_g[0, 0], ln_b[0, 0],
                         n_seq=n_seq, seg=seg, **kw)

    def mixer_cd(x, row_start, buf_p, length, n_seq, seg, pos0, emit_v):
        return _mixer_cd(x, row_start, buf_p, length, cd_w_in_b, pool_bd, cd_pool_scale[0], cd_v_norm_g[0],
                         cd_v_norm_b[0], cd_w_s[0], cd_b_s[0], cd_w_out_b, ln_g[1, 0], ln_b[1, 0],
                         n_seq=n_seq, seg=seg, pos0=pos0, emit_v=emit_v)

    def attn_moe(layer, x_p, x_s):
        rows, meta, counts = _attn_router(x_p, x_s, k_p, v_p, k_s, v_s, wq_b, wo_b, ln_g[layer, 1], ln_b[layer, 1],
                                          rw_t, router_b, layer=layer, prompt_len=seq, sample_len=dec_seq)
        return _moe(rows, meta, counts, wg_b, wu_b, wd_b, ln_g[layer, 2], ln_b[layer, 2], layer)

    xp_flat = x_prompt.reshape(n_prompt, D_MODEL)
    tiles_p = seq // ROW_TILE
    x_p, conv_a_p, conv_b_p = mixer_ab(xp_flat, jnp.zeros((bsz, CONV_A - 1, D_HALF), F32),
                                       jnp.zeros((bsz, CONV_B - 1, D_HALF), F32), 1, ROW_TILE,
                                       precise=False, skip_tail=STATE_TILES)
    x_p, conv_a_p, conv_b_p = mixer_ab(xp_flat, conv_a_p, conv_b_p, 1, ROW_TILE, precise=True,
                                       tile_lo=tiles_p - STATE_TILES, tiles=STATE_TILES, y_prev=x_p)
    x_s, conv_a_s, conv_b_s = mixer_ab(x_sample.reshape(n_sample, D_MODEL), state_conv_a[0], state_conv_b[0],
                                       sample_seqs, dec_seq, precise=False)
    x_all = attn_moe(0, x_p, x_s)

    x_p, pool_p = mixer_cd(x_all, 0, jnp.zeros((bsz, POOL_BUF, D_HALF), F32), seq, 1, ROW_TILE, 0, False)
    x_s, pool_s, v_s_rows = mixer_cd(x_all, n_prompt, state_pool[0], dec_seq, sample_seqs, dec_seq, PAST_LEN, True)
    x_all = attn_moe(1, x_p, x_s)

    y_prompt = x_all[0:n_prompt].reshape(bsz, seq, D_MODEL)
    y_sample = x_all[n_prompt:n_prompt + n_sample].reshape(dec_b, dec_seq, D_MODEL)
    return (y_prompt, y_sample, mem_k_prompt, mem_v_prompt, conv_a_p[None], conv_b_p[None], pool_p[None],
            conv_a_s[None], conv_b_s[None], pool_s[None], v_s_rows[None])
```
